```python
import jax, jax.numpy as jnp
from jax import lax
import numpy as np

D_MODEL = 2048
BATCH = 8
SEQ = 2048
DEPTH = 4

HEAD_DIM = 128
D_ATT = D_MODEL // 2
N_HEADS_ATT = D_ATT // HEAD_DIM
Q_BLOCK = 128
D_SC = D_MODEL // 4
N_GROUPS_SC = 4
SC_WIDTH = 3
D_CF = D_MODEL // 4
N_GROUPS_CF = 4
CF_WIDTH = 31
N_BRANCH = 3
D_FF = 4 * D_MODEL
N_IN = 3 * D_ATT + 3 * D_SC + 2 * D_CF + N_BRANCH * D_MODEL
RMS_EPS = 1e-6
LN_EPS = 1e-5

kernel_name = 'gated_parallel_hybrid_sb_conv_block'


def rms_norm(x, g):
    xf = x.astype(jnp.float32)
    y = xf * lax.rsqrt(jnp.mean(jnp.square(xf), axis=-1, keepdims=True) + RMS_EPS)
    return (y * g.astype(jnp.float32)).astype(x.dtype)


def layer_norm(x, g, b):
    xf = x.astype(jnp.float32)
    mu = jnp.mean(xf, axis=-1, keepdims=True)
    var = jnp.mean(jnp.square(xf - mu), axis=-1, keepdims=True)
    y = (xf - mu) * lax.rsqrt(var + LN_EPS)
    return (y * g.astype(jnp.float32) + b.astype(jnp.float32)).astype(x.dtype)


def causal_dwconv(u, w):
    K, C = w.shape
    return lax.conv_general_dilated(
        u, w[:, None, :].astype(u.dtype), window_strides=(1,), padding=[(K - 1, 0)],
        dimension_numbers=('NWC', 'WIO', 'NWC'), feature_group_count=C)


def stick_breaking_attention(q, k, v):
    S = q.shape[1]
    scale = HEAD_DIM ** -0.5
    outs = []
    for blk in range(S // Q_BLOCK):
        q0 = blk * Q_BLOCK
        kend = q0 + Q_BLOCK
        qb = q[:, q0:kend]
        kb = k[:, :kend]
        vb = v[:, :kend]
        z = jnp.einsum('bthd,bshd->bhts', qb, kb).astype(jnp.float32) * scale
        t_idx = q0 + jnp.arange(Q_BLOCK)[:, None]
        s_idx = jnp.arange(kend)[None, :]
        mask = s_idx < t_idx
        log_fail = jnp.where(mask, jax.nn.log_sigmoid(-z), 0.0)
        suffix = lax.cumsum(log_fail, axis=3, reverse=True) - log_fail
        log_a = jax.nn.log_sigmoid(z) + suffix
        a = jnp.where(mask, jnp.exp(log_a), 0.0)
        outs.append(jnp.einsum('bhts,bshd->bthd', a.astype(v.dtype), vb))
    return jnp.concatenate(outs, axis=1)


def _fwd_setup_inputs(seed: int = 0) -> dict:
    key = jax.random.key(seed)
    ks = jax.random.split(key, 20)
    f32 = jnp.float32

    def nrm(k, shape, fan_in):
        return jax.random.normal(k, shape, f32) * (fan_in ** -0.5)

    def gain(k, shape):
        return jnp.ones(shape, f32) + 0.02 * jax.random.normal(k, shape, f32)

    return {
        'x': jax.random.normal(ks[0], (BATCH, SEQ, D_MODEL), f32),
        'ln_mix_pre': gain(ks[1], (DEPTH, D_MODEL)),
        'ln_mix_post': gain(ks[2], (DEPTH, D_MODEL)),
        'ln_mlp_pre': gain(ks[3], (DEPTH, D_MODEL)),
        'ln_mlp_post': gain(ks[4], (DEPTH, D_MODEL)),
        'w_in': nrm(ks[5], (DEPTH, D_MODEL, N_IN), D_MODEL),
        'conv_a_w': nrm(ks[6], (DEPTH, SC_WIDTH, D_SC), SC_WIDTH),
        'proj_a': nrm(ks[7], (DEPTH, D_SC, D_MODEL), D_SC),
        'proj_b': nrm(ks[8], (DEPTH, D_ATT, D_MODEL), D_ATT),
        'conv_c_w': nrm(ks[9], (DEPTH, CF_WIDTH, D_CF), CF_WIDTH),
        'conv_c_b': 0.02 * jax.random.normal(ks[10], (DEPTH, D_CF), f32),
        'norm_c_g': gain(ks[11], (DEPTH, D_CF)),
        'norm_c_b': 0.02 * jax.random.normal(ks[12], (DEPTH, D_CF), f32),
        'proj_c': nrm(ks[13], (DEPTH, D_CF, D_MODEL), D_CF),
        'w_o': nrm(ks[14], (DEPTH, D_MODEL, D_MODEL), D_MODEL),
        'w_up': nrm(ks[15], (DEPTH, D_MODEL, D_FF), D_MODEL),
        'w_down': nrm(ks[16], (DEPTH, D_FF, D_MODEL), D_FF),
    }


def _fwd_reference(x, ln_mix_pre, ln_mix_post, ln_mlp_pre, ln_mlp_post, w_in, conv_a_w, proj_a,
              proj_b, conv_c_w, conv_c_b, norm_c_g, norm_c_b, proj_c, w_o, w_up, w_down):
    Bsz, S, _ = x.shape
    sizes = [D_ATT, D_ATT, D_ATT, D_SC, D_SC, D_SC, 2 * D_CF, N_BRANCH * D_MODEL]
    splits = [int(c) for c in np.cumsum(sizes)[:-1]]
    for l in range(DEPTH):
        h = rms_norm(x, ln_mix_pre[l])
        proj = h @ w_in[l]
        q, k, v, sc_b, sc_c, sc_u, cf_in, gate_logits = jnp.split(proj, splits, axis=-1)
        ya = (sc_b * causal_dwconv(sc_c * sc_u, conv_a_w[l])) @ proj_a[l]
        q = q.reshape(Bsz, S, N_HEADS_ATT, HEAD_DIM)
        k = k.reshape(Bsz, S, N_HEADS_ATT, HEAD_DIM)
        v = v.reshape(Bsz, S, N_HEADS_ATT, HEAD_DIM)
        yb = stick_breaking_attention(q, k, v).reshape(Bsz, S, D_ATT) @ proj_b[l]
        cf_a, cf_g = jnp.split(cf_in, 2, axis=-1)
        u = cf_a * jax.nn.sigmoid(cf_g)
        u = causal_dwconv(u, conv_c_w[l]) + conv_c_b[l]
        u = jax.nn.silu(layer_norm(u, norm_c_g[l], norm_c_b[l]))
        yc = u @ proj_c[l]
        g_a, g_b, g_c = jnp.split(jax.nn.sigmoid(gate_logits), N_BRANCH, axis=-1)
        mixed = (g_a * ya + g_b * yb + g_c * yc) @ w_o[l]
        x = x + rms_norm(mixed, ln_mix_post[l])
        h = rms_norm(x, ln_mlp_pre[l])
        f = jnp.square(jax.nn.relu(h @ w_up[l])) @ w_down[l]
        x = x + rms_norm(f, ln_mlp_post[l])
    return x


import jax as _jax
import jax.numpy as _jnp

TWIN_FORMAT = 'train_step'
FWD_PARAMS = ['x', 'ln_mix_pre', 'ln_mix_post', 'ln_mlp_pre', 'ln_mlp_post', 'w_in', 'conv_a_w', 'proj_a', 'proj_b', 'conv_c_w', 'conv_c_b', 'norm_c_g', 'norm_c_b', 'proj_c', 'w_o', 'w_up', 'w_down']
TWIN_WEIGHTS = ['ln_mix_pre', 'ln_mix_post', 'ln_mlp_pre', 'ln_mlp_post', 'w_in', 'conv_a_w', 'proj_a', 'proj_b', 'conv_c_w', 'conv_c_b', 'norm_c_g', 'norm_c_b', 'proj_c', 'w_o', 'w_up', 'w_down']
TWIN_DIFF_INPUT = 'x'
TWIN_INPUTS = ['x', 'ln_mix_pre', 'ln_mix_post', 'ln_mlp_pre', 'ln_mlp_post', 'w_in', 'conv_a_w', 'proj_a', 'proj_b', 'conv_c_w', 'conv_c_b', 'norm_c_g', 'norm_c_b', 'proj_c', 'w_o', 'w_up', 'w_down', 'loss_target', 'm_ln_mix_pre', 'm_ln_mix_post', 'm_ln_mlp_pre', 'm_ln_mlp_post', 'm_w_in', 'm_conv_a_w', 'm_proj_a', 'm_proj_b', 'm_conv_c_w', 'm_conv_c_b', 'm_norm_c_g', 'm_norm_c_b', 'm_proj_c', 'm_w_o', 'm_w_up', 'm_w_down', 'v_ln_mix_pre', 'v_ln_mix_post', 'v_ln_mlp_pre', 'v_ln_mlp_post', 'v_w_in', 'v_conv_a_w', 'v_proj_a', 'v_proj_b', 'v_conv_c_w', 'v_conv_c_b', 'v_norm_c_g', 'v_norm_c_b', 'v_proj_c', 'v_w_o', 'v_w_up', 'v_w_down']
TWIN_OUTPUTS = ['loss', 'grad_x', 'grad_ln_mix_pre', 'grad_ln_mix_post', 'grad_ln_mlp_pre', 'grad_ln_mlp_post', 'grad_w_in', 'grad_conv_a_w', 'grad_proj_a', 'grad_proj_b', 'grad_conv_c_w', 'grad_conv_c_b', 'grad_norm_c_g', 'grad_norm_c_b', 'grad_proj_c', 'grad_w_o', 'grad_w_up', 'grad_w_down', 'delta_ln_mix_pre', 'delta_ln_mix_post', 'delta_ln_mlp_pre', 'delta_ln_mlp_post', 'delta_w_in', 'delta_conv_a_w', 'delta_proj_a', 'delta_proj_b', 'delta_conv_c_w', 'delta_conv_c_b', 'delta_norm_c_g', 'delta_norm_c_b', 'delta_proj_c', 'delta_w_o', 'delta_w_up', 'delta_w_down', 'new_m_ln_mix_pre', 'new_m_ln_mix_post', 'new_m_ln_mlp_pre', 'new_m_ln_mlp_post', 'new_m_w_in', 'new_m_conv_a_w', 'new_m_proj_a', 'new_m_proj_b', 'new_m_conv_c_w', 'new_m_conv_c_b', 'new_m_norm_c_g', 'new_m_norm_c_b', 'new_m_proj_c', 'new_m_w_o', 'new_m_w_up', 'new_m_w_down', 'new_v_ln_mix_pre', 'new_v_ln_mix_post', 'new_v_ln_mlp_pre', 'new_v_ln_mlp_post', 'new_v_w_in', 'new_v_conv_a_w', 'new_v_proj_a', 'new_v_proj_b', 'new_v_conv_c_w', 'new_v_conv_c_b', 'new_v_norm_c_g', 'new_v_norm_c_b', 'new_v_proj_c', 'new_v_w_o', 'new_v_w_up', 'new_v_w_down']
TWIN_LEAF_KINDS = {'loss': 'loss', 'grad_x': 'grad_x', 'grad_ln_mix_pre': 'grad_w', 'grad_ln_mix_post': 'grad_w', 'grad_ln_mlp_pre': 'grad_w', 'grad_ln_mlp_post': 'grad_w', 'grad_w_in': 'grad_w', 'grad_conv_a_w': 'grad_w', 'grad_proj_a': 'grad_w', 'grad_proj_b': 'grad_w', 'grad_conv_c_w': 'grad_w', 'grad_conv_c_b': 'grad_w', 'grad_norm_c_g': 'grad_w', 'grad_norm_c_b': 'grad_w', 'grad_proj_c': 'grad_w', 'grad_w_o': 'grad_w', 'grad_w_up': 'grad_w', 'grad_w_down': 'grad_w', 'delta_ln_mix_pre': 'delta_w', 'delta_ln_mix_post': 'delta_w', 'delta_ln_mlp_pre': 'delta_w', 'delta_ln_mlp_post': 'delta_w', 'delta_w_in': 'delta_w', 'delta_conv_a_w': 'delta_w', 'delta_proj_a': 'delta_w', 'delta_proj_b': 'delta_w', 'delta_conv_c_w': 'delta_w', 'delta_conv_c_b': 'delta_w', 'delta_norm_c_g': 'delta_w', 'delta_norm_c_b': 'delta_w', 'delta_proj_c': 'delta_w', 'delta_w_o': 'delta_w', 'delta_w_up': 'delta_w', 'delta_w_down': 'delta_w', 'new_m_ln_mix_pre': 'new_m', 'new_m_ln_mix_post': 'new_m', 'new_m_ln_mlp_pre': 'new_m', 'new_m_ln_mlp_post': 'new_m', 'new_m_w_in': 'new_m', 'new_m_conv_a_w': 'new_m', 'new_m_proj_a': 'new_m', 'new_m_proj_b': 'new_m', 'new_m_conv_c_w': 'new_m', 'new_m_conv_c_b': 'new_m', 'new_m_norm_c_g': 'new_m', 'new_m_norm_c_b': 'new_m', 'new_m_proj_c': 'new_m', 'new_m_w_o': 'new_m', 'new_m_w_up': 'new_m', 'new_m_w_down': 'new_m', 'new_v_ln_mix_pre': 'new_v', 'new_v_ln_mix_post': 'new_v', 'new_v_ln_mlp_pre': 'new_v', 'new_v_ln_mlp_post': 'new_v', 'new_v_w_in': 'new_v', 'new_v_conv_a_w': 'new_v', 'new_v_proj_a': 'new_v', 'new_v_proj_b': 'new_v', 'new_v_conv_c_w': 'new_v', 'new_v_conv_c_b': 'new_v', 'new_v_norm_c_g': 'new_v', 'new_v_norm_c_b': 'new_v', 'new_v_proj_c': 'new_v', 'new_v_w_o': 'new_v', 'new_v_w_up': 'new_v', 'new_v_w_down': 'new_v'}


def _forward(args):
    return _fwd_reference(*[args[k] for k in FWD_PARAMS])


def _output_shape():
    out = _jax.eval_shape(lambda: _forward(_fwd_setup_inputs(0)))
    return out.shape, out.dtype

N_MICROBATCH = 1
ADAM_LR = 0.001
ADAM_B1 = 0.9
ADAM_B2 = 0.999
ADAM_EPS = 1e-08
ADAM_WD = 0.01
ADAM_STEP = 10
PER_EXAMPLE_BATCH_AXIS = {'x': 0, 'loss_target': 0}
SHARED_INPUTS = []
_WEIGHT_DTYPES = {'ln_mix_pre': _jnp.float32, 'ln_mix_post': _jnp.float32, 'ln_mlp_pre': _jnp.float32, 'ln_mlp_post': _jnp.float32, 'w_in': _jnp.float32, 'conv_a_w': _jnp.float32, 'proj_a': _jnp.float32, 'proj_b': _jnp.float32, 'conv_c_w': _jnp.float32, 'conv_c_b': _jnp.float32, 'norm_c_g': _jnp.float32, 'norm_c_b': _jnp.float32, 'proj_c': _jnp.float32, 'w_o': _jnp.float32, 'w_up': _jnp.float32, 'w_down': _jnp.float32}
MOMENT_SCALE = {'ln_mix_pre': 1.505996e+00, 'ln_mix_post': 8.301540e+00, 'ln_mlp_pre': 1.357733e+00, 'ln_mlp_post': 9.130779e+00, 'w_in': 6.380864e-01, 'conv_a_w': 6.335852e-01, 'proj_a': 3.408502e-01, 'proj_b': 1.447092e+00, 'conv_c_w': 1.534677e+00, 'conv_c_b': 1.218236e+01, 'norm_c_g': 5.031409e+00, 'norm_c_b': 6.941200e+00, 'proj_c': 1.566743e+00, 'w_o': 2.058710e+00, 'w_up': 6.717624e-01, 'w_down': 3.777524e+00}


def _to_microbatches(a, axis):
    t = _jnp.moveaxis(a, axis, 0)
    t = t.reshape((N_MICROBATCH, t.shape[0] // N_MICROBATCH) + t.shape[1:])
    return _jnp.moveaxis(t, 1, axis + 1)


def setup_inputs(seed: int = 0) -> dict:
    inp = _fwd_setup_inputs(seed)
    key = _jax.random.fold_in(_jax.random.key(seed), 7919)
    shape, _ = _output_shape()
    out = dict(inp)
    out["loss_target"] = _jax.random.normal(_jax.random.fold_in(key, 0), shape, _jnp.float32)
    for i, name in enumerate(TWIN_WEIGHTS):
        w = inp[name].astype(_jnp.float32)
        if MOMENT_SCALE is None:
            s = _jnp.sqrt(_jnp.mean(_jnp.square(w)) + 1e-30)
        else:
            s = MOMENT_SCALE[name]
        km, kv = _jax.random.split(_jax.random.fold_in(key, i + 1))
        out[name] = w
        out["m_" + name] = s * _jax.random.normal(km, w.shape, _jnp.float32)
        out["v_" + name] = (s * s) * _jax.random.uniform(kv, w.shape, _jnp.float32, 0.5, 1.5)
    if N_MICROBATCH > 1:
        for name, axis in PER_EXAMPLE_BATCH_AXIS.items():
            out[name] = _to_microbatches(out[name], axis)
    return {'x': out['x'], 'ln_mix_pre': out['ln_mix_pre'], 'ln_mix_post': out['ln_mix_post'], 'ln_mlp_pre': out['ln_mlp_pre'], 'ln_mlp_post': out['ln_mlp_post'], 'w_in': out['w_in'], 'conv_a_w': out['conv_a_w'], 'proj_a': out['proj_a'], 'proj_b': out['proj_b'], 'conv_c_w': out['conv_c_w'], 'conv_c_b': out['conv_c_b'], 'norm_c_g': out['norm_c_g'], 'norm_c_b': out['norm_c_b'], 'proj_c': out['proj_c'], 'w_o': out['w_o'], 'w_up': out['w_up'], 'w_down': out['w_down'], 'loss_target': out['loss_target'], 'm_ln_mix_pre': out['m_ln_mix_pre'], 'm_ln_mix_post': out['m_ln_mix_post'], 'm_ln_mlp_pre': out['m_ln_mlp_pre'], 'm_ln_mlp_post': out['m_ln_mlp_post'], 'm_w_in': out['m_w_in'], 'm_conv_a_w': out['m_conv_a_w'], 'm_proj_a': out['m_proj_a'], 'm_proj_b': out['m_proj_b'], 'm_conv_c_w': out['m_conv_c_w'], 'm_conv_c_b': out['m_conv_c_b'], 'm_norm_c_g': out['m_norm_c_g'], 'm_norm_c_b': out['m_norm_c_b'], 'm_proj_c': out['m_proj_c'], 'm_w_o': out['m_w_o'], 'm_w_up': out['m_w_up'], 'm_w_down': out['m_w_down'], 'v_ln_mix_pre': out['v_ln_mix_pre'], 'v_ln_mix_post': out['v_ln_mix_post'], 'v_ln_mlp_pre': out['v_ln_mlp_pre'], 'v_ln_mlp_post': out['v_ln_mlp_post'], 'v_w_in': out['v_w_in'], 'v_conv_a_w': out['v_conv_a_w'], 'v_proj_a': out['v_proj_a'], 'v_proj_b': out['v_proj_b'], 'v_conv_c_w': out['v_conv_c_w'], 'v_conv_c_b': out['v_conv_c_b'], 'v_norm_c_g': out['v_norm_c_g'], 'v_norm_c_b': out['v_norm_c_b'], 'v_proj_c': out['v_proj_c'], 'v_w_o': out['v_w_o'], 'v_w_up': out['v_w_up'], 'v_w_down': out['v_w_down']}


def _loss(weights, diff, rest, loss_target):
    with _jax.named_scope("forward"):
        args = {**rest, TWIN_DIFF_INPUT: diff, **{k: w.astype(_WEIGHT_DTYPES[k]) for k, w in weights.items()}}
        y = _forward(args)
    with _jax.named_scope("loss_head"):
        err = _jnp.square(y.astype(_jnp.float32) - loss_target)
        return 0.5 * _jnp.sum(_jnp.mean(err, axis=-1)) if err.ndim else 0.5 * err


def _adamw(w, g, m, v):
    m = ADAM_B1 * m + (1.0 - ADAM_B1) * g
    v = ADAM_B2 * v + (1.0 - ADAM_B2) * _jnp.square(g)
    m_hat = m / (1.0 - ADAM_B1 ** ADAM_STEP)
    v_hat = v / (1.0 - ADAM_B2 ** ADAM_STEP)
    delta = -ADAM_LR * (m_hat / (_jnp.sqrt(v_hat) + ADAM_EPS) + ADAM_WD * w)
    return delta, m, v


def reference(x, ln_mix_pre, ln_mix_post, ln_mlp_pre, ln_mlp_post, w_in, conv_a_w, proj_a, proj_b, conv_c_w, conv_c_b, norm_c_g, norm_c_b, proj_c, w_o, w_up, w_down, loss_target, m_ln_mix_pre, m_ln_mix_post, m_ln_mlp_pre, m_ln_mlp_post, m_w_in, m_conv_a_w, m_proj_a, m_proj_b, m_conv_c_w, m_conv_c_b, m_norm_c_g, m_norm_c_b, m_proj_c, m_w_o, m_w_up, m_w_down, v_ln_mix_pre, v_ln_mix_post, v_ln_mlp_pre, v_ln_mlp_post, v_w_in, v_conv_a_w, v_proj_a, v_proj_b, v_conv_c_w, v_conv_c_b, v_norm_c_g, v_norm_c_b, v_proj_c, v_w_o, v_w_up, v_w_down):
    given = dict(x=x, ln_mix_pre=ln_mix_pre, ln_mix_post=ln_mix_post, ln_mlp_pre=ln_mlp_pre, ln_mlp_post=ln_mlp_post, w_in=w_in, conv_a_w=conv_a_w, proj_a=proj_a, proj_b=proj_b, conv_c_w=conv_c_w, conv_c_b=conv_c_b, norm_c_g=norm_c_g, norm_c_b=norm_c_b, proj_c=proj_c, w_o=w_o, w_up=w_up, w_down=w_down, loss_target=loss_target, m_ln_mix_pre=m_ln_mix_pre, m_ln_mix_post=m_ln_mix_post, m_ln_mlp_pre=m_ln_mlp_pre, m_ln_mlp_post=m_ln_mlp_post, m_w_in=m_w_in, m_conv_a_w=m_conv_a_w, m_proj_a=m_proj_a, m_proj_b=m_proj_b, m_conv_c_w=m_conv_c_w, m_conv_c_b=m_conv_c_b, m_norm_c_g=m_norm_c_g, m_norm_c_b=m_norm_c_b, m_proj_c=m_proj_c, m_w_o=m_w_o, m_w_up=m_w_up, m_w_down=m_w_down, v_ln_mix_pre=v_ln_mix_pre, v_ln_mix_post=v_ln_mix_post, v_ln_mlp_pre=v_ln_mlp_pre, v_ln_mlp_post=v_ln_mlp_post, v_w_in=v_w_in, v_conv_a_w=v_conv_a_w, v_proj_a=v_proj_a, v_proj_b=v_proj_b, v_conv_c_w=v_conv_c_w, v_conv_c_b=v_conv_c_b, v_norm_c_g=v_norm_c_g, v_norm_c_b=v_norm_c_b, v_proj_c=v_proj_c, v_w_o=v_w_o, v_w_up=v_w_up, v_w_down=v_w_down)
    weights = {n: given[n] for n in TWIN_WEIGHTS}
    shared = {n: given[n] for n in SHARED_INPUTS}
    per_example = {n: given[n] for n in ['x']}
    grad_fn = _jax.value_and_grad(_loss, argnums=(0, 1))

    def one_microbatch(ex, loss_target):
        ex = dict(ex)
        diff = ex.pop(TWIN_DIFF_INPUT)
        return grad_fn(weights, diff, {**shared, **ex}, loss_target)

    if N_MICROBATCH == 1:
        loss, (grad_w, grad_x) = one_microbatch(per_example, given["loss_target"])
    else:
        def body(carry, xs):
            loss_sum, grad_sum = carry
            l_k, (gw_k, gx_k) = one_microbatch(xs[0], xs[1])
            with _jax.named_scope("update"):
                return (loss_sum + l_k, _jax.tree.map(_jnp.add, grad_sum, gw_k)), gx_k

        init = (_jnp.zeros((), _jnp.float32), _jax.tree.map(_jnp.zeros_like, weights))
        (loss, grad_w), grad_x = _jax.lax.scan(body, init, (per_example, given["loss_target"]))
    with _jax.named_scope("update"):
        delta_w, new_m, new_v = {}, {}, {}
        for n in TWIN_WEIGHTS:
            delta_w[n], new_m[n], new_v[n] = _adamw(weights[n], grad_w[n], given["m_" + n], given["v_" + n])
    return (loss, grad_x, *[grad_w[n] for n in TWIN_WEIGHTS], *[delta_w[n] for n in TWIN_WEIGHTS],
            *[new_m[n] for n in TWIN_WEIGHTS], *[new_v[n] for n in TWIN_WEIGHTS])
```

```python
import functools

import jax
import jax.numpy as jnp
from jax import lax
from jax.experimental import pallas as pl
from jax.experimental.pallas import tpu as pltpu

F32 = jnp.float32
MXU_DTYPE = jnp.bfloat16
WIRE_DTYPE = jnp.bfloat16
MESH = pl.DeviceIdType.MESH
ANY = pl.BlockSpec(memory_space=pl.ANY)

N_DEV = 8
HEAD_DIM = 128
RMS_EPS = 1e-6
LN_EPS = 1e-5
SC_WIDTH = 3
CF_WIDTH = 31
CONV_PAD = 32
ADAM_LR, ADAM_B1, ADAM_B2, ADAM_EPS, ADAM_WD, ADAM_STEP = 0.001, 0.9, 0.999, 1e-08, 0.01, 10
VMEM_LIMIT = 56 * 1024 * 1024
LANE = 128


def _cp(sem=None, **kw):
    return pltpu.CompilerParams(dimension_semantics=sem, vmem_limit_bytes=VMEM_LIMIT, **kw)


def _sigmoid(x):
    return 1.0 / (1.0 + jnp.exp(-x))


def _row_tile(rows, want):
    t = min(rows, want)
    while rows % t:
        t //= 2
    return t


_DN = {"NN": (((1,), (0,)), ((), ())), "NT": (((1,), (1,)), ((), ())), "TN": (((0,), (0,)), ((), ()))}


def _mm(a, b, mode, out_dtypes, name, *, a_view=None, b_view=None, tm=2048, tn=512, tk=2048, epilogue=None, extras=()):
    a_view = a_view or (0, 0) + tuple(a.shape)
    b_view = b_view or (0, 0) + tuple(b.shape)
    ar, ac, an, am = a_view
    br, bc, bn, bm = b_view
    if mode == "NN":
        M, K, K2, N = an, am, bn, bm
    elif mode == "NT":
        M, K, N, K2 = an, am, bn, bm
    else:
        K, M, K2, N = an, am, bn, bm
    assert K == K2, (name, a_view, b_view)
    tm, tn, tk = _row_tile(M, tm), _row_tile(N, tn), _row_tile(K, tk)
    (a_m_off, a_k_off) = (ac, ar) if mode == "TN" else (ar, ac)
    (b_n_off, b_k_off) = (br, bc) if mode == "NT" else (bc, br)
    while a_m_off % tm:
        tm //= 2
    while b_n_off % tn:
        tn //= 2
    while a_k_off % tk or b_k_off % tk:
        tk //= 2
    nk = K // tk
    a_blk = (tk, tm) if mode == "TN" else (tm, tk)
    b_blk = (tn, tk) if mode == "NT" else (tk, tn)
    assert ar % a_blk[0] == 0 and ac % a_blk[1] == 0, (name, a_view, a_blk)
    assert br % b_blk[0] == 0 and bc % b_blk[1] == 0, (name, b_view, b_blk)
    ao, bo = (ar // a_blk[0], ac // a_blk[1]), (br // b_blk[0], bc // b_blk[1])
    if mode == "TN":
        a_spec = pl.BlockSpec(a_blk, lambda i, j, k: (ao[0] + k, ao[1] + i))
    else:
        a_spec = pl.BlockSpec(a_blk, lambda i, j, k: (ao[0] + i, ao[1] + k))
    if mode == "NT":
        b_spec = pl.BlockSpec(b_blk, lambda i, j, k: (bo[0] + j, bo[1] + k))
    else:
        b_spec = pl.BlockSpec(b_blk, lambda i, j, k: (bo[0] + k, bo[1] + j))
    o_spec = pl.BlockSpec((tm, tn), lambda i, j, k: (i, j))
    n_ex, n_out = len(extras), len(out_dtypes)
    dn = _DN[mode]

    def body(*refs):
        a_ref, b_ref = refs[:2]
        ex_refs = refs[2:2 + n_ex]
        o_refs = refs[2 + n_ex:2 + n_ex + n_out]
        p = lax.dot_general(a_ref[...], b_ref[...], dn, preferred_element_type=F32)

        def finish(acc):
            outs = epilogue(acc, *[r[...] for r in ex_refs]) if epilogue else (acc,)
            for o_ref, o in zip(o_refs, outs):
                o_ref[...] = o.astype(o_ref.dtype)

        if nk == 1:
            finish(p)
        else:
            acc_ref = refs[-1]
            k = pl.program_id(2)

            @pl.when(k == 0)
            def _():
                acc_ref[...] = p

            @pl.when(k > 0)
            def _():
                acc_ref[...] += p

            @pl.when(k == nk - 1)
            def _():
                finish(acc_ref[...])

    outs = pl.pallas_call(
        body, name=name, grid=(M // tm, N // tn, nk),
        in_specs=[a_spec, b_spec] + [o_spec] * n_ex, out_specs=[o_spec] * n_out,
        out_shape=[jax.ShapeDtypeStruct((M, N), d) for d in out_dtypes],
        scratch_shapes=[pltpu.VMEM((tm, tn), F32)] if nk > 1 else [],
        compiler_params=_cp(("parallel", "parallel", "arbitrary")),
    )(a, b, *extras)
    return outs[0] if n_out == 1 else outs


def _rms_fwd(x, g, name):
    S, D = x.shape
    tr = _row_tile(S, 256)

    def body(x_ref, g_ref, h_ref):
        xv = x_ref[...]
        r = lax.rsqrt(jnp.mean(xv * xv, axis=-1, keepdims=True) + RMS_EPS)
        h_ref[...] = ((xv * r) * g_ref[...]).astype(h_ref.dtype)

    return pl.pallas_call(
        body, name=name, grid=(S // tr,),
        in_specs=[pl.BlockSpec((tr, D), lambda i: (i, 0)), pl.BlockSpec((1, D), lambda i: (0, 0))],
        out_specs=pl.BlockSpec((tr, D), lambda i: (i, 0)),
        out_shape=jax.ShapeDtypeStruct((S, D), MXU_DTYPE), compiler_params=_cp(("parallel",)),
    )(x, g)


def _resid_post(xres, y, g_post, g_next, name):
    S, D = y.shape
    tr = _row_tile(S, 256)
    has_next = g_next is not None

    def body(*refs):
        xr_ref, y_ref, gp_ref = refs[:3]
        yv = y_ref[...]
        r = lax.rsqrt(jnp.mean(yv * yv, axis=-1, keepdims=True) + RMS_EPS)
        xn = xr_ref[...] + (yv * r) * gp_ref[...]
        if has_next:
            gn_ref, xo_ref, h_ref = refs[3:]
            r2 = lax.rsqrt(jnp.mean(xn * xn, axis=-1, keepdims=True) + RMS_EPS)
            h_ref[...] = ((xn * r2) * gn_ref[...]).astype(h_ref.dtype)
        else:
            xo_ref = refs[3]
        xo_ref[...] = xn

    row = pl.BlockSpec((tr, D), lambda i: (i, 0))
    vec = pl.BlockSpec((1, D), lambda i: (0, 0))
    outs = pl.pallas_call(
        body, name=name, grid=(S // tr,),
        in_specs=[row, row, vec] + ([vec] if has_next else []),
        out_specs=[row] + ([row] if has_next else []),
        out_shape=[jax.ShapeDtypeStruct((S, D), F32)] + ([jax.ShapeDtypeStruct((S, D), MXU_DTYPE)] if has_next else []),
        compiler_params=_cp(("parallel",)),
    )(xres, y, g_post, *([g_next] if has_next else []))
    return (outs[0], outs[1]) if has_next else (outs[0], None)


def _rms_bwd(xin, g, dy, dres, out_dtype, name):
    S, D = xin.shape
    tr = _row_tile(S, 256)
    has_res = dres is not None

    def body(*refs):
        x_ref, g_ref, dy_ref = refs[:3]
        dx_ref, dg_ref = refs[-2:]
        xv, dyv = x_ref[...], dy_ref[...].astype(F32)
        r = lax.rsqrt(jnp.mean(xv * xv, axis=-1, keepdims=True) + RMS_EPS)
        n = xv * r
        dyg = dyv * g_ref[...]
        dx = r * (dyg - n * jnp.mean(dyg * n, axis=-1, keepdims=True))
        if has_res:
            dx = dx + refs[3][...]
        dx_ref[...] = dx.astype(dx_ref.dtype)

        @pl.when(pl.program_id(0) == 0)
        def _():
            dg_ref[...] = jnp.zeros_like(dg_ref)

        dg_ref[...] += jnp.sum(dyv * n, axis=0, keepdims=True)

    row = pl.BlockSpec((tr, D), lambda i: (i, 0))
    vec = pl.BlockSpec((1, D), lambda i: (0, 0))
    return pl.pallas_call(
        body, name=name, grid=(S // tr,),
        in_specs=[row, vec, row] + ([row] if has_res else []), out_specs=[row, vec],
        out_shape=[jax.ShapeDtypeStruct((S, D), out_dtype), jax.ShapeDtypeStruct((1, D), F32)],
        compiler_params=_cp(("arbitrary",)),
    )(xin, g, dy, *([dres] if has_res else []))


def _loss_head(y, target):
    S, D = y.shape
    tr = _row_tile(S, 256)

    def body(y_ref, t_ref, dy_ref, l_ref):
        e = y_ref[...] - t_ref[...]
        dy_ref[...] = e * (1.0 / D)

        @pl.when(pl.program_id(0) == 0)
        def _():
            l_ref[...] = jnp.zeros_like(l_ref)

        l_ref[...] += 0.5 * jnp.sum(jnp.mean(e * e, axis=-1, keepdims=True), axis=0, keepdims=True)

    row = pl.BlockSpec((tr, D), lambda i: (i, 0))
    return pl.pallas_call(
        body, name="loss_head", grid=(S // tr,), in_specs=[row, row],
        out_specs=[row, pl.BlockSpec((1, LANE), lambda i: (0, 0))],
        out_shape=[jax.ShapeDtypeStruct((S, D), F32), jax.ShapeDtypeStruct((1, LANE), F32)],
        compiler_params=_cp(("arbitrary",)),
    )(y, target)


def _gate_specs(S, U, tr):
    gl = [pl.BlockSpec((tr, U), functools.partial(lambda i, j, o: (i, o + j), o=o)) for o in (11, 15, 19)]
    return gl, pl.BlockSpec((tr, U), lambda i, j: (i, j))


def _gate_fwd(proj, ya, yb, yc):
    S, D = ya.shape
    U = D // 4
    tr = _row_tile(S, 256)

    def body(ga_ref, gb_ref, gc_ref, ya_ref, yb_ref, yc_ref, o_ref):
        m = _sigmoid(ga_ref[...]) * ya_ref[...] + _sigmoid(gb_ref[...]) * yb_ref[...] + _sigmoid(gc_ref[...]) * yc_ref[...]
        o_ref[...] = m.astype(o_ref.dtype)

    gl, blk = _gate_specs(S, U, tr)
    return pl.pallas_call(
        body, name="gate_fwd", grid=(S // tr, 4), in_specs=gl + [blk] * 3, out_specs=blk,
        out_shape=jax.ShapeDtypeStruct((S, D), MXU_DTYPE), compiler_params=_cp(("parallel", "parallel")),
    )(proj, proj, proj, ya, yb, yc)


def _gate_bwd(dm, proj, ya, yb, yc):
    S, D = ya.shape
    U = D // 4
    tr = _row_tile(S, 256)

    def body(dm_ref, ga_ref, gb_ref, gc_ref, ya_ref, yb_ref, yc_ref, da_ref, db_ref, dc_ref, la_ref, lb_ref, lc_ref):
        d = dm_ref[...]
        for g_ref, y_ref, dy_ref, dl_ref in ((ga_ref, ya_ref, da_ref, la_ref), (gb_ref, yb_ref, db_ref, lb_ref),
                                             (gc_ref, yc_ref, dc_ref, lc_ref)):
            g = _sigmoid(g_ref[...])
            dy_ref[...] = (d * g).astype(dy_ref.dtype)
            dl_ref[...] = (d * y_ref[...] * g * (1.0 - g)).astype(dl_ref.dtype)

    gl, blk = _gate_specs(S, U, tr)
    return pl.pallas_call(
        body, name="gate_bwd", grid=(S // tr, 4), in_specs=[blk] + gl + [blk] * 3, out_specs=[blk] * 6,
        out_shape=[jax.ShapeDtypeStruct((S, D), MXU_DTYPE)] * 6, compiler_params=_cp(("parallel", "parallel")),
    )(dm, proj, proj, proj, ya, yb, yc)


def _chunks(S):
    r = _row_tile(S, 256)
    return [(r0, r) for r0 in range(0, S, r)]


def _conv_causal(front_ref, w_ref, K, r0, R):
    acc = None
    for j in range(K):
        term = w_ref[pl.ds(K - 1 - j, 1), :] * front_ref[pl.ds(CONV_PAD + r0 - j, R), :]
        acc = term if acc is None else acc + term
    return acc


def _conv_anticausal(back_ref, w_ref, K, r0, R):
    acc = None
    for j in range(K):
        term = w_ref[pl.ds(K - 1 - j, 1), :] * back_ref[pl.ds(r0 + j, R), :]
        acc = term if acc is None else acc + term
    return acc


def _conv_wgrad(front_ref, back_ref, dw_ref, K, S):
    for j in range(K):
        tot = None
        for r0, R in _chunks(S):
            part = jnp.sum(back_ref[pl.ds(r0, R), :] * front_ref[pl.ds(CONV_PAD + r0 - j, R), :], axis=0, keepdims=True)
            tot = part if tot is None else tot + part
        dw_ref[pl.ds(K - 1 - j, 1), :] = tot


def _col(S, cw, unit_off):
    return pl.BlockSpec((S, cw), functools.partial(lambda cb, o: (0, o + cb), o=unit_off))


def _branch_a_fwd(proj, wa, U):
    S = proj.shape[0]
    cw = min(LANE, U)
    nb = U // cw
    K = SC_WIDTH

    def body(b_ref, c_ref, u_ref, w_ref, o_ref, front):
        front[pl.ds(0, CONV_PAD), :] = jnp.zeros((CONV_PAD, cw), F32)
        front[pl.ds(CONV_PAD, S), :] = c_ref[...] * u_ref[...]
        for r0, R in _chunks(S):
            o_ref[pl.ds(r0, R), :] = (b_ref[pl.ds(r0, R), :] * _conv_causal(front, w_ref, K, r0, R)).astype(o_ref.dtype)

    return pl.pallas_call(
        body, name="branch_a_fwd", grid=(nb,),
        in_specs=[_col(S, cw, 6 * nb), _col(S, cw, 7 * nb), _col(S, cw, 8 * nb), pl.BlockSpec((K, cw), lambda cb: (0, cb))],
        out_specs=pl.BlockSpec((S, cw), lambda cb: (0, cb)), out_shape=jax.ShapeDtypeStruct((S, U), MXU_DTYPE),
        scratch_shapes=[pltpu.VMEM((S + CONV_PAD, cw), F32)], compiler_params=_cp(("parallel",)),
    )(proj, proj, proj, wa)


def _branch_a_bwd(d_out, proj, wa, U):
    S = proj.shape[0]
    cw = min(LANE, U)
    nb = U // cw
    K = SC_WIDTH

    def body(d_ref, b_ref, c_ref, u_ref, w_ref, db_ref, dc_ref, du_ref, dw_ref, front, back):
        front[pl.ds(0, CONV_PAD), :] = jnp.zeros((CONV_PAD, cw), F32)
        front[pl.ds(CONV_PAD, S), :] = c_ref[...] * u_ref[...]
        back[pl.ds(S, CONV_PAD), :] = jnp.zeros((CONV_PAD, cw), F32)
        back[pl.ds(0, S), :] = d_ref[...] * b_ref[...]
        for r0, R in _chunks(S):
            rows = pl.ds(r0, R)
            db_ref[rows, :] = (d_ref[rows, :] * _conv_causal(front, w_ref, K, r0, R)).astype(db_ref.dtype)
            d_ai = _conv_anticausal(back, w_ref, K, r0, R)
            dc_ref[rows, :] = (d_ai * u_ref[rows, :]).astype(dc_ref.dtype)
            du_ref[rows, :] = (d_ai * c_ref[rows, :]).astype(du_ref.dtype)
        _conv_wgrad(front, back, dw_ref, K, S)

    blk = pl.BlockSpec((S, cw), lambda cb: (0, cb))
    wblk = pl.BlockSpec((K, cw), lambda cb: (0, cb))
    return pl.pallas_call(
        body, name="branch_a_bwd", grid=(nb,),
        in_specs=[blk, _col(S, cw, 6 * nb), _col(S, cw, 7 * nb), _col(S, cw, 8 * nb), wblk],
        out_specs=[blk, blk, blk, wblk],
        out_shape=[jax.ShapeDtypeStruct((S, U), MXU_DTYPE)] * 3 + [jax.ShapeDtypeStruct((K, U), F32)],
        scratch_shapes=[pltpu.VMEM((S + CONV_PAD, cw), F32)] * 2, compiler_params=_cp(("parallel",)),
    )(d_out, proj, proj, proj, wa)


def _branch_c_conv_fwd(proj, wc, cb, U):
    S = proj.shape[0]
    cw = min(LANE, U)
    nb = U // cw
    K = CF_WIDTH

    def body(a_ref, g_ref, w_ref, bias_ref, o_ref, front):
        front[pl.ds(0, CONV_PAD), :] = jnp.zeros((CONV_PAD, cw), F32)
        front[pl.ds(CONV_PAD, S), :] = a_ref[...] * _sigmoid(g_ref[...])
        for r0, R in _chunks(S):
            o_ref[pl.ds(r0, R), :] = _conv_causal(front, w_ref, K, r0, R) + bias_ref[...]

    return pl.pallas_call(
        body, name="branch_c_conv_fwd", grid=(nb,),
        in_specs=[_col(S, cw, 9 * nb), _col(S, cw, 10 * nb), pl.BlockSpec((K, cw), lambda c: (0, c)),
                  pl.BlockSpec((1, cw), lambda c: (0, c))],
        out_specs=pl.BlockSpec((S, cw), lambda c: (0, c)), out_shape=jax.ShapeDtypeStruct((S, U), F32),
        scratch_shapes=[pltpu.VMEM((S + CONV_PAD, cw), F32)], compiler_params=_cp(("parallel",)),
    )(proj, proj, wc, cb)


def _branch_c_conv_bwd(d_u1, proj, wc, U):
    S = proj.shape[0]
    cw = min(LANE, U)
    nb = U // cw
    K = CF_WIDTH

    def body(d_ref, a_ref, g_ref, w_ref, da_ref, dg_ref, dw_ref, dbias_ref, front, back):
        sg = _sigmoid(g_ref[...])
        front[pl.ds(0, CONV_PAD), :] = jnp.zeros((CONV_PAD, cw), F32)
        front[pl.ds(CONV_PAD, S), :] = a_ref[...] * sg
        back[pl.ds(S, CONV_PAD), :] = jnp.zeros((CONV_PAD, cw), F32)
        back[pl.ds(0, S), :] = d_ref[...]
        dbias_ref[...] = jnp.sum(d_ref[...], axis=0, keepdims=True)
        for r0, R in _chunks(S):
            rows = pl.ds(r0, R)
            d_u0 = _conv_anticausal(back, w_ref, K, r0, R)
            s = _sigmoid(g_ref[rows, :])
            da_ref[rows, :] = (d_u0 * s).astype(da_ref.dtype)
            dg_ref[rows, :] = (d_u0 * a_ref[rows, :] * s * (1.0 - s)).astype(dg_ref.dtype)
        _conv_wgrad(front, back, dw_ref, K, S)

    blk = pl.BlockSpec((S, cw), lambda c: (0, c))
    wblk = pl.BlockSpec((K, cw), lambda c: (0, c))
    vblk = pl.BlockSpec((1, cw), lambda c: (0, c))
    return pl.pallas_call(
        body, name="branch_c_conv_bwd", grid=(nb,),
        in_specs=[blk, _col(S, cw, 9 * nb), _col(S, cw, 10 * nb), wblk], out_specs=[blk, blk, wblk, vblk],
        out_shape=[jax.ShapeDtypeStruct((S, U), MXU_DTYPE)] * 2 + [jax.ShapeDtypeStruct((K, U), F32), jax.ShapeDtypeStruct((1, U), F32)],
        scratch_shapes=[pltpu.VMEM((S + CONV_PAD, cw), F32)] * 2, compiler_params=_cp(("parallel",)),
    )(d_u1, proj, proj, wc)


def _branch_c_norm_fwd(u1, ng, nbias):
    S, U = u1.shape
    tr = _row_tile(S, 256)

    def body(u_ref, g_ref, b_ref, o_ref):
        u = u_ref[...]
        mu = jnp.mean(u, axis=-1, keepdims=True)
        var = jnp.mean(jnp.square(u - mu), axis=-1, keepdims=True)
        u2 = ((u - mu) * lax.rsqrt(var + LN_EPS)) * g_ref[...] + b_ref[...]
        o_ref[...] = (u2 * _sigmoid(u2)).astype(o_ref.dtype)

    row = pl.BlockSpec((tr, U), lambda i: (i, 0))
    vec = pl.BlockSpec((1, U), lambda i: (0, 0))
    return pl.pallas_call(
        body, name="branch_c_norm_fwd", grid=(S // tr,), in_specs=[row, vec, vec], out_specs=row,
        out_shape=jax.ShapeDtypeStruct((S, U), MXU_DTYPE), compiler_params=_cp(("parallel",)),
    )(u1, ng, nbias)


def _branch_c_norm_bwd(d_u3, u1, ng, nbias):
    S, U = u1.shape
    tr = _row_tile(S, 256)

    def body(d_ref, u_ref, g_ref, b_ref, du_ref, dg_ref, db_ref):
        u = u_ref[...]
        mu = jnp.mean(u, axis=-1, keepdims=True)
        var = jnp.mean(jnp.square(u - mu), axis=-1, keepdims=True)
        rstd = lax.rsqrt(var + LN_EPS)
        xh = (u - mu) * rstd
        u2 = xh * g_ref[...] + b_ref[...]
        s = _sigmoid(u2)
        d_u2 = d_ref[...] * (s * (1.0 + u2 * (1.0 - s)))
        d_xh = d_u2 * g_ref[...]
        du_ref[...] = rstd * (d_xh - jnp.mean(d_xh, axis=-1, keepdims=True) - xh * jnp.mean(d_xh * xh, axis=-1, keepdims=True))

        @pl.when(pl.program_id(0) == 0)
        def _():
            dg_ref[...] = jnp.zeros_like(dg_ref)
            db_ref[...] = jnp.zeros_like(db_ref)

        dg_ref[...] += jnp.sum(d_u2 * xh, axis=0, keepdims=True)
        db_ref[...] += jnp.sum(d_u2, axis=0, keepdims=True)

    row = pl.BlockSpec((tr, U), lambda i: (i, 0))
    vec = pl.BlockSpec((1, U), lambda i: (0, 0))
    return pl.pallas_call(
        body, name="branch_c_norm_bwd", grid=(S // tr,), in_specs=[row, row, vec, vec], out_specs=[row, vec, vec],
        out_shape=[jax.ShapeDtypeStruct((S, U), F32), jax.ShapeDtypeStruct((1, U), F32), jax.ShapeDtypeStruct((1, U), F32)],
        compiler_params=_cp(("arbitrary",)),
    )(d_u3, u1, ng, nbias)


def _tri(T, inclusive):
    j = lax.broadcasted_iota(jnp.int32, (T, T), 0)
    s = lax.broadcasted_iota(jnp.int32, (T, T), 1)
    return ((j >= s) if inclusive else (j > s)).astype(MXU_DTYPE)


def _split_dot(x, tri):
    if MXU_DTYPE == F32:
        return jnp.dot(x, tri, preferred_element_type=F32)
    hi = x.astype(MXU_DTYPE)
    lo = (x - hi.astype(F32)).astype(MXU_DTYPE)
    return jnp.dot(hi, tri, preferred_element_type=F32) + jnp.dot(lo, tri, preferred_element_type=F32)


def _sb_block(qb, kb, i, j, T, tri_strict, c_lf):
    z = lax.dot_general(qb, kb, _DN["NT"], preferred_element_type=F32) * (HEAD_DIM ** -0.5)
    t_idx = i * T + lax.broadcasted_iota(jnp.int32, (T, T), 0)
    s_idx = j * T + lax.broadcasted_iota(jnp.int32, (T, T), 1)
    mask = s_idx < t_idx
    e = jnp.exp(-jnp.abs(z))
    lg = jnp.log(1.0 + e)
    log_beta = jnp.minimum(z, 0.0) - lg
    lf = jnp.where(mask, jnp.minimum(-z, 0.0) - lg, 0.0)
    log_a = log_beta + _split_dot(lf, tri_strict) + c_lf
    a = jnp.where(mask, jnp.exp(log_a), 0.0)
    return z, e, mask, lf, a


def _attn_specs(S, U, h_blocks):
    nh = (2 * U) // HEAD_DIM
    return [pl.BlockSpec((S, HEAD_DIM), functools.partial(lambda h, o: (0, o + h), o=o * nh)) for o in range(h_blocks)]


def _attn_fwd(proj, U):
    S = proj.shape[0]
    nh = (2 * U) // HEAD_DIM
    T = _row_tile(S, 256)
    nq = S // T

    def body(q_ref, k_ref, v_ref, o_ref, of_ref, qs, ks, vs):
        qs[...] = q_ref[...].astype(MXU_DTYPE)
        ks[...] = k_ref[...].astype(MXU_DTYPE)
        vs[...] = v_ref[...].astype(MXU_DTYPE)
        tri = _tri(T, False)

        def q_loop(i, _):
            rows = pl.ds(pl.multiple_of(i * T, T), T)
            qb = qs[rows, :]

            def k_loop(jj, carry):
                c_lf, acc = carry
                j = i - jj
                cols = pl.ds(pl.multiple_of(j * T, T), T)
                _, _, _, lf, a = _sb_block(qb, ks[cols, :], i, j, T, tri, c_lf)
                acc = acc + jnp.dot(a.astype(MXU_DTYPE), vs[cols, :], preferred_element_type=F32)
                return c_lf + jnp.sum(lf, axis=1, keepdims=True), acc

            _, acc = lax.fori_loop(0, i + 1, k_loop, (jnp.zeros((T, 1), F32), jnp.zeros((T, HEAD_DIM), F32)))
            o_ref[rows, :] = acc.astype(o_ref.dtype)
            of_ref[rows, :] = acc
            return 0

        lax.fori_loop(0, nq, q_loop, 0)

    hblk = pl.BlockSpec((S, HEAD_DIM), lambda h: (0, h))
    return pl.pallas_call(
        body, name="attn_fwd", grid=(nh,), in_specs=_attn_specs(S, U, 3), out_specs=[hblk, hblk],
        out_shape=[jax.ShapeDtypeStruct((S, 2 * U), MXU_DTYPE), jax.ShapeDtypeStruct((S, 2 * U), F32)],
        scratch_shapes=[pltpu.VMEM((S, HEAD_DIM), MXU_DTYPE)] * 3, compiler_params=_cp(("parallel",)),
    )(proj, proj, proj)


def _attn_bwd(proj, att_f32, d_att, U):
    S = proj.shape[0]
    nh = (2 * U) // HEAD_DIM
    T = _row_tile(S, 256)
    nq = S // T
    scale = HEAD_DIM ** -0.5

    def body(q_ref, k_ref, v_ref, o_ref, do_ref, dq_ref, dk_ref, dv_ref, qs, ks, vs, dos, dka, dva):
        qs[...] = q_ref[...].astype(MXU_DTYPE)
        ks[...] = k_ref[...].astype(MXU_DTYPE)
        vs[...] = v_ref[...].astype(MXU_DTYPE)
        dos[...] = do_ref[...].astype(MXU_DTYPE)
        dka[...] = jnp.zeros_like(dka)
        dva[...] = jnp.zeros_like(dva)
        tri = _tri(T, False)
        tri_inc = _tri(T, True)

        def q_loop(i, _):
            rows = pl.ds(pl.multiple_of(i * T, T), T)
            qb = qs[rows, :]
            dob = dos[rows, :]
            delta = jnp.sum(dob.astype(F32) * o_ref[rows, :], axis=1, keepdims=True)

            def k_loop(jj, carry):
                c_lf, c_g, dq = carry
                j = i - jj
                cols = pl.ds(pl.multiple_of(j * T, T), T)
                kb, vb = ks[cols, :], vs[cols, :]
                z, e, mask, lf, a = _sb_block(qb, kb, i, j, T, tri, c_lf)
                a_mx = a.astype(MXU_DTYPE)
                d_a = lax.dot_general(dob, vb, _DN["NT"], preferred_element_type=F32)
                g = a_mx.astype(F32) * d_a
                prefix = delta - (_split_dot(g, tri_inc) + c_g)
                inv = 1.0 / (1.0 + e)
                beta = jnp.where(z >= 0.0, 1.0, e) * inv
                one_m_beta = jnp.where(z >= 0.0, e, 1.0) * inv
                dz = jnp.where(mask, (g * one_m_beta - prefix * beta) * scale, 0.0).astype(MXU_DTYPE)
                dq = dq + jnp.dot(dz, kb, preferred_element_type=F32)
                dka[cols, :] += lax.dot_general(dz, qb, _DN["TN"], preferred_element_type=F32)
                dva[cols, :] += lax.dot_general(a_mx, dob, _DN["TN"], preferred_element_type=F32)
                return c_lf + jnp.sum(lf, axis=1, keepdims=True), c_g + jnp.sum(g, axis=1, keepdims=True), dq

            zero = jnp.zeros((T, 1), F32)
            _, _, dq = lax.fori_loop(0, i + 1, k_loop, (zero, zero, jnp.zeros((T, HEAD_DIM), F32)))
            dq_ref[rows, :] = dq.astype(dq_ref.dtype)
            return 0

        lax.fori_loop(0, nq, q_loop, 0)
        dk_ref[...] = dka[...].astype(dk_ref.dtype)
        dv_ref[...] = dva[...].astype(dv_ref.dtype)

    hblk = pl.BlockSpec((S, HEAD_DIM), lambda h: (0, h))
    return pl.pallas_call(
        body, name="attn_bwd", grid=(nh,), in_specs=_attn_specs(S, U, 3) + [hblk, hblk], out_specs=[hblk] * 3,
        out_shape=[jax.ShapeDtypeStruct((S, 2 * U), MXU_DTYPE)] * 3,
        scratch_shapes=[pltpu.VMEM((S, HEAD_DIM), MXU_DTYPE)] * 4 + [pltpu.VMEM((S, HEAD_DIM), F32)] * 2,
        compiler_params=_cp(("parallel",)),
    )(proj, proj, proj, att_f32, d_att)


def _place():
    return lax.axis_index("x"), lax.axis_index("y"), lax.axis_index("c")


def _flip(v, bit):
    return 1 - v if bit else v


def _related(x, y, r):
    return _flip(x, r & 1), _flip(y, r >> 1)


def _segments(seg_rows):
    out, off = [], 0
    for r in seg_rows:
        out.append((off, r))
        off += r
    return out, off


def _seg_rows_of(ref, dev, off, rows):
    return ref.at[pl.ds(pl.multiple_of(N_DEV * off + dev * rows, 16), rows), :]


def _all_gather_weights(loc, seg_rows):
    segs, R = _segments(seg_rows)
    C = loc.shape[1]
    nseg = len(segs)

    def body(loc_ref, out_ref, send_sems, recv_sems, local_sems):
        x, y, c = _place()
        sibling = (x, y, 1 - c)

        def dev(px, py, pc):
            return 4 * px + 2 * py + pc

        def copy(n, k, block_dev, to, from_loc):
            off, rows = segs[k]
            dst = _seg_rows_of(out_ref, block_dev, off, rows)
            src = loc_ref.at[pl.ds(off, rows), :] if from_loc else dst
            return pltpu.make_async_remote_copy(src_ref=src, dst_ref=dst, send_sem=send_sems.at[n, k], recv_sem=recv_sems.at[n, k],
                                                device_id=to, device_id_type=MESH)

        me = dev(x, y, c)
        mine = [pltpu.make_async_copy(loc_ref.at[pl.ds(off, rows), :], _seg_rows_of(out_ref, me, off, rows), local_sems.at[k])
                for k, (off, rows) in enumerate(segs)]
        for cp in mine:
            cp.start()
        chips = [_related(x, y, r) for r in (1, 2, 3)]
        first = [copy(0, k, me, sibling, True) for k in range(nseg)]
        for n, (px, py) in enumerate(chips):
            first += [copy(1 + n, k, me, (px, py, c), True) for k in range(nseg)]
        for cp in first:
            cp.start()
        passed = []
        for n, (px, py) in enumerate(chips):
            for k in range(nseg):
                copy(1 + n, k, dev(px, py, c), (x, y, c), False).wait_recv()
                fwd = copy(4 + n, k, dev(px, py, c), sibling, False)
                fwd.start()
                passed.append(fwd)
        for k in range(nseg):
            copy(0, k, dev(x, y, 1 - c), (x, y, c), False).wait_recv()
        for n, (px, py) in enumerate(chips):
            for k in range(nseg):
                copy(4 + n, k, dev(px, py, 1 - c), (x, y, c), False).wait_recv()
        for cp in first + passed:
            cp.wait_send()
        for cp in mine:
            cp.wait()

    return pl.pallas_call(
        body, name="all_gather_weights", in_specs=[ANY], out_specs=ANY,
        out_shape=jax.ShapeDtypeStruct((N_DEV * R, C), loc.dtype),
        scratch_shapes=[pltpu.SemaphoreType.DMA((7, nseg)), pltpu.SemaphoreType.DMA((7, nseg)), pltpu.SemaphoreType.DMA((nseg,))],
        compiler_params=_cp(),
    )(loc)


def _reduce_scatter_pair(grads, seg_rows):
    segs, R = _segments(seg_rows)
    C = grads.shape[1]
    nseg = len(segs)

    def body(g_ref, own_ref, land_ref, send_sems, recv_sems, local_sems):
        x, y, c = _place()
        copies, locals_ = [], []
        for r in range(4):
            px, py = _related(x, y, r)
            for k, (off, rows) in enumerate(segs):
                slot = pl.ds(off, rows)
                copies.append(pltpu.make_async_remote_copy(
                    src_ref=_seg_rows_of(g_ref, 4 * px + 2 * py + (1 - c), off, rows), dst_ref=land_ref.at[r, slot, :],
                    send_sem=send_sems.at[r, k], recv_sem=recv_sems.at[r, k], device_id=(x, y, 1 - c), device_id_type=MESH))
                locals_.append(pltpu.make_async_copy(_seg_rows_of(g_ref, 4 * px + 2 * py + c, off, rows), own_ref.at[r, slot, :],
                                                     local_sems.at[r, k]))
        for cp in copies + locals_:
            cp.start()
        for cp in copies + locals_:
            cp.wait()

    shape = jax.ShapeDtypeStruct((4, R, C), grads.dtype)
    return pl.pallas_call(
        body, name="reduce_scatter_pair", in_specs=[ANY], out_specs=[ANY, ANY], out_shape=[shape, shape],
        scratch_shapes=[pltpu.SemaphoreType.DMA((4, nseg))] * 3, compiler_params=_cp(),
    )(grads)


def _pair_add(own, landed):
    n, R, C = own.shape
    tr = _pack_tile(R)

    def body(a_ref, b_ref, o_ref):
        o_ref[...] = (a_ref[...].astype(F32) + b_ref[...].astype(F32)).astype(o_ref.dtype)

    blk = pl.BlockSpec((None, tr, C), lambda r, i: (r, i, 0))
    return pl.pallas_call(
        body, name="pair_add", grid=(n, R // tr), in_specs=[blk, blk], out_specs=blk,
        out_shape=jax.ShapeDtypeStruct(own.shape, own.dtype), compiler_params=_cp(("parallel", "parallel")),
    )(own, landed)


def _reduce_scatter_chips(pair_sums):
    _, R, C = pair_sums.shape

    def body(p_ref, land_ref, send_sems, recv_sems):
        x, y, c = _place()
        copies = []
        for r in (1, 2, 3):
            px, py = _related(x, y, r)
            copies.append(pltpu.make_async_remote_copy(
                src_ref=p_ref.at[r], dst_ref=land_ref.at[r - 1], send_sem=send_sems.at[r - 1], recv_sem=recv_sems.at[r - 1],
                device_id=(px, py, c), device_id_type=MESH))
        for cp in copies:
            cp.start()
        for cp in copies:
            cp.wait()

    return pl.pallas_call(
        body, name="reduce_scatter_chips", in_specs=[ANY], out_specs=ANY,
        out_shape=jax.ShapeDtypeStruct((3, R, C), pair_sums.dtype),
        scratch_shapes=[pltpu.SemaphoreType.DMA((3,))] * 2, compiler_params=_cp(),
    )(pair_sums)


def _pack_tile(R):
    for t in (576, 512, 448, 256, 128, 64, 32, 16, 8):
        if R % t == 0:
            return t
    return R


def _final_sum(pair_sums, landed):
    _, R, C = pair_sums.shape
    tr = _pack_tile(R)

    def body(p_ref, l1_ref, l2_ref, l3_ref, o_ref):
        o_ref[...] = ((p_ref[...].astype(F32) + l1_ref[...].astype(F32)) + l2_ref[...].astype(F32)) + l3_ref[...].astype(F32)

    specs = [pl.BlockSpec((None, tr, C), functools.partial(lambda i, s: (s, i, 0), s=s)) for s in (0, 0, 1, 2)]
    return pl.pallas_call(
        body, name="final_sum", grid=(R // tr,), in_specs=specs, out_specs=pl.BlockSpec((tr, C), lambda i: (i, 0)),
        out_shape=jax.ShapeDtypeStruct((R, C), F32), compiler_params=_cp(("parallel",)),
    )(pair_sums, landed, landed, landed)


def _all_gather_small(v, reduce):
    M, N = v.shape

    def body(x_ref, out_ref, *rest):
        if reduce:
            sum_ref, send_sems, recv_sems, local_sem = rest
        else:
            send_sems, recv_sems, local_sem = rest
        x, y, c = _place()
        me, sibling = (x, y, c), (x, y, 1 - c)
        chips = [_related(x, y, r) for r in (1, 2, 3)]

        def rows(px, py, pc):
            return out_ref.at[pl.ds(pl.multiple_of((4 * px + 2 * py + pc) * M, 8), M), :]

        def copy(k, block, to, src=None):
            return pltpu.make_async_remote_copy(src_ref=rows(*block) if src is None else src, dst_ref=rows(*block),
                                                send_sem=send_sems.at[k], recv_sem=recv_sems.at[k], device_id=to, device_id_type=MESH)

        mine = pltpu.make_async_copy(x_ref, rows(*me), local_sem)
        mine.start()
        first = [copy(0, me, sibling, src=x_ref)]
        first += [copy(1 + j, me, (*chip, c), src=x_ref) for j, chip in enumerate(chips)]
        for cp in first:
            cp.start()
        passed = [copy(4 + j, (*chip, c), sibling) for j, chip in enumerate(chips)]
        for j, chip in enumerate(chips):
            copy(1 + j, (*chip, c), me).wait_recv()
            passed[j].start()
        copy(0, sibling, me).wait_recv()
        for j, chip in enumerate(chips):
            copy(4 + j, (*chip, 1 - c), me).wait_recv()
        for cp in first + passed:
            cp.wait_send()
        mine.wait()
        if reduce:
            tot = out_ref[pl.ds(0, M), :]
            for p in range(1, N_DEV):
                tot = tot + out_ref[pl.ds(p * M, M), :]
            sum_ref[...] = tot

    vm = pl.BlockSpec(memory_space=pltpu.VMEM)
    gathered = jax.ShapeDtypeStruct((N_DEV * M, N), v.dtype)
    outs = pl.pallas_call(
        body, name="all_reduce_small" if reduce else "all_gather_small", in_specs=[vm],
        out_specs=[vm, vm] if reduce else vm,
        out_shape=[gathered, jax.ShapeDtypeStruct((M, N), v.dtype)] if reduce else gathered,
        scratch_shapes=[pltpu.SemaphoreType.DMA((7,)), pltpu.SemaphoreType.DMA((7,)), pltpu.SemaphoreType.DMA],
        compiler_params=_cp(),
    )(v)
    return outs[1] if reduce else outs


def _adamw(w, g, m, v):
    shape = w.shape
    cols = shape[-1]
    rows = w.size // cols
    tr = _row_tile(rows, 256) if rows % 8 == 0 else rows
    c1 = 1.0 / (1.0 - ADAM_B1 ** ADAM_STEP)
    c2 = 1.0 / (1.0 - ADAM_B2 ** ADAM_STEP)

    def body(w_ref, g_ref, m_ref, v_ref, d_ref, nm_ref, nv_ref):
        gv = g_ref[...]
        nm = ADAM_B1 * m_ref[...] + (1.0 - ADAM_B1) * gv
        nv = ADAM_B2 * v_ref[...] + (1.0 - ADAM_B2) * (gv * gv)
        d_ref[...] = -ADAM_LR * ((nm * c1) / (jnp.sqrt(nv * c2) + ADAM_EPS) + ADAM_WD * w_ref[...])
        nm_ref[...] = nm
        nv_ref[...] = nv

    blk = pl.BlockSpec((tr, cols), lambda i: (i, 0))
    outs = pl.pallas_call(
        body, name="adamw", grid=(rows // tr,), in_specs=[blk] * 4, out_specs=[blk] * 3,
        out_shape=[jax.ShapeDtypeStruct((rows, cols), F32)] * 3, compiler_params=_cp(("parallel",)),
    )(*[a.reshape(rows, cols) for a in (w, g, m, v)])
    return tuple(o.reshape(shape) for o in outs)


def _relu2(acc):
    r = jnp.maximum(acc, 0.0)
    return acc, r * r


def _relu2_bwd(acc, up):
    return (acc * (2.0 * jnp.maximum(up.astype(F32), 0.0)),)


def kernel(x, ln_mix_pre, ln_mix_post, ln_mlp_pre, ln_mlp_post, w_in, conv_a_w, proj_a, proj_b, conv_c_w, conv_c_b, norm_c_g, norm_c_b, proj_c, w_o, w_up, w_down, loss_target, m_ln_mix_pre, m_ln_mix_post, m_ln_mlp_pre, m_ln_mlp_post, m_w_in, m_conv_a_w, m_proj_a, m_proj_b, m_conv_c_w, m_conv_c_b, m_norm_c_g, m_norm_c_b, m_proj_c, m_w_o, m_w_up, m_w_down, v_ln_mix_pre, v_ln_mix_post, v_ln_mlp_pre, v_ln_mlp_post, v_w_in, v_conv_a_w, v_proj_a, v_proj_b, v_conv_c_w, v_conv_c_b, v_norm_c_g, v_norm_c_b, v_proj_c, v_w_o, v_w_up, v_w_down):
    L, D, n_in_loc = w_in.shape
    S = x.shape[1]
    U = D // 4
    N_IN = n_in_loc * N_DEV
    D_FF = w_up.shape[2] * N_DEV
    assert N_IN == 23 * U and x.shape[0] == 1
    dloc = D // N_DEV
    floc = D_FF // N_DEV
    seg_rows = (dloc, dloc, floc, floc, n_in_loc)
    (o_p, _), (o_o, _), (o_up, _), (o_dn, _), (o_in, _) = _segments(seg_rows)[0]
    R_IN, R_P, R_O, R_UP, R_DN = (N_DEV * o for o in (o_in, o_p, o_o, o_up, o_dn))
    x_i, y_i, c_i = _place()
    me = 4 * x_i + 2 * y_i + c_i

    wb = []
    for l in range(L):
        p_t = jnp.concatenate([proj_b[l].T, proj_a[l].T, proj_c[l].T], axis=1)
        loc = jnp.concatenate([p_t, w_o[l], w_up[l].T, w_down[l], w_in[l].T], axis=0).astype(WIRE_DTYPE)
        wb.append(_all_gather_weights(loc, seg_rows).astype(MXU_DTYPE))
    cu = U // N_DEV
    conv_loc = jnp.concatenate([conv_a_w, conv_c_w], axis=1).reshape(L * (SC_WIDTH + CF_WIDTH), cu)
    conv_all = _all_gather_small(conv_loc, False).reshape(N_DEV, L, SC_WIDTH + CF_WIDTH, cu)
    conv_all = conv_all.transpose(1, 2, 0, 3).reshape(L, SC_WIDTH + CF_WIDTH, U)
    wa_full, wc_full = conv_all[:, :SC_WIDTH], conv_all[:, SC_WIDTH:]

    def vec(p, l):
        return p[l][None, :]

    xs = x[0]
    saved = []
    h1 = _rms_fwd(xs, vec(ln_mix_pre, 0), "rms_fwd")
    for l in range(L):
        W = wb[l]
        proj = _mm(h1, W, "NT", (F32,), "mm_proj", b_view=(R_IN, 0, N_IN, D))
        a_out = _branch_a_fwd(proj, wa_full[l], U)
        u1 = _branch_c_conv_fwd(proj, wc_full[l], vec(conv_c_b, l), U)
        u3 = _branch_c_norm_fwd(u1, vec(norm_c_g, l), vec(norm_c_b, l))
        att, att_f32 = _attn_fwd(proj, U)
        yb = _mm(att, W, "NT", (F32,), "mm_yb", b_view=(R_P, 0, D, 2 * U))
        ya = _mm(a_out, W, "NT", (F32,), "mm_ya", b_view=(R_P, 2 * U, D, U))
        yc = _mm(u3, W, "NT", (F32,), "mm_yc", b_view=(R_P, 3 * U, D, U))
        merged = _gate_fwd(proj, ya, yb, yc)
        mixed = _mm(merged, W, "NN", (F32,), "mm_mixed", b_view=(R_O, 0, D, D))
        x1, h2 = _resid_post(xs, mixed, vec(ln_mix_post, l), vec(ln_mlp_pre, l), "resid_post_mix")
        up, act = _mm(h2, W, "NT", (MXU_DTYPE, MXU_DTYPE), "mm_up", b_view=(R_UP, 0, D_FF, D), epilogue=_relu2)
        f = _mm(act, W, "NN", (F32,), "mm_down", b_view=(R_DN, 0, D_FF, D))
        saved.append((xs, h1, proj, a_out, u1, u3, att, att_f32, ya, yb, yc, merged, mixed, x1, h2, up, act, f))
        if l + 1 < L:
            xs, h1 = _resid_post(x1, f, vec(ln_mlp_post, l), vec(ln_mix_pre, l + 1), "resid_post_mlp")
        else:
            xs, _ = _resid_post(x1, f, vec(ln_mlp_post, l), None, "resid_post_last")
    dxo, loss_row = _loss_head(xs, loss_target[0])
    loss = lax.psum(loss_row[0, 0], ("x", "y", "c"))

    small = {k: [None] * L for k in ("g1", "g2", "g3", "g4", "cb", "ng", "nb", "wa", "wc")}
    g_shard = [None] * L
    for l in reversed(range(L)):
        W = wb[l]
        xs, h1, proj, a_out, u1, u3, att, att_f32, ya, yb, yc, merged, mixed, x1, h2, up, act, f = saved[l]
        df, small["g4"][l] = _rms_bwd(f, vec(ln_mlp_post, l), dxo, None, MXU_DTYPE, "rms_bwd_post_mlp")
        d_up = _mm(df, W, "NT", (MXU_DTYPE,), "mm_d_up", b_view=(R_DN, 0, D_FF, D), epilogue=_relu2_bwd, extras=(up,))
        g_dn = _mm(act, df, "TN", (WIRE_DTYPE,), "mm_g_down", tm=512, tn=2048)
        dh2 = _mm(d_up, W, "NN", (F32,), "mm_dh2", b_view=(R_UP, 0, D_FF, D))
        g_up = _mm(d_up, h2, "TN", (WIRE_DTYPE,), "mm_g_up", tm=512, tn=2048)
        dx1, small["g3"][l] = _rms_bwd(x1, vec(ln_mlp_pre, l), dh2, dxo, F32, "rms_bwd_pre_mlp")
        dmixed, small["g2"][l] = _rms_bwd(mixed, vec(ln_mix_post, l), dx1, None, MXU_DTYPE, "rms_bwd_post_mix")
        dmerged = _mm(dmixed, W, "NT", (F32,), "mm_dmerged", b_view=(R_O, 0, D, D))
        g_o = _mm(merged, dmixed, "TN", (WIRE_DTYPE,), "mm_g_o", tm=512, tn=2048)
        dya, dyb, dyc, dgla, dglb, dglc = _gate_bwd(dmerged, proj, ya, yb, yc)
        d_att = _mm(dyb, W, "NN", (F32,), "mm_d_att", b_view=(R_P, 0, D, 2 * U))
        d_a_out = _mm(dya, W, "NN", (F32,), "mm_d_a_out", b_view=(R_P, 2 * U, D, U))
        d_u3 = _mm(dyc, W, "NN", (F32,), "mm_d_u3", b_view=(R_P, 3 * U, D, U))
        g_pb = _mm(dyb, att, "TN", (WIRE_DTYPE,), "mm_g_pb", tm=512, tn=2048)
        g_pa = _mm(dya, a_out, "TN", (WIRE_DTYPE,), "mm_g_pa", tm=512, tn=2048)
        g_pc = _mm(dyc, u3, "TN", (WIRE_DTYPE,), "mm_g_pc", tm=512, tn=2048)
        d_scb, d_scc, d_scu, small["wa"][l] = _branch_a_bwd(d_a_out, proj, wa_full[l], U)
        d_u1, small["ng"][l], small["nb"][l] = _branch_c_norm_bwd(d_u3, u1, vec(norm_c_g, l), vec(norm_c_b, l))
        d_cfa, d_cfg, small["wc"][l], small["cb"][l] = _branch_c_conv_bwd(d_u1, proj, wc_full[l], U)
        dq, dk, dv = _attn_bwd(proj, att_f32, d_att, U)
        dproj = jnp.concatenate([dq, dk, dv, d_scb, d_scc, d_scu, d_cfa, d_cfg, dgla, dglb, dglc], axis=1)
        dh1 = _mm(dproj, W, "NN", (F32,), "mm_dh1", b_view=(R_IN, 0, N_IN, D), tk=512, tn=1024)
        g_in = _mm(dproj, h1, "TN", (WIRE_DTYPE,), "mm_g_in", tm=512, tn=2048)
        dxo, small["g1"][l] = _rms_bwd(xs, vec(ln_mix_pre, l), dh1, dx1, F32, "rms_bwd_pre_mix")
        grads = jnp.concatenate([jnp.concatenate([g_pb, g_pa, g_pc], axis=1), g_o, g_up, g_dn, g_in], axis=0)
        own, landed = _reduce_scatter_pair(grads, seg_rows)
        pair_sums = _pair_add(own, landed)
        g_shard[l] = _final_sum(pair_sums, _reduce_scatter_chips(pair_sums))
    grad_x = dxo[None]

    order = ("g1", "g2", "g3", "g4", "cb", "ng", "nb", "wa", "wc")
    parts = [jnp.stack(small[k]).reshape(-1) for k in order]
    flat = jnp.concatenate(parts)
    n_flat = flat.shape[0]
    pad = (-n_flat) % (8 * LANE)
    flat = jnp.pad(flat, (0, pad)).reshape(-1, LANE)
    tot = _all_gather_small(flat, True).reshape(-1)[:n_flat]
    red, pos = {}, 0
    for k, p in zip(order, parts):
        red[k] = tot[pos:pos + p.shape[0]]
        pos += p.shape[0]
    g_ln_mix_pre, g_ln_mix_post = red["g1"].reshape(L, D), red["g2"].reshape(L, D)
    g_ln_mlp_pre, g_ln_mlp_post = red["g3"].reshape(L, D), red["g4"].reshape(L, D)
    g_conv_c_b, g_norm_c_g, g_norm_c_b = red["cb"].reshape(L, U), red["ng"].reshape(L, U), red["nb"].reshape(L, U)
    g_conv_a_w = lax.dynamic_slice_in_dim(red["wa"].reshape(L, SC_WIDTH, U), me * cu, cu, axis=2)
    g_conv_c_w = lax.dynamic_slice_in_dim(red["wc"].reshape(L, CF_WIDTH, U), me * cu, cu, axis=2)

    gs = jnp.stack(g_shard)
    g_w_in = gs[:, o_in:o_in + n_in_loc].transpose(0, 2, 1)
    g_pt = gs[:, o_p:o_p + dloc]
    g_proj_b = g_pt[:, :, :2 * U].transpose(0, 2, 1)
    g_proj_a = g_pt[:, :, 2 * U:3 * U].transpose(0, 2, 1)
    g_proj_c = g_pt[:, :, 3 * U:].transpose(0, 2, 1)
    g_w_o = gs[:, o_o:o_o + dloc]
    g_w_up = gs[:, o_up:o_up + floc].transpose(0, 2, 1)
    g_w_down = gs[:, o_dn:o_dn + floc]

    grads_w = [g_ln_mix_pre, g_ln_mix_post, g_ln_mlp_pre, g_ln_mlp_post, g_w_in, g_conv_a_w, g_proj_a, g_proj_b, g_conv_c_w,
               g_conv_c_b, g_norm_c_g, g_norm_c_b, g_proj_c, g_w_o, g_w_up, g_w_down]
    weights = [ln_mix_pre, ln_mix_post, ln_mlp_pre, ln_mlp_post, w_in, conv_a_w, proj_a, proj_b, conv_c_w, conv_c_b, norm_c_g,
               norm_c_b, proj_c, w_o, w_up, w_down]
    ms = [m_ln_mix_pre, m_ln_mix_post, m_ln_mlp_pre, m_ln_mlp_post, m_w_in, m_conv_a_w, m_proj_a, m_proj_b, m_conv_c_w,
          m_conv_c_b, m_norm_c_g, m_norm_c_b, m_proj_c, m_w_o, m_w_up, m_w_down]
    vs = [v_ln_mix_pre, v_ln_mix_post, v_ln_mlp_pre, v_ln_mlp_post, v_w_in, v_conv_a_w, v_proj_a, v_proj_b, v_conv_c_w,
          v_conv_c_b, v_norm_c_g, v_norm_c_b, v_proj_c, v_w_o, v_w_up, v_w_down]
    upd = [_adamw(w, g, m, v) for w, g, m, v in zip(weights, grads_w, ms, vs)]
    return (loss, grad_x, *grads_w, *[u[0] for u in upd], *[u[1] for u in upd], *[u[2] for u in upd])
```

```python
import functools

import jax
import jax.numpy as jnp
from jax import lax
from jax.experimental import pallas as pl
from jax.experimental.pallas import tpu as pltpu

F32 = jnp.float32
MXU_DTYPE = jnp.bfloat16
WIRE_DTYPE = jnp.bfloat16
MESH = pl.DeviceIdType.MESH
ANY = pl.BlockSpec(memory_space=pl.ANY)
HBM = pl.BlockSpec(memory_space=pltpu.HBM)
SEM = pl.BlockSpec(memory_space=pltpu.SEMAPHORE)
EFFECT = pltpu.SideEffectType.DATAFLOW_SIDE_EFFECTING

N_DEV = 8
HEAD_DIM = 128
RMS_EPS = 1e-6
LN_EPS = 1e-5
SC_WIDTH = 3
CF_WIDTH = 31
CONV_PAD = 32
ADAM_LR, ADAM_B1, ADAM_B2, ADAM_EPS, ADAM_WD, ADAM_STEP = 0.001, 0.9, 0.999, 1e-08, 0.01, 10
VMEM_LIMIT = 56 * 1024 * 1024
LANE = 128


def _cp(sem=None, **kw):
    return pltpu.CompilerParams(dimension_semantics=sem, vmem_limit_bytes=VMEM_LIMIT, **kw)


def _sigmoid(x):
    return 1.0 / (1.0 + jnp.exp(-x))


def _row_tile(rows, want):
    t = min(rows, want)
    while rows % t:
        t //= 2
    return t


_DN = {"NN": (((1,), (0,)), ((), ())), "NT": (((1,), (1,)), ((), ())), "TN": (((0,), (0,)), ((), ()))}


def _mm(a, b, mode, out_dtypes, name, *, a_view=None, b_view=None, tm=2048, tn=512, tk=2048, epilogue=None, extras=()):
    a_view = a_view or (0, 0) + tuple(a.shape)
    b_view = b_view or (0, 0) + tuple(b.shape)
    ar, ac, an, am = a_view
    br, bc, bn, bm = b_view
    if mode == "NN":
        M, K, K2, N = an, am, bn, bm
    elif mode == "NT":
        M, K, N, K2 = an, am, bn, bm
    else:
        K, M, K2, N = an, am, bn, bm
    assert K == K2, (name, a_view, b_view)
    tm, tn, tk = _row_tile(M, tm), _row_tile(N, tn), _row_tile(K, tk)
    (a_m_off, a_k_off) = (ac, ar) if mode == "TN" else (ar, ac)
    (b_n_off, b_k_off) = (br, bc) if mode == "NT" else (bc, br)
    while a_m_off % tm:
        tm //= 2
    while b_n_off % tn:
        tn //= 2
    while a_k_off % tk or b_k_off % tk:
        tk //= 2
    nk = K // tk
    a_blk = (tk, tm) if mode == "TN" else (tm, tk)
    b_blk = (tn, tk) if mode == "NT" else (tk, tn)
    assert ar % a_blk[0] == 0 and ac % a_blk[1] == 0, (name, a_view, a_blk)
    assert br % b_blk[0] == 0 and bc % b_blk[1] == 0, (name, b_view, b_blk)
    ao, bo = (ar // a_blk[0], ac // a_blk[1]), (br // b_blk[0], bc // b_blk[1])
    if mode == "TN":
        a_spec = pl.BlockSpec(a_blk, lambda i, j, k: (ao[0] + k, ao[1] + i))
    else:
        a_spec = pl.BlockSpec(a_blk, lambda i, j, k: (ao[0] + i, ao[1] + k))
    if mode == "NT":
        b_spec = pl.BlockSpec(b_blk, lambda i, j, k: (bo[0] + j, bo[1] + k))
    else:
        b_spec = pl.BlockSpec(b_blk, lambda i, j, k: (bo[0] + k, bo[1] + j))
    o_spec = pl.BlockSpec((tm, tn), lambda i, j, k: (i, j))
    n_ex, n_out = len(extras), len(out_dtypes)
    dn = _DN[mode]

    def body(*refs):
        a_ref, b_ref = refs[:2]
        ex_refs = refs[2:2 + n_ex]
        o_refs = refs[2 + n_ex:2 + n_ex + n_out]
        p = lax.dot_general(a_ref[...], b_ref[...], dn, preferred_element_type=F32)

        def finish(acc):
            outs = epilogue(acc, *[r[...] for r in ex_refs]) if epilogue else (acc,)
            for o_ref, o in zip(o_refs, outs):
                o_ref[...] = o.astype(o_ref.dtype)

        if nk == 1:
            finish(p)
        else:
            acc_ref = refs[-1]
            k = pl.program_id(2)

            @pl.when(k == 0)
            def _():
                acc_ref[...] = p

            @pl.when(k > 0)
            def _():
                acc_ref[...] += p

            @pl.when(k == nk - 1)
            def _():
                finish(acc_ref[...])

    outs = pl.pallas_call(
        body, name=name, grid=(M // tm, N // tn, nk),
        in_specs=[a_spec, b_spec] + [o_spec] * n_ex, out_specs=[o_spec] * n_out,
        out_shape=[jax.ShapeDtypeStruct((M, N), d) for d in out_dtypes],
        scratch_shapes=[pltpu.VMEM((tm, tn), F32)] if nk > 1 else [],
        compiler_params=_cp(("parallel", "parallel", "arbitrary")),
    )(a, b, *extras)
    return outs[0] if n_out == 1 else outs


def _rms_fwd(x, g, name):
    S, D = x.shape
    tr = _row_tile(S, 256)

    def body(x_ref, g_ref, h_ref):
        xv = x_ref[...]
        r = lax.rsqrt(jnp.mean(xv * xv, axis=-1, keepdims=True) + RMS_EPS)
        h_ref[...] = ((xv * r) * g_ref[...]).astype(h_ref.dtype)

    return pl.pallas_call(
        body, name=name, grid=(S // tr,),
        in_specs=[pl.BlockSpec((tr, D), lambda i: (i, 0)), pl.BlockSpec((1, D), lambda i: (0, 0))],
        out_specs=pl.BlockSpec((tr, D), lambda i: (i, 0)),
        out_shape=jax.ShapeDtypeStruct((S, D), MXU_DTYPE), compiler_params=_cp(("parallel",)),
    )(x, g)


def _resid_post(xres, y, g_post, g_next, name):
    S, D = y.shape
    tr = _row_tile(S, 256)
    has_next = g_next is not None

    def body(*refs):
        xr_ref, y_ref, gp_ref = refs[:3]
        yv = y_ref[...]
        r = lax.rsqrt(jnp.mean(yv * yv, axis=-1, keepdims=True) + RMS_EPS)
        xn = xr_ref[...] + (yv * r) * gp_ref[...]
        if has_next:
            gn_ref, xo_ref, h_ref = refs[3:]
            r2 = lax.rsqrt(jnp.mean(xn * xn, axis=-1, keepdims=True) + RMS_EPS)
            h_ref[...] = ((xn * r2) * gn_ref[...]).astype(h_ref.dtype)
        else:
            xo_ref = refs[3]
        xo_ref[...] = xn

    row = pl.BlockSpec((tr, D), lambda i: (i, 0))
    vec = pl.BlockSpec((1, D), lambda i: (0, 0))
    outs = pl.pallas_call(
        body, name=name, grid=(S // tr,),
        in_specs=[row, row, vec] + ([vec] if has_next else []),
        out_specs=[row] + ([row] if has_next else []),
        out_shape=[jax.ShapeDtypeStruct((S, D), F32)] + ([jax.ShapeDtypeStruct((S, D), MXU_DTYPE)] if has_next else []),
        compiler_params=_cp(("parallel",)),
    )(xres, y, g_post, *([g_next] if has_next else []))
    return (outs[0], outs[1]) if has_next else (outs[0], None)


def _rms_bwd(xin, g, dy, dres, out_dtype, name):
    S, D = xin.shape
    tr = _row_tile(S, 256)
    has_res = dres is not None

    def body(*refs):
        x_ref, g_ref, dy_ref = refs[:3]
        dx_ref, dg_ref = refs[-2:]
        xv, dyv = x_ref[...], dy_ref[...].astype(F32)
        r = lax.rsqrt(jnp.mean(xv * xv, axis=-1, keepdims=True) + RMS_EPS)
        n = xv * r
        dyg = dyv * g_ref[...]
        dx = r * (dyg - n * jnp.mean(dyg * n, axis=-1, keepdims=True))
        if has_res:
            dx = dx + refs[3][...]
        dx_ref[...] = dx.astype(dx_ref.dtype)

        @pl.when(pl.program_id(0) == 0)
        def _():
            dg_ref[...] = jnp.zeros_like(dg_ref)

        dg_ref[...] += jnp.sum(dyv * n, axis=0, keepdims=True)

    row = pl.BlockSpec((tr, D), lambda i: (i, 0))
    vec = pl.BlockSpec((1, D), lambda i: (0, 0))
    return pl.pallas_call(
        body, name=name, grid=(S // tr,),
        in_specs=[row, vec, row] + ([row] if has_res else []), out_specs=[row, vec],
        out_shape=[jax.ShapeDtypeStruct((S, D), out_dtype), jax.ShapeDtypeStruct((1, D), F32)],
        compiler_params=_cp(("arbitrary",)),
    )(xin, g, dy, *([dres] if has_res else []))


def _loss_head(y, target):
    S, D = y.shape
    tr = _row_tile(S, 256)

    def body(y_ref, t_ref, dy_ref, l_ref):
        e = y_ref[...] - t_ref[...]
        dy_ref[...] = e * (1.0 / D)

        @pl.when(pl.program_id(0) == 0)
        def _():
            l_ref[...] = jnp.zeros_like(l_ref)

        l_ref[...] += 0.5 * jnp.sum(jnp.mean(e * e, axis=-1, keepdims=True), axis=0, keepdims=True)

    row = pl.BlockSpec((tr, D), lambda i: (i, 0))
    return pl.pallas_call(
        body, name="loss_head", grid=(S // tr,), in_specs=[row, row],
        out_specs=[row, pl.BlockSpec((1, LANE), lambda i: (0, 0))],
        out_shape=[jax.ShapeDtypeStruct((S, D), F32), jax.ShapeDtypeStruct((1, LANE), F32)],
        compiler_params=_cp(("arbitrary",)),
    )(y, target)


def _gate_specs(S, U, tr):
    gl = [pl.BlockSpec((tr, U), functools.partial(lambda i, j, o: (i, o + j), o=o)) for o in (11, 15, 19)]
    return gl, pl.BlockSpec((tr, U), lambda i, j: (i, j))


def _gate_fwd(proj, ya, yb, yc):
    S, D = ya.shape
    U = D // 4
    tr = _row_tile(S, 256)

    def body(ga_ref, gb_ref, gc_ref, ya_ref, yb_ref, yc_ref, o_ref):
        m = _sigmoid(ga_ref[...]) * ya_ref[...] + _sigmoid(gb_ref[...]) * yb_ref[...] + _sigmoid(gc_ref[...]) * yc_ref[...]
        o_ref[...] = m.astype(o_ref.dtype)

    gl, blk = _gate_specs(S, U, tr)
    return pl.pallas_call(
        body, name="gate_fwd", grid=(S // tr, 4), in_specs=gl + [blk] * 3, out_specs=blk,
        out_shape=jax.ShapeDtypeStruct((S, D), MXU_DTYPE), compiler_params=_cp(("parallel", "parallel")),
    )(proj, proj, proj, ya, yb, yc)


def _gate_bwd(dm, proj, ya, yb, yc):
    S, D = ya.shape
    U = D // 4
    tr = _row_tile(S, 256)

    def body(dm_ref, ga_ref, gb_ref, gc_ref, ya_ref, yb_ref, yc_ref, da_ref, db_ref, dc_ref, la_ref, lb_ref, lc_ref):
        d = dm_ref[...]
        for g_ref, y_ref, dy_ref, dl_ref in ((ga_ref, ya_ref, da_ref, la_ref), (gb_ref, yb_ref, db_ref, lb_ref),
                                             (gc_ref, yc_ref, dc_ref, lc_ref)):
            g = _sigmoid(g_ref[...])
            dy_ref[...] = (d * g).astype(dy_ref.dtype)
            dl_ref[...] = (d * y_ref[...] * g * (1.0 - g)).astype(dl_ref.dtype)

    gl, blk = _gate_specs(S, U, tr)
    return pl.pallas_call(
        body, name="gate_bwd", grid=(S // tr, 4), in_specs=[blk] + gl + [blk] * 3, out_specs=[blk] * 6,
        out_shape=[jax.ShapeDtypeStruct((S, D), MXU_DTYPE)] * 6, compiler_params=_cp(("parallel", "parallel")),
    )(dm, proj, proj, proj, ya, yb, yc)


def _chunks(S):
    r = _row_tile(S, 256)
    return [(r0, r) for r0 in range(0, S, r)]


def _conv_causal(front_ref, w_ref, K, r0, R):
    acc = None
    for j in range(K):
        term = w_ref[pl.ds(K - 1 - j, 1), :] * front_ref[pl.ds(CONV_PAD + r0 - j, R), :]
        acc = term if acc is None else acc + term
    return acc


def _conv_anticausal(back_ref, w_ref, K, r0, R):
    acc = None
    for j in range(K):
        term = w_ref[pl.ds(K - 1 - j, 1), :] * back_ref[pl.ds(r0 + j, R), :]
        acc = term if acc is None else acc + term
    return acc


def _conv_wgrad(front_ref, back_ref, dw_ref, K, S):
    for j in range(K):
        tot = None
        for r0, R in _chunks(S):
            part = jnp.sum(back_ref[pl.ds(r0, R), :] * front_ref[pl.ds(CONV_PAD + r0 - j, R), :], axis=0, keepdims=True)
            tot = part if tot is None else tot + part
        dw_ref[pl.ds(K - 1 - j, 1), :] = tot


def _col(S, cw, unit_off):
    return pl.BlockSpec((S, cw), functools.partial(lambda cb, o: (0, o + cb), o=unit_off))


def _branch_a_fwd(proj, wa, U):
    S = proj.shape[0]
    cw = min(LANE, U)
    nb = U // cw
    K = SC_WIDTH

    def body(b_ref, c_ref, u_ref, w_ref, o_ref, front):
        front[pl.ds(0, CONV_PAD), :] = jnp.zeros((CONV_PAD, cw), F32)
        front[pl.ds(CONV_PAD, S), :] = c_ref[...] * u_ref[...]
        for r0, R in _chunks(S):
            o_ref[pl.ds(r0, R), :] = (b_ref[pl.ds(r0, R), :] * _conv_causal(front, w_ref, K, r0, R)).astype(o_ref.dtype)

    return pl.pallas_call(
        body, name="branch_a_fwd", grid=(nb,),
        in_specs=[_col(S, cw, 6 * nb), _col(S, cw, 7 * nb), _col(S, cw, 8 * nb), pl.BlockSpec((K, cw), lambda cb: (0, cb))],
        out_specs=pl.BlockSpec((S, cw), lambda cb: (0, cb)), out_shape=jax.ShapeDtypeStruct((S, U), MXU_DTYPE),
        scratch_shapes=[pltpu.VMEM((S + CONV_PAD, cw), F32)], compiler_params=_cp(("parallel",)),
    )(proj, proj, proj, wa)


def _branch_a_bwd(d_out, proj, wa, U):
    S = proj.shape[0]
    cw = min(LANE, U)
    nb = U // cw
    K = SC_WIDTH

    def body(d_ref, b_ref, c_ref, u_ref, w_ref, db_ref, dc_ref, du_ref, dw_ref, front, back):
        front[pl.ds(0, CONV_PAD), :] = jnp.zeros((CONV_PAD, cw), F32)
        front[pl.ds(CONV_PAD, S), :] = c_ref[...] * u_ref[...]
        back[pl.ds(S, CONV_PAD), :] = jnp.zeros((CONV_PAD, cw), F32)
        back[pl.ds(0, S), :] = d_ref[...] * b_ref[...]
        for r0, R in _chunks(S):
            rows = pl.ds(r0, R)
            db_ref[rows, :] = (d_ref[rows, :] * _conv_causal(front, w_ref, K, r0, R)).astype(db_ref.dtype)
            d_ai = _conv_anticausal(back, w_ref, K, r0, R)
            dc_ref[rows, :] = (d_ai * u_ref[rows, :]).astype(dc_ref.dtype)
            du_ref[rows, :] = (d_ai * c_ref[rows, :]).astype(du_ref.dtype)
        _conv_wgrad(front, back, dw_ref, K, S)

    blk = pl.BlockSpec((S, cw), lambda cb: (0, cb))
    wblk = pl.BlockSpec((K, cw), lambda cb: (0, cb))
    return pl.pallas_call(
        body, name="branch_a_bwd", grid=(nb,),
        in_specs=[blk, _col(S, cw, 6 * nb), _col(S, cw, 7 * nb), _col(S, cw, 8 * nb), wblk],
        out_specs=[blk, blk, blk, wblk],
        out_shape=[jax.ShapeDtypeStruct((S, U), MXU_DTYPE)] * 3 + [jax.ShapeDtypeStruct((K, U), F32)],
        scratch_shapes=[pltpu.VMEM((S + CONV_PAD, cw), F32)] * 2, compiler_params=_cp(("parallel",)),
    )(d_out, proj, proj, proj, wa)


def _branch_c_conv_fwd(proj, wc, cb, U):
    S = proj.shape[0]
    cw = min(LANE, U)
    nb = U // cw
    K = CF_WIDTH

    def body(a_ref, g_ref, w_ref, bias_ref, o_ref, front):
        front[pl.ds(0, CONV_PAD), :] = jnp.zeros((CONV_PAD, cw), F32)
        front[pl.ds(CONV_PAD, S), :] = a_ref[...] * _sigmoid(g_ref[...])
        for r0, R in _chunks(S):
            o_ref[pl.ds(r0, R), :] = _conv_causal(front, w_ref, K, r0, R) + bias_ref[...]

    return pl.pallas_call(
        body, name="branch_c_conv_fwd", grid=(nb,),
        in_specs=[_col(S, cw, 9 * nb), _col(S, cw, 10 * nb), pl.BlockSpec((K, cw), lambda c: (0, c)),
                  pl.BlockSpec((1, cw), lambda c: (0, c))],
        out_specs=pl.BlockSpec((S, cw), lambda c: (0, c)), out_shape=jax.ShapeDtypeStruct((S, U), F32),
        scratch_shapes=[pltpu.VMEM((S + CONV_PAD, cw), F32)], compiler_params=_cp(("parallel",)),
    )(proj, proj, wc, cb)


def _branch_c_conv_bwd(d_u1, proj, wc, U):
    S = proj.shape[0]
    cw = min(LANE, U)
    nb = U // cw
    K = CF_WIDTH

    def body(d_ref, a_ref, g_ref, w_ref, da_ref, dg_ref, dw_ref, dbias_ref, front, back):
        sg = _sigmoid(g_ref[...])
        front[pl.ds(0, CONV_PAD), :] = jnp.zeros((CONV_PAD, cw), F32)
        front[pl.ds(CONV_PAD, S), :] = a_ref[...] * sg
        back[pl.ds(S, CONV_PAD), :] = jnp.zeros((CONV_PAD, cw), F32)
        back[pl.ds(0, S), :] = d_ref[...]
        dbias_ref[...] = jnp.sum(d_ref[...], axis=0, keepdims=True)
        for r0, R in _chunks(S):
            rows = pl.ds(r0, R)
            d_u0 = _conv_anticausal(back, w_ref, K, r0, R)
            s = _sigmoid(g_ref[rows, :])
            da_ref[rows, :] = (d_u0 * s).astype(da_ref.dtype)
            dg_ref[rows, :] = (d_u0 * a_ref[rows, :] * s * (1.0 - s)).astype(dg_ref.dtype)
        _conv_wgrad(front, back, dw_ref, K, S)

    blk = pl.BlockSpec((S, cw), lambda c: (0, c))
    wblk = pl.BlockSpec((K, cw), lambda c: (0, c))
    vblk = pl.BlockSpec((1, cw), lambda c: (0, c))
    return pl.pallas_call(
        body, name="branch_c_conv_bwd", grid=(nb,),
        in_specs=[blk, _col(S, cw, 9 * nb), _col(S, cw, 10 * nb), wblk], out_specs=[blk, blk, wblk, vblk],
        out_shape=[jax.ShapeDtypeStruct((S, U), MXU_DTYPE)] * 2 + [jax.ShapeDtypeStruct((K, U), F32), jax.ShapeDtypeStruct((1, U), F32)],
        scratch_shapes=[pltpu.VMEM((S + CONV_PAD, cw), F32)] * 2, compiler_params=_cp(("parallel",)),
    )(d_u1, proj, proj, wc)


def _branch_c_norm_fwd(u1, ng, nbias):
    S, U = u1.shape
    tr = _row_tile(S, 256)

    def body(u_ref, g_ref, b_ref, o_ref):
        u = u_ref[...]
        mu = jnp.mean(u, axis=-1, keepdims=True)
        var = jnp.mean(jnp.square(u - mu), axis=-1, keepdims=True)
        u2 = ((u - mu) * lax.rsqrt(var + LN_EPS)) * g_ref[...] + b_ref[...]
        o_ref[...] = (u2 * _sigmoid(u2)).astype(o_ref.dtype)

    row = pl.BlockSpec((tr, U), lambda i: (i, 0))
    vec = pl.BlockSpec((1, U), lambda i: (0, 0))
    return pl.pallas_call(
        body, name="branch_c_norm_fwd", grid=(S // tr,), in_specs=[row, vec, vec], out_specs=row,
        out_shape=jax.ShapeDtypeStruct((S, U), MXU_DTYPE), compiler_params=_cp(("parallel",)),
    )(u1, ng, nbias)


def _branch_c_norm_bwd(d_u3, u1, ng, nbias):
    S, U = u1.shape
    tr = _row_tile(S, 256)

    def body(d_ref, u_ref, g_ref, b_ref, du_ref, dg_ref, db_ref):
        u = u_ref[...]
        mu = jnp.mean(u, axis=-1, keepdims=True)
        var = jnp.mean(jnp.square(u - mu), axis=-1, keepdims=True)
        rstd = lax.rsqrt(var + LN_EPS)
        xh = (u - mu) * rstd
        u2 = xh * g_ref[...] + b_ref[...]
        s = _sigmoid(u2)
        d_u2 = d_ref[...] * (s * (1.0 + u2 * (1.0 - s)))
        d_xh = d_u2 * g_ref[...]
        du_ref[...] = rstd * (d_xh - jnp.mean(d_xh, axis=-1, keepdims=True) - xh * jnp.mean(d_xh * xh, axis=-1, keepdims=True))

        @pl.when(pl.program_id(0) == 0)
        def _():
            dg_ref[...] = jnp.zeros_like(dg_ref)
            db_ref[...] = jnp.zeros_like(db_ref)

        dg_ref[...] += jnp.sum(d_u2 * xh, axis=0, keepdims=True)
        db_ref[...] += jnp.sum(d_u2, axis=0, keepdims=True)

    row = pl.BlockSpec((tr, U), lambda i: (i, 0))
    vec = pl.BlockSpec((1, U), lambda i: (0, 0))
    return pl.pallas_call(
        body, name="branch_c_norm_bwd", grid=(S // tr,), in_specs=[row, row, vec, vec], out_specs=[row, vec, vec],
        out_shape=[jax.ShapeDtypeStruct((S, U), F32), jax.ShapeDtypeStruct((1, U), F32), jax.ShapeDtypeStruct((1, U), F32)],
        compiler_params=_cp(("arbitrary",)),
    )(d_u3, u1, ng, nbias)


def _tri(T, inclusive):
    j = lax.broadcasted_iota(jnp.int32, (T, T), 0)
    s = lax.broadcasted_iota(jnp.int32, (T, T), 1)
    return ((j >= s) if inclusive else (j > s)).astype(MXU_DTYPE)


def _split_dot(x, tri):
    if MXU_DTYPE == F32:
        return jnp.dot(x, tri, preferred_element_type=F32)
    hi = x.astype(MXU_DTYPE)
    lo = (x - hi.astype(F32)).astype(MXU_DTYPE)
    return jnp.dot(hi, tri, preferred_element_type=F32) + jnp.dot(lo, tri, preferred_element_type=F32)


def _sb_block(qb, kb, i, j, T, tri_strict, c_lf):
    z = lax.dot_general(qb, kb, _DN["NT"], preferred_element_type=F32) * (HEAD_DIM ** -0.5)
    t_idx = i * T + lax.broadcasted_iota(jnp.int32, (T, T), 0)
    s_idx = j * T + lax.broadcasted_iota(jnp.int32, (T, T), 1)
    mask = s_idx < t_idx
    e = jnp.exp(-jnp.abs(z))
    lg = jnp.log(1.0 + e)
    log_beta = jnp.minimum(z, 0.0) - lg
    lf = jnp.where(mask, jnp.minimum(-z, 0.0) - lg, 0.0)
    log_a = log_beta + _split_dot(lf, tri_strict) + c_lf
    a = jnp.where(mask, jnp.exp(log_a), 0.0)
    return z, e, mask, lf, a


def _attn_specs(S, U, h_blocks):
    nh = (2 * U) // HEAD_DIM
    return [pl.BlockSpec((S, HEAD_DIM), functools.partial(lambda h, o: (0, o + h), o=o * nh)) for o in range(h_blocks)]


def _attn_fwd(proj, U):
    S = proj.shape[0]
    nh = (2 * U) // HEAD_DIM
    T = _row_tile(S, 256)
    nq = S // T

    def body(q_ref, k_ref, v_ref, o_ref, of_ref, qs, ks, vs):
        qs[...] = q_ref[...].astype(MXU_DTYPE)
        ks[...] = k_ref[...].astype(MXU_DTYPE)
        vs[...] = v_ref[...].astype(MXU_DTYPE)
        tri = _tri(T, False)

        def q_loop(i, _):
            rows = pl.ds(pl.multiple_of(i * T, T), T)
            qb = qs[rows, :]

            def k_loop(jj, carry):
                c_lf, acc = carry
                j = i - jj
                cols = pl.ds(pl.multiple_of(j * T, T), T)
                _, _, _, lf, a = _sb_block(qb, ks[cols, :], i, j, T, tri, c_lf)
                acc = acc + jnp.dot(a.astype(MXU_DTYPE), vs[cols, :], preferred_element_type=F32)
                return c_lf + jnp.sum(lf, axis=1, keepdims=True), acc

            _, acc = lax.fori_loop(0, i + 1, k_loop, (jnp.zeros((T, 1), F32), jnp.zeros((T, HEAD_DIM), F32)))
            o_ref[rows, :] = acc.astype(o_ref.dtype)
            of_ref[rows, :] = acc
            return 0

        lax.fori_loop(0, nq, q_loop, 0)

    hblk = pl.BlockSpec((S, HEAD_DIM), lambda h: (0, h))
    return pl.pallas_call(
        body, name="attn_fwd", grid=(nh,), in_specs=_attn_specs(S, U, 3), out_specs=[hblk, hblk],
        out_shape=[jax.ShapeDtypeStruct((S, 2 * U), MXU_DTYPE), jax.ShapeDtypeStruct((S, 2 * U), F32)],
        scratch_shapes=[pltpu.VMEM((S, HEAD_DIM), MXU_DTYPE)] * 3, compiler_params=_cp(("parallel",)),
    )(proj, proj, proj)


def _attn_bwd(proj, att_f32, d_att, U):
    S = proj.shape[0]
    nh = (2 * U) // HEAD_DIM
    T = _row_tile(S, 256)
    nq = S // T
    scale = HEAD_DIM ** -0.5

    def body(q_ref, k_ref, v_ref, o_ref, do_ref, dq_ref, dk_ref, dv_ref, qs, ks, vs, dos, dka, dva):
        qs[...] = q_ref[...].astype(MXU_DTYPE)
        ks[...] = k_ref[...].astype(MXU_DTYPE)
        vs[...] = v_ref[...].astype(MXU_DTYPE)
        dos[...] = do_ref[...].astype(MXU_DTYPE)
        dka[...] = jnp.zeros_like(dka)
        dva[...] = jnp.zeros_like(dva)
        tri = _tri(T, False)
        tri_inc = _tri(T, True)

        def q_loop(i, _):
            rows = pl.ds(pl.multiple_of(i * T, T), T)
            qb = qs[rows, :]
            dob = dos[rows, :]
            delta = jnp.sum(dob.astype(F32) * o_ref[rows, :], axis=1, keepdims=True)

            def k_loop(jj, carry):
                c_lf, c_g, dq = carry
                j = i - jj
                cols = pl.ds(pl.multiple_of(j * T, T), T)
                kb, vb = ks[cols, :], vs[cols, :]
                z, e, mask, lf, a = _sb_block(qb, kb, i, j, T, tri, c_lf)
                a_mx = a.astype(MXU_DTYPE)
                d_a = lax.dot_general(dob, vb, _DN["NT"], preferred_element_type=F32)
                g = a_mx.astype(F32) * d_a
                prefix = delta - (_split_dot(g, tri_inc) + c_g)
                inv = 1.0 / (1.0 + e)
                beta = jnp.where(z >= 0.0, 1.0, e) * inv
                one_m_beta = jnp.where(z >= 0.0, e, 1.0) * inv
                dz = jnp.where(mask, (g * one_m_beta - prefix * beta) * scale, 0.0).astype(MXU_DTYPE)
                dq = dq + jnp.dot(dz, kb, preferred_element_type=F32)
                dka[cols, :] += lax.dot_general(dz, qb, _DN["TN"], preferred_element_type=F32)
                dva[cols, :] += lax.dot_general(a_mx, dob, _DN["TN"], preferred_element_type=F32)
                return c_lf + jnp.sum(lf, axis=1, keepdims=True), c_g + jnp.sum(g, axis=1, keepdims=True), dq

            zero = jnp.zeros((T, 1), F32)
            _, _, dq = lax.fori_loop(0, i + 1, k_loop, (zero, zero, jnp.zeros((T, HEAD_DIM), F32)))
            dq_ref[rows, :] = dq.astype(dq_ref.dtype)
            return 0

        lax.fori_loop(0, nq, q_loop, 0)
        dk_ref[...] = dka[...].astype(dk_ref.dtype)
        dv_ref[...] = dva[...].astype(dv_ref.dtype)

    hblk = pl.BlockSpec((S, HEAD_DIM), lambda h: (0, h))
    return pl.pallas_call(
        body, name="attn_bwd", grid=(nh,), in_specs=_attn_specs(S, U, 3) + [hblk, hblk], out_specs=[hblk] * 3,
        out_shape=[jax.ShapeDtypeStruct((S, 2 * U), MXU_DTYPE)] * 3,
        scratch_shapes=[pltpu.VMEM((S, HEAD_DIM), MXU_DTYPE)] * 4 + [pltpu.VMEM((S, HEAD_DIM), F32)] * 2,
        compiler_params=_cp(("parallel",)),
    )(proj, proj, proj, att_f32, d_att)


def _place():
    return lax.axis_index("x"), lax.axis_index("y"), lax.axis_index("c")


def _flip(v, bit):
    return 1 - v if bit else v


def _related(x, y, r):
    return _flip(x, r & 1), _flip(y, r >> 1)


def _segments(seg_rows):
    out, off = [], 0
    for r in seg_rows:
        out.append((off, r))
        off += r
    return out, off


def _seg_rows_of(ref, dev, off, rows):
    return ref.at[pl.ds(pl.multiple_of(N_DEV * off + dev * rows, 16), rows), :]


def _hbm(a):
    return pltpu.with_memory_space_constraint(a, pltpu.HBM)


def _ag_copies(loc_ref, wb_ref, send_sems, recv_sems, segs, outgoing):
    x, y, c = _place()
    peers = [(x, y, 1 - c)] + [(*_related(x, y, r), c) for r in (1, 2, 3)]
    out = []
    for n, peer in enumerate(peers):
        block = 4 * x + 2 * y + c if outgoing else 4 * peer[0] + 2 * peer[1] + peer[2]
        for k, (off, rows) in enumerate(segs):
            sem = n * len(segs) + k
            out.append(pltpu.make_async_remote_copy(
                src_ref=loc_ref.at[pl.ds(off, rows), :], dst_ref=_seg_rows_of(wb_ref, block, off, rows),
                send_sem=send_sems.at[sem], recv_sem=recv_sems.at[sem], device_id=peer, device_id_type=MESH))
    return out


def _ag_start(loc, wb, carry, seg_rows, name):
    segs, _ = _segments(seg_rows)
    n_sem = 4 * len(segs)

    def body(loc_ref, wb_ref, carry_ref, send_sems, recv_sems, loc_thru, wb_thru, carry_thru):
        for cp in _ag_copies(loc_ref, wb_ref, send_sems, recv_sems, segs, True):
            cp.start()

    return pl.pallas_call(
        body, name=name,
        out_shape=(pltpu.SemaphoreType.DMA((n_sem,)), pltpu.SemaphoreType.DMA((n_sem,)), pltpu.HBM(loc.shape, loc.dtype),
                   pltpu.HBM(wb.shape, wb.dtype), pltpu.HBM(carry.shape, carry.dtype)),
        in_specs=(HBM, HBM, HBM), out_specs=(SEM, SEM, HBM, HBM, HBM), input_output_aliases={0: 2, 1: 3, 2: 4},
        compiler_params=pltpu.CompilerParams(has_side_effects=EFFECT),
    )(_hbm(loc), _hbm(wb), _hbm(carry))


def _ag_wait(loc, wb, send_sems, recv_sems, after, seg_rows, name):
    segs, _ = _segments(seg_rows)

    def body(loc_ref, wb_ref, send_sems, recv_sems, after_ref, loc_dead, wb_out):
        for cp in _ag_copies(loc_ref, wb_ref, send_sems, recv_sems, segs, False):
            cp.wait_send()
            cp.wait_recv()

    return pl.pallas_call(
        body, name=name, out_shape=(pltpu.HBM(loc.shape, loc.dtype), pltpu.HBM(wb.shape, wb.dtype)),
        in_specs=(HBM, HBM, SEM, SEM, ANY), out_specs=(HBM, HBM), input_output_aliases={0: 0, 1: 1},
        compiler_params=pltpu.CompilerParams(has_side_effects=EFFECT),
    )(loc, wb, send_sems, recv_sems, after)[1]


def _ag_finish(wb, seg_rows):
    segs, _ = _segments(seg_rows)
    nseg = len(segs)

    def body(wb_in, wb_ref, tok_ref, send_sems, recv_sems):
        x, y, c = _place()
        sent = []
        for n, r in enumerate((1, 2, 3)):
            px, py = _related(x, y, r)
            for k, (off, rows) in enumerate(segs):
                cp = pltpu.make_async_remote_copy(
                    src_ref=_seg_rows_of(wb_in, 4 * px + 2 * py + c, off, rows), dst_ref=_seg_rows_of(wb_ref, 4 * px + 2 * py + c, off, rows),
                    send_sem=send_sems.at[n, k], recv_sem=recv_sems.at[n, k], device_id=(x, y, 1 - c), device_id_type=MESH)
                cp.start()
                sent.append(cp)
        for n, r in enumerate((1, 2, 3)):
            px, py = _related(x, y, r)
            for k, (off, rows) in enumerate(segs):
                theirs = _seg_rows_of(wb_ref, 4 * px + 2 * py + (1 - c), off, rows)
                pltpu.make_async_remote_copy(src_ref=theirs, dst_ref=theirs, send_sem=send_sems.at[n, k], recv_sem=recv_sems.at[n, k],
                                             device_id=(x, y, 1 - c), device_id_type=MESH).wait_recv()
        for cp in sent:
            cp.wait_send()
        tok_ref[...] = jnp.zeros_like(tok_ref)

    return pl.pallas_call(
        body, name="ag_finish", in_specs=[ANY], out_specs=[ANY, pl.BlockSpec(memory_space=pltpu.VMEM)],
        out_shape=[jax.ShapeDtypeStruct(wb.shape, wb.dtype), jax.ShapeDtypeStruct((8, LANE), F32)], input_output_aliases={0: 0},
        scratch_shapes=[pltpu.SemaphoreType.DMA((3, nseg))] * 2, compiler_params=_cp(),
    )(wb)


def _rs_pair(grads, seg_rows):
    segs, R = _segments(seg_rows)
    C = grads.shape[1]
    nseg = len(segs)

    def body(g_ref, land_ref, send_sems, recv_sems):
        x, y, c = _place()
        copies = []
        for r in range(4):
            px, py = _related(x, y, r)
            for k, (off, rows) in enumerate(segs):
                copies.append(pltpu.make_async_remote_copy(
                    src_ref=_seg_rows_of(g_ref, 4 * px + 2 * py + (1 - c), off, rows), dst_ref=land_ref.at[r, pl.ds(off, rows), :],
                    send_sem=send_sems.at[r, k], recv_sem=recv_sems.at[r, k], device_id=(x, y, 1 - c), device_id_type=MESH))
        for cp in copies:
            cp.start()
        for cp in copies:
            cp.wait()

    return pl.pallas_call(
        body, name="rs_pair", in_specs=[ANY], out_specs=ANY, out_shape=jax.ShapeDtypeStruct((4, R, C), grads.dtype),
        scratch_shapes=[pltpu.SemaphoreType.DMA((4, nseg))] * 2, compiler_params=_cp(),
    )(grads)


def _gcd_tile(seg_rows):
    t = 64
    while any(r % t for r in seg_rows):
        t //= 2
    return t


def _pair_add(grads, landed, seg_rows):
    segs, R = _segments(seg_rows)
    C = grads.shape[1]
    t = _gcd_tile(seg_rows)

    def g_index(r, i):
        x, y, c = _place()
        px = jnp.where(r % 2 == 1, 1 - x, x)
        py = jnp.where(r // 2 == 1, 1 - y, y)
        dev = 4 * px + 2 * py + c
        blk = 0
        for off, rows in segs:
            inside = jnp.logical_and(i >= off // t, i < (off + rows) // t)
            blk = blk + jnp.where(inside, ((N_DEV - 1) * off) // t + dev * (rows // t) + i, 0)
        return blk, 0

    def body(a_ref, b_ref, o_ref):
        o_ref[...] = (a_ref[...].astype(F32) + b_ref[...].astype(F32)).astype(o_ref.dtype)

    slot = pl.BlockSpec((None, t, C), lambda r, i: (r, i, 0))
    return pl.pallas_call(
        body, name="pair_add", grid=(4, R // t), in_specs=[pl.BlockSpec((t, C), g_index), slot], out_specs=slot,
        out_shape=jax.ShapeDtypeStruct((4, R, C), grads.dtype), compiler_params=_cp(("parallel", "parallel")),
    )(grads, landed)


def _rs_copies(p_ref, land_ref, send_sems, recv_sems):
    x, y, c = _place()
    return [pltpu.make_async_remote_copy(src_ref=p_ref.at[r], dst_ref=land_ref.at[r - 1], send_sem=send_sems.at[r - 1],
                                         recv_sem=recv_sems.at[r - 1], device_id=(*_related(x, y, r), c), device_id_type=MESH)
            for r in (1, 2, 3)]


def _rs_start(pair_sums, land, carry, name):
    def body(p_ref, land_ref, carry_ref, send_sems, recv_sems, p_thru, land_thru, carry_thru):
        for cp in _rs_copies(p_ref, land_ref, send_sems, recv_sems):
            cp.start()

    return pl.pallas_call(
        body, name=name,
        out_shape=(pltpu.SemaphoreType.DMA((3,)), pltpu.SemaphoreType.DMA((3,)), pltpu.HBM(pair_sums.shape, pair_sums.dtype),
                   pltpu.HBM(land.shape, land.dtype), pltpu.HBM(carry.shape, carry.dtype)),
        in_specs=(HBM, HBM, HBM), out_specs=(SEM, SEM, HBM, HBM, HBM), input_output_aliases={0: 2, 1: 3, 2: 4},
        compiler_params=pltpu.CompilerParams(has_side_effects=EFFECT),
    )(_hbm(pair_sums), _hbm(land), _hbm(carry))


def _rs_wait(pair_sums, land, send_sems, recv_sems, after, name):
    def body(p_ref, land_ref, send_sems, recv_sems, after_ref, p_out, land_out):
        for cp in _rs_copies(p_ref, land_ref, send_sems, recv_sems):
            cp.wait_send()
            cp.wait_recv()

    return pl.pallas_call(
        body, name=name, out_shape=(pltpu.HBM(pair_sums.shape, pair_sums.dtype), pltpu.HBM(land.shape, land.dtype)),
        in_specs=(HBM, HBM, SEM, SEM, ANY), out_specs=(HBM, HBM), input_output_aliases={0: 0, 1: 1},
        compiler_params=pltpu.CompilerParams(has_side_effects=EFFECT),
    )(pair_sums, land, send_sems, recv_sems, after)


def _pack_tile(R):
    for t in (576, 512, 448, 256, 128, 64, 32, 16, 8):
        if R % t == 0:
            return t
    return R


def _final_sum(pair_sums, landed):
    _, R, C = pair_sums.shape
    tr = _pack_tile(R)

    def body(p_ref, l1_ref, l2_ref, l3_ref, o_ref):
        o_ref[...] = ((p_ref[...].astype(F32) + l1_ref[...].astype(F32)) + l2_ref[...].astype(F32)) + l3_ref[...].astype(F32)

    specs = [pl.BlockSpec((None, tr, C), functools.partial(lambda i, s: (s, i, 0), s=s)) for s in (0, 0, 1, 2)]
    return pl.pallas_call(
        body, name="final_sum", grid=(R // tr,), in_specs=specs, out_specs=pl.BlockSpec((tr, C), lambda i: (i, 0)),
        out_shape=jax.ShapeDtypeStruct((R, C), F32), compiler_params=_cp(("parallel",)),
    )(pair_sums, landed, landed, landed)


def _all_gather_small(v, reduce):
    M, N = v.shape

    def body(x_ref, out_ref, *rest):
        if reduce:
            sum_ref, send_sems, recv_sems, local_sem = rest
        else:
            send_sems, recv_sems, local_sem = rest
        x, y, c = _place()
        me, sibling = (x, y, c), (x, y, 1 - c)
        chips = [_related(x, y, r) for r in (1, 2, 3)]

        def rows(px, py, pc):
            return out_ref.at[pl.ds(pl.multiple_of((4 * px + 2 * py + pc) * M, 8), M), :]

        def copy(k, block, to, src=None):
            return pltpu.make_async_remote_copy(src_ref=rows(*block) if src is None else src, dst_ref=rows(*block),
                                                send_sem=send_sems.at[k], recv_sem=recv_sems.at[k], device_id=to, device_id_type=MESH)

        mine = pltpu.make_async_copy(x_ref, rows(*me), local_sem)
        mine.start()
        first = [copy(0, me, sibling, src=x_ref)]
        first += [copy(1 + j, me, (*chip, c), src=x_ref) for j, chip in enumerate(chips)]
        for cp in first:
            cp.start()
        passed = [copy(4 + j, (*chip, c), sibling) for j, chip in enumerate(chips)]
        for j, chip in enumerate(chips):
            copy(1 + j, (*chip, c), me).wait_recv()
            passed[j].start()
        copy(0, sibling, me).wait_recv()
        for j, chip in enumerate(chips):
            copy(4 + j, (*chip, 1 - c), me).wait_recv()
        for cp in first + passed:
            cp.wait_send()
        mine.wait()
        if reduce:
            tot = out_ref[pl.ds(0, M), :]
            for p in range(1, N_DEV):
                tot = tot + out_ref[pl.ds(p * M, M), :]
            sum_ref[...] = tot

    vm = pl.BlockSpec(memory_space=pltpu.VMEM)
    gathered = jax.ShapeDtypeStruct((N_DEV * M, N), v.dtype)
    outs = pl.pallas_call(
        body, name="all_reduce_small" if reduce else "all_gather_small", in_specs=[vm],
        out_specs=[vm, vm] if reduce else vm,
        out_shape=[gathered, jax.ShapeDtypeStruct((M, N), v.dtype)] if reduce else gathered,
        scratch_shapes=[pltpu.SemaphoreType.DMA((7,)), pltpu.SemaphoreType.DMA((7,)), pltpu.SemaphoreType.DMA],
        compiler_params=_cp(),
    )(v)
    return outs[1] if reduce else outs


def _adamw(w, g, m, v):
    shape = w.shape
    cols = shape[-1]
    rows = w.size // cols
    tr = _row_tile(rows, 256) if rows % 8 == 0 else rows
    c1 = 1.0 / (1.0 - ADAM_B1 ** ADAM_STEP)
    c2 = 1.0 / (1.0 - ADAM_B2 ** ADAM_STEP)

    def body(w_ref, g_ref, m_ref, v_ref, d_ref, nm_ref, nv_ref):
        gv = g_ref[...]
        nm = ADAM_B1 * m_ref[...] + (1.0 - ADAM_B1) * gv
        nv = ADAM_B2 * v_ref[...] + (1.0 - ADAM_B2) * (gv * gv)
        d_ref[...] = -ADAM_LR * ((nm * c1) / (jnp.sqrt(nv * c2) + ADAM_EPS) + ADAM_WD * w_ref[...])
        nm_ref[...] = nm
        nv_ref[...] = nv

    blk = pl.BlockSpec((tr, cols), lambda i: (i, 0))
    outs = pl.pallas_call(
        body, name="adamw", grid=(rows // tr,), in_specs=[blk] * 4, out_specs=[blk] * 3,
        out_shape=[jax.ShapeDtypeStruct((rows, cols), F32)] * 3, compiler_params=_cp(("parallel",)),
    )(*[a.reshape(rows, cols) for a in (w, g, m, v)])
    return tuple(o.reshape(shape) for o in outs)


def _relu2(acc):
    r = jnp.maximum(acc, 0.0)
    return acc, r * r


def _relu2_bwd(acc, up):
    return (acc * (2.0 * jnp.maximum(up.astype(F32), 0.0)),)


def kernel(x, ln_mix_pre, ln_mix_post, ln_mlp_pre, ln_mlp_post, w_in, conv_a_w, proj_a, proj_b, conv_c_w, conv_c_b, norm_c_g, norm_c_b, proj_c, w_o, w_up, w_down, loss_target, m_ln_mix_pre, m_ln_mix_post, m_ln_mlp_pre, m_ln_mlp_post, m_w_in, m_conv_a_w, m_proj_a, m_proj_b, m_conv_c_w, m_conv_c_b, m_norm_c_g, m_norm_c_b, m_proj_c, m_w_o, m_w_up, m_w_down, v_ln_mix_pre, v_ln_mix_post, v_ln_mlp_pre, v_ln_mlp_post, v_w_in, v_conv_a_w, v_proj_a, v_proj_b, v_conv_c_w, v_conv_c_b, v_norm_c_g, v_norm_c_b, v_proj_c, v_w_o, v_w_up, v_w_down):
    L, D, n_in_loc = w_in.shape
    S = x.shape[1]
    U = D // 4
    N_IN = n_in_loc * N_DEV
    D_FF = w_up.shape[2] * N_DEV
    assert N_IN == 23 * U and x.shape[0] == 1
    dloc = D // N_DEV
    floc = D_FF // N_DEV
    seg_rows = (dloc, dloc, floc, floc, n_in_loc)
    (o_p, _), (o_o, _), (o_up, _), (o_dn, _), (o_in, _) = _segments(seg_rows)[0]
    R_IN, R_P, R_O, R_UP, R_DN = (N_DEV * o for o in (o_in, o_p, o_o, o_up, o_dn))
    x_i, y_i, c_i = _place()
    me = 4 * x_i + 2 * y_i + c_i

    segs, r_loc = _segments(seg_rows)

    def gather_start(l, zero, carry):
        p_t = jnp.concatenate([proj_b[l].T, proj_a[l].T, proj_c[l].T], axis=1)
        loc = jnp.concatenate([p_t, w_o[l] + zero, w_up[l].T, w_down[l], w_in[l].T], axis=0).astype(WIRE_DTYPE)
        wb0 = lax.empty((N_DEV * r_loc, D), WIRE_DTYPE)
        for off, rows in segs:
            wb0 = lax.dynamic_update_slice(wb0, loc[off:off + rows], (N_DEV * off + me * rows, 0))
        return _ag_start(loc, wb0, carry, seg_rows, f"ag_start_{l}")

    def gather_end(l, started, after):
        send_sems, recv_sems, loc, wb0, _ = started
        return _ag_finish(_ag_wait(loc, wb0, send_sems, recv_sems, after, seg_rows, f"ag_wait_{l}"), seg_rows)

    started = gather_start(0, jnp.zeros((), F32), jnp.zeros((8, LANE), F32))
    w_next, tok = gather_end(0, started, started[4])
    wb = []
    cu = U // N_DEV
    conv_loc = jnp.concatenate([conv_a_w, conv_c_w], axis=1).reshape(L * (SC_WIDTH + CF_WIDTH), cu)
    conv_all = _all_gather_small(conv_loc, False).reshape(N_DEV, L, SC_WIDTH + CF_WIDTH, cu)
    conv_all = conv_all.transpose(1, 2, 0, 3).reshape(L, SC_WIDTH + CF_WIDTH, U)
    wa_full, wc_full = conv_all[:, :SC_WIDTH], conv_all[:, SC_WIDTH:]

    def vec(p, l):
        return p[l][None, :]

    xs = x[0]
    saved = []
    h1 = _rms_fwd(xs, vec(ln_mix_pre, 0), "rms_fwd")
    for l in range(L):
        W = w_next.astype(MXU_DTYPE)
        wb.append(W)
        if l + 1 < L:
            started = gather_start(l + 1, tok[0, 0], h1)
            h1 = started[4]
        proj = _mm(h1, W, "NT", (F32,), "mm_proj", b_view=(R_IN, 0, N_IN, D))
        a_out = _branch_a_fwd(proj, wa_full[l], U)
        u1 = _branch_c_conv_fwd(proj, wc_full[l], vec(conv_c_b, l), U)
        u3 = _branch_c_norm_fwd(u1, vec(norm_c_g, l), vec(norm_c_b, l))
        att, att_f32 = _attn_fwd(proj, U)
        yb = _mm(att, W, "NT", (F32,), "mm_yb", b_view=(R_P, 0, D, 2 * U))
        ya = _mm(a_out, W, "NT", (F32,), "mm_ya", b_view=(R_P, 2 * U, D, U))
        yc = _mm(u3, W, "NT", (F32,), "mm_yc", b_view=(R_P, 3 * U, D, U))
        merged = _gate_fwd(proj, ya, yb, yc)
        mixed = _mm(merged, W, "NN", (F32,), "mm_mixed", b_view=(R_O, 0, D, D))
        x1, h2 = _resid_post(xs, mixed, vec(ln_mix_post, l), vec(ln_mlp_pre, l), "resid_post_mix")
        up, act = _mm(h2, W, "NT", (MXU_DTYPE, MXU_DTYPE), "mm_up", b_view=(R_UP, 0, D_FF, D), epilogue=_relu2)
        f = _mm(act, W, "NN", (F32,), "mm_down", b_view=(R_DN, 0, D_FF, D))
        saved.append((xs, h1, proj, a_out, u1, u3, att, att_f32, ya, yb, yc, merged, mixed, x1, h2, up, act, f))
        if l + 1 < L:
            xs, h1 = _resid_post(x1, f, vec(ln_mlp_post, l), vec(ln_mix_pre, l + 1), "resid_post_mlp")
            w_next, tok = gather_end(l + 1, started, h1)
        else:
            xs, _ = _resid_post(x1, f, vec(ln_mlp_post, l), None, "resid_post_last")
    dxo, loss_row = _loss_head(xs, loss_target[0])
    loss = lax.psum(loss_row[0, 0], ("x", "y", "c"))

    small = {k: [None] * L for k in ("g1", "g2", "g3", "g4", "cb", "ng", "nb", "wa", "wc")}
    g_shard = [None] * L
    in_flight = None
    for l in reversed(range(L)):
        W = wb[l]
        xs, h1, proj, a_out, u1, u3, att, att_f32, ya, yb, yc, merged, mixed, x1, h2, up, act, f = saved[l]
        df, small["g4"][l] = _rms_bwd(f, vec(ln_mlp_post, l), dxo, None, MXU_DTYPE, "rms_bwd_post_mlp")
        d_up = _mm(df, W, "NT", (MXU_DTYPE,), "mm_d_up", b_view=(R_DN, 0, D_FF, D), epilogue=_relu2_bwd, extras=(up,))
        g_dn = _mm(act, df, "TN", (WIRE_DTYPE,), "mm_g_down", tm=512, tn=2048)
        dh2 = _mm(d_up, W, "NN", (F32,), "mm_dh2", b_view=(R_UP, 0, D_FF, D))
        g_up = _mm(d_up, h2, "TN", (WIRE_DTYPE,), "mm_g_up", tm=512, tn=2048)
        dx1, small["g3"][l] = _rms_bwd(x1, vec(ln_mlp_pre, l), dh2, dxo, F32, "rms_bwd_pre_mlp")
        dmixed, small["g2"][l] = _rms_bwd(mixed, vec(ln_mix_post, l), dx1, None, MXU_DTYPE, "rms_bwd_post_mix")
        dmerged = _mm(dmixed, W, "NT", (F32,), "mm_dmerged", b_view=(R_O, 0, D, D))
        g_o = _mm(merged, dmixed, "TN", (WIRE_DTYPE,), "mm_g_o", tm=512, tn=2048)
        dya, dyb, dyc, dgla, dglb, dglc = _gate_bwd(dmerged, proj, ya, yb, yc)
        d_att = _mm(dyb, W, "NN", (F32,), "mm_d_att", b_view=(R_P, 0, D, 2 * U))
        d_a_out = _mm(dya, W, "NN", (F32,), "mm_d_a_out", b_view=(R_P, 2 * U, D, U))
        d_u3 = _mm(dyc, W, "NN", (F32,), "mm_d_u3", b_view=(R_P, 3 * U, D, U))
        g_pb = _mm(dyb, att, "TN", (WIRE_DTYPE,), "mm_g_pb", tm=512, tn=2048)
        g_pa = _mm(dya, a_out, "TN", (WIRE_DTYPE,), "mm_g_pa", tm=512, tn=2048)
        g_pc = _mm(dyc, u3, "TN", (WIRE_DTYPE,), "mm_g_pc", tm=512, tn=2048)
        d_scb, d_scc, d_scu, small["wa"][l] = _branch_a_bwd(d_a_out, proj, wa_full[l], U)
        d_u1, small["ng"][l], small["nb"][l] = _branch_c_norm_bwd(d_u3, u1, vec(norm_c_g, l), vec(norm_c_b, l))
        d_cfa, d_cfg, small["wc"][l], small["cb"][l] = _branch_c_conv_bwd(d_u1, proj, wc_full[l], U)
        dq, dk, dv = _attn_bwd(proj, att_f32, d_att, U)
        dproj = jnp.concatenate([dq, dk, dv, d_scb, d_scc, d_scu, d_cfa, d_cfg, dgla, dglb, dglc], axis=1)
        dh1 = _mm(dproj, W, "NN", (F32,), "mm_dh1", b_view=(R_IN, 0, N_IN, D), tk=512, tn=1024)
        g_in = _mm(dproj, h1, "TN", (WIRE_DTYPE,), "mm_g_in", tm=512, tn=2048)
        dxo, small["g1"][l] = _rms_bwd(xs, vec(ln_mix_pre, l), dh1, dx1, F32, "rms_bwd_pre_mix")
        if in_flight is not None:
            g_shard[l + 1] = _final_sum(*_rs_wait(*in_flight, dxo, f"rs_wait_{l + 1}"))
        grads = jnp.concatenate([jnp.concatenate([g_pb, g_pa, g_pc], axis=1), g_o, g_up, g_dn, g_in], axis=0)
        pair_sums = _pair_add(grads, _rs_pair(grads, seg_rows), seg_rows)
        carry = dxo if l > 0 else jnp.zeros((8, LANE), F32)
        send_sems, recv_sems, pair_sums, land, carry = _rs_start(pair_sums, lax.empty((3, r_loc, D), WIRE_DTYPE), carry, f"rs_start_{l}")
        in_flight = (pair_sums, land, send_sems, recv_sems)
        if l > 0:
            dxo = carry
    grad_x = dxo[None]

    order = ("g1", "g2", "g3", "g4", "cb", "ng", "nb", "wa", "wc")
    parts = [jnp.stack(small[k]).reshape(-1) for k in order]
    flat = jnp.concatenate(parts)
    n_flat = flat.shape[0]
    pad = (-n_flat) % (8 * LANE)
    flat = jnp.pad(flat, (0, pad)).reshape(-1, LANE)
    tot = _all_gather_small(flat, True)
    g_shard[0] = _final_sum(*_rs_wait(*in_flight, tot, "rs_wait_0"))
    tot = tot.reshape(-1)[:n_flat]
    red, pos = {}, 0
    for k, p in zip(order, parts):
        red[k] = tot[pos:pos + p.shape[0]]
        pos += p.shape[0]
    g_ln_mix_pre, g_ln_mix_post = red["g1"].reshape(L, D), red["g2"].reshape(L, D)
    g_ln_mlp_pre, g_ln_mlp_post = red["g3"].reshape(L, D), red["g4"].reshape(L, D)
    g_conv_c_b, g_norm_c_g, g_norm_c_b = red["cb"].reshape(L, U), red["ng"].reshape(L, U), red["nb"].reshape(L, U)
    g_conv_a_w = lax.dynamic_slice_in_dim(red["wa"].reshape(L, SC_WIDTH, U), me * cu, cu, axis=2)
    g_conv_c_w = lax.dynamic_slice_in_dim(red["wc"].reshape(L, CF_WIDTH, U), me * cu, cu, axis=2)

    gs = jnp.stack(g_shard)
    g_w_in = gs[:, o_in:o_in + n_in_loc].transpose(0, 2, 1)
    g_pt = gs[:, o_p:o_p + dloc]
    g_proj_b = g_pt[:, :, :2 * U].transpose(0, 2, 1)
    g_proj_a = g_pt[:, :, 2 * U:3 * U].transpose(0, 2, 1)
    g_proj_c = g_pt[:, :, 3 * U:].transpose(0, 2, 1)
    g_w_o = gs[:, o_o:o_o + dloc]
    g_w_up = gs[:, o_up:o_up + floc].transpose(0, 2, 1)
    g_w_down = gs[:, o_dn:o_dn + floc]

    grads_w = [g_ln_mix_pre, g_ln_mix_post, g_ln_mlp_pre, g_ln_mlp_post, g_w_in, g_conv_a_w, g_proj_a, g_proj_b, g_conv_c_w,
               g_conv_c_b, g_norm_c_g, g_norm_c_b, g_proj_c, g_w_o, g_w_up, g_w_down]
    weights = [ln_mix_pre, ln_mix_post, ln_mlp_pre, ln_mlp_post, w_in, conv_a_w, proj_a, proj_b, conv_c_w, conv_c_b, norm_c_g,
               norm_c_b, proj_c, w_o, w_up, w_down]
    ms = [m_ln_mix_pre, m_ln_mix_post, m_ln_mlp_pre, m_ln_mlp_post, m_w_in, m_conv_a_w, m_proj_a, m_proj_b, m_conv_c_w,
          m_conv_c_b, m_norm_c_g, m_norm_c_b, m_proj_c, m_w_o, m_w_up, m_w_down]
    vs = [v_ln_mix_pre, v_ln_mix_post, v_ln_mlp_pre, v_ln_mlp_post, v_w_in, v_conv_a_w, v_proj_a, v_proj_b, v_conv_c_w,
          v_conv_c_b, v_norm_c_g, v_norm_c_b, v_proj_c, v_w_o, v_w_up, v_w_down]
    upd = [_adamw(w, g, m, v) for w, g, m, v in zip(weights, grads_w, ms, vs)]
    return (loss, grad_x, *grads_w, *[u[0] for u in upd], *[u[1] for u in upd], *[u[2] for u in upd])
```

```python
import functools

import jax
import jax.numpy as jnp
from jax import lax
from jax.experimental import pallas as pl
from jax.experimental.pallas import tpu as pltpu

F32 = jnp.float32
MXU_DTYPE = jnp.bfloat16
WIRE_DTYPE = jnp.bfloat16
MESH = pl.DeviceIdType.MESH
ANY = pl.BlockSpec(memory_space=pl.ANY)
HBM = pl.BlockSpec(memory_space=pltpu.HBM)
SEM = pl.BlockSpec(memory_space=pltpu.SEMAPHORE)
EFFECT = pltpu.SideEffectType.DATAFLOW_SIDE_EFFECTING

N_DEV = 8
HEAD_DIM = 128
RMS_EPS = 1e-6
LN_EPS = 1e-5
SC_WIDTH = 3
CF_WIDTH = 31
CONV_PAD = 32
ADAM_LR, ADAM_B1, ADAM_B2, ADAM_EPS, ADAM_WD, ADAM_STEP = 0.001, 0.9, 0.999, 1e-08, 0.01, 10
VMEM_LIMIT = 56 * 1024 * 1024
LANE = 128


def _cp(sem=None, **kw):
    return pltpu.CompilerParams(dimension_semantics=sem, vmem_limit_bytes=VMEM_LIMIT, **kw)


def _sigmoid(x):
    return 1.0 / (1.0 + jnp.exp(-x))


def _row_tile(rows, want):
    t = min(rows, want)
    while rows % t:
        t //= 2
    return t


_DN = {"NN": (((1,), (0,)), ((), ())), "NT": (((1,), (1,)), ((), ())), "TN": (((0,), (0,)), ((), ()))}


def _mm(a, b, mode, out_dtypes, name, *, a_view=None, b_view=None, tm=2048, tn=512, tk=2048, epilogue=None, extras=()):
    a_view = a_view or (0, 0) + tuple(a.shape)
    b_view = b_view or (0, 0) + tuple(b.shape)
    ar, ac, an, am = a_view
    br, bc, bn, bm = b_view
    if mode == "NN":
        M, K, K2, N = an, am, bn, bm
    elif mode == "NT":
        M, K, N, K2 = an, am, bn, bm
    else:
        K, M, K2, N = an, am, bn, bm
    assert K == K2, (name, a_view, b_view)
    tm, tn, tk = _row_tile(M, tm), _row_tile(N, tn), _row_tile(K, tk)
    (a_m_off, a_k_off) = (ac, ar) if mode == "TN" else (ar, ac)
    (b_n_off, b_k_off) = (br, bc) if mode == "NT" else (bc, br)
    while a_m_off % tm:
        tm //= 2
    while b_n_off % tn:
        tn //= 2
    while a_k_off % tk or b_k_off % tk:
        tk //= 2
    nk = K // tk
    a_blk = (tk, tm) if mode == "TN" else (tm, tk)
    b_blk = (tn, tk) if mode == "NT" else (tk, tn)
    assert ar % a_blk[0] == 0 and ac % a_blk[1] == 0, (name, a_view, a_blk)
    assert br % b_blk[0] == 0 and bc % b_blk[1] == 0, (name, b_view, b_blk)
    ao, bo = (ar // a_blk[0], ac // a_blk[1]), (br // b_blk[0], bc // b_blk[1])
    if mode == "TN":
        a_spec = pl.BlockSpec(a_blk, lambda i, j, k: (ao[0] + k, ao[1] + i))
    else:
        a_spec = pl.BlockSpec(a_blk, lambda i, j, k: (ao[0] + i, ao[1] + k))
    if mode == "NT":
        b_spec = pl.BlockSpec(b_blk, lambda i, j, k: (bo[0] + j, bo[1] + k))
    else:
        b_spec = pl.BlockSpec(b_blk, lambda i, j, k: (bo[0] + k, bo[1] + j))
    o_spec = pl.BlockSpec((tm, tn), lambda i, j, k: (i, j))
    n_ex, n_out = len(extras), len(out_dtypes)
    dn = _DN[mode]

    def body(*refs):
        a_ref, b_ref = refs[:2]
        ex_refs = refs[2:2 + n_ex]
        o_refs = refs[2 + n_ex:2 + n_ex + n_out]
        p = lax.dot_general(a_ref[...], b_ref[...], dn, preferred_element_type=F32)

        def finish(acc):
            outs = epilogue(acc, *[r[...] for r in ex_refs]) if epilogue else (acc,)
            for o_ref, o in zip(o_refs, outs):
                o_ref[...] = o.astype(o_ref.dtype)

        if nk == 1:
            finish(p)
        else:
            acc_ref = refs[-1]
            k = pl.program_id(2)

            @pl.when(k == 0)
            def _():
                acc_ref[...] = p

            @pl.when(k > 0)
            def _():
                acc_ref[...] += p

            @pl.when(k == nk - 1)
            def _():
                finish(acc_ref[...])

    outs = pl.pallas_call(
        body, name=name, grid=(M // tm, N // tn, nk),
        in_specs=[a_spec, b_spec] + [o_spec] * n_ex, out_specs=[o_spec] * n_out,
        out_shape=[jax.ShapeDtypeStruct((M, N), d) for d in out_dtypes],
        scratch_shapes=[pltpu.VMEM((tm, tn), F32)] if nk > 1 else [],
        compiler_params=_cp(("parallel", "parallel", "arbitrary")),
    )(a, b, *extras)
    return outs[0] if n_out == 1 else outs


def _rms_fwd(x, g, name):
    S, D = x.shape
    tr = _row_tile(S, 256)

    def body(x_ref, g_ref, h_ref):
        xv = x_ref[...]
        r = lax.rsqrt(jnp.mean(xv * xv, axis=-1, keepdims=True) + RMS_EPS)
        h_ref[...] = ((xv * r) * g_ref[...]).astype(h_ref.dtype)

    return pl.pallas_call(
        body, name=name, grid=(S // tr,),
        in_specs=[pl.BlockSpec((tr, D), lambda i: (i, 0)), pl.BlockSpec((1, D), lambda i: (0, 0))],
        out_specs=pl.BlockSpec((tr, D), lambda i: (i, 0)),
        out_shape=jax.ShapeDtypeStruct((S, D), MXU_DTYPE), compiler_params=_cp(("parallel",)),
    )(x, g)


def _resid_post(xres, y, g_post, g_next, name):
    S, D = y.shape
    tr = _row_tile(S, 256)
    has_next = g_next is not None

    def body(*refs):
        xr_ref, y_ref, gp_ref = refs[:3]
        yv = y_ref[...]
        r = lax.rsqrt(jnp.mean(yv * yv, axis=-1, keepdims=True) + RMS_EPS)
        xn = xr_ref[...] + (yv * r) * gp_ref[...]
        if has_next:
            gn_ref, xo_ref, h_ref = refs[3:]
            r2 = lax.rsqrt(jnp.mean(xn * xn, axis=-1, keepdims=True) + RMS_EPS)
            h_ref[...] = ((xn * r2) * gn_ref[...]).astype(h_ref.dtype)
        else:
            xo_ref = refs[3]
        xo_ref[...] = xn

    row = pl.BlockSpec((tr, D), lambda i: (i, 0))
    vec = pl.BlockSpec((1, D), lambda i: (0, 0))
    outs = pl.pallas_call(
        body, name=name, grid=(S // tr,),
        in_specs=[row, row, vec] + ([vec] if has_next else []),
        out_specs=[row] + ([row] if has_next else []),
        out_shape=[jax.ShapeDtypeStruct((S, D), F32)] + ([jax.ShapeDtypeStruct((S, D), MXU_DTYPE)] if has_next else []),
        compiler_params=_cp(("parallel",)),
    )(xres, y, g_post, *([g_next] if has_next else []))
    return (outs[0], outs[1]) if has_next else (outs[0], None)


def _rms_bwd(xin, g, dy, dres, out_dtype, name):
    S, D = xin.shape
    tr = _row_tile(S, 256)
    has_res = dres is not None

    def body(*refs):
        x_ref, g_ref, dy_ref = refs[:3]
        dx_ref, dg_ref = refs[-2:]
        xv, dyv = x_ref[...], dy_ref[...].astype(F32)
        r = lax.rsqrt(jnp.mean(xv * xv, axis=-1, keepdims=True) + RMS_EPS)
        n = xv * r
        dyg = dyv * g_ref[...]
        dx = r * (dyg - n * jnp.mean(dyg * n, axis=-1, keepdims=True))
        if has_res:
            dx = dx + refs[3][...]
        dx_ref[...] = dx.astype(dx_ref.dtype)

        @pl.when(pl.program_id(0) == 0)
        def _():
            dg_ref[...] = jnp.zeros_like(dg_ref)

        dg_ref[...] += jnp.sum(dyv * n, axis=0, keepdims=True)

    row = pl.BlockSpec((tr, D), lambda i: (i, 0))
    vec = pl.BlockSpec((1, D), lambda i: (0, 0))
    return pl.pallas_call(
        body, name=name, grid=(S // tr,),
        in_specs=[row, vec, row] + ([row] if has_res else []), out_specs=[row, vec],
        out_shape=[jax.ShapeDtypeStruct((S, D), out_dtype), jax.ShapeDtypeStruct((1, D), F32)],
        compiler_params=_cp(("arbitrary",)),
    )(xin, g, dy, *([dres] if has_res else []))


def _loss_head(y, target):
    S, D = y.shape
    tr = _row_tile(S, 256)

    def body(y_ref, t_ref, dy_ref, l_ref):
        e = y_ref[...] - t_ref[...]
        dy_ref[...] = e * (1.0 / D)

        @pl.when(pl.program_id(0) == 0)
        def _():
            l_ref[...] = jnp.zeros_like(l_ref)

        l_ref[...] += 0.5 * jnp.sum(jnp.mean(e * e, axis=-1, keepdims=True), axis=0, keepdims=True)

    row = pl.BlockSpec((tr, D), lambda i: (i, 0))
    return pl.pallas_call(
        body, name="loss_head", grid=(S // tr,), in_specs=[row, row],
        out_specs=[row, pl.BlockSpec((1, LANE), lambda i: (0, 0))],
        out_shape=[jax.ShapeDtypeStruct((S, D), F32), jax.ShapeDtypeStruct((1, LANE), F32)],
        compiler_params=_cp(("arbitrary",)),
    )(y, target)


def _gate_specs(S, U, tr):
    gl = [pl.BlockSpec((tr, U), functools.partial(lambda i, j, o: (i, o + j), o=o)) for o in (11, 15, 19)]
    return gl, pl.BlockSpec((tr, U), lambda i, j: (i, j))


def _gate_fwd(proj, ya, yb, yc):
    S, D = ya.shape
    U = D // 4
    tr = _row_tile(S, 256)

    def body(ga_ref, gb_ref, gc_ref, ya_ref, yb_ref, yc_ref, o_ref):
        m = _sigmoid(ga_ref[...]) * ya_ref[...] + _sigmoid(gb_ref[...]) * yb_ref[...] + _sigmoid(gc_ref[...]) * yc_ref[...]
        o_ref[...] = m.astype(o_ref.dtype)

    gl, blk = _gate_specs(S, U, tr)
    return pl.pallas_call(
        body, name="gate_fwd", grid=(S // tr, 4), in_specs=gl + [blk] * 3, out_specs=blk,
        out_shape=jax.ShapeDtypeStruct((S, D), MXU_DTYPE), compiler_params=_cp(("parallel", "parallel")),
    )(proj, proj, proj, ya, yb, yc)


def _gate_bwd(dm, proj, ya, yb, yc):
    S, D = ya.shape
    U = D // 4
    tr = _row_tile(S, 256)

    def body(dm_ref, ga_ref, gb_ref, gc_ref, ya_ref, yb_ref, yc_ref, da_ref, db_ref, dc_ref, la_ref, lb_ref, lc_ref):
        d = dm_ref[...]
        for g_ref, y_ref, dy_ref, dl_ref in ((ga_ref, ya_ref, da_ref, la_ref), (gb_ref, yb_ref, db_ref, lb_ref),
                                             (gc_ref, yc_ref, dc_ref, lc_ref)):
            g = _sigmoid(g_ref[...])
            dy_ref[...] = (d * g).astype(dy_ref.dtype)
            dl_ref[...] = (d * y_ref[...] * g * (1.0 - g)).astype(dl_ref.dtype)

    gl, blk = _gate_specs(S, U, tr)
    return pl.pallas_call(
        body, name="gate_bwd", grid=(S // tr, 4), in_specs=[blk] + gl + [blk] * 3, out_specs=[blk] * 6,
        out_shape=[jax.ShapeDtypeStruct((S, D), MXU_DTYPE)] * 6, compiler_params=_cp(("parallel", "parallel")),
    )(dm, proj, proj, proj, ya, yb, yc)


def _chunks(S):
    r = _row_tile(S, 256)
    return [(r0, r) for r0 in range(0, S, r)]


def _conv_causal(front_ref, w_ref, K, r0, R):
    acc = None
    for j in range(K):
        term = w_ref[pl.ds(K - 1 - j, 1), :] * front_ref[pl.ds(CONV_PAD + r0 - j, R), :]
        acc = term if acc is None else acc + term
    return acc


def _conv_anticausal(back_ref, w_ref, K, r0, R):
    acc = None
    for j in range(K):
        term = w_ref[pl.ds(K - 1 - j, 1), :] * back_ref[pl.ds(r0 + j, R), :]
        acc = term if acc is None else acc + term
    return acc


def _conv_wgrad(front_ref, back_ref, dw_ref, K, S):
    for j in range(K):
        tot = None
        for r0, R in _chunks(S):
            part = jnp.sum(back_ref[pl.ds(r0, R), :] * front_ref[pl.ds(CONV_PAD + r0 - j, R), :], axis=0, keepdims=True)
            tot = part if tot is None else tot + part
        dw_ref[pl.ds(K - 1 - j, 1), :] = tot


def _col(S, cw, unit_off):
    return pl.BlockSpec((S, cw), functools.partial(lambda cb, o: (0, o + cb), o=unit_off))


def _branch_a_fwd(proj, wa, U):
    S = proj.shape[0]
    cw = min(LANE, U)
    nb = U // cw
    K = SC_WIDTH

    def body(b_ref, c_ref, u_ref, w_ref, o_ref, front):
        front[pl.ds(0, CONV_PAD), :] = jnp.zeros((CONV_PAD, cw), F32)
        front[pl.ds(CONV_PAD, S), :] = c_ref[...] * u_ref[...]
        for r0, R in _chunks(S):
            o_ref[pl.ds(r0, R), :] = (b_ref[pl.ds(r0, R), :] * _conv_causal(front, w_ref, K, r0, R)).astype(o_ref.dtype)

    return pl.pallas_call(
        body, name="branch_a_fwd", grid=(nb,),
        in_specs=[_col(S, cw, 6 * nb), _col(S, cw, 7 * nb), _col(S, cw, 8 * nb), pl.BlockSpec((K, cw), lambda cb: (0, cb))],
        out_specs=pl.BlockSpec((S, cw), lambda cb: (0, cb)), out_shape=jax.ShapeDtypeStruct((S, U), MXU_DTYPE),
        scratch_shapes=[pltpu.VMEM((S + CONV_PAD, cw), F32)], compiler_params=_cp(("parallel",)),
    )(proj, proj, proj, wa)


def _branch_a_bwd(d_out, proj, wa, U):
    S = proj.shape[0]
    cw = min(LANE, U)
    nb = U // cw
    K = SC_WIDTH

    def body(d_ref, b_ref, c_ref, u_ref, w_ref, db_ref, dc_ref, du_ref, dw_ref, front, back):
        front[pl.ds(0, CONV_PAD), :] = jnp.zeros((CONV_PAD, cw), F32)
        front[pl.ds(CONV_PAD, S), :] = c_ref[...] * u_ref[...]
        back[pl.ds(S, CONV_PAD), :] = jnp.zeros((CONV_PAD, cw), F32)
        back[pl.ds(0, S), :] = d_ref[...] * b_ref[...]
        for r0, R in _chunks(S):
            rows = pl.ds(r0, R)
            db_ref[rows, :] = (d_ref[rows, :] * _conv_causal(front, w_ref, K, r0, R)).astype(db_ref.dtype)
            d_ai = _conv_anticausal(back, w_ref, K, r0, R)
            dc_ref[rows, :] = (d_ai * u_ref[rows, :]).astype(dc_ref.dtype)
            du_ref[rows, :] = (d_ai * c_ref[rows, :]).astype(du_ref.dtype)
        _conv_wgrad(front, back, dw_ref, K, S)

    blk = pl.BlockSpec((S, cw), lambda cb: (0, cb))
    wblk = pl.BlockSpec((K, cw), lambda cb: (0, cb))
    return pl.pallas_call(
        body, name="branch_a_bwd", grid=(nb,),
        in_specs=[blk, _col(S, cw, 6 * nb), _col(S, cw, 7 * nb), _col(S, cw, 8 * nb), wblk],
        out_specs=[blk, blk, blk, wblk],
        out_shape=[jax.ShapeDtypeStruct((S, U), MXU_DTYPE)] * 3 + [jax.ShapeDtypeStruct((K, U), F32)],
        scratch_shapes=[pltpu.VMEM((S + CONV_PAD, cw), F32)] * 2, compiler_params=_cp(("parallel",)),
    )(d_out, proj, proj, proj, wa)


def _branch_c_conv_fwd(proj, wc, cb, U):
    S = proj.shape[0]
    cw = min(LANE, U)
    nb = U // cw
    K = CF_WIDTH

    def body(a_ref, g_ref, w_ref, bias_ref, o_ref, front):
        front[pl.ds(0, CONV_PAD), :] = jnp.zeros((CONV_PAD, cw), F32)
        front[pl.ds(CONV_PAD, S), :] = a_ref[...] * _sigmoid(g_ref[...])
        for r0, R in _chunks(S):
            o_ref[pl.ds(r0, R), :] = _conv_causal(front, w_ref, K, r0, R) + bias_ref[...]

    return pl.pallas_call(
        body, name="branch_c_conv_fwd", grid=(nb,),
        in_specs=[_col(S, cw, 9 * nb), _col(S, cw, 10 * nb), pl.BlockSpec((K, cw), lambda c: (0, c)),
                  pl.BlockSpec((1, cw), lambda c: (0, c))],
        out_specs=pl.BlockSpec((S, cw), lambda c: (0, c)), out_shape=jax.ShapeDtypeStruct((S, U), F32),
        scratch_shapes=[pltpu.VMEM((S + CONV_PAD, cw), F32)], compiler_params=_cp(("parallel",)),
    )(proj, proj, wc, cb)


def _branch_c_conv_bwd(d_u1, proj, wc, U):
    S = proj.shape[0]
    cw = min(LANE, U)
    nb = U // cw
    K = CF_WIDTH

    def body(d_ref, a_ref, g_ref, w_ref, da_ref, dg_ref, dw_ref, dbias_ref, front, back):
        sg = _sigmoid(g_ref[...])
        front[pl.ds(0, CONV_PAD), :] = jnp.zeros((CONV_PAD, cw), F32)
        front[pl.ds(CONV_PAD, S), :] = a_ref[...] * sg
        back[pl.ds(S, CONV_PAD), :] = jnp.zeros((CONV_PAD, cw), F32)
        back[pl.ds(0, S), :] = d_ref[...]
        dbias_ref[...] = jnp.sum(d_ref[...], axis=0, keepdims=True)
        for r0, R in _chunks(S):
            rows = pl.ds(r0, R)
            d_u0 = _conv_anticausal(back, w_ref, K, r0, R)
            s = _sigmoid(g_ref[rows, :])
            da_ref[rows, :] = (d_u0 * s).astype(da_ref.dtype)
            dg_ref[rows, :] = (d_u0 * a_ref[rows, :] * s * (1.0 - s)).astype(dg_ref.dtype)
        _conv_wgrad(front, back, dw_ref, K, S)

    blk = pl.BlockSpec((S, cw), lambda c: (0, c))
    wblk = pl.BlockSpec((K, cw), lambda c: (0, c))
    vblk = pl.BlockSpec((1, cw), lambda c: (0, c))
    return pl.pallas_call(
        body, name="branch_c_conv_bwd", grid=(nb,),
        in_specs=[blk, _col(S, cw, 9 * nb), _col(S, cw, 10 * nb), wblk], out_specs=[blk, blk, wblk, vblk],
        out_shape=[jax.ShapeDtypeStruct((S, U), MXU_DTYPE)] * 2 + [jax.ShapeDtypeStruct((K, U), F32), jax.ShapeDtypeStruct((1, U), F32)],
        scratch_shapes=[pltpu.VMEM((S + CONV_PAD, cw), F32)] * 2, compiler_params=_cp(("parallel",)),
    )(d_u1, proj, proj, wc)


def _branch_c_norm_fwd(u1, ng, nbias):
    S, U = u1.shape
    tr = _row_tile(S, 256)

    def body(u_ref, g_ref, b_ref, o_ref):
        u = u_ref[...]
        mu = jnp.mean(u, axis=-1, keepdims=True)
        var = jnp.mean(jnp.square(u - mu), axis=-1, keepdims=True)
        u2 = ((u - mu) * lax.rsqrt(var + LN_EPS)) * g_ref[...] + b_ref[...]
        o_ref[...] = (u2 * _sigmoid(u2)).astype(o_ref.dtype)

    row = pl.BlockSpec((tr, U), lambda i: (i, 0))
    vec = pl.BlockSpec((1, U), lambda i: (0, 0))
    return pl.pallas_call(
        body, name="branch_c_norm_fwd", grid=(S // tr,), in_specs=[row, vec, vec], out_specs=row,
        out_shape=jax.ShapeDtypeStruct((S, U), MXU_DTYPE), compiler_params=_cp(("parallel",)),
    )(u1, ng, nbias)


def _branch_c_norm_bwd(d_u3, u1, ng, nbias):
    S, U = u1.shape
    tr = _row_tile(S, 256)

    def body(d_ref, u_ref, g_ref, b_ref, du_ref, dg_ref, db_ref):
        u = u_ref[...]
        mu = jnp.mean(u, axis=-1, keepdims=True)
        var = jnp.mean(jnp.square(u - mu), axis=-1, keepdims=True)
        rstd = lax.rsqrt(var + LN_EPS)
        xh = (u - mu) * rstd
        u2 = xh * g_ref[...] + b_ref[...]
        s = _sigmoid(u2)
        d_u2 = d_ref[...] * (s * (1.0 + u2 * (1.0 - s)))
        d_xh = d_u2 * g_ref[...]
        du_ref[...] = rstd * (d_xh - jnp.mean(d_xh, axis=-1, keepdims=True) - xh * jnp.mean(d_xh * xh, axis=-1, keepdims=True))

        @pl.when(pl.program_id(0) == 0)
        def _():
            dg_ref[...] = jnp.zeros_like(dg_ref)
            db_ref[...] = jnp.zeros_like(db_ref)

        dg_ref[...] += jnp.sum(d_u2 * xh, axis=0, keepdims=True)
        db_ref[...] += jnp.sum(d_u2, axis=0, keepdims=True)

    row = pl.BlockSpec((tr, U), lambda i: (i, 0))
    vec = pl.BlockSpec((1, U), lambda i: (0, 0))
    return pl.pallas_call(
        body, name="branch_c_norm_bwd", grid=(S // tr,), in_specs=[row, row, vec, vec], out_specs=[row, vec, vec],
        out_shape=[jax.ShapeDtypeStruct((S, U), F32), jax.ShapeDtypeStruct((1, U), F32), jax.ShapeDtypeStruct((1, U), F32)],
        compiler_params=_cp(("arbitrary",)),
    )(d_u3, u1, ng, nbias)


def _tri(T, inclusive):
    j = lax.broadcasted_iota(jnp.int32, (T, T), 0)
    s = lax.broadcasted_iota(jnp.int32, (T, T), 1)
    return ((j >= s) if inclusive else (j > s)).astype(MXU_DTYPE)


def _split_dot(x, tri):
    if MXU_DTYPE == F32:
        return jnp.dot(x, tri, preferred_element_type=F32)
    hi = x.astype(MXU_DTYPE)
    lo = (x - hi.astype(F32)).astype(MXU_DTYPE)
    return jnp.dot(hi, tri, preferred_element_type=F32) + jnp.dot(lo, tri, preferred_element_type=F32)


def _sb_block(qb, kb, i, j, T, tri_strict, c_lf):
    z = lax.dot_general(qb, kb, _DN["NT"], preferred_element_type=F32) * (HEAD_DIM ** -0.5)
    t_idx = i * T + lax.broadcasted_iota(jnp.int32, (T, T), 0)
    s_idx = j * T + lax.broadcasted_iota(jnp.int32, (T, T), 1)
    mask = s_idx < t_idx
    e = jnp.exp(-jnp.abs(z))
    lg = jnp.log(1.0 + e)
    log_beta = jnp.minimum(z, 0.0) - lg
    lf = jnp.where(mask, jnp.minimum(-z, 0.0) - lg, 0.0)
    log_a = log_beta + _split_dot(lf, tri_strict) + c_lf
    a = jnp.where(mask, jnp.exp(log_a), 0.0)
    return z, e, mask, lf, a


def _attn_specs(S, U, h_blocks):
    nh = (2 * U) // HEAD_DIM
    return [pl.BlockSpec((S, HEAD_DIM), functools.partial(lambda h, o: (0, o + h), o=o * nh)) for o in range(h_blocks)]


def _attn_fwd(proj, U):
    S = proj.shape[0]
    nh = (2 * U) // HEAD_DIM
    T = _row_tile(S, 256)
    nq = S // T

    def body(q_ref, k_ref, v_ref, o_ref, of_ref, qs, ks, vs):
        qs[...] = q_ref[...].astype(MXU_DTYPE)
        ks[...] = k_ref[...].astype(MXU_DTYPE)
        vs[...] = v_ref[...].astype(MXU_DTYPE)
        tri = _tri(T, False)

        def q_loop(i, _):
            rows = pl.ds(pl.multiple_of(i * T, T), T)
            qb = qs[rows, :]

            def k_loop(jj, carry):
                c_lf, acc = carry
                j = i - jj
                cols = pl.ds(pl.multiple_of(j * T, T), T)
                _, _, _, lf, a = _sb_block(qb, ks[cols, :], i, j, T, tri, c_lf)
                acc = acc + jnp.dot(a.astype(MXU_DTYPE), vs[cols, :], preferred_element_type=F32)
                return c_lf + jnp.sum(lf, axis=1, keepdims=True), acc

            _, acc = lax.fori_loop(0, i + 1, k_loop, (jnp.zeros((T, 1), F32), jnp.zeros((T, HEAD_DIM), F32)))
            o_ref[rows, :] = acc.astype(o_ref.dtype)
            of_ref[rows, :] = acc
            return 0

        lax.fori_loop(0, nq, q_loop, 0)

    hblk = pl.BlockSpec((S, HEAD_DIM), lambda h: (0, h))
    return pl.pallas_call(
        body, name="attn_fwd", grid=(nh,), in_specs=_attn_specs(S, U, 3), out_specs=[hblk, hblk],
        out_shape=[jax.ShapeDtypeStruct((S, 2 * U), MXU_DTYPE), jax.ShapeDtypeStruct((S, 2 * U), F32)],
        scratch_shapes=[pltpu.VMEM((S, HEAD_DIM), MXU_DTYPE)] * 3, compiler_params=_cp(("parallel",)),
    )(proj, proj, proj)


def _attn_bwd(proj, att_f32, d_att, U):
    S = proj.shape[0]
    nh = (2 * U) // HEAD_DIM
    T = _row_tile(S, 256)
    nq = S // T
    scale = HEAD_DIM ** -0.5

    def body(q_ref, k_ref, v_ref, o_ref, do_ref, dq_ref, dk_ref, dv_ref, qs, ks, vs, dos, dka, dva):
        qs[...] = q_ref[...].astype(MXU_DTYPE)
        ks[...] = k_ref[...].astype(MXU_DTYPE)
        vs[...] = v_ref[...].astype(MXU_DTYPE)
        dos[...] = do_ref[...].astype(MXU_DTYPE)
        dka[...] = jnp.zeros_like(dka)
        dva[...] = jnp.zeros_like(dva)
        tri = _tri(T, False)
        tri_inc = _tri(T, True)

        def q_loop(i, _):
            rows = pl.ds(pl.multiple_of(i * T, T), T)
            qb = qs[rows, :]
            dob = dos[rows, :]
            delta = jnp.sum(dob.astype(F32) * o_ref[rows, :], axis=1, keepdims=True)

            def k_loop(jj, carry):
                c_lf, c_g, dq = carry
                j = i - jj
                cols = pl.ds(pl.multiple_of(j * T, T), T)
                kb, vb = ks[cols, :], vs[cols, :]
                z, e, mask, lf, a = _sb_block(qb, kb, i, j, T, tri, c_lf)
                a_mx = a.astype(MXU_DTYPE)
                d_a = lax.dot_general(dob, vb, _DN["NT"], preferred_element_type=F32)
                g = a_mx.astype(F32) * d_a
                prefix = delta - (_split_dot(g, tri_inc) + c_g)
                inv = 1.0 / (1.0 + e)
                beta = jnp.where(z >= 0.0, 1.0, e) * inv
                one_m_beta = jnp.where(z >= 0.0, e, 1.0) * inv
                dz = jnp.where(mask, (g * one_m_beta - prefix * beta) * scale, 0.0).astype(MXU_DTYPE)
                dq = dq + jnp.dot(dz, kb, preferred_element_type=F32)
                dka[cols, :] += lax.dot_general(dz, qb, _DN["TN"], preferred_element_type=F32)
                dva[cols, :] += lax.dot_general(a_mx, dob, _DN["TN"], preferred_element_type=F32)
                return c_lf + jnp.sum(lf, axis=1, keepdims=True), c_g + jnp.sum(g, axis=1, keepdims=True), dq

            zero = jnp.zeros((T, 1), F32)
            _, _, dq = lax.fori_loop(0, i + 1, k_loop, (zero, zero, jnp.zeros((T, HEAD_DIM), F32)))
            dq_ref[rows, :] = dq.astype(dq_ref.dtype)
            return 0

        lax.fori_loop(0, nq, q_loop, 0)
        dk_ref[...] = dka[...].astype(dk_ref.dtype)
        dv_ref[...] = dva[...].astype(dv_ref.dtype)

    hblk = pl.BlockSpec((S, HEAD_DIM), lambda h: (0, h))
    return pl.pallas_call(
        body, name="attn_bwd", grid=(nh,), in_specs=_attn_specs(S, U, 3) + [hblk, hblk], out_specs=[hblk] * 3,
        out_shape=[jax.ShapeDtypeStruct((S, 2 * U), MXU_DTYPE)] * 3,
        scratch_shapes=[pltpu.VMEM((S, HEAD_DIM), MXU_DTYPE)] * 4 + [pltpu.VMEM((S, HEAD_DIM), F32)] * 2,
        compiler_params=_cp(("parallel",)),
    )(proj, proj, proj, att_f32, d_att)


def _place():
    return lax.axis_index("x"), lax.axis_index("y"), lax.axis_index("c")


def _flip(v, bit):
    return 1 - v if bit else v


def _related(x, y, r):
    return _flip(x, r & 1), _flip(y, r >> 1)


def _own_rows(ref, dev):
    rows = ref.shape[0] // N_DEV
    return ref.at[pl.ds(pl.multiple_of(dev * rows, 16), rows), :]


def _hbm(a):
    return pltpu.with_memory_space_constraint(a, pltpu.HBM)


def _hbm_like(arrays):
    return tuple(pltpu.HBM(a.shape, a.dtype) for a in arrays)


def _ag_copies(wb_refs, send_sems, recv_sems, outgoing):
    x, y, c = _place()
    me = 4 * x + 2 * y + c
    peers = [(x, y, 1 - c)] + [(*_related(x, y, r), c) for r in (1, 2, 3)]
    out = []
    for n, peer in enumerate(peers):
        block = me if outgoing else 4 * peer[0] + 2 * peer[1] + peer[2]
        for k, wb_ref in enumerate(wb_refs):
            sem = n * len(wb_refs) + k
            out.append(pltpu.make_async_remote_copy(
                src_ref=_own_rows(wb_ref, me), dst_ref=_own_rows(wb_ref, block),
                send_sem=send_sems.at[sem], recv_sem=recv_sems.at[sem], device_id=peer, device_id_type=MESH))
    return out


def _ag_start(wbs, carry, name):
    K = len(wbs)

    def body(*refs):
        send_sems, recv_sems = refs[K + 1:K + 3]
        for cp in _ag_copies(refs[:K], send_sems, recv_sems, True):
            cp.start()

    outs = pl.pallas_call(
        body, name=name,
        out_shape=(pltpu.SemaphoreType.DMA((4 * K,)), pltpu.SemaphoreType.DMA((4 * K,))) + _hbm_like(list(wbs) + [carry]),
        in_specs=(HBM,) * (K + 1), out_specs=(SEM, SEM) + (HBM,) * (K + 1), input_output_aliases={k: 2 + k for k in range(K + 1)},
        compiler_params=pltpu.CompilerParams(has_side_effects=EFFECT),
    )(*[_hbm(a) for a in wbs], _hbm(carry))
    return outs[0], outs[1], list(outs[2:2 + K]), outs[2 + K]


def _ag_wait(wbs, send_sems, recv_sems, after, name):
    K = len(wbs)

    def body(*refs):
        for cp in _ag_copies(refs[:K], refs[K], refs[K + 1], False):
            cp.wait_send()
            cp.wait_recv()

    return list(pl.pallas_call(
        body, name=name, out_shape=_hbm_like(wbs),
        in_specs=(HBM,) * K + (SEM, SEM, ANY), out_specs=(HBM,) * K, input_output_aliases={k: k for k in range(K)},
        compiler_params=pltpu.CompilerParams(has_side_effects=EFFECT),
    )(*wbs, send_sems, recv_sems, after))


def _ag_finish(wbs):
    K = len(wbs)

    def body(*refs):
        ins, outs, tok_ref, send_sems, recv_sems = refs[:K], refs[K:2 * K], refs[2 * K], refs[2 * K + 1], refs[2 * K + 2]
        x, y, c = _place()
        sent = []
        for n, r in enumerate((1, 2, 3)):
            px, py = _related(x, y, r)
            for k in range(K):
                cp = pltpu.make_async_remote_copy(
                    src_ref=_own_rows(ins[k], 4 * px + 2 * py + c), dst_ref=_own_rows(outs[k], 4 * px + 2 * py + c),
                    send_sem=send_sems.at[n, k], recv_sem=recv_sems.at[n, k], device_id=(x, y, 1 - c), device_id_type=MESH)
                cp.start()
                sent.append(cp)
        for n, r in enumerate((1, 2, 3)):
            px, py = _related(x, y, r)
            for k in range(K):
                theirs = _own_rows(outs[k], 4 * px + 2 * py + (1 - c))
                pltpu.make_async_remote_copy(src_ref=theirs, dst_ref=theirs, send_sem=send_sems.at[n, k], recv_sem=recv_sems.at[n, k],
                                             device_id=(x, y, 1 - c), device_id_type=MESH).wait_recv()
        for cp in sent:
            cp.wait_send()
        tok_ref[...] = jnp.zeros_like(tok_ref)

    outs = pl.pallas_call(
        body, name="ag_finish", in_specs=[ANY] * K, out_specs=[ANY] * K + [pl.BlockSpec(memory_space=pltpu.VMEM)],
        out_shape=[jax.ShapeDtypeStruct(a.shape, a.dtype) for a in wbs] + [jax.ShapeDtypeStruct((8, LANE), F32)],
        input_output_aliases={k: k for k in range(K)},
        scratch_shapes=[pltpu.SemaphoreType.DMA((3, K))] * 2, compiler_params=_cp(),
    )(*wbs)
    return list(outs[:K]), outs[K]


def _rs_pair(grads):
    K = len(grads)

    def body(*refs):
        g_refs, land_refs, send_sems, recv_sems = refs[:K], refs[K:2 * K], refs[2 * K], refs[2 * K + 1]
        x, y, c = _place()
        copies = []
        for r in range(4):
            px, py = _related(x, y, r)
            for k in range(K):
                copies.append(pltpu.make_async_remote_copy(
                    src_ref=_own_rows(g_refs[k], 4 * px + 2 * py + (1 - c)), dst_ref=land_refs[k].at[r],
                    send_sem=send_sems.at[r, k], recv_sem=recv_sems.at[r, k], device_id=(x, y, 1 - c), device_id_type=MESH))
        for cp in copies:
            cp.start()
        for cp in copies:
            cp.wait()

    return list(pl.pallas_call(
        body, name="rs_pair", in_specs=[ANY] * K, out_specs=[ANY] * K,
        out_shape=[jax.ShapeDtypeStruct((4, g.shape[0] // N_DEV, g.shape[1]), g.dtype) for g in grads],
        scratch_shapes=[pltpu.SemaphoreType.DMA((4, K))] * 2, compiler_params=_cp(),
    )(*grads))


def _shard_tile(rows):
    for t in (512, 736, 256, 128, 64, 32, 16, 8):
        if rows % t == 0:
            return t
    return rows


def _pair_add(grad, landed):
    _, rows, C = landed.shape
    t = _shard_tile(rows)

    def g_index(r, i):
        x, y, c = _place()
        px = jnp.where(r % 2 == 1, 1 - x, x)
        py = jnp.where(r // 2 == 1, 1 - y, y)
        return (4 * px + 2 * py + c) * (rows // t) + i, 0

    def body(a_ref, b_ref, o_ref):
        o_ref[...] = (a_ref[...].astype(F32) + b_ref[...].astype(F32)).astype(o_ref.dtype)

    slot = pl.BlockSpec((None, t, C), lambda r, i: (r, i, 0))
    return pl.pallas_call(
        body, name="pair_add", grid=(4, rows // t), in_specs=[pl.BlockSpec((t, C), g_index), slot], out_specs=slot,
        out_shape=jax.ShapeDtypeStruct((4, rows, C), grad.dtype), compiler_params=_cp(("parallel", "parallel")),
    )(grad, landed)


def _rs_copies(p_refs, land_refs, send_sems, recv_sems):
    x, y, c = _place()
    K = len(p_refs)
    return [pltpu.make_async_remote_copy(src_ref=p_refs[k].at[r], dst_ref=land_refs[k].at[r - 1], send_sem=send_sems.at[(r - 1) * K + k],
                                         recv_sem=recv_sems.at[(r - 1) * K + k], device_id=(*_related(x, y, r), c), device_id_type=MESH)
            for r in (1, 2, 3) for k in range(K)]


def _rs_start(pair_sums, lands, carry, name):
    K = len(pair_sums)

    def body(*refs):
        for cp in _rs_copies(refs[:K], refs[K:2 * K], refs[2 * K + 1], refs[2 * K + 2]):
            cp.start()

    n_thru = 2 * K + 1
    outs = pl.pallas_call(
        body, name=name,
        out_shape=(pltpu.SemaphoreType.DMA((3 * K,)), pltpu.SemaphoreType.DMA((3 * K,))) + _hbm_like(list(pair_sums) + list(lands) + [carry]),
        in_specs=(HBM,) * n_thru, out_specs=(SEM, SEM) + (HBM,) * n_thru, input_output_aliases={k: 2 + k for k in range(n_thru)},
        compiler_params=pltpu.CompilerParams(has_side_effects=EFFECT),
    )(*[_hbm(a) for a in list(pair_sums) + list(lands) + [carry]])
    return outs[0], outs[1], list(outs[2:2 + K]), list(outs[2 + K:2 + 2 * K]), outs[2 + 2 * K]


def _rs_wait(pair_sums, lands, send_sems, recv_sems, after, name):
    K = len(pair_sums)

    def body(*refs):
        for cp in _rs_copies(refs[:K], refs[K:2 * K], refs[2 * K], refs[2 * K + 1]):
            cp.wait_send()
            cp.wait_recv()

    outs = pl.pallas_call(
        body, name=name, out_shape=_hbm_like(list(pair_sums) + list(lands)),
        in_specs=(HBM,) * (2 * K) + (SEM, SEM) + (ANY,) * len(after), out_specs=(HBM,) * (2 * K),
        input_output_aliases={k: k for k in range(2 * K)}, compiler_params=pltpu.CompilerParams(has_side_effects=EFFECT),
    )(*pair_sums, *lands, send_sems, recv_sems, *after)
    return list(outs[:K]), list(outs[K:])


def _final_sum(pair_sums, landed):
    _, R, C = pair_sums.shape
    tr = _shard_tile(R)

    def body(p_ref, l1_ref, l2_ref, l3_ref, o_ref):
        o_ref[...] = ((p_ref[...].astype(F32) + l1_ref[...].astype(F32)) + l2_ref[...].astype(F32)) + l3_ref[...].astype(F32)

    specs = [pl.BlockSpec((None, tr, C), functools.partial(lambda i, s: (s, i, 0), s=s)) for s in (0, 0, 1, 2)]
    return pl.pallas_call(
        body, name="final_sum", grid=(R // tr,), in_specs=specs, out_specs=pl.BlockSpec((tr, C), lambda i: (i, 0)),
        out_shape=jax.ShapeDtypeStruct((R, C), F32), compiler_params=_cp(("parallel",)),
    )(pair_sums, landed, landed, landed)


def _all_gather_small(v, reduce):
    M, N = v.shape

    def body(x_ref, out_ref, sum_ref, send_sems, recv_sems, local_sem):
        x, y, c = _place()
        me, sibling = (x, y, c), (x, y, 1 - c)
        chips = [_related(x, y, r) for r in (1, 2, 3)]

        def rows(px, py, pc):
            return out_ref.at[pl.ds(pl.multiple_of((4 * px + 2 * py + pc) * M, 8), M), :]

        def copy(k, block, to, src=None):
            return pltpu.make_async_remote_copy(src_ref=rows(*block) if src is None else src, dst_ref=rows(*block),
                                                send_sem=send_sems.at[k], recv_sem=recv_sems.at[k], device_id=to, device_id_type=MESH)

        mine = pltpu.make_async_copy(x_ref, rows(*me), local_sem)
        mine.start()
        first = [copy(0, me, sibling, src=x_ref)]
        first += [copy(1 + j, me, (*chip, c), src=x_ref) for j, chip in enumerate(chips)]
        for cp in first:
            cp.start()
        passed = [copy(4 + j, (*chip, c), sibling) for j, chip in enumerate(chips)]
        for j, chip in enumerate(chips):
            copy(1 + j, (*chip, c), me).wait_recv()
            passed[j].start()
        copy(0, sibling, me).wait_recv()
        for j, chip in enumerate(chips):
            copy(4 + j, (*chip, 1 - c), me).wait_recv()
        for cp in first + passed:
            cp.wait_send()
        mine.wait()
        if reduce:
            tot = out_ref[pl.ds(0, M), :]
            for p in range(1, N_DEV):
                tot = tot + out_ref[pl.ds(p * M, M), :]
            sum_ref[...] = tot
        else:
            sum_ref[...] = jnp.zeros_like(sum_ref)

    vm = pl.BlockSpec(memory_space=pltpu.VMEM)
    second = jax.ShapeDtypeStruct((M, N) if reduce else (8, LANE), F32)
    outs = pl.pallas_call(
        body, name="all_reduce_small" if reduce else "all_gather_small", in_specs=[vm], out_specs=[vm, vm],
        out_shape=[jax.ShapeDtypeStruct((N_DEV * M, N), v.dtype), second],
        scratch_shapes=[pltpu.SemaphoreType.DMA((7,)), pltpu.SemaphoreType.DMA((7,)), pltpu.SemaphoreType.DMA],
        compiler_params=_cp(),
    )(v)
    return outs[1] if reduce else outs


def _adamw(w, g, m, v):
    shape = w.shape
    cols = shape[-1]
    rows = w.size // cols
    tr = _row_tile(rows, 256) if rows % 8 == 0 else rows
    c1 = 1.0 / (1.0 - ADAM_B1 ** ADAM_STEP)
    c2 = 1.0 / (1.0 - ADAM_B2 ** ADAM_STEP)

    def body(w_ref, g_ref, m_ref, v_ref, d_ref, nm_ref, nv_ref):
        gv = g_ref[...]
        nm = ADAM_B1 * m_ref[...] + (1.0 - ADAM_B1) * gv
        nv = ADAM_B2 * v_ref[...] + (1.0 - ADAM_B2) * (gv * gv)
        d_ref[...] = -ADAM_LR * ((nm * c1) / (jnp.sqrt(nv * c2) + ADAM_EPS) + ADAM_WD * w_ref[...])
        nm_ref[...] = nm
        nv_ref[...] = nv

    blk = pl.BlockSpec((tr, cols), lambda i: (i, 0))
    outs = pl.pallas_call(
        body, name="adamw", grid=(rows // tr,), in_specs=[blk] * 4, out_specs=[blk] * 3,
        out_shape=[jax.ShapeDtypeStruct((rows, cols), F32)] * 3, compiler_params=_cp(("parallel",)),
    )(*[a.reshape(rows, cols) for a in (w, g, m, v)])
    return tuple(o.reshape(shape) for o in outs)


def _adamw_layer(w, g_l, m, v, l, prev):
    L, A, B = w.shape
    ta = _row_tile(A, 256) if A % 8 == 0 else A
    c1 = 1.0 / (1.0 - ADAM_B1 ** ADAM_STEP)
    c2 = 1.0 / (1.0 - ADAM_B2 ** ADAM_STEP)
    n_prev = 0 if prev is None else 4

    def body(*refs):
        w_ref, g_ref, m_ref, v_ref = refs[:4]
        go_ref, d_ref, nm_ref, nv_ref = refs[4 + n_prev:]
        gv = g_ref[...]
        nm = ADAM_B1 * m_ref[...] + (1.0 - ADAM_B1) * gv
        nv = ADAM_B2 * v_ref[...] + (1.0 - ADAM_B2) * (gv * gv)
        d_ref[...] = -ADAM_LR * ((nm * c1) / (jnp.sqrt(nv * c2) + ADAM_EPS) + ADAM_WD * w_ref[...])
        go_ref[...] = gv
        nm_ref[...] = nm
        nv_ref[...] = nv

    lay = pl.BlockSpec((None, ta, B), lambda i: (l, i, 0))
    return pl.pallas_call(
        body, name="adamw_layer", grid=(A // ta,),
        in_specs=[lay, pl.BlockSpec((ta, B), lambda i: (i, 0)), lay, lay] + [ANY] * n_prev, out_specs=[lay] * 4,
        out_shape=[jax.ShapeDtypeStruct((L, A, B), F32)] * 4, input_output_aliases={4 + j: j for j in range(n_prev)},
        compiler_params=_cp(("parallel",)),
    )(w, g_l, m, v, *(prev or ()))


def _relu2(acc):
    r = jnp.maximum(acc, 0.0)
    return acc, r * r


def _relu2_bwd(acc, up):
    return (acc * (2.0 * jnp.maximum(up.astype(F32), 0.0)),)


def kernel(x, ln_mix_pre, ln_mix_post, ln_mlp_pre, ln_mlp_post, w_in, conv_a_w, proj_a, proj_b, conv_c_w, conv_c_b, norm_c_g, norm_c_b, proj_c, w_o, w_up, w_down, loss_target, m_ln_mix_pre, m_ln_mix_post, m_ln_mlp_pre, m_ln_mlp_post, m_w_in, m_conv_a_w, m_proj_a, m_proj_b, m_conv_c_w, m_conv_c_b, m_norm_c_g, m_norm_c_b, m_proj_c, m_w_o, m_w_up, m_w_down, v_ln_mix_pre, v_ln_mix_post, v_ln_mlp_pre, v_ln_mlp_post, v_w_in, v_conv_a_w, v_proj_a, v_proj_b, v_conv_c_w, v_conv_c_b, v_norm_c_g, v_norm_c_b, v_proj_c, v_w_o, v_w_up, v_w_down):
    L, D, n_in_loc = w_in.shape
    S = x.shape[1]
    U = D // 4
    N_IN = n_in_loc * N_DEV
    D_FF = w_up.shape[2] * N_DEV
    assert N_IN == 23 * U and x.shape[0] == 1
    x_i, y_i, c_i = _place()
    me = 4 * x_i + 2 * y_i + c_i

    def gather_start(l, which, zero, carry, name):
        def shard(k):
            if k == 0:
                return jnp.concatenate([proj_b[l].T, proj_a[l].T, proj_c[l].T], axis=1) + zero
            return (w_o[l] + zero, w_up[l].T + zero, w_down[l] + zero, w_in[l].T + zero)[k - 1]

        wbs = []
        for k in which:
            s = shard(k)
            wbs.append(lax.dynamic_update_slice(lax.empty((N_DEV * s.shape[0], D), WIRE_DTYPE), s.astype(WIRE_DTYPE), (me * s.shape[0], 0)))
        return _ag_start(wbs, carry, name)

    def gather_end(started, after, name):
        send_sems, recv_sems, wbs, _ = started
        wbs, zero = _ag_finish(_ag_wait(wbs, send_sems, recv_sems, after, name))
        return [w.astype(MXU_DTYPE) for w in wbs], zero

    cu = U // N_DEV
    conv_loc = jnp.concatenate([conv_a_w, conv_c_w], axis=1).reshape(L * (SC_WIDTH + CF_WIDTH), cu)
    conv_all, tok = _all_gather_small(conv_loc, False)
    conv_all = conv_all.reshape(N_DEV, L, SC_WIDTH + CF_WIDTH, cu).transpose(1, 2, 0, 3).reshape(L, SC_WIDTH + CF_WIDTH, U)
    wa_full, wc_full = conv_all[:, :SC_WIDTH], conv_all[:, SC_WIDTH:]
    first_in = gather_start(0, (4,), tok[0, 0], jnp.zeros((8, LANE), F32), "ag_start_0_in")
    first_rest = gather_start(0, (0, 1, 2, 3), tok[0, 0], first_in[3], "ag_start_0_rest")
    started = gather_start(1, (0, 1, 2, 3, 4), tok[0, 0], first_rest[3], "ag_start_1") if L > 1 else None
    wb = []

    def vec(p, l):
        return p[l][None, :]

    xs = x[0]
    saved = []
    h1 = _rms_fwd(xs, vec(ln_mix_pre, 0), "rms_fwd")
    for l in range(L):
        if l == 0:
            (w_in_t,), _ = gather_end(first_in, (started or first_rest)[3], "ag_wait_0_in")
        else:
            w_p, w_o_l, w_up_t, w_dn, w_in_t = w_next
            if l + 1 < L:
                started = gather_start(l + 1, (0, 1, 2, 3, 4), tok[0, 0], h1, f"ag_start_{l + 1}")
                h1 = started[3]
        proj = _mm(h1, w_in_t, "NT", (F32,), "mm_proj")
        a_out = _branch_a_fwd(proj, wa_full[l], U)
        u1 = _branch_c_conv_fwd(proj, wc_full[l], vec(conv_c_b, l), U)
        u3 = _branch_c_norm_fwd(u1, vec(norm_c_g, l), vec(norm_c_b, l))
        att, att_f32 = _attn_fwd(proj, U)
        if l == 0:
            (w_p, w_o_l, w_up_t, w_dn), tok = gather_end(first_rest, att, "ag_wait_0_rest")
        wb.append((w_p, w_o_l, w_up_t, w_dn, w_in_t))
        yb = _mm(att, w_p, "NT", (F32,), "mm_yb", b_view=(0, 0, D, 2 * U))
        ya = _mm(a_out, w_p, "NT", (F32,), "mm_ya", b_view=(0, 2 * U, D, U))
        yc = _mm(u3, w_p, "NT", (F32,), "mm_yc", b_view=(0, 3 * U, D, U))
        merged = _gate_fwd(proj, ya, yb, yc)
        mixed = _mm(merged, w_o_l, "NN", (F32,), "mm_mixed")
        x1, h2 = _resid_post(xs, mixed, vec(ln_mix_post, l), vec(ln_mlp_pre, l), "resid_post_mix")
        up, act = _mm(h2, w_up_t, "NT", (MXU_DTYPE, MXU_DTYPE), "mm_up", epilogue=_relu2)
        f = _mm(act, w_dn, "NN", (F32,), "mm_down")
        saved.append((xs, h1, proj, a_out, u1, u3, att, att_f32, ya, yb, yc, merged, mixed, x1, h2, up, act, f))
        if l + 1 < L:
            xs, h1 = _resid_post(x1, f, vec(ln_mlp_post, l), vec(ln_mix_pre, l + 1), "resid_post_mlp")
            w_next, tok = gather_end(started, h1, f"ag_wait_{l + 1}")
        else:
            xs, _ = _resid_post(x1, f, vec(ln_mlp_post, l), None, "resid_post_last")
    dxo, loss_row = _loss_head(xs, loss_target[0])
    loss = lax.psum(loss_row[0, 0], ("x", "y", "c"))

    small = {k: [None] * L for k in ("g1", "g2", "g3", "g4", "cb", "ng", "nb", "wa", "wc")}
    def reduce_start(grads, carry, name):
        pair_sums = [_pair_add(g, landed) for g, landed in zip(grads, _rs_pair(grads))]
        lands = [lax.empty((3,) + p.shape[1:], WIRE_DTYPE) for p in pair_sums]
        send_sems, recv_sems, pair_sums, lands, carry = _rs_start(pair_sums, lands, carry, name)
        return (pair_sums, lands, send_sems, recv_sems), carry

    in_flight = []
    for l in reversed(range(L)):
        w_p, w_o_l, w_up_t, w_dn, w_in_t = wb[l]
        xs, h1, proj, a_out, u1, u3, att, att_f32, ya, yb, yc, merged, mixed, x1, h2, up, act, f = saved[l]
        df, small["g4"][l] = _rms_bwd(f, vec(ln_mlp_post, l), dxo, None, MXU_DTYPE, "rms_bwd_post_mlp")
        d_up = _mm(df, w_dn, "NT", (MXU_DTYPE,), "mm_d_up", epilogue=_relu2_bwd, extras=(up,))
        g_dn = _mm(act, df, "TN", (WIRE_DTYPE,), "mm_g_down", tm=512, tn=2048)
        dh2 = _mm(d_up, w_up_t, "NN", (F32,), "mm_dh2")
        g_up = _mm(d_up, h2, "TN", (WIRE_DTYPE,), "mm_g_up", tm=512, tn=2048)
        flight, dh2 = reduce_start([g_up, g_dn], dh2, f"rs_start_mlp_{l}")
        in_flight.append((l, ("up", "dn"), flight))
        dx1, small["g3"][l] = _rms_bwd(x1, vec(ln_mlp_pre, l), dh2, dxo, F32, "rms_bwd_pre_mlp")
        dmixed, small["g2"][l] = _rms_bwd(mixed, vec(ln_mix_post, l), dx1, None, MXU_DTYPE, "rms_bwd_post_mix")
        dmerged = _mm(dmixed, w_o_l, "NT", (F32,), "mm_dmerged")
        g_o = _mm(merged, dmixed, "TN", (WIRE_DTYPE,), "mm_g_o", tm=512, tn=2048)
        dya, dyb, dyc, dgla, dglb, dglc = _gate_bwd(dmerged, proj, ya, yb, yc)
        d_att = _mm(dyb, w_p, "NN", (F32,), "mm_d_att", b_view=(0, 0, D, 2 * U))
        d_a_out = _mm(dya, w_p, "NN", (F32,), "mm_d_a_out", b_view=(0, 2 * U, D, U))
        d_u3 = _mm(dyc, w_p, "NN", (F32,), "mm_d_u3", b_view=(0, 3 * U, D, U))
        g_pb = _mm(dyb, att, "TN", (WIRE_DTYPE,), "mm_g_pb", tm=512, tn=2048)
        g_pa = _mm(dya, a_out, "TN", (WIRE_DTYPE,), "mm_g_pa", tm=512, tn=2048)
        g_pc = _mm(dyc, u3, "TN", (WIRE_DTYPE,), "mm_g_pc", tm=512, tn=2048)
        d_scb, d_scc, d_scu, small["wa"][l] = _branch_a_bwd(d_a_out, proj, wa_full[l], U)
        d_u1, small["ng"][l], small["nb"][l] = _branch_c_norm_bwd(d_u3, u1, vec(norm_c_g, l), vec(norm_c_b, l))
        d_cfa, d_cfg, small["wc"][l], small["cb"][l] = _branch_c_conv_bwd(d_u1, proj, wc_full[l], U)
        dq, dk, dv = _attn_bwd(proj, att_f32, d_att, U)
        dproj = jnp.concatenate([dq, dk, dv, d_scb, d_scc, d_scu, d_cfa, d_cfg, dgla, dglb, dglc], axis=1)
        dh1 = _mm(dproj, w_in_t, "NN", (F32,), "mm_dh1", tk=23 * LANE)
        g_in = _mm(dproj, h1, "TN", (WIRE_DTYPE,), "mm_g_in", tm=512, tn=2048)
        dxo, small["g1"][l] = _rms_bwd(xs, vec(ln_mix_pre, l), dh1, dx1, F32, "rms_bwd_pre_mix")
        carry = dxo if l > 0 else jnp.zeros((8, LANE), F32)
        flight, carry = reduce_start([jnp.concatenate([g_pb, g_pa, g_pc], axis=1), g_o, g_in], carry, f"rs_start_mix_{l}")
        in_flight.append((l, ("p", "o", "in"), flight))
        if l > 0:
            dxo = carry
    grad_x = dxo[None]
    big = {"w_in": (w_in, m_w_in, v_w_in), "proj_a": (proj_a, m_proj_a, v_proj_a), "proj_b": (proj_b, m_proj_b, v_proj_b),
           "proj_c": (proj_c, m_proj_c, v_proj_c), "w_o": (w_o, m_w_o, v_w_o), "w_up": (w_up, m_w_up, v_w_up),
           "w_down": (w_down, m_w_down, v_w_down)}
    done = {k: None for k in big}
    for n, (l, keys, flight) in enumerate(in_flight):
        after = [carry]
        if n == len(in_flight) - 1:
            after += [done[k][3] for k in big if done[k] is not None]
        pair_sums, lands = _rs_wait(*flight, after, f"rs_wait_{keys[0]}_{l}")
        g = {key: _final_sum(p, ld) for key, p, ld in zip(keys, pair_sums, lands)}
        if "up" in g:
            layer_grads = {"w_up": g["up"].T, "w_down": g["dn"]}
        else:
            layer_grads = {"w_in": g["in"].T, "proj_b": g["p"][:, :2 * U].T, "proj_a": g["p"][:, 2 * U:3 * U].T,
                           "proj_c": g["p"][:, 3 * U:].T, "w_o": g["o"]}
        for k, g_l in layer_grads.items():
            done[k] = _adamw_layer(*big[k][:1], g_l, *big[k][1:], l, done[k])

    order = ("g1", "g2", "g3", "g4", "cb", "ng", "nb", "wa", "wc")
    parts = [jnp.stack(small[k]).reshape(-1) for k in order]
    flat = jnp.concatenate(parts)
    n_flat = flat.shape[0]
    pad = (-n_flat) % (8 * LANE)
    flat = jnp.pad(flat, (0, pad)).reshape(-1, LANE)
    tot = _all_gather_small(flat, True).reshape(-1)[:n_flat]
    red, pos = {}, 0
    for k, p in zip(order, parts):
        red[k] = tot[pos:pos + p.shape[0]]
        pos += p.shape[0]
    g_ln_mix_pre, g_ln_mix_post = red["g1"].reshape(L, D), red["g2"].reshape(L, D)
    g_ln_mlp_pre, g_ln_mlp_post = red["g3"].reshape(L, D), red["g4"].reshape(L, D)
    g_conv_c_b, g_norm_c_g, g_norm_c_b = red["cb"].reshape(L, U), red["ng"].reshape(L, U), red["nb"].reshape(L, U)
    g_conv_a_w = lax.dynamic_slice_in_dim(red["wa"].reshape(L, SC_WIDTH, U), me * cu, cu, axis=2)
    g_conv_c_w = lax.dynamic_slice_in_dim(red["wc"].reshape(L, CF_WIDTH, U), me * cu, cu, axis=2)

    small_w = {"ln_mix_pre": (ln_mix_pre, g_ln_mix_pre, m_ln_mix_pre, v_ln_mix_pre),
               "ln_mix_post": (ln_mix_post, g_ln_mix_post, m_ln_mix_post, v_ln_mix_post),
               "ln_mlp_pre": (ln_mlp_pre, g_ln_mlp_pre, m_ln_mlp_pre, v_ln_mlp_pre),
               "ln_mlp_post": (ln_mlp_post, g_ln_mlp_post, m_ln_mlp_post, v_ln_mlp_post),
               "conv_a_w": (conv_a_w, g_conv_a_w, m_conv_a_w, v_conv_a_w), "conv_c_w": (conv_c_w, g_conv_c_w, m_conv_c_w, v_conv_c_w),
               "conv_c_b": (conv_c_b, g_conv_c_b, m_conv_c_b, v_conv_c_b), "norm_c_g": (norm_c_g, g_norm_c_g, m_norm_c_g, v_norm_c_g),
               "norm_c_b": (norm_c_b, g_norm_c_b, m_norm_c_b, v_norm_c_b)}
    for k, (w, g, m, v) in small_w.items():
        done[k] = (g,) + _adamw(w, g, m, v)
    names = ("ln_mix_pre", "ln_mix_post", "ln_mlp_pre", "ln_mlp_post", "w_in", "conv_a_w", "proj_a", "proj_b", "conv_c_w", "conv_c_b",
             "norm_c_g", "norm_c_b", "proj_c", "w_o", "w_up", "w_down")
    return (loss, grad_x, *[done[k][0] for k in names], *[done[k][1] for k in names], *[done[k][2] for k in names],
            *[done[k][3] for k in names])
```

```python
import functools

import jax
import jax.numpy as jnp
from jax import lax
from jax.experimental import pallas as pl
from jax.experimental.pallas import tpu as pltpu

F32 = jnp.float32
MXU_DTYPE = jnp.bfloat16
WIRE_DTYPE = jnp.bfloat16
MESH = pl.DeviceIdType.MESH
ANY = pl.BlockSpec(memory_space=pl.ANY)
HBM = pl.BlockSpec(memory_space=pltpu.HBM)
SEM = pl.BlockSpec(memory_space=pltpu.SEMAPHORE)
EFFECT = pltpu.SideEffectType.DATAFLOW_SIDE_EFFECTING

N_DEV = 8
HEAD_DIM = 128
RMS_EPS = 1e-6
LN_EPS = 1e-5
SC_WIDTH = 3
CF_WIDTH = 31
CONV_PAD = 32
ADAM_LR, ADAM_B1, ADAM_B2, ADAM_EPS, ADAM_WD, ADAM_STEP = 0.001, 0.9, 0.999, 1e-08, 0.01, 10
VMEM_LIMIT = 56 * 1024 * 1024
LANE = 128


def _cp(sem=None, **kw):
    return pltpu.CompilerParams(dimension_semantics=sem, vmem_limit_bytes=VMEM_LIMIT, **kw)


def _sigmoid(x):
    return 1.0 / (1.0 + jnp.exp(-x))


def _row_tile(rows, want):
    t = min(rows, want)
    while rows % t:
        t //= 2
    return t


_DN = {"NN": (((1,), (0,)), ((), ())), "NT": (((1,), (1,)), ((), ())), "TN": (((0,), (0,)), ((), ()))}


def _mm(a, b, mode, out_dtypes, name, *, a_view=None, b_view=None, tm=2048, tn=512, tk=2048, epilogue=None, extras=()):
    a_view = a_view or (0, 0) + tuple(a.shape)
    b_view = b_view or (0, 0) + tuple(b.shape)
    ar, ac, an, am = a_view
    br, bc, bn, bm = b_view
    if mode == "NN":
        M, K, K2, N = an, am, bn, bm
    elif mode == "NT":
        M, K, N, K2 = an, am, bn, bm
    else:
        K, M, K2, N = an, am, bn, bm
    assert K == K2, (name, a_view, b_view)
    tm, tn, tk = _row_tile(M, tm), _row_tile(N, tn), _row_tile(K, tk)
    (a_m_off, a_k_off) = (ac, ar) if mode == "TN" else (ar, ac)
    (b_n_off, b_k_off) = (br, bc) if mode == "NT" else (bc, br)
    while a_m_off % tm:
        tm //= 2
    while b_n_off % tn:
        tn //= 2
    while a_k_off % tk or b_k_off % tk:
        tk //= 2
    nk = K // tk
    a_blk = (tk, tm) if mode == "TN" else (tm, tk)
    b_blk = (tn, tk) if mode == "NT" else (tk, tn)
    assert ar % a_blk[0] == 0 and ac % a_blk[1] == 0, (name, a_view, a_blk)
    assert br % b_blk[0] == 0 and bc % b_blk[1] == 0, (name, b_view, b_blk)
    ao, bo = (ar // a_blk[0], ac // a_blk[1]), (br // b_blk[0], bc // b_blk[1])
    if mode == "TN":
        a_spec = pl.BlockSpec(a_blk, lambda i, j, k: (ao[0] + k, ao[1] + i))
    else:
        a_spec = pl.BlockSpec(a_blk, lambda i, j, k: (ao[0] + i, ao[1] + k))
    if mode == "NT":
        b_spec = pl.BlockSpec(b_blk, lambda i, j, k: (bo[0] + j, bo[1] + k))
    else:
        b_spec = pl.BlockSpec(b_blk, lambda i, j, k: (bo[0] + k, bo[1] + j))
    o_spec = pl.BlockSpec((tm, tn), lambda i, j, k: (i, j))
    n_ex, n_out = len(extras), len(out_dtypes)
    dn = _DN[mode]

    def body(*refs):
        a_ref, b_ref = refs[:2]
        ex_refs = refs[2:2 + n_ex]
        o_refs = refs[2 + n_ex:2 + n_ex + n_out]
        p = lax.dot_general(a_ref[...], b_ref[...], dn, preferred_element_type=F32)

        def finish(acc):
            outs = epilogue(acc, *[r[...] for r in ex_refs]) if epilogue else (acc,)
            for o_ref, o in zip(o_refs, outs):
                o_ref[...] = o.astype(o_ref.dtype)

        if nk == 1:
            finish(p)
        else:
            acc_ref = refs[-1]
            k = pl.program_id(2)

            @pl.when(k == 0)
            def _():
                acc_ref[...] = p

            @pl.when(k > 0)
            def _():
                acc_ref[...] += p

            @pl.when(k == nk - 1)
            def _():
                finish(acc_ref[...])

    outs = pl.pallas_call(
        body, name=name, grid=(M // tm, N // tn, nk),
        in_specs=[a_spec, b_spec] + [o_spec] * n_ex, out_specs=[o_spec] * n_out,
        out_shape=[jax.ShapeDtypeStruct((M, N), d) for d in out_dtypes],
        scratch_shapes=[pltpu.VMEM((tm, tn), F32)] if nk > 1 else [],
        compiler_params=_cp(("parallel", "parallel", "arbitrary")),
    )(a, b, *extras)
    return outs[0] if n_out == 1 else outs


def _rms_fwd(x, g, name):
    S, D = x.shape
    tr = _row_tile(S, 256)

    def body(x_ref, g_ref, h_ref):
        xv = x_ref[...]
        r = lax.rsqrt(jnp.mean(xv * xv, axis=-1, keepdims=True) + RMS_EPS)
        h_ref[...] = ((xv * r) * g_ref[...]).astype(h_ref.dtype)

    return pl.pallas_call(
        body, name=name, grid=(S // tr,),
        in_specs=[pl.BlockSpec((tr, D), lambda i: (i, 0)), pl.BlockSpec((1, D), lambda i: (0, 0))],
        out_specs=pl.BlockSpec((tr, D), lambda i: (i, 0)),
        out_shape=jax.ShapeDtypeStruct((S, D), MXU_DTYPE), compiler_params=_cp(("parallel",)),
    )(x, g)


def _resid_post(xres, y, g_post, g_next, name):
    S, D = y.shape
    tr = _row_tile(S, 256)
    has_next = g_next is not None

    def body(*refs):
        xr_ref, y_ref, gp_ref = refs[:3]
        yv = y_ref[...]
        r = lax.rsqrt(jnp.mean(yv * yv, axis=-1, keepdims=True) + RMS_EPS)
        xn = xr_ref[...] + (yv * r) * gp_ref[...]
        if has_next:
            gn_ref, xo_ref, h_ref = refs[3:]
            r2 = lax.rsqrt(jnp.mean(xn * xn, axis=-1, keepdims=True) + RMS_EPS)
            h_ref[...] = ((xn * r2) * gn_ref[...]).astype(h_ref.dtype)
        else:
            xo_ref = refs[3]
        xo_ref[...] = xn

    row = pl.BlockSpec((tr, D), lambda i: (i, 0))
    vec = pl.BlockSpec((1, D), lambda i: (0, 0))
    outs = pl.pallas_call(
        body, name=name, grid=(S // tr,),
        in_specs=[row, row, vec] + ([vec] if has_next else []),
        out_specs=[row] + ([row] if has_next else []),
        out_shape=[jax.ShapeDtypeStruct((S, D), F32)] + ([jax.ShapeDtypeStruct((S, D), MXU_DTYPE)] if has_next else []),
        compiler_params=_cp(("parallel",)),
    )(xres, y, g_post, *([g_next] if has_next else []))
    return (outs[0], outs[1]) if has_next else (outs[0], None)


def _rms_bwd(xin, g, dy, dres, out_dtype, name):
    S, D = xin.shape
    tr = _row_tile(S, 256)
    has_res = dres is not None

    def body(*refs):
        x_ref, g_ref, dy_ref = refs[:3]
        dx_ref, dg_ref = refs[-2:]
        xv, dyv = x_ref[...], dy_ref[...].astype(F32)
        r = lax.rsqrt(jnp.mean(xv * xv, axis=-1, keepdims=True) + RMS_EPS)
        n = xv * r
        dyg = dyv * g_ref[...]
        dx = r * (dyg - n * jnp.mean(dyg * n, axis=-1, keepdims=True))
        if has_res:
            dx = dx + refs[3][...]
        dx_ref[...] = dx.astype(dx_ref.dtype)

        @pl.when(pl.program_id(0) == 0)
        def _():
            dg_ref[...] = jnp.zeros_like(dg_ref)

        dg_ref[...] += jnp.sum(dyv * n, axis=0, keepdims=True)

    row = pl.BlockSpec((tr, D), lambda i: (i, 0))
    vec = pl.BlockSpec((1, D), lambda i: (0, 0))
    return pl.pallas_call(
        body, name=name, grid=(S // tr,),
        in_specs=[row, vec, row] + ([row] if has_res else []), out_specs=[row, vec],
        out_shape=[jax.ShapeDtypeStruct((S, D), out_dtype), jax.ShapeDtypeStruct((1, D), F32)],
        compiler_params=_cp(("arbitrary",)),
    )(xin, g, dy, *([dres] if has_res else []))


def _loss_head(y, target):
    S, D = y.shape
    tr = _row_tile(S, 256)

    def body(y_ref, t_ref, dy_ref, l_ref):
        e = y_ref[...] - t_ref[...]
        dy_ref[...] = e * (1.0 / D)

        @pl.when(pl.program_id(0) == 0)
        def _():
            l_ref[...] = jnp.zeros_like(l_ref)

        l_ref[...] += 0.5 * jnp.sum(jnp.mean(e * e, axis=-1, keepdims=True), axis=0, keepdims=True)

    row = pl.BlockSpec((tr, D), lambda i: (i, 0))
    return pl.pallas_call(
        body, name="loss_head", grid=(S // tr,), in_specs=[row, row],
        out_specs=[row, pl.BlockSpec((1, LANE), lambda i: (0, 0))],
        out_shape=[jax.ShapeDtypeStruct((S, D), F32), jax.ShapeDtypeStruct((1, LANE), F32)],
        compiler_params=_cp(("arbitrary",)),
    )(y, target)


def _gate_specs(S, U, tr):
    gl = [pl.BlockSpec((tr, U), functools.partial(lambda i, j, o: (i, o + j), o=o)) for o in (11, 15, 19)]
    return gl, pl.BlockSpec((tr, U), lambda i, j: (i, j))


def _gate_fwd(proj, ya, yb, yc):
    S, D = ya.shape
    U = D // 4
    tr = _row_tile(S, 256)

    def body(ga_ref, gb_ref, gc_ref, ya_ref, yb_ref, yc_ref, o_ref):
        m = _sigmoid(ga_ref[...]) * ya_ref[...] + _sigmoid(gb_ref[...]) * yb_ref[...] + _sigmoid(gc_ref[...]) * yc_ref[...]
        o_ref[...] = m.astype(o_ref.dtype)

    gl, blk = _gate_specs(S, U, tr)
    return pl.pallas_call(
        body, name="gate_fwd", grid=(S // tr, 4), in_specs=gl + [blk] * 3, out_specs=blk,
        out_shape=jax.ShapeDtypeStruct((S, D), MXU_DTYPE), compiler_params=_cp(("parallel", "parallel")),
    )(proj, proj, proj, ya, yb, yc)


def _gate_bwd(dm, proj, ya, yb, yc):
    S, D = ya.shape
    U = D // 4
    tr = _row_tile(S, 256)

    def body(dm_ref, ga_ref, gb_ref, gc_ref, ya_ref, yb_ref, yc_ref, da_ref, db_ref, dc_ref, la_ref, lb_ref, lc_ref):
        d = dm_ref[...]
        for g_ref, y_ref, dy_ref, dl_ref in ((ga_ref, ya_ref, da_ref, la_ref), (gb_ref, yb_ref, db_ref, lb_ref),
                                             (gc_ref, yc_ref, dc_ref, lc_ref)):
            g = _sigmoid(g_ref[...])
            dy_ref[...] = (d * g).astype(dy_ref.dtype)
            dl_ref[...] = (d * y_ref[...] * g * (1.0 - g)).astype(dl_ref.dtype)

    gl, blk = _gate_specs(S, U, tr)
    return pl.pallas_call(
        body, name="gate_bwd", grid=(S // tr, 4), in_specs=[blk] + gl + [blk] * 3, out_specs=[blk] * 6,
        out_shape=[jax.ShapeDtypeStruct((S, D), MXU_DTYPE)] * 6, compiler_params=_cp(("parallel", "parallel")),
    )(dm, proj, proj, proj, ya, yb, yc)


def _chunks(S):
    r = _row_tile(S, 256)
    return [(r0, r) for r0 in range(0, S, r)]


def _conv_causal(front_ref, w_ref, K, r0, R):
    acc = None
    for j in range(K):
        term = w_ref[pl.ds(K - 1 - j, 1), :] * front_ref[pl.ds(CONV_PAD + r0 - j, R), :]
        acc = term if acc is None else acc + term
    return acc


def _conv_anticausal(back_ref, w_ref, K, r0, R):
    acc = None
    for j in range(K):
        term = w_ref[pl.ds(K - 1 - j, 1), :] * back_ref[pl.ds(r0 + j, R), :]
        acc = term if acc is None else acc + term
    return acc


def _conv_wgrad(front_ref, back_ref, dw_ref, K, S):
    for j in range(K):
        tot = None
        for r0, R in _chunks(S):
            part = jnp.sum(back_ref[pl.ds(r0, R), :] * front_ref[pl.ds(CONV_PAD + r0 - j, R), :], axis=0, keepdims=True)
            tot = part if tot is None else tot + part
        dw_ref[pl.ds(K - 1 - j, 1), :] = tot


def _col(S, cw, unit_off):
    return pl.BlockSpec((S, cw), functools.partial(lambda cb, o: (0, o + cb), o=unit_off))


def _branch_a_fwd(proj, wa, U):
    S = proj.shape[0]
    cw = min(LANE, U)
    nb = U // cw
    K = SC_WIDTH

    def body(b_ref, c_ref, u_ref, w_ref, o_ref, front):
        front[pl.ds(0, CONV_PAD), :] = jnp.zeros((CONV_PAD, cw), F32)
        front[pl.ds(CONV_PAD, S), :] = c_ref[...] * u_ref[...]
        for r0, R in _chunks(S):
            o_ref[pl.ds(r0, R), :] = (b_ref[pl.ds(r0, R), :] * _conv_causal(front, w_ref, K, r0, R)).astype(o_ref.dtype)

    return pl.pallas_call(
        body, name="branch_a_fwd", grid=(nb,),
        in_specs=[_col(S, cw, 6 * nb), _col(S, cw, 7 * nb), _col(S, cw, 8 * nb), pl.BlockSpec((K, cw), lambda cb: (0, cb))],
        out_specs=pl.BlockSpec((S, cw), lambda cb: (0, cb)), out_shape=jax.ShapeDtypeStruct((S, U), MXU_DTYPE),
        scratch_shapes=[pltpu.VMEM((S + CONV_PAD, cw), F32)], compiler_params=_cp(("parallel",)),
    )(proj, proj, proj, wa)


def _branch_a_bwd(d_out, proj, wa, U):
    S = proj.shape[0]
    cw = min(LANE, U)
    nb = U // cw
    K = SC_WIDTH

    def body(d_ref, b_ref, c_ref, u_ref, w_ref, db_ref, dc_ref, du_ref, dw_ref, front, back):
        front[pl.ds(0, CONV_PAD), :] = jnp.zeros((CONV_PAD, cw), F32)
        front[pl.ds(CONV_PAD, S), :] = c_ref[...] * u_ref[...]
        back[pl.ds(S, CONV_PAD), :] = jnp.zeros((CONV_PAD, cw), F32)
        back[pl.ds(0, S), :] = d_ref[...] * b_ref[...]
        for r0, R in _chunks(S):
            rows = pl.ds(r0, R)
            db_ref[rows, :] = (d_ref[rows, :] * _conv_causal(front, w_ref, K, r0, R)).astype(db_ref.dtype)
            d_ai = _conv_anticausal(back, w_ref, K, r0, R)
            dc_ref[rows, :] = (d_ai * u_ref[rows, :]).astype(dc_ref.dtype)
            du_ref[rows, :] = (d_ai * c_ref[rows, :]).astype(du_ref.dtype)
        _conv_wgrad(front, back, dw_ref, K, S)

    blk = pl.BlockSpec((S, cw), lambda cb: (0, cb))
    wblk = pl.BlockSpec((K, cw), lambda cb: (0, cb))
    return pl.pallas_call(
        body, name="branch_a_bwd", grid=(nb,),
        in_specs=[blk, _col(S, cw, 6 * nb), _col(S, cw, 7 * nb), _col(S, cw, 8 * nb), wblk],
        out_specs=[blk, blk, blk, wblk],
        out_shape=[jax.ShapeDtypeStruct((S, U), MXU_DTYPE)] * 3 + [jax.ShapeDtypeStruct((K, U), F32)],
        scratch_shapes=[pltpu.VMEM((S + CONV_PAD, cw), F32)] * 2, compiler_params=_cp(("parallel",)),
    )(d_out, proj, proj, proj, wa)


def _branch_c_conv_fwd(proj, wc, cb, U):
    S = proj.shape[0]
    cw = min(LANE, U)
    nb = U // cw
    K = CF_WIDTH

    def body(a_ref, g_ref, w_ref, bias_ref, o_ref, front):
        front[pl.ds(0, CONV_PAD), :] = jnp.zeros((CONV_PAD, cw), F32)
        front[pl.ds(CONV_PAD, S), :] = a_ref[...] * _sigmoid(g_ref[...])
        for r0, R in _chunks(S):
            o_ref[pl.ds(r0, R), :] = _conv_causal(front, w_ref, K, r0, R) + bias_ref[...]

    return pl.pallas_call(
        body, name="branch_c_conv_fwd", grid=(nb,),
        in_specs=[_col(S, cw, 9 * nb), _col(S, cw, 10 * nb), pl.BlockSpec((K, cw), lambda c: (0, c)),
                  pl.BlockSpec((1, cw), lambda c: (0, c))],
        out_specs=pl.BlockSpec((S, cw), lambda c: (0, c)), out_shape=jax.ShapeDtypeStruct((S, U), F32),
        scratch_shapes=[pltpu.VMEM((S + CONV_PAD, cw), F32)], compiler_params=_cp(("parallel",)),
    )(proj, proj, wc, cb)


def _branch_c_conv_bwd(d_u1, proj, wc, U):
    S = proj.shape[0]
    cw = min(LANE, U)
    nb = U // cw
    K = CF_WIDTH

    def body(d_ref, a_ref, g_ref, w_ref, da_ref, dg_ref, dw_ref, dbias_ref, front, back):
        sg = _sigmoid(g_ref[...])
        front[pl.ds(0, CONV_PAD), :] = jnp.zeros((CONV_PAD, cw), F32)
        front[pl.ds(CONV_PAD, S), :] = a_ref[...] * sg
        back[pl.ds(S, CONV_PAD), :] = jnp.zeros((CONV_PAD, cw), F32)
        back[pl.ds(0, S), :] = d_ref[...]
        dbias_ref[...] = jnp.sum(d_ref[...], axis=0, keepdims=True)
        for r0, R in _chunks(S):
            rows = pl.ds(r0, R)
            d_u0 = _conv_anticausal(back, w_ref, K, r0, R)
            s = _sigmoid(g_ref[rows, :])
            da_ref[rows, :] = (d_u0 * s).astype(da_ref.dtype)
            dg_ref[rows, :] = (d_u0 * a_ref[rows, :] * s * (1.0 - s)).astype(dg_ref.dtype)
        _conv_wgrad(front, back, dw_ref, K, S)

    blk = pl.BlockSpec((S, cw), lambda c: (0, c))
    wblk = pl.BlockSpec((K, cw), lambda c: (0, c))
    vblk = pl.BlockSpec((1, cw), lambda c: (0, c))
    return pl.pallas_call(
        body, name="branch_c_conv_bwd", grid=(nb,),
        in_specs=[blk, _col(S, cw, 9 * nb), _col(S, cw, 10 * nb), wblk], out_specs=[blk, blk, wblk, vblk],
        out_shape=[jax.ShapeDtypeStruct((S, U), MXU_DTYPE)] * 2 + [jax.ShapeDtypeStruct((K, U), F32), jax.ShapeDtypeStruct((1, U), F32)],
        scratch_shapes=[pltpu.VMEM((S + CONV_PAD, cw), F32)] * 2, compiler_params=_cp(("parallel",)),
    )(d_u1, proj, proj, wc)


def _branch_c_norm_fwd(u1, ng, nbias):
    S, U = u1.shape
    tr = _row_tile(S, 256)

    def body(u_ref, g_ref, b_ref, o_ref):
        u = u_ref[...]
        mu = jnp.mean(u, axis=-1, keepdims=True)
        var = jnp.mean(jnp.square(u - mu), axis=-1, keepdims=True)
        u2 = ((u - mu) * lax.rsqrt(var + LN_EPS)) * g_ref[...] + b_ref[...]
        o_ref[...] = (u2 * _sigmoid(u2)).astype(o_ref.dtype)

    row = pl.BlockSpec((tr, U), lambda i: (i, 0))
    vec = pl.BlockSpec((1, U), lambda i: (0, 0))
    return pl.pallas_call(
        body, name="branch_c_norm_fwd", grid=(S // tr,), in_specs=[row, vec, vec], out_specs=row,
        out_shape=jax.ShapeDtypeStruct((S, U), MXU_DTYPE), compiler_params=_cp(("parallel",)),
    )(u1, ng, nbias)


def _branch_c_norm_bwd(d_u3, u1, ng, nbias):
    S, U = u1.shape
    tr = _row_tile(S, 256)

    def body(d_ref, u_ref, g_ref, b_ref, du_ref, dg_ref, db_ref):
        u = u_ref[...]
        mu = jnp.mean(u, axis=-1, keepdims=True)
        var = jnp.mean(jnp.square(u - mu), axis=-1, keepdims=True)
        rstd = lax.rsqrt(var + LN_EPS)
        xh = (u - mu) * rstd
        u2 = xh * g_ref[...] + b_ref[...]
        s = _sigmoid(u2)
        d_u2 = d_ref[...] * (s * (1.0 + u2 * (1.0 - s)))
        d_xh = d_u2 * g_ref[...]
        du_ref[...] = rstd * (d_xh - jnp.mean(d_xh, axis=-1, keepdims=True) - xh * jnp.mean(d_xh * xh, axis=-1, keepdims=True))

        @pl.when(pl.program_id(0) == 0)
        def _():
            dg_ref[...] = jnp.zeros_like(dg_ref)
            db_ref[...] = jnp.zeros_like(db_ref)

        dg_ref[...] += jnp.sum(d_u2 * xh, axis=0, keepdims=True)
        db_ref[...] += jnp.sum(d_u2, axis=0, keepdims=True)

    row = pl.BlockSpec((tr, U), lambda i: (i, 0))
    vec = pl.BlockSpec((1, U), lambda i: (0, 0))
    return pl.pallas_call(
        body, name="branch_c_norm_bwd", grid=(S // tr,), in_specs=[row, row, vec, vec], out_specs=[row, vec, vec],
        out_shape=[jax.ShapeDtypeStruct((S, U), F32), jax.ShapeDtypeStruct((1, U), F32), jax.ShapeDtypeStruct((1, U), F32)],
        compiler_params=_cp(("arbitrary",)),
    )(d_u3, u1, ng, nbias)


def _tri(T, inclusive):
    j = lax.broadcasted_iota(jnp.int32, (T, T), 0)
    s = lax.broadcasted_iota(jnp.int32, (T, T), 1)
    return ((j >= s) if inclusive else (j > s)).astype(MXU_DTYPE)


def _split_dot(x, tri):
    if MXU_DTYPE == F32:
        return jnp.dot(x, tri, preferred_element_type=F32)
    hi = x.astype(MXU_DTYPE)
    lo = (x - hi.astype(F32)).astype(MXU_DTYPE)
    return jnp.dot(hi, tri, preferred_element_type=F32) + jnp.dot(lo, tri, preferred_element_type=F32)


def _sb_block(qb, kb, T, tri_strict, c_lf, diag):
    z = lax.dot_general(qb, kb, _DN["NT"], preferred_element_type=F32) * (HEAD_DIM ** -0.5)
    e = jnp.exp(-jnp.abs(z))
    lg = jnp.log(1.0 + e)
    log_beta = jnp.minimum(z, 0.0) - lg
    lf = jnp.minimum(-z, 0.0) - lg
    mask = None
    if diag:
        mask = lax.broadcasted_iota(jnp.int32, (T, T), 1) < lax.broadcasted_iota(jnp.int32, (T, T), 0)
        lf = jnp.where(mask, lf, 0.0)
    a = jnp.exp(log_beta + _split_dot(lf, tri_strict) + c_lf)
    if diag:
        a = jnp.where(mask, a, 0.0)
    return z, e, mask, lf, a


def _key_blocks(i, step, init):
    carry = step(i, init, True)
    return lax.fori_loop(1, i + 1, lambda jj, c: step(i - jj, c, False), carry)


def _attn_specs(S, U, h_blocks):
    nh = (2 * U) // HEAD_DIM
    return [pl.BlockSpec((S, HEAD_DIM), functools.partial(lambda h, o: (0, o + h), o=o * nh)) for o in range(h_blocks)]


def _attn_fwd(proj, U):
    S = proj.shape[0]
    nh = (2 * U) // HEAD_DIM
    T = _row_tile(S, 256)
    nq = S // T

    def body(q_ref, k_ref, v_ref, o_ref, of_ref, qs, ks, vs):
        qs[...] = q_ref[...].astype(MXU_DTYPE)
        ks[...] = k_ref[...].astype(MXU_DTYPE)
        vs[...] = v_ref[...].astype(MXU_DTYPE)
        tri = _tri(T, False)

        def q_loop(i, _):
            rows = pl.ds(pl.multiple_of(i * T, T), T)
            qb = qs[rows, :]

            def step(j, carry, diag):
                c_lf, acc = carry
                cols = pl.ds(pl.multiple_of(j * T, T), T)
                _, _, _, lf, a = _sb_block(qb, ks[cols, :], T, tri, c_lf, diag)
                acc = acc + jnp.dot(a.astype(MXU_DTYPE), vs[cols, :], preferred_element_type=F32)
                return c_lf + jnp.sum(lf, axis=1, keepdims=True), acc

            _, acc = _key_blocks(i, step, (jnp.zeros((T, 1), F32), jnp.zeros((T, HEAD_DIM), F32)))
            o_ref[rows, :] = acc.astype(o_ref.dtype)
            of_ref[rows, :] = acc
            return 0

        lax.fori_loop(0, nq, q_loop, 0)

    hblk = pl.BlockSpec((S, HEAD_DIM), lambda h: (0, h))
    return pl.pallas_call(
        body, name="attn_fwd", grid=(nh,), in_specs=_attn_specs(S, U, 3), out_specs=[hblk, hblk],
        out_shape=[jax.ShapeDtypeStruct((S, 2 * U), MXU_DTYPE), jax.ShapeDtypeStruct((S, 2 * U), F32)],
        scratch_shapes=[pltpu.VMEM((S, HEAD_DIM), MXU_DTYPE)] * 3, compiler_params=_cp(("parallel",)),
    )(proj, proj, proj)


def _attn_bwd(proj, att_f32, d_att, U):
    S = proj.shape[0]
    nh = (2 * U) // HEAD_DIM
    T = _row_tile(S, 256)
    nq = S // T
    scale = HEAD_DIM ** -0.5

    def body(q_ref, k_ref, v_ref, o_ref, do_ref, dq_ref, dk_ref, dv_ref, qs, ks, vs, dos, dka, dva):
        qs[...] = q_ref[...].astype(MXU_DTYPE)
        ks[...] = k_ref[...].astype(MXU_DTYPE)
        vs[...] = v_ref[...].astype(MXU_DTYPE)
        dos[...] = do_ref[...].astype(MXU_DTYPE)
        dka[...] = jnp.zeros_like(dka)
        dva[...] = jnp.zeros_like(dva)
        tri = _tri(T, False)
        tri_inc = _tri(T, True)

        def q_loop(i, _):
            rows = pl.ds(pl.multiple_of(i * T, T), T)
            qb = qs[rows, :]
            dob = dos[rows, :]
            delta = jnp.sum(dob.astype(F32) * o_ref[rows, :], axis=1, keepdims=True)

            def step(j, carry, diag):
                c_lf, c_g, dq = carry
                cols = pl.ds(pl.multiple_of(j * T, T), T)
                kb, vb = ks[cols, :], vs[cols, :]
                z, e, mask, lf, a = _sb_block(qb, kb, T, tri, c_lf, diag)
                a_mx = a.astype(MXU_DTYPE)
                d_a = lax.dot_general(dob, vb, _DN["NT"], preferred_element_type=F32)
                g = a_mx.astype(F32) * d_a
                prefix = delta - (_split_dot(g, tri_inc) + c_g)
                inv = 1.0 / (1.0 + e)
                beta = jnp.where(z >= 0.0, 1.0, e) * inv
                one_m_beta = jnp.where(z >= 0.0, e, 1.0) * inv
                dz = (g * one_m_beta - prefix * beta) * scale
                if diag:
                    dz = jnp.where(mask, dz, 0.0)
                dz = dz.astype(MXU_DTYPE)
                dq = dq + jnp.dot(dz, kb, preferred_element_type=F32)
                dka[cols, :] += lax.dot_general(dz, qb, _DN["TN"], preferred_element_type=F32)
                dva[cols, :] += lax.dot_general(a_mx, dob, _DN["TN"], preferred_element_type=F32)
                return c_lf + jnp.sum(lf, axis=1, keepdims=True), c_g + jnp.sum(g, axis=1, keepdims=True), dq

            zero = jnp.zeros((T, 1), F32)
            _, _, dq = _key_blocks(i, step, (zero, zero, jnp.zeros((T, HEAD_DIM), F32)))
            dq_ref[rows, :] = dq.astype(dq_ref.dtype)
            return 0

        lax.fori_loop(0, nq, q_loop, 0)
        dk_ref[...] = dka[...].astype(dk_ref.dtype)
        dv_ref[...] = dva[...].astype(dv_ref.dtype)

    hblk = pl.BlockSpec((S, HEAD_DIM), lambda h: (0, h))
    return pl.pallas_call(
        body, name="attn_bwd", grid=(nh,), in_specs=_attn_specs(S, U, 3) + [hblk, hblk], out_specs=[hblk] * 3,
        out_shape=[jax.ShapeDtypeStruct((S, 2 * U), MXU_DTYPE)] * 3,
        scratch_shapes=[pltpu.VMEM((S, HEAD_DIM), MXU_DTYPE)] * 4 + [pltpu.VMEM((S, HEAD_DIM), F32)] * 2,
        compiler_params=_cp(("parallel",)),
    )(proj, proj, proj, att_f32, d_att)


def _place():
    return lax.axis_index("x"), lax.axis_index("y"), lax.axis_index("c")


def _flip(v, bit):
    return 1 - v if bit else v


def _related(x, y, r):
    return _flip(x, r & 1), _flip(y, r >> 1)


def _own_rows(ref, dev):
    rows = ref.shape[0] // N_DEV
    return ref.at[pl.ds(pl.multiple_of(dev * rows, 16), rows), :]


def _my_block(n_blocks):
    def index(i):
        x, y, c = _place()
        return (4 * x + 2 * y + c) * n_blocks + i, 0
    return index


def _gathered(rows, C):
    return jax.ShapeDtypeStruct((N_DEV * rows, C), WIRE_DTYPE)


def _pack_plain(w, zero):
    rows, C = w.shape
    t = _shard_tile(rows)

    def body(w_ref, z_ref, o_ref):
        o_ref[...] = (w_ref[...] + z_ref[0:1, 0:1]).astype(o_ref.dtype)

    return pl.pallas_call(
        body, name="pack_plain", grid=(rows // t,),
        in_specs=[pl.BlockSpec((t, C), lambda i: (i, 0)), pl.BlockSpec((8, LANE), lambda i: (0, 0))],
        out_specs=pl.BlockSpec((t, C), _my_block(rows // t)), out_shape=_gathered(rows, C), compiler_params=_cp(("parallel",)),
    )(w, zero)


def _pack_transposed(ws, zero):
    rows = ws[0].shape[1]
    C = sum(w.shape[0] for w in ws)
    t = _row_tile(rows, 256)
    n = len(ws)

    def body(*refs):
        z_ref, o_ref = refs[n], refs[n + 1]
        col = 0
        for w_ref in refs[:n]:
            k = w_ref.shape[0]
            o_ref[:, col:col + k] = (w_ref[...] + z_ref[0:1, 0:1]).T.astype(o_ref.dtype)
            col += k

    return pl.pallas_call(
        body, name="pack_transposed", grid=(rows // t,),
        in_specs=[pl.BlockSpec((w.shape[0], t), lambda i: (0, i)) for w in ws] + [pl.BlockSpec((8, LANE), lambda i: (0, 0))],
        out_specs=pl.BlockSpec((t, C), _my_block(rows // t)), out_shape=_gathered(rows, C), compiler_params=_cp(("parallel",)),
    )(*ws, zero)


def _pack_transposed_mxu(w, zero):
    K, rows = w.shape
    tk = _row_tile(K, 512)

    def body(w_ref, z_ref, o_ref, eye):
        @pl.when(pl.program_id(0) == 0)
        def _():
            eye[...] = (lax.broadcasted_iota(jnp.int32, (rows, rows), 0) == lax.broadcasted_iota(jnp.int32, (rows, rows), 1)).astype(eye.dtype)

        x = (w_ref[...] + z_ref[0:1, 0:1]).astype(MXU_DTYPE)
        o_ref[...] = lax.dot_general(eye[...], x, _DN["NT"], preferred_element_type=F32).astype(o_ref.dtype)

    def out_index(j):
        x, y, c = _place()
        return 4 * x + 2 * y + c, j

    return pl.pallas_call(
        body, name="pack_transposed_mxu", grid=(K // tk,),
        in_specs=[pl.BlockSpec((tk, rows), lambda j: (j, 0)), pl.BlockSpec((8, LANE), lambda j: (0, 0))],
        out_specs=pl.BlockSpec((rows, tk), out_index), out_shape=_gathered(rows, K),
        scratch_shapes=[pltpu.VMEM((rows, rows), MXU_DTYPE)], compiler_params=_cp(("arbitrary",)),
    )(w, zero)


def _hbm(a):
    return pltpu.with_memory_space_constraint(a, pltpu.HBM)


def _hbm_like(arrays):
    return tuple(pltpu.HBM(a.shape, a.dtype) for a in arrays)


def _ag_copies(wb_refs, send_sems, recv_sems, outgoing):
    x, y, c = _place()
    me = 4 * x + 2 * y + c
    peers = [(x, y, 1 - c)] + [(*_related(x, y, r), c) for r in (1, 2, 3)]
    out = []
    for n, peer in enumerate(peers):
        block = me if outgoing else 4 * peer[0] + 2 * peer[1] + peer[2]
        for k, wb_ref in enumerate(wb_refs):
            sem = n * len(wb_refs) + k
            out.append(pltpu.make_async_remote_copy(
                src_ref=_own_rows(wb_ref, me), dst_ref=_own_rows(wb_ref, block),
                send_sem=send_sems.at[sem], recv_sem=recv_sems.at[sem], device_id=peer, device_id_type=MESH))
    return out


def _ag_start(wbs, carry, name):
    K = len(wbs)

    def body(*refs):
        send_sems, recv_sems = refs[K + 1:K + 3]
        for cp in _ag_copies(refs[:K], send_sems, recv_sems, True):
            cp.start()

    outs = pl.pallas_call(
        body, name=name,
        out_shape=(pltpu.SemaphoreType.DMA((4 * K,)), pltpu.SemaphoreType.DMA((4 * K,))) + _hbm_like(list(wbs) + [carry]),
        in_specs=(HBM,) * (K + 1), out_specs=(SEM, SEM) + (HBM,) * (K + 1), input_output_aliases={k: 2 + k for k in range(K + 1)},
        compiler_params=pltpu.CompilerParams(has_side_effects=EFFECT),
    )(*[_hbm(a) for a in wbs], _hbm(carry))
    return outs[0], outs[1], list(outs[2:2 + K]), outs[2 + K]


def _ag_wait(wbs, send_sems, recv_sems, after, name):
    K = len(wbs)

    def body(*refs):
        for cp in _ag_copies(refs[:K], refs[K], refs[K + 1], False):
            cp.wait_send()
            cp.wait_recv()

    return list(pl.pallas_call(
        body, name=name, out_shape=_hbm_like(wbs),
        in_specs=(HBM,) * K + (SEM, SEM, ANY), out_specs=(HBM,) * K, input_output_aliases={k: k for k in range(K)},
        compiler_params=pltpu.CompilerParams(has_side_effects=EFFECT),
    )(*wbs, send_sems, recv_sems, after))


def _ag_finish(wbs):
    K = len(wbs)

    def body(*refs):
        ins, outs, tok_ref, send_sems, recv_sems = refs[:K], refs[K:2 * K], refs[2 * K], refs[2 * K + 1], refs[2 * K + 2]
        x, y, c = _place()
        sent = []
        for n, r in enumerate((1, 2, 3)):
            px, py = _related(x, y, r)
            for k in range(K):
                cp = pltpu.make_async_remote_copy(
                    src_ref=_own_rows(ins[k], 4 * px + 2 * py + c), dst_ref=_own_rows(outs[k], 4 * px + 2 * py + c),
                    send_sem=send_sems.at[n, k], recv_sem=recv_sems.at[n, k], device_id=(x, y, 1 - c), device_id_type=MESH)
                cp.start()
                sent.append(cp)
        for n, r in enumerate((1, 2, 3)):
            px, py = _related(x, y, r)
            for k in range(K):
                theirs = _own_rows(outs[k], 4 * px + 2 * py + (1 - c))
                pltpu.make_async_remote_copy(src_ref=theirs, dst_ref=theirs, send_sem=send_sems.at[n, k], recv_sem=recv_sems.at[n, k],
                                             device_id=(x, y, 1 - c), device_id_type=MESH).wait_recv()
        for cp in sent:
            cp.wait_send()
        tok_ref[...] = jnp.zeros_like(tok_ref)

    outs = pl.pallas_call(
        body, name="ag_finish", in_specs=[ANY] * K, out_specs=[ANY] * K + [pl.BlockSpec(memory_space=pltpu.VMEM)],
        out_shape=[jax.ShapeDtypeStruct(a.shape, a.dtype) for a in wbs] + [jax.ShapeDtypeStruct((8, LANE), F32)],
        input_output_aliases={k: k for k in range(K)},
        scratch_shapes=[pltpu.SemaphoreType.DMA((3, K))] * 2, compiler_params=_cp(),
    )(*wbs)
    return list(outs[:K]), outs[K]


def _pair_copies(g_refs, land_refs, send_sems, recv_sems):
    x, y, c = _place()
    K = len(g_refs)
    out = []
    for r in range(4):
        px, py = _related(x, y, r)
        for k in range(K):
            out.append(pltpu.make_async_remote_copy(
                src_ref=_own_rows(g_refs[k], 4 * px + 2 * py + (1 - c)), dst_ref=land_refs[k].at[r],
                send_sem=send_sems.at[r * K + k], recv_sem=recv_sems.at[r * K + k], device_id=(x, y, 1 - c), device_id_type=MESH))
    return out


def _shard_tile(rows):
    for t in (512, 736, 256, 128, 64, 32, 16, 8):
        if rows % t == 0:
            return t
    return rows


def _pair_add(grad, landed):
    _, rows, C = landed.shape
    t = _shard_tile(rows)

    def g_index(r, i):
        x, y, c = _place()
        px = jnp.where(r % 2 == 1, 1 - x, x)
        py = jnp.where(r // 2 == 1, 1 - y, y)
        return (4 * px + 2 * py + c) * (rows // t) + i, 0

    def body(a_ref, b_ref, o_ref):
        o_ref[...] = (a_ref[...].astype(F32) + b_ref[...].astype(F32)).astype(o_ref.dtype)

    slot = pl.BlockSpec((None, t, C), lambda r, i: (r, i, 0))
    return pl.pallas_call(
        body, name="pair_add", grid=(4, rows // t), in_specs=[pl.BlockSpec((t, C), g_index), slot], out_specs=slot,
        out_shape=jax.ShapeDtypeStruct((4, rows, C), grad.dtype), compiler_params=_cp(("parallel", "parallel")),
    )(grad, landed)


def _rs_copies(p_refs, land_refs, send_sems, recv_sems):
    x, y, c = _place()
    K = len(p_refs)
    return [pltpu.make_async_remote_copy(src_ref=p_refs[k].at[r], dst_ref=land_refs[k].at[r - 1], send_sem=send_sems.at[(r - 1) * K + k],
                                         recv_sem=recv_sems.at[(r - 1) * K + k], device_id=(*_related(x, y, r), c), device_id_type=MESH)
            for r in (1, 2, 3) for k in range(K)]


def _rs_start(copies, n_slots, srcs, lands, carry, name):
    K = len(srcs)

    def body(*refs):
        for cp in copies(refs[:K], refs[K:2 * K], refs[2 * K + 1], refs[2 * K + 2]):
            cp.start()

    n_thru = 2 * K + 1
    outs = pl.pallas_call(
        body, name=name,
        out_shape=(pltpu.SemaphoreType.DMA((n_slots * K,)), pltpu.SemaphoreType.DMA((n_slots * K,))) + _hbm_like(list(srcs) + list(lands) + [carry]),
        in_specs=(HBM,) * n_thru, out_specs=(SEM, SEM) + (HBM,) * n_thru, input_output_aliases={k: 2 + k for k in range(n_thru)},
        compiler_params=pltpu.CompilerParams(has_side_effects=EFFECT),
    )(*[_hbm(a) for a in list(srcs) + list(lands) + [carry]])
    return outs[0], outs[1], list(outs[2:2 + K]), list(outs[2 + K:2 + 2 * K]), outs[2 + 2 * K]


def _rs_wait(copies, srcs, lands, send_sems, recv_sems, after, name):
    K = len(srcs)

    def body(*refs):
        for cp in copies(refs[:K], refs[K:2 * K], refs[2 * K], refs[2 * K + 1]):
            cp.wait_send()
            cp.wait_recv()

    outs = pl.pallas_call(
        body, name=name, out_shape=_hbm_like(list(srcs) + list(lands)),
        in_specs=(HBM,) * (2 * K) + (SEM, SEM) + (ANY,) * len(after), out_specs=(HBM,) * (2 * K),
        input_output_aliases={k: k for k in range(2 * K)}, compiler_params=pltpu.CompilerParams(has_side_effects=EFFECT),
    )(*srcs, *lands, send_sems, recv_sems, *after)
    return list(outs[:K]), list(outs[K:])


def _final_sum(pair_sums, landed):
    _, R, C = pair_sums.shape
    tr = _shard_tile(R)

    def body(p_ref, l1_ref, l2_ref, l3_ref, o_ref):
        o_ref[...] = ((p_ref[...].astype(F32) + l1_ref[...].astype(F32)) + l2_ref[...].astype(F32)) + l3_ref[...].astype(F32)

    specs = [pl.BlockSpec((None, tr, C), functools.partial(lambda i, s: (s, i, 0), s=s)) for s in (0, 0, 1, 2)]
    return pl.pallas_call(
        body, name="final_sum", grid=(R // tr,), in_specs=specs, out_specs=pl.BlockSpec((tr, C), lambda i: (i, 0)),
        out_shape=jax.ShapeDtypeStruct((R, C), F32), compiler_params=_cp(("parallel",)),
    )(pair_sums, landed, landed, landed)


def _all_gather_small(v, reduce):
    M, N = v.shape

    def body(x_ref, out_ref, sum_ref, send_sems, recv_sems, local_sem):
        x, y, c = _place()
        me, sibling = (x, y, c), (x, y, 1 - c)
        chips = [_related(x, y, r) for r in (1, 2, 3)]

        def rows(px, py, pc):
            return out_ref.at[pl.ds(pl.multiple_of((4 * px + 2 * py + pc) * M, 8), M), :]

        def copy(k, block, to, src=None):
            return pltpu.make_async_remote_copy(src_ref=rows(*block) if src is None else src, dst_ref=rows(*block),
                                                send_sem=send_sems.at[k], recv_sem=recv_sems.at[k], device_id=to, device_id_type=MESH)

        mine = pltpu.make_async_copy(x_ref, rows(*me), local_sem)
        mine.start()
        first = [copy(0, me, sibling, src=x_ref)]
        first += [copy(1 + j, me, (*chip, c), src=x_ref) for j, chip in enumerate(chips)]
        for cp in first:
            cp.start()
        passed = [copy(4 + j, (*chip, c), sibling) for j, chip in enumerate(chips)]
        for j, chip in enumerate(chips):
            copy(1 + j, (*chip, c), me).wait_recv()
            passed[j].start()
        copy(0, sibling, me).wait_recv()
        for j, chip in enumerate(chips):
            copy(4 + j, (*chip, 1 - c), me).wait_recv()
        for cp in first + passed:
            cp.wait_send()
        mine.wait()
        if reduce:
            tot = out_ref[pl.ds(0, M), :]
            for p in range(1, N_DEV):
                tot = tot + out_ref[pl.ds(p * M, M), :]
            sum_ref[...] = tot
        else:
            sum_ref[...] = jnp.zeros_like(sum_ref)

    vm = pl.BlockSpec(memory_space=pltpu.VMEM)
    second = jax.ShapeDtypeStruct((M, N) if reduce else (8, LANE), F32)
    outs = pl.pallas_call(
        body, name="all_reduce_small" if reduce else "all_gather_small", in_specs=[vm], out_specs=[vm, vm],
        out_shape=[jax.ShapeDtypeStruct((N_DEV * M, N), v.dtype), second],
        scratch_shapes=[pltpu.SemaphoreType.DMA((7,)), pltpu.SemaphoreType.DMA((7,)), pltpu.SemaphoreType.DMA],
        compiler_params=_cp(),
    )(v)
    return outs[1] if reduce else outs


def _adamw(w, g, m, v):
    shape = w.shape
    cols = shape[-1]
    rows = w.size // cols
    tr = _row_tile(rows, 256) if rows % 8 == 0 else rows
    c1 = 1.0 / (1.0 - ADAM_B1 ** ADAM_STEP)
    c2 = 1.0 / (1.0 - ADAM_B2 ** ADAM_STEP)

    def body(w_ref, g_ref, m_ref, v_ref, d_ref, nm_ref, nv_ref):
        gv = g_ref[...]
        nm = ADAM_B1 * m_ref[...] + (1.0 - ADAM_B1) * gv
        nv = ADAM_B2 * v_ref[...] + (1.0 - ADAM_B2) * (gv * gv)
        d_ref[...] = -ADAM_LR * ((nm * c1) / (jnp.sqrt(nv * c2) + ADAM_EPS) + ADAM_WD * w_ref[...])
        nm_ref[...] = nm
        nv_ref[...] = nv

    blk = pl.BlockSpec((tr, cols), lambda i: (i, 0))
    outs = pl.pallas_call(
        body, name="adamw", grid=(rows // tr,), in_specs=[blk] * 4, out_specs=[blk] * 3,
        out_shape=[jax.ShapeDtypeStruct((rows, cols), F32)] * 3, compiler_params=_cp(("parallel",)),
    )(*[a.reshape(rows, cols) for a in (w, g, m, v)])
    return tuple(o.reshape(shape) for o in outs)


def _adamw_layer(w, g_l, m, v, l, prev):
    L, A, B = w.shape
    ta = _row_tile(A, 256) if A % 8 == 0 else A
    c1 = 1.0 / (1.0 - ADAM_B1 ** ADAM_STEP)
    c2 = 1.0 / (1.0 - ADAM_B2 ** ADAM_STEP)
    n_prev = 0 if prev is None else 4

    def body(*refs):
        w_ref, g_ref, m_ref, v_ref = refs[:4]
        go_ref, d_ref, nm_ref, nv_ref = refs[4 + n_prev:]
        gv = g_ref[...]
        nm = ADAM_B1 * m_ref[...] + (1.0 - ADAM_B1) * gv
        nv = ADAM_B2 * v_ref[...] + (1.0 - ADAM_B2) * (gv * gv)
        d_ref[...] = -ADAM_LR * ((nm * c1) / (jnp.sqrt(nv * c2) + ADAM_EPS) + ADAM_WD * w_ref[...])
        go_ref[...] = gv
        nm_ref[...] = nm
        nv_ref[...] = nv

    lay = pl.BlockSpec((None, ta, B), lambda i: (l, i, 0))
    return pl.pallas_call(
        body, name="adamw_layer", grid=(A // ta,),
        in_specs=[lay, pl.BlockSpec((ta, B), lambda i: (i, 0)), lay, lay] + [ANY] * n_prev, out_specs=[lay] * 4,
        out_shape=[jax.ShapeDtypeStruct((L, A, B), F32)] * 4, input_output_aliases={4 + j: j for j in range(n_prev)},
        compiler_params=_cp(("parallel",)),
    )(w, g_l, m, v, *(prev or ()))


def _relu2(acc):
    r = jnp.maximum(acc, 0.0)
    return acc, r * r


def _relu2_bwd(acc, up):
    return (acc * (2.0 * jnp.maximum(up.astype(F32), 0.0)),)


def kernel(x, ln_mix_pre, ln_mix_post, ln_mlp_pre, ln_mlp_post, w_in, conv_a_w, proj_a, proj_b, conv_c_w, conv_c_b, norm_c_g, norm_c_b, proj_c, w_o, w_up, w_down, loss_target, m_ln_mix_pre, m_ln_mix_post, m_ln_mlp_pre, m_ln_mlp_post, m_w_in, m_conv_a_w, m_proj_a, m_proj_b, m_conv_c_w, m_conv_c_b, m_norm_c_g, m_norm_c_b, m_proj_c, m_w_o, m_w_up, m_w_down, v_ln_mix_pre, v_ln_mix_post, v_ln_mlp_pre, v_ln_mlp_post, v_w_in, v_conv_a_w, v_proj_a, v_proj_b, v_conv_c_w, v_conv_c_b, v_norm_c_g, v_norm_c_b, v_proj_c, v_w_o, v_w_up, v_w_down):
    L, D, n_in_loc = w_in.shape
    S = x.shape[1]
    U = D // 4
    N_IN = n_in_loc * N_DEV
    D_FF = w_up.shape[2] * N_DEV
    assert N_IN == 23 * U and x.shape[0] == 1
    x_i, y_i, c_i = _place()
    me = 4 * x_i + 2 * y_i + c_i

    def gather_start(l, which, zero, carry, name):
        pack = (lambda: _pack_transposed([proj_b[l], proj_a[l], proj_c[l]], zero), lambda: _pack_plain(w_o[l], zero),
                lambda: _pack_transposed([w_up[l]], zero), lambda: _pack_plain(w_down[l], zero),
                lambda: _pack_transposed_mxu(w_in[l], zero))
        return _ag_start([pack[k]() for k in which], carry, name)

    def gather_end(started, after, name):
        send_sems, recv_sems, wbs, _ = started
        wbs, zero = _ag_finish(_ag_wait(wbs, send_sems, recv_sems, after, name))
        return [w.astype(MXU_DTYPE) for w in wbs], zero

    cu = U // N_DEV
    conv_loc = jnp.concatenate([conv_a_w, conv_c_w], axis=1).reshape(L * (SC_WIDTH + CF_WIDTH), cu)
    conv_all, tok = _all_gather_small(conv_loc, False)
    conv_all = conv_all.reshape(N_DEV, L, SC_WIDTH + CF_WIDTH, cu).transpose(1, 2, 0, 3).reshape(L, SC_WIDTH + CF_WIDTH, U)
    wa_full, wc_full = conv_all[:, :SC_WIDTH], conv_all[:, SC_WIDTH:]
    first_in = gather_start(0, (4,), tok, jnp.zeros((8, LANE), F32), "ag_start_0_in")
    first_rest = gather_start(0, (0, 1, 2, 3), tok, first_in[3], "ag_start_0_rest")
    started = gather_start(1, (0, 1, 2, 3, 4), tok, first_rest[3], "ag_start_1") if L > 1 else None
    wb = []

    def vec(p, l):
        return p[l][None, :]

    xs = x[0]
    saved = []
    h1 = _rms_fwd(xs, vec(ln_mix_pre, 0), "rms_fwd")
    for l in range(L):
        if l == 0:
            (w_in_t,), _ = gather_end(first_in, (started or first_rest)[3], "ag_wait_0_in")
        else:
            w_p, w_o_l, w_up_t, w_dn, w_in_t = w_next
            if l + 1 < L:
                started = gather_start(l + 1, (0, 1, 2, 3, 4), tok, h1, f"ag_start_{l + 1}")
                h1 = started[3]
        proj = _mm(h1, w_in_t, "NT", (F32,), "mm_proj")
        a_out = _branch_a_fwd(proj, wa_full[l], U)
        u1 = _branch_c_conv_fwd(proj, wc_full[l], vec(conv_c_b, l), U)
        u3 = _branch_c_norm_fwd(u1, vec(norm_c_g, l), vec(norm_c_b, l))
        att, att_f32 = _attn_fwd(proj, U)
        if l == 0:
            (w_p, w_o_l, w_up_t, w_dn), tok = gather_end(first_rest, att, "ag_wait_0_rest")
        wb.append((w_p, w_o_l, w_up_t, w_dn, w_in_t))
        yb = _mm(att, w_p, "NT", (F32,), "mm_yb", b_view=(0, 0, D, 2 * U))
        ya = _mm(a_out, w_p, "NT", (F32,), "mm_ya", b_view=(0, 2 * U, D, U))
        yc = _mm(u3, w_p, "NT", (F32,), "mm_yc", b_view=(0, 3 * U, D, U))
        merged = _gate_fwd(proj, ya, yb, yc)
        mixed = _mm(merged, w_o_l, "NN", (F32,), "mm_mixed")
        x1, h2 = _resid_post(xs, mixed, vec(ln_mix_post, l), vec(ln_mlp_pre, l), "resid_post_mix")
        up, act = _mm(h2, w_up_t, "NT", (MXU_DTYPE, MXU_DTYPE), "mm_up", epilogue=_relu2)
        f = _mm(act, w_dn, "NN", (F32,), "mm_down")
        saved.append((xs, h1, proj, a_out, u1, u3, att, att_f32, ya, yb, yc, merged, mixed, x1, h2, up, act, f))
        if l + 1 < L:
            xs, h1 = _resid_post(x1, f, vec(ln_mlp_post, l), vec(ln_mix_pre, l + 1), "resid_post_mlp")
            w_next, tok = gather_end(started, h1, f"ag_wait_{l + 1}")
        else:
            xs, _ = _resid_post(x1, f, vec(ln_mlp_post, l), None, "resid_post_last")
    dxo, loss_row = _loss_head(xs, loss_target[0])
    loss = lax.psum(loss_row[0, 0], ("x", "y", "c"))

    small = {k: [None] * L for k in ("g1", "g2", "g3", "g4", "cb", "ng", "nb", "wa", "wc")}
    def pair_start(grads, carry, name):
        lands = [lax.empty((4, g.shape[0] // N_DEV, D), WIRE_DTYPE) for g in grads]
        send_sems, recv_sems, grads, lands, carry = _rs_start(_pair_copies, 4, grads, lands, carry, name)
        return (grads, lands, send_sems, recv_sems), carry

    def chips_start(pair_flight, after, carry, name):
        grads, landed = _rs_wait(_pair_copies, *pair_flight, after, name + "_pair_wait")
        pair_sums = [_pair_add(g, ld) for g, ld in zip(grads, landed)]
        lands = [lax.empty((3,) + p.shape[1:], WIRE_DTYPE) for p in pair_sums]
        send_sems, recv_sems, pair_sums, lands, carry = _rs_start(_rs_copies, 3, pair_sums, lands, carry, name)
        return (pair_sums, lands, send_sems, recv_sems), carry

    in_flight = []
    mix_pairs = None
    for l in reversed(range(L)):
        w_p, w_o_l, w_up_t, w_dn, w_in_t = wb[l]
        xs, h1, proj, a_out, u1, u3, att, att_f32, ya, yb, yc, merged, mixed, x1, h2, up, act, f = saved[l]
        df, small["g4"][l] = _rms_bwd(f, vec(ln_mlp_post, l), dxo, None, MXU_DTYPE, "rms_bwd_post_mlp")
        d_up = _mm(df, w_dn, "NT", (MXU_DTYPE,), "mm_d_up", epilogue=_relu2_bwd, extras=(up,))
        g_dn = _mm(act, df, "TN", (WIRE_DTYPE,), "mm_g_down", tm=512, tn=2048)
        if mix_pairs is not None:
            flight, d_up = chips_start(mix_pairs, [g_dn], d_up, f"rs_start_mix_{l + 1}")
            in_flight.append((l + 1, ("p", "o", "in"), flight))
        dh2 = _mm(d_up, w_up_t, "NN", (F32,), "mm_dh2")
        g_up = _mm(d_up, h2, "TN", (WIRE_DTYPE,), "mm_g_up", tm=512, tn=2048)
        mlp_pairs, dh2 = pair_start([g_up, g_dn], dh2, f"rs_pair_mlp_{l}")
        dx1, small["g3"][l] = _rms_bwd(x1, vec(ln_mlp_pre, l), dh2, dxo, F32, "rms_bwd_pre_mlp")
        dmixed, small["g2"][l] = _rms_bwd(mixed, vec(ln_mix_post, l), dx1, None, MXU_DTYPE, "rms_bwd_post_mix")
        dmerged = _mm(dmixed, w_o_l, "NT", (F32,), "mm_dmerged")
        g_o = _mm(merged, dmixed, "TN", (WIRE_DTYPE,), "mm_g_o", tm=512, tn=2048)
        flight, dmerged = chips_start(mlp_pairs, [g_o], dmerged, f"rs_start_mlp_{l}")
        in_flight.append((l, ("up", "dn"), flight))
        dya, dyb, dyc, dgla, dglb, dglc = _gate_bwd(dmerged, proj, ya, yb, yc)
        d_att = _mm(dyb, w_p, "NN", (F32,), "mm_d_att", b_view=(0, 0, D, 2 * U))
        d_a_out = _mm(dya, w_p, "NN", (F32,), "mm_d_a_out", b_view=(0, 2 * U, D, U))
        d_u3 = _mm(dyc, w_p, "NN", (F32,), "mm_d_u3", b_view=(0, 3 * U, D, U))
        g_pb = _mm(dyb, att, "TN", (WIRE_DTYPE,), "mm_g_pb", tm=512, tn=2048)
        g_pa = _mm(dya, a_out, "TN", (WIRE_DTYPE,), "mm_g_pa", tm=512, tn=2048)
        g_pc = _mm(dyc, u3, "TN", (WIRE_DTYPE,), "mm_g_pc", tm=512, tn=2048)
        d_scb, d_scc, d_scu, small["wa"][l] = _branch_a_bwd(d_a_out, proj, wa_full[l], U)
        d_u1, small["ng"][l], small["nb"][l] = _branch_c_norm_bwd(d_u3, u1, vec(norm_c_g, l), vec(norm_c_b, l))
        d_cfa, d_cfg, small["wc"][l], small["cb"][l] = _branch_c_conv_bwd(d_u1, proj, wc_full[l], U)
        dq, dk, dv = _attn_bwd(proj, att_f32, d_att, U)
        dproj = jnp.concatenate([dq, dk, dv, d_scb, d_scc, d_scu, d_cfa, d_cfg, dgla, dglb, dglc], axis=1)
        dh1 = _mm(dproj, w_in_t, "NN", (F32,), "mm_dh1", tk=23 * LANE)
        g_in = _mm(dproj, h1, "TN", (WIRE_DTYPE,), "mm_g_in", tm=512, tn=2048)
        dxo, small["g1"][l] = _rms_bwd(xs, vec(ln_mix_pre, l), dh1, dx1, F32, "rms_bwd_pre_mix")
        carry = dxo if l > 0 else jnp.zeros((8, LANE), F32)
        mix_pairs, carry = pair_start([jnp.concatenate([g_pb, g_pa, g_pc], axis=1), g_o, g_in], carry, f"rs_pair_mix_{l}")
        if l > 0:
            dxo = carry
    grad_x = dxo[None]
    flight, carry = chips_start(mix_pairs, [carry], jnp.zeros((8, LANE), F32), "rs_start_mix_0")
    in_flight.append((0, ("p", "o", "in"), flight))
    big = {"w_in": (w_in, m_w_in, v_w_in), "proj_a": (proj_a, m_proj_a, v_proj_a), "proj_b": (proj_b, m_proj_b, v_proj_b),
           "proj_c": (proj_c, m_proj_c, v_proj_c), "w_o": (w_o, m_w_o, v_w_o), "w_up": (w_up, m_w_up, v_w_up),
           "w_down": (w_down, m_w_down, v_w_down)}
    done = {k: None for k in big}
    for n, (l, keys, flight) in enumerate(in_flight):
        after = [carry]
        if n == len(in_flight) - 1:
            after += [done[k][3] for k in big if done[k] is not None]
        pair_sums, lands = _rs_wait(_rs_copies, *flight, after, f"rs_wait_{keys[0]}_{l}")
        g = {key: _final_sum(p, ld) for key, p, ld in zip(keys, pair_sums, lands)}
        if "up" in g:
            layer_grads = {"w_up": g["up"].T, "w_down": g["dn"]}
        else:
            layer_grads = {"w_in": g["in"].T, "proj_b": g["p"][:, :2 * U].T, "proj_a": g["p"][:, 2 * U:3 * U].T,
                           "proj_c": g["p"][:, 3 * U:].T, "w_o": g["o"]}
        for k, g_l in layer_grads.items():
            done[k] = _adamw_layer(*big[k][:1], g_l, *big[k][1:], l, done[k])

    order = ("g1", "g2", "g3", "g4", "cb", "ng", "nb", "wa", "wc")
    parts = [jnp.stack(small[k]).reshape(-1) for k in order]
    flat = jnp.concatenate(parts)
    n_flat = flat.shape[0]
    pad = (-n_flat) % (8 * LANE)
    flat = jnp.pad(flat, (0, pad)).reshape(-1, LANE)
    tot = _all_gather_small(flat, True).reshape(-1)[:n_flat]
    red, pos = {}, 0
    for k, p in zip(order, parts):
        red[k] = tot[pos:pos + p.shape[0]]
        pos += p.shape[0]
    g_ln_mix_pre, g_ln_mix_post = red["g1"].reshape(L, D), red["g2"].reshape(L, D)
    g_ln_mlp_pre, g_ln_mlp_post = red["g3"].reshape(L, D), red["g4"].reshape(L, D)
    g_conv_c_b, g_norm_c_g, g_norm_c_b = red["cb"].reshape(L, U), red["ng"].reshape(L, U), red["nb"].reshape(L, U)
    g_conv_a_w = lax.dynamic_slice_in_dim(red["wa"].reshape(L, SC_WIDTH, U), me * cu, cu, axis=2)
    g_conv_c_w = lax.dynamic_slice_in_dim(red["wc"].reshape(L, CF_WIDTH, U), me * cu, cu, axis=2)

    small_w = {"ln_mix_pre": (ln_mix_pre, g_ln_mix_pre, m_ln_mix_pre, v_ln_mix_pre),
               "ln_mix_post": (ln_mix_post, g_ln_mix_post, m_ln_mix_post, v_ln_mix_post),
               "ln_mlp_pre": (ln_mlp_pre, g_ln_mlp_pre, m_ln_mlp_pre, v_ln_mlp_pre),
               "ln_mlp_post": (ln_mlp_post, g_ln_mlp_post, m_ln_mlp_post, v_ln_mlp_post),
               "conv_a_w": (conv_a_w, g_conv_a_w, m_conv_a_w, v_conv_a_w), "conv_c_w": (conv_c_w, g_conv_c_w, m_conv_c_w, v_conv_c_w),
               "conv_c_b": (conv_c_b, g_conv_c_b, m_conv_c_b, v_conv_c_b), "norm_c_g": (norm_c_g, g_norm_c_g, m_norm_c_g, v_norm_c_g),
               "norm_c_b": (norm_c_b, g_norm_c_b, m_norm_c_b, v_norm_c_b)}
    for k, (w, g, m, v) in small_w.items():
        done[k] = (g,) + _adamw(w, g, m, v)
    names = ("ln_mix_pre", "ln_mix_post", "ln_mlp_pre", "ln_mlp_post", "w_in", "conv_a_w", "proj_a", "proj_b", "conv_c_w", "conv_c_b",
             "norm_c_g", "norm_c_b", "proj_c", "w_o", "w_up", "w_down")
    return (loss, grad_x, *[done[k][0] for k in names], *[done[k][1] for k in names], *[done[k][2] for k in names],
            *[done[k][3] for k in names])
```

```python
import functools

import jax
import jax.numpy as jnp
from jax import lax
from jax.experimental import pallas as pl
from jax.experimental.pallas import tpu as pltpu

F32 = jnp.float32
MXU_DTYPE = jnp.bfloat16
WIRE_DTYPE = jnp.bfloat16
MESH = pl.DeviceIdType.MESH
ANY = pl.BlockSpec(memory_space=pl.ANY)
HBM = pl.BlockSpec(memory_space=pltpu.HBM)
SEM = pl.BlockSpec(memory_space=pltpu.SEMAPHORE)
EFFECT = pltpu.SideEffectType.DATAFLOW_SIDE_EFFECTING

N_DEV = 8
HEAD_DIM = 128
RMS_EPS = 1e-6
LN_EPS = 1e-5
SC_WIDTH = 3
CF_WIDTH = 31
CONV_PAD = 32
ADAM_LR, ADAM_B1, ADAM_B2, ADAM_EPS, ADAM_WD, ADAM_STEP = 0.001, 0.9, 0.999, 1e-08, 0.01, 10
VMEM_LIMIT = 56 * 1024 * 1024
LANE = 128


def _cp(sem=None, **kw):
    return pltpu.CompilerParams(dimension_semantics=sem, vmem_limit_bytes=VMEM_LIMIT, **kw)


def _sigmoid(x):
    return 1.0 / (1.0 + jnp.exp(-x))


def _row_tile(rows, want):
    t = min(rows, want)
    while rows % t:
        t //= 2
    return t


_DN = {"NN": (((1,), (0,)), ((), ())), "NT": (((1,), (1,)), ((), ())), "TN": (((0,), (0,)), ((), ()))}


def _mm(a, b, mode, out_dtypes, name, *, a_view=None, b_view=None, tm=2048, tn=512, tk=2048, epilogue=None, extras=()):
    a_view = a_view or (0, 0) + tuple(a.shape)
    b_view = b_view or (0, 0) + tuple(b.shape)
    ar, ac, an, am = a_view
    br, bc, bn, bm = b_view
    if mode == "NN":
        M, K, K2, N = an, am, bn, bm
    elif mode == "NT":
        M, K, N, K2 = an, am, bn, bm
    else:
        K, M, K2, N = an, am, bn, bm
    assert K == K2, (name, a_view, b_view)
    tm, tn, tk = _row_tile(M, tm), _row_tile(N, tn), _row_tile(K, tk)
    (a_m_off, a_k_off) = (ac, ar) if mode == "TN" else (ar, ac)
    (b_n_off, b_k_off) = (br, bc) if mode == "NT" else (bc, br)
    while a_m_off % tm:
        tm //= 2
    while b_n_off % tn:
        tn //= 2
    while a_k_off % tk or b_k_off % tk:
        tk //= 2
    nk = K // tk
    a_blk = (tk, tm) if mode == "TN" else (tm, tk)
    b_blk = (tn, tk) if mode == "NT" else (tk, tn)
    assert ar % a_blk[0] == 0 and ac % a_blk[1] == 0, (name, a_view, a_blk)
    assert br % b_blk[0] == 0 and bc % b_blk[1] == 0, (name, b_view, b_blk)
    ao, bo = (ar // a_blk[0], ac // a_blk[1]), (br // b_blk[0], bc // b_blk[1])
    if mode == "TN":
        a_spec = pl.BlockSpec(a_blk, lambda i, j, k: (ao[0] + k, ao[1] + i))
    else:
        a_spec = pl.BlockSpec(a_blk, lambda i, j, k: (ao[0] + i, ao[1] + k))
    if mode == "NT":
        b_spec = pl.BlockSpec(b_blk, lambda i, j, k: (bo[0] + j, bo[1] + k))
    else:
        b_spec = pl.BlockSpec(b_blk, lambda i, j, k: (bo[0] + k, bo[1] + j))
    o_spec = pl.BlockSpec((tm, tn), lambda i, j, k: (i, j))
    n_ex, n_out = len(extras), len(out_dtypes)
    dn = _DN[mode]

    def body(*refs):
        a_ref, b_ref = refs[:2]
        ex_refs = refs[2:2 + n_ex]
        o_refs = refs[2 + n_ex:2 + n_ex + n_out]
        p = lax.dot_general(a_ref[...], b_ref[...], dn, preferred_element_type=F32)

        def finish(acc):
            outs = epilogue(acc, *[r[...] for r in ex_refs]) if epilogue else (acc,)
            for o_ref, o in zip(o_refs, outs):
                o_ref[...] = o.astype(o_ref.dtype)

        if nk == 1:
            finish(p)
        else:
            acc_ref = refs[-1]
            k = pl.program_id(2)

            @pl.when(k == 0)
            def _():
                acc_ref[...] = p

            @pl.when(k > 0)
            def _():
                acc_ref[...] += p

            @pl.when(k == nk - 1)
            def _():
                finish(acc_ref[...])

    outs = pl.pallas_call(
        body, name=name, grid=(M // tm, N // tn, nk),
        in_specs=[a_spec, b_spec] + [o_spec] * n_ex, out_specs=[o_spec] * n_out,
        out_shape=[jax.ShapeDtypeStruct((M, N), d) for d in out_dtypes],
        scratch_shapes=[pltpu.VMEM((tm, tn), F32)] if nk > 1 else [],
        compiler_params=_cp(("parallel", "parallel", "arbitrary")),
    )(a, b, *extras)
    return outs[0] if n_out == 1 else outs


def _rms_fwd(x, g, name):
    S, D = x.shape
    tr = _row_tile(S, 256)

    def body(x_ref, g_ref, h_ref):
        xv = x_ref[...]
        r = lax.rsqrt(jnp.mean(xv * xv, axis=-1, keepdims=True) + RMS_EPS)
        h_ref[...] = ((xv * r) * g_ref[...]).astype(h_ref.dtype)

    return pl.pallas_call(
        body, name=name, grid=(S // tr,),
        in_specs=[pl.BlockSpec((tr, D), lambda i: (i, 0)), pl.BlockSpec((1, D), lambda i: (0, 0))],
        out_specs=pl.BlockSpec((tr, D), lambda i: (i, 0)),
        out_shape=jax.ShapeDtypeStruct((S, D), MXU_DTYPE), compiler_params=_cp(("parallel",)),
    )(x, g)


def _resid_post(xres, y, g_post, g_next, name):
    S, D = y.shape
    tr = _row_tile(S, 256)
    has_next = g_next is not None

    def body(*refs):
        xr_ref, y_ref, gp_ref = refs[:3]
        yv = y_ref[...]
        r = lax.rsqrt(jnp.mean(yv * yv, axis=-1, keepdims=True) + RMS_EPS)
        xn = xr_ref[...] + (yv * r) * gp_ref[...]
        if has_next:
            gn_ref, xo_ref, h_ref = refs[3:]
            r2 = lax.rsqrt(jnp.mean(xn * xn, axis=-1, keepdims=True) + RMS_EPS)
            h_ref[...] = ((xn * r2) * gn_ref[...]).astype(h_ref.dtype)
        else:
            xo_ref = refs[3]
        xo_ref[...] = xn

    row = pl.BlockSpec((tr, D), lambda i: (i, 0))
    vec = pl.BlockSpec((1, D), lambda i: (0, 0))
    outs = pl.pallas_call(
        body, name=name, grid=(S // tr,),
        in_specs=[row, row, vec] + ([vec] if has_next else []),
        out_specs=[row] + ([row] if has_next else []),
        out_shape=[jax.ShapeDtypeStruct((S, D), F32)] + ([jax.ShapeDtypeStruct((S, D), MXU_DTYPE)] if has_next else []),
        compiler_params=_cp(("parallel",)),
    )(xres, y, g_post, *([g_next] if has_next else []))
    return (outs[0], outs[1]) if has_next else (outs[0], None)


def _rms_bwd(xin, g, dy, dres, out_dtype, name):
    S, D = xin.shape
    tr = _row_tile(S, 256)
    has_res = dres is not None

    def body(*refs):
        x_ref, g_ref, dy_ref = refs[:3]
        dx_ref, dg_ref = refs[-2:]
        xv, dyv = x_ref[...], dy_ref[...].astype(F32)
        r = lax.rsqrt(jnp.mean(xv * xv, axis=-1, keepdims=True) + RMS_EPS)
        n = xv * r
        dyg = dyv * g_ref[...]
        dx = r * (dyg - n * jnp.mean(dyg * n, axis=-1, keepdims=True))
        if has_res:
            dx = dx + refs[3][...]
        dx_ref[...] = dx.astype(dx_ref.dtype)

        @pl.when(pl.program_id(0) == 0)
        def _():
            dg_ref[...] = jnp.zeros_like(dg_ref)

        dg_ref[...] += jnp.sum(dyv * n, axis=0, keepdims=True)

    row = pl.BlockSpec((tr, D), lambda i: (i, 0))
    vec = pl.BlockSpec((1, D), lambda i: (0, 0))
    return pl.pallas_call(
        body, name=name, grid=(S // tr,),
        in_specs=[row, vec, row] + ([row] if has_res else []), out_specs=[row, vec],
        out_shape=[jax.ShapeDtypeStruct((S, D), out_dtype), jax.ShapeDtypeStruct((1, D), F32)],
        compiler_params=_cp(("arbitrary",)),
    )(xin, g, dy, *([dres] if has_res else []))


def _loss_head(y, target):
    S, D = y.shape
    tr = _row_tile(S, 256)

    def body(y_ref, t_ref, dy_ref, l_ref):
        e = y_ref[...] - t_ref[...]
        dy_ref[...] = e * (1.0 / D)

        @pl.when(pl.program_id(0) == 0)
        def _():
            l_ref[...] = jnp.zeros_like(l_ref)

        l_ref[...] += 0.5 * jnp.sum(jnp.mean(e * e, axis=-1, keepdims=True), axis=0, keepdims=True)

    row = pl.BlockSpec((tr, D), lambda i: (i, 0))
    return pl.pallas_call(
        body, name="loss_head", grid=(S // tr,), in_specs=[row, row],
        out_specs=[row, pl.BlockSpec((1, LANE), lambda i: (0, 0))],
        out_shape=[jax.ShapeDtypeStruct((S, D), F32), jax.ShapeDtypeStruct((1, LANE), F32)],
        compiler_params=_cp(("arbitrary",)),
    )(y, target)


def _gate_specs(S, U, tr):
    gl = [pl.BlockSpec((tr, U), functools.partial(lambda i, j, o: (i, o + j), o=o)) for o in (11, 15, 19)]
    return gl, pl.BlockSpec((tr, U), lambda i, j: (i, j))


def _gate_fwd(proj, ya, yb, yc):
    S, D = ya.shape
    U = D // 4
    tr = _row_tile(S, 256)

    def body(ga_ref, gb_ref, gc_ref, ya_ref, yb_ref, yc_ref, o_ref):
        m = _sigmoid(ga_ref[...]) * ya_ref[...] + _sigmoid(gb_ref[...]) * yb_ref[...] + _sigmoid(gc_ref[...]) * yc_ref[...]
        o_ref[...] = m.astype(o_ref.dtype)

    gl, blk = _gate_specs(S, U, tr)
    return pl.pallas_call(
        body, name="gate_fwd", grid=(S // tr, 4), in_specs=gl + [blk] * 3, out_specs=blk,
        out_shape=jax.ShapeDtypeStruct((S, D), MXU_DTYPE), compiler_params=_cp(("parallel", "parallel")),
    )(proj, proj, proj, ya, yb, yc)


def _gate_bwd(dm, proj, ya, yb, yc):
    S, D = ya.shape
    U = D // 4
    tr = _row_tile(S, 256)

    def body(dm_ref, ga_ref, gb_ref, gc_ref, ya_ref, yb_ref, yc_ref, da_ref, db_ref, dc_ref, la_ref, lb_ref, lc_ref):
        d = dm_ref[...]
        for g_ref, y_ref, dy_ref, dl_ref in ((ga_ref, ya_ref, da_ref, la_ref), (gb_ref, yb_ref, db_ref, lb_ref),
                                             (gc_ref, yc_ref, dc_ref, lc_ref)):
            g = _sigmoid(g_ref[...])
            dy_ref[...] = (d * g).astype(dy_ref.dtype)
            dl_ref[...] = (d * y_ref[...] * g * (1.0 - g)).astype(dl_ref.dtype)

    gl, blk = _gate_specs(S, U, tr)
    return pl.pallas_call(
        body, name="gate_bwd", grid=(S // tr, 4), in_specs=[blk] + gl + [blk] * 3, out_specs=[blk] * 6,
        out_shape=[jax.ShapeDtypeStruct((S, D), MXU_DTYPE)] * 6, compiler_params=_cp(("parallel", "parallel")),
    )(dm, proj, proj, proj, ya, yb, yc)


def _chunks(S):
    r = _row_tile(S, 256)
    return [(r0, r) for r0 in range(0, S, r)]


def _conv_causal(front_ref, w_ref, K, r0, R):
    acc = None
    for j in range(K):
        term = w_ref[pl.ds(K - 1 - j, 1), :] * front_ref[pl.ds(CONV_PAD + r0 - j, R), :]
        acc = term if acc is None else acc + term
    return acc


def _conv_anticausal(back_ref, w_ref, K, r0, R):
    acc = None
    for j in range(K):
        term = w_ref[pl.ds(K - 1 - j, 1), :] * back_ref[pl.ds(r0 + j, R), :]
        acc = term if acc is None else acc + term
    return acc


def _conv_wgrad(front_ref, back_ref, dw_ref, K, S):
    for j in range(K):
        tot = None
        for r0, R in _chunks(S):
            part = jnp.sum(back_ref[pl.ds(r0, R), :] * front_ref[pl.ds(CONV_PAD + r0 - j, R), :], axis=0, keepdims=True)
            tot = part if tot is None else tot + part
        dw_ref[pl.ds(K - 1 - j, 1), :] = tot


def _col(S, cw, unit_off):
    return pl.BlockSpec((S, cw), functools.partial(lambda cb, o: (0, o + cb), o=unit_off))


def _branch_a_fwd(proj, wa, U):
    S = proj.shape[0]
    cw = min(LANE, U)
    nb = U // cw
    K = SC_WIDTH

    def body(b_ref, c_ref, u_ref, w_ref, o_ref, front):
        front[pl.ds(0, CONV_PAD), :] = jnp.zeros((CONV_PAD, cw), F32)
        front[pl.ds(CONV_PAD, S), :] = c_ref[...] * u_ref[...]
        for r0, R in _chunks(S):
            o_ref[pl.ds(r0, R), :] = (b_ref[pl.ds(r0, R), :] * _conv_causal(front, w_ref, K, r0, R)).astype(o_ref.dtype)

    return pl.pallas_call(
        body, name="branch_a_fwd", grid=(nb,),
        in_specs=[_col(S, cw, 6 * nb), _col(S, cw, 7 * nb), _col(S, cw, 8 * nb), pl.BlockSpec((K, cw), lambda cb: (0, cb))],
        out_specs=pl.BlockSpec((S, cw), lambda cb: (0, cb)), out_shape=jax.ShapeDtypeStruct((S, U), MXU_DTYPE),
        scratch_shapes=[pltpu.VMEM((S + CONV_PAD, cw), F32)], compiler_params=_cp(("parallel",)),
    )(proj, proj, proj, wa)


def _branch_a_bwd(d_out, proj, wa, U):
    S = proj.shape[0]
    cw = min(LANE, U)
    nb = U // cw
    K = SC_WIDTH

    def body(d_ref, b_ref, c_ref, u_ref, w_ref, db_ref, dc_ref, du_ref, dw_ref, front, back):
        front[pl.ds(0, CONV_PAD), :] = jnp.zeros((CONV_PAD, cw), F32)
        front[pl.ds(CONV_PAD, S), :] = c_ref[...] * u_ref[...]
        back[pl.ds(S, CONV_PAD), :] = jnp.zeros((CONV_PAD, cw), F32)
        back[pl.ds(0, S), :] = d_ref[...] * b_ref[...]
        for r0, R in _chunks(S):
            rows = pl.ds(r0, R)
            db_ref[rows, :] = (d_ref[rows, :] * _conv_causal(front, w_ref, K, r0, R)).astype(db_ref.dtype)
            d_ai = _conv_anticausal(back, w_ref, K, r0, R)
            dc_ref[rows, :] = (d_ai * u_ref[rows, :]).astype(dc_ref.dtype)
            du_ref[rows, :] = (d_ai * c_ref[rows, :]).astype(du_ref.dtype)
        _conv_wgrad(front, back, dw_ref, K, S)

    blk = pl.BlockSpec((S, cw), lambda cb: (0, cb))
    wblk = pl.BlockSpec((K, cw), lambda cb: (0, cb))
    return pl.pallas_call(
        body, name="branch_a_bwd", grid=(nb,),
        in_specs=[blk, _col(S, cw, 6 * nb), _col(S, cw, 7 * nb), _col(S, cw, 8 * nb), wblk],
        out_specs=[blk, blk, blk, wblk],
        out_shape=[jax.ShapeDtypeStruct((S, U), MXU_DTYPE)] * 3 + [jax.ShapeDtypeStruct((K, U), F32)],
        scratch_shapes=[pltpu.VMEM((S + CONV_PAD, cw), F32)] * 2, compiler_params=_cp(("parallel",)),
    )(d_out, proj, proj, proj, wa)


def _branch_c_conv_fwd(proj, wc, cb, U):
    S = proj.shape[0]
    cw = min(LANE, U)
    nb = U // cw
    K = CF_WIDTH

    def body(a_ref, g_ref, w_ref, bias_ref, o_ref, front):
        front[pl.ds(0, CONV_PAD), :] = jnp.zeros((CONV_PAD, cw), F32)
        front[pl.ds(CONV_PAD, S), :] = a_ref[...] * _sigmoid(g_ref[...])
        for r0, R in _chunks(S):
            o_ref[pl.ds(r0, R), :] = _conv_causal(front, w_ref, K, r0, R) + bias_ref[...]

    return pl.pallas_call(
        body, name="branch_c_conv_fwd", grid=(nb,),
        in_specs=[_col(S, cw, 9 * nb), _col(S, cw, 10 * nb), pl.BlockSpec((K, cw), lambda c: (0, c)),
                  pl.BlockSpec((1, cw), lambda c: (0, c))],
        out_specs=pl.BlockSpec((S, cw), lambda c: (0, c)), out_shape=jax.ShapeDtypeStruct((S, U), F32),
        scratch_shapes=[pltpu.VMEM((S + CONV_PAD, cw), F32)], compiler_params=_cp(("parallel",)),
    )(proj, proj, wc, cb)


def _branch_c_conv_bwd(d_u1, proj, wc, U):
    S = proj.shape[0]
    cw = min(LANE, U)
    nb = U // cw
    K = CF_WIDTH

    def body(d_ref, a_ref, g_ref, w_ref, da_ref, dg_ref, dw_ref, dbias_ref, front, back):
        sg = _sigmoid(g_ref[...])
        front[pl.ds(0, CONV_PAD), :] = jnp.zeros((CONV_PAD, cw), F32)
        front[pl.ds(CONV_PAD, S), :] = a_ref[...] * sg
        back[pl.ds(S, CONV_PAD), :] = jnp.zeros((CONV_PAD, cw), F32)
        back[pl.ds(0, S), :] = d_ref[...]
        dbias_ref[...] = jnp.sum(d_ref[...], axis=0, keepdims=True)
        for r0, R in _chunks(S):
            rows = pl.ds(r0, R)
            d_u0 = _conv_anticausal(back, w_ref, K, r0, R)
            s = _sigmoid(g_ref[rows, :])
            da_ref[rows, :] = (d_u0 * s).astype(da_ref.dtype)
            dg_ref[rows, :] = (d_u0 * a_ref[rows, :] * s * (1.0 - s)).astype(dg_ref.dtype)
        _conv_wgrad(front, back, dw_ref, K, S)

    blk = pl.BlockSpec((S, cw), lambda c: (0, c))
    wblk = pl.BlockSpec((K, cw), lambda c: (0, c))
    vblk = pl.BlockSpec((1, cw), lambda c: (0, c))
    return pl.pallas_call(
        body, name="branch_c_conv_bwd", grid=(nb,),
        in_specs=[blk, _col(S, cw, 9 * nb), _col(S, cw, 10 * nb), wblk], out_specs=[blk, blk, wblk, vblk],
        out_shape=[jax.ShapeDtypeStruct((S, U), MXU_DTYPE)] * 2 + [jax.ShapeDtypeStruct((K, U), F32), jax.ShapeDtypeStruct((1, U), F32)],
        scratch_shapes=[pltpu.VMEM((S + CONV_PAD, cw), F32)] * 2, compiler_params=_cp(("parallel",)),
    )(d_u1, proj, proj, wc)


def _branch_c_norm_fwd(u1, ng, nbias):
    S, U = u1.shape
    tr = _row_tile(S, 256)

    def body(u_ref, g_ref, b_ref, o_ref):
        u = u_ref[...]
        mu = jnp.mean(u, axis=-1, keepdims=True)
        var = jnp.mean(jnp.square(u - mu), axis=-1, keepdims=True)
        u2 = ((u - mu) * lax.rsqrt(var + LN_EPS)) * g_ref[...] + b_ref[...]
        o_ref[...] = (u2 * _sigmoid(u2)).astype(o_ref.dtype)

    row = pl.BlockSpec((tr, U), lambda i: (i, 0))
    vec = pl.BlockSpec((1, U), lambda i: (0, 0))
    return pl.pallas_call(
        body, name="branch_c_norm_fwd", grid=(S // tr,), in_specs=[row, vec, vec], out_specs=row,
        out_shape=jax.ShapeDtypeStruct((S, U), MXU_DTYPE), compiler_params=_cp(("parallel",)),
    )(u1, ng, nbias)


def _branch_c_norm_bwd(d_u3, u1, ng, nbias):
    S, U = u1.shape
    tr = _row_tile(S, 256)

    def body(d_ref, u_ref, g_ref, b_ref, du_ref, dg_ref, db_ref):
        u = u_ref[...]
        mu = jnp.mean(u, axis=-1, keepdims=True)
        var = jnp.mean(jnp.square(u - mu), axis=-1, keepdims=True)
        rstd = lax.rsqrt(var + LN_EPS)
        xh = (u - mu) * rstd
        u2 = xh * g_ref[...] + b_ref[...]
        s = _sigmoid(u2)
        d_u2 = d_ref[...] * (s * (1.0 + u2 * (1.0 - s)))
        d_xh = d_u2 * g_ref[...]
        du_ref[...] = rstd * (d_xh - jnp.mean(d_xh, axis=-1, keepdims=True) - xh * jnp.mean(d_xh * xh, axis=-1, keepdims=True))

        @pl.when(pl.program_id(0) == 0)
        def _():
            dg_ref[...] = jnp.zeros_like(dg_ref)
            db_ref[...] = jnp.zeros_like(db_ref)

        dg_ref[...] += jnp.sum(d_u2 * xh, axis=0, keepdims=True)
        db_ref[...] += jnp.sum(d_u2, axis=0, keepdims=True)

    row = pl.BlockSpec((tr, U), lambda i: (i, 0))
    vec = pl.BlockSpec((1, U), lambda i: (0, 0))
    return pl.pallas_call(
        body, name="branch_c_norm_bwd", grid=(S // tr,), in_specs=[row, row, vec, vec], out_specs=[row, vec, vec],
        out_shape=[jax.ShapeDtypeStruct((S, U), F32), jax.ShapeDtypeStruct((1, U), F32), jax.ShapeDtypeStruct((1, U), F32)],
        compiler_params=_cp(("arbitrary",)),
    )(d_u3, u1, ng, nbias)


def _tri(T, inclusive):
    j = lax.broadcasted_iota(jnp.int32, (T, T), 0)
    s = lax.broadcasted_iota(jnp.int32, (T, T), 1)
    return ((j >= s) if inclusive else (j > s)).astype(MXU_DTYPE)


def _split_dot(x, tri):
    if MXU_DTYPE == F32:
        return jnp.dot(x, tri, preferred_element_type=F32)
    hi = x.astype(MXU_DTYPE)
    lo = (x - hi.astype(F32)).astype(MXU_DTYPE)
    return jnp.dot(hi, tri, preferred_element_type=F32) + jnp.dot(lo, tri, preferred_element_type=F32)


def _sb_block(qb, kb, T, tri_strict, c_lf, diag):
    z = lax.dot_general(qb, kb, _DN["NT"], preferred_element_type=F32) * (HEAD_DIM ** -0.5)
    e = jnp.exp(-jnp.abs(z))
    lg = jnp.log(1.0 + e)
    log_beta = jnp.minimum(z, 0.0) - lg
    lf = jnp.minimum(-z, 0.0) - lg
    mask = None
    if diag:
        mask = lax.broadcasted_iota(jnp.int32, (T, T), 1) < lax.broadcasted_iota(jnp.int32, (T, T), 0)
        lf = jnp.where(mask, lf, 0.0)
    a = jnp.exp(log_beta + _split_dot(lf, tri_strict) + c_lf)
    if diag:
        a = jnp.where(mask, a, 0.0)
    return z, e, mask, lf, a


def _key_blocks(i, step, init):
    carry = step(i, init, True)
    return lax.fori_loop(1, i + 1, lambda jj, c: step(i - jj, c, False), carry)


def _attn_specs(S, U, h_blocks):
    nh = (2 * U) // HEAD_DIM
    return [pl.BlockSpec((S, HEAD_DIM), functools.partial(lambda h, o: (0, o + h), o=o * nh)) for o in range(h_blocks)]


def _attn_fwd(proj, U):
    S = proj.shape[0]
    nh = (2 * U) // HEAD_DIM
    T = _row_tile(S, 256)
    nq = S // T

    def body(q_ref, k_ref, v_ref, o_ref, of_ref, qs, ks, vs):
        qs[...] = q_ref[...].astype(MXU_DTYPE)
        ks[...] = k_ref[...].astype(MXU_DTYPE)
        vs[...] = v_ref[...].astype(MXU_DTYPE)
        tri = _tri(T, False)

        def q_loop(i, _):
            rows = pl.ds(pl.multiple_of(i * T, T), T)
            qb = qs[rows, :]

            def step(j, carry, diag):
                c_lf, acc = carry
                cols = pl.ds(pl.multiple_of(j * T, T), T)
                _, _, _, lf, a = _sb_block(qb, ks[cols, :], T, tri, c_lf, diag)
                acc = acc + jnp.dot(a.astype(MXU_DTYPE), vs[cols, :], preferred_element_type=F32)
                return c_lf + jnp.sum(lf, axis=1, keepdims=True), acc

            _, acc = _key_blocks(i, step, (jnp.zeros((T, 1), F32), jnp.zeros((T, HEAD_DIM), F32)))
            o_ref[rows, :] = acc.astype(o_ref.dtype)
            of_ref[rows, :] = acc
            return 0

        lax.fori_loop(0, nq, q_loop, 0)

    hblk = pl.BlockSpec((S, HEAD_DIM), lambda h: (0, h))
    return pl.pallas_call(
        body, name="attn_fwd", grid=(nh,), in_specs=_attn_specs(S, U, 3), out_specs=[hblk, hblk],
        out_shape=[jax.ShapeDtypeStruct((S, 2 * U), MXU_DTYPE), jax.ShapeDtypeStruct((S, 2 * U), F32)],
        scratch_shapes=[pltpu.VMEM((S, HEAD_DIM), MXU_DTYPE)] * 3, compiler_params=_cp(("parallel",)),
    )(proj, proj, proj)


def _attn_bwd(proj, att_f32, d_att, U):
    S = proj.shape[0]
    nh = (2 * U) // HEAD_DIM
    T = _row_tile(S, 256)
    nq = S // T
    scale = HEAD_DIM ** -0.5

    def body(q_ref, k_ref, v_ref, o_ref, do_ref, dq_ref, dk_ref, dv_ref, qs, ks, vs, dos, dka, dva):
        qs[...] = q_ref[...].astype(MXU_DTYPE)
        ks[...] = k_ref[...].astype(MXU_DTYPE)
        vs[...] = v_ref[...].astype(MXU_DTYPE)
        dos[...] = do_ref[...].astype(MXU_DTYPE)
        dka[...] = jnp.zeros_like(dka)
        dva[...] = jnp.zeros_like(dva)
        tri = _tri(T, False)
        tri_inc = _tri(T, True)

        def q_loop(i, _):
            rows = pl.ds(pl.multiple_of(i * T, T), T)
            qb = qs[rows, :]
            dob = dos[rows, :]
            delta = jnp.sum(dob.astype(F32) * o_ref[rows, :], axis=1, keepdims=True)

            def step(j, carry, diag):
                c_lf, c_g, dq = carry
                cols = pl.ds(pl.multiple_of(j * T, T), T)
                kb, vb = ks[cols, :], vs[cols, :]
                z, e, mask, lf, a = _sb_block(qb, kb, T, tri, c_lf, diag)
                a_mx = a.astype(MXU_DTYPE)
                d_a = lax.dot_general(dob, vb, _DN["NT"], preferred_element_type=F32)
                g = a_mx.astype(F32) * d_a
                prefix = delta - (_split_dot(g, tri_inc) + c_g)
                inv = 1.0 / (1.0 + e)
                beta = jnp.where(z >= 0.0, 1.0, e) * inv
                one_m_beta = jnp.where(z >= 0.0, e, 1.0) * inv
                dz = (g * one_m_beta - prefix * beta) * scale
                if diag:
                    dz = jnp.where(mask, dz, 0.0)
                dz = dz.astype(MXU_DTYPE)
                dq = dq + jnp.dot(dz, kb, preferred_element_type=F32)
                dka[cols, :] += lax.dot_general(dz, qb, _DN["TN"], preferred_element_type=F32)
                dva[cols, :] += lax.dot_general(a_mx, dob, _DN["TN"], preferred_element_type=F32)
                return c_lf + jnp.sum(lf, axis=1, keepdims=True), c_g + jnp.sum(g, axis=1, keepdims=True), dq

            zero = jnp.zeros((T, 1), F32)
            _, _, dq = _key_blocks(i, step, (zero, zero, jnp.zeros((T, HEAD_DIM), F32)))
            dq_ref[rows, :] = dq.astype(dq_ref.dtype)
            return 0

        lax.fori_loop(0, nq, q_loop, 0)
        dk_ref[...] = dka[...].astype(dk_ref.dtype)
        dv_ref[...] = dva[...].astype(dv_ref.dtype)

    hblk = pl.BlockSpec((S, HEAD_DIM), lambda h: (0, h))
    return pl.pallas_call(
        body, name="attn_bwd", grid=(nh,), in_specs=_attn_specs(S, U, 3) + [hblk, hblk], out_specs=[hblk] * 3,
        out_shape=[jax.ShapeDtypeStruct((S, 2 * U), MXU_DTYPE)] * 3,
        scratch_shapes=[pltpu.VMEM((S, HEAD_DIM), MXU_DTYPE)] * 4 + [pltpu.VMEM((S, HEAD_DIM), F32)] * 2,
        compiler_params=_cp(("parallel",)),
    )(proj, proj, proj, att_f32, d_att)


def _place():
    return lax.axis_index("x"), lax.axis_index("y"), lax.axis_index("c")


def _flip(v, bit):
    return 1 - v if bit else v


def _related(x, y, r):
    return _flip(x, r & 1), _flip(y, r >> 1)


def _own_rows(ref, dev):
    rows = ref.shape[0] // N_DEV
    return ref.at[pl.ds(pl.multiple_of(dev * rows, 16), rows), :]


def _my_block(n_blocks):
    def index(i):
        x, y, c = _place()
        return (4 * x + 2 * y + c) * n_blocks + i, 0
    return index


def _gathered(rows, C):
    return jax.ShapeDtypeStruct((N_DEV * rows, C), WIRE_DTYPE)


def _pack_plain(w, zero):
    rows, C = w.shape
    t = _shard_tile(rows)

    def body(w_ref, z_ref, o_ref):
        o_ref[...] = (w_ref[...] + z_ref[0:1, 0:1]).astype(o_ref.dtype)

    return pl.pallas_call(
        body, name="pack_plain", grid=(rows // t,),
        in_specs=[pl.BlockSpec((t, C), lambda i: (i, 0)), pl.BlockSpec((8, LANE), lambda i: (0, 0))],
        out_specs=pl.BlockSpec((t, C), _my_block(rows // t)), out_shape=_gathered(rows, C), compiler_params=_cp(("parallel",)),
    )(w, zero)


def _pack_transposed(ws, zero):
    rows = ws[0].shape[1]
    C = sum(w.shape[0] for w in ws)
    t = _row_tile(rows, 256)
    n = len(ws)

    def body(*refs):
        z_ref, o_ref = refs[n], refs[n + 1]
        col = 0
        for w_ref in refs[:n]:
            k = w_ref.shape[0]
            o_ref[:, col:col + k] = (w_ref[...] + z_ref[0:1, 0:1]).T.astype(o_ref.dtype)
            col += k

    return pl.pallas_call(
        body, name="pack_transposed", grid=(rows // t,),
        in_specs=[pl.BlockSpec((w.shape[0], t), lambda i: (0, i)) for w in ws] + [pl.BlockSpec((8, LANE), lambda i: (0, 0))],
        out_specs=pl.BlockSpec((t, C), _my_block(rows // t)), out_shape=_gathered(rows, C), compiler_params=_cp(("parallel",)),
    )(*ws, zero)


def _pack_transposed_mxu(w, zero):
    K, rows = w.shape
    tk = _row_tile(K, 512)

    def body(w_ref, z_ref, o_ref, eye):
        @pl.when(pl.program_id(0) == 0)
        def _():
            eye[...] = (lax.broadcasted_iota(jnp.int32, (rows, rows), 0) == lax.broadcasted_iota(jnp.int32, (rows, rows), 1)).astype(eye.dtype)

        x = (w_ref[...] + z_ref[0:1, 0:1]).astype(MXU_DTYPE)
        o_ref[...] = lax.dot_general(eye[...], x, _DN["NT"], preferred_element_type=F32).astype(o_ref.dtype)

    def out_index(j):
        x, y, c = _place()
        return 4 * x + 2 * y + c, j

    return pl.pallas_call(
        body, name="pack_transposed_mxu", grid=(K // tk,),
        in_specs=[pl.BlockSpec((tk, rows), lambda j: (j, 0)), pl.BlockSpec((8, LANE), lambda j: (0, 0))],
        out_specs=pl.BlockSpec((rows, tk), out_index), out_shape=_gathered(rows, K),
        scratch_shapes=[pltpu.VMEM((rows, rows), MXU_DTYPE)], compiler_params=_cp(("arbitrary",)),
    )(w, zero)


def _hbm(a):
    return pltpu.with_memory_space_constraint(a, pltpu.HBM)


def _hbm_like(arrays):
    return tuple(pltpu.HBM(a.shape, a.dtype) for a in arrays)


def _dev(px, py, pc):
    return 4 * px + 2 * py + pc


def _block_copies(wb_refs, send_sems, recv_sems, slot, block_out, block_in, peer, outgoing):
    K = len(wb_refs)
    return [pltpu.make_async_remote_copy(
        src_ref=_own_rows(wb_ref, block_out), dst_ref=_own_rows(wb_ref, block_out if outgoing else block_in),
        send_sem=send_sems.at[slot * K + k], recv_sem=recv_sems.at[slot * K + k], device_id=peer, device_id_type=MESH)
        for k, wb_ref in enumerate(wb_refs)]


def _ag_stage1(wb_refs, send_sems, recv_sems, outgoing):
    x, y, c = _place()
    me = _dev(x, y, c)
    out = []
    for slot, peer in enumerate(((x, y, 1 - c), (1 - x, y, c), (x, 1 - y, c))):
        out += _block_copies(wb_refs, send_sems, recv_sems, slot, me, _dev(*peer), peer, outgoing)
    return out


def _ag_stage2(wb_refs, send_sems, recv_sems, outgoing):
    x, y, c = _place()
    via = ((1 - x) + c * (2 * x - 1), y + c * (1 - 2 * y))
    to = (x + c * (1 - 2 * x), (1 - y) + c * (2 * y - 1), c)
    out = _block_copies(wb_refs, send_sems, recv_sems, 0, _dev(*via, c), _dev(1 - x, 1 - y, c), to, outgoing)
    for slot, (px, py) in ((1, (1 - x, y)), (2, (x, 1 - y))):
        out += _block_copies(wb_refs, send_sems, recv_sems, slot, _dev(px, py, c), _dev(px, py, 1 - c), (x, y, 1 - c), outgoing)
    return out


def _ag_start(copies, wbs, carry, name):
    K = len(wbs)

    def body(*refs):
        send_sems, recv_sems = refs[K + 1:K + 3]
        for cp in copies(refs[:K], send_sems, recv_sems, True):
            cp.start()

    outs = pl.pallas_call(
        body, name=name,
        out_shape=(pltpu.SemaphoreType.DMA((3 * K,)), pltpu.SemaphoreType.DMA((3 * K,))) + _hbm_like(list(wbs) + [carry]),
        in_specs=(HBM,) * (K + 1), out_specs=(SEM, SEM) + (HBM,) * (K + 1), input_output_aliases={k: 2 + k for k in range(K + 1)},
        compiler_params=pltpu.CompilerParams(has_side_effects=EFFECT),
    )(*[_hbm(a) for a in wbs], _hbm(carry))
    return outs[0], outs[1], list(outs[2:2 + K]), outs[2 + K]


def _ag_wait(copies, wbs, send_sems, recv_sems, after, name):
    K = len(wbs)

    def body(*refs):
        for cp in copies(refs[:K], refs[K], refs[K + 1], False):
            cp.wait_send()
            cp.wait_recv()

    return list(pl.pallas_call(
        body, name=name, out_shape=_hbm_like(wbs),
        in_specs=(HBM,) * K + (SEM, SEM, ANY), out_specs=(HBM,) * K, input_output_aliases={k: k for k in range(K)},
        compiler_params=pltpu.CompilerParams(has_side_effects=EFFECT),
    )(*wbs, send_sems, recv_sems, after))


def _ag_finish(wbs):
    K = len(wbs)

    def body(*refs):
        ins, outs, tok_ref, send_sems, recv_sems = refs[:K], refs[K:2 * K], refs[2 * K], refs[2 * K + 1], refs[2 * K + 2]
        x, y, c = _place()
        sent = []
        for k in range(K):
            cp = pltpu.make_async_remote_copy(
                src_ref=_own_rows(ins[k], _dev(1 - x, 1 - y, c)), dst_ref=_own_rows(outs[k], _dev(1 - x, 1 - y, c)),
                send_sem=send_sems.at[k], recv_sem=recv_sems.at[k], device_id=(x, y, 1 - c), device_id_type=MESH)
            cp.start()
            sent.append(cp)
        for k in range(K):
            theirs = _own_rows(outs[k], _dev(1 - x, 1 - y, 1 - c))
            pltpu.make_async_remote_copy(src_ref=theirs, dst_ref=theirs, send_sem=send_sems.at[k], recv_sem=recv_sems.at[k],
                                         device_id=(x, y, 1 - c), device_id_type=MESH).wait_recv()
        for cp in sent:
            cp.wait_send()
        tok_ref[...] = jnp.zeros_like(tok_ref)

    outs = pl.pallas_call(
        body, name="ag_finish", in_specs=[ANY] * K, out_specs=[ANY] * K + [pl.BlockSpec(memory_space=pltpu.VMEM)],
        out_shape=[jax.ShapeDtypeStruct(a.shape, a.dtype) for a in wbs] + [jax.ShapeDtypeStruct((8, LANE), F32)],
        input_output_aliases={k: k for k in range(K)},
        scratch_shapes=[pltpu.SemaphoreType.DMA((K,))] * 2, compiler_params=_cp(),
    )(*wbs)
    return list(outs[:K]), outs[K]


def _pair_copies(g_refs, land_refs, send_sems, recv_sems):
    x, y, c = _place()
    K = len(g_refs)
    out = []
    for r in range(4):
        px, py = _related(x, y, r)
        for k in range(K):
            out.append(pltpu.make_async_remote_copy(
                src_ref=_own_rows(g_refs[k], 4 * px + 2 * py + (1 - c)), dst_ref=land_refs[k].at[r],
                send_sem=send_sems.at[r * K + k], recv_sem=recv_sems.at[r * K + k], device_id=(x, y, 1 - c), device_id_type=MESH))
    return out


def _shard_tile(rows):
    for t in (512, 736, 256, 128, 64, 32, 16, 8):
        if rows % t == 0:
            return t
    return rows


def _pair_add(grad, landed):
    _, rows, C = landed.shape
    t = _shard_tile(rows)

    def g_index(r, i):
        x, y, c = _place()
        px = jnp.where(r % 2 == 1, 1 - x, x)
        py = jnp.where(r // 2 == 1, 1 - y, y)
        return (4 * px + 2 * py + c) * (rows // t) + i, 0

    def body(a_ref, b_ref, o_ref):
        o_ref[...] = (a_ref[...].astype(F32) + b_ref[...].astype(F32)).astype(o_ref.dtype)

    slot = pl.BlockSpec((None, t, C), lambda r, i: (r, i, 0))
    return pl.pallas_call(
        body, name="pair_add", grid=(4, rows // t), in_specs=[pl.BlockSpec((t, C), g_index), slot], out_specs=slot,
        out_shape=jax.ShapeDtypeStruct((4, rows, C), grad.dtype), compiler_params=_cp(("parallel", "parallel")),
    )(grad, landed)


def _rs_copies(p_refs, land_refs, send_sems, recv_sems):
    x, y, c = _place()
    K = len(p_refs)
    return [pltpu.make_async_remote_copy(src_ref=p_refs[k].at[r], dst_ref=land_refs[k].at[r - 1], send_sem=send_sems.at[(r - 1) * K + k],
                                         recv_sem=recv_sems.at[(r - 1) * K + k], device_id=(*_related(x, y, r), c), device_id_type=MESH)
            for r in (1, 2, 3) for k in range(K)]


def _rs_start(copies, n_slots, srcs, lands, carry, name):
    K = len(srcs)

    def body(*refs):
        for cp in copies(refs[:K], refs[K:2 * K], refs[2 * K + 1], refs[2 * K + 2]):
            cp.start()

    n_thru = 2 * K + 1
    outs = pl.pallas_call(
        body, name=name,
        out_shape=(pltpu.SemaphoreType.DMA((n_slots * K,)), pltpu.SemaphoreType.DMA((n_slots * K,))) + _hbm_like(list(srcs) + list(lands) + [carry]),
        in_specs=(HBM,) * n_thru, out_specs=(SEM, SEM) + (HBM,) * n_thru, input_output_aliases={k: 2 + k for k in range(n_thru)},
        compiler_params=pltpu.CompilerParams(has_side_effects=EFFECT),
    )(*[_hbm(a) for a in list(srcs) + list(lands) + [carry]])
    return outs[0], outs[1], list(outs[2:2 + K]), list(outs[2 + K:2 + 2 * K]), outs[2 + 2 * K]


def _rs_wait(copies, srcs, lands, send_sems, recv_sems, after, name):
    K = len(srcs)

    def body(*refs):
        for cp in copies(refs[:K], refs[K:2 * K], refs[2 * K], refs[2 * K + 1]):
            cp.wait_send()
            cp.wait_recv()

    outs = pl.pallas_call(
        body, name=name, out_shape=_hbm_like(list(srcs) + list(lands)),
        in_specs=(HBM,) * (2 * K) + (SEM, SEM) + (ANY,) * len(after), out_specs=(HBM,) * (2 * K),
        input_output_aliases={k: k for k in range(2 * K)}, compiler_params=pltpu.CompilerParams(has_side_effects=EFFECT),
    )(*srcs, *lands, send_sems, recv_sems, *after)
    return list(outs[:K]), list(outs[K:])


def _final_sum(pair_sums, landed):
    _, R, C = pair_sums.shape
    tr = _shard_tile(R)

    def body(p_ref, l1_ref, l2_ref, l3_ref, o_ref):
        o_ref[...] = ((p_ref[...].astype(F32) + l1_ref[...].astype(F32)) + l2_ref[...].astype(F32)) + l3_ref[...].astype(F32)

    specs = [pl.BlockSpec((None, tr, C), functools.partial(lambda i, s: (s, i, 0), s=s)) for s in (0, 0, 1, 2)]
    return pl.pallas_call(
        body, name="final_sum", grid=(R // tr,), in_specs=specs, out_specs=pl.BlockSpec((tr, C), lambda i: (i, 0)),
        out_shape=jax.ShapeDtypeStruct((R, C), F32), compiler_params=_cp(("parallel",)),
    )(pair_sums, landed, landed, landed)


def _all_gather_small(v, reduce):
    M, N = v.shape

    def body(x_ref, out_ref, sum_ref, send_sems, recv_sems, local_sem):
        x, y, c = _place()
        me, sibling = (x, y, c), (x, y, 1 - c)
        chips = [_related(x, y, r) for r in (1, 2, 3)]

        def rows(px, py, pc):
            return out_ref.at[pl.ds(pl.multiple_of((4 * px + 2 * py + pc) * M, 8), M), :]

        def copy(k, block, to, src=None):
            return pltpu.make_async_remote_copy(src_ref=rows(*block) if src is None else src, dst_ref=rows(*block),
                                                send_sem=send_sems.at[k], recv_sem=recv_sems.at[k], device_id=to, device_id_type=MESH)

        mine = pltpu.make_async_copy(x_ref, rows(*me), local_sem)
        mine.start()
        first = [copy(0, me, sibling, src=x_ref)]
        first += [copy(1 + j, me, (*chip, c), src=x_ref) for j, chip in enumerate(chips)]
        for cp in first:
            cp.start()
        passed = [copy(4 + j, (*chip, c), sibling) for j, chip in enumerate(chips)]
        for j, chip in enumerate(chips):
            copy(1 + j, (*chip, c), me).wait_recv()
            passed[j].start()
        copy(0, sibling, me).wait_recv()
        for j, chip in enumerate(chips):
            copy(4 + j, (*chip, 1 - c), me).wait_recv()
        for cp in first + passed:
            cp.wait_send()
        mine.wait()
        if reduce:
            tot = out_ref[pl.ds(0, M), :]
            for p in range(1, N_DEV):
                tot = tot + out_ref[pl.ds(p * M, M), :]
            sum_ref[...] = tot
        else:
            sum_ref[...] = jnp.zeros_like(sum_ref)

    vm = pl.BlockSpec(memory_space=pltpu.VMEM)
    second = jax.ShapeDtypeStruct((M, N) if reduce else (8, LANE), F32)
    outs = pl.pallas_call(
        body, name="all_reduce_small" if reduce else "all_gather_small", in_specs=[vm], out_specs=[vm, vm],
        out_shape=[jax.ShapeDtypeStruct((N_DEV * M, N), v.dtype), second],
        scratch_shapes=[pltpu.SemaphoreType.DMA((7,)), pltpu.SemaphoreType.DMA((7,)), pltpu.SemaphoreType.DMA],
        compiler_params=_cp(),
    )(v)
    return outs[1] if reduce else outs


def _adamw(w, g, m, v):
    shape = w.shape
    cols = shape[-1]
    rows = w.size // cols
    tr = _row_tile(rows, 256) if rows % 8 == 0 else rows
    c1 = 1.0 / (1.0 - ADAM_B1 ** ADAM_STEP)
    c2 = 1.0 / (1.0 - ADAM_B2 ** ADAM_STEP)

    def body(w_ref, g_ref, m_ref, v_ref, d_ref, nm_ref, nv_ref):
        gv = g_ref[...]
        nm = ADAM_B1 * m_ref[...] + (1.0 - ADAM_B1) * gv
        nv = ADAM_B2 * v_ref[...] + (1.0 - ADAM_B2) * (gv * gv)
        d_ref[...] = -ADAM_LR * ((nm * c1) / (jnp.sqrt(nv * c2) + ADAM_EPS) + ADAM_WD * w_ref[...])
        nm_ref[...] = nm
        nv_ref[...] = nv

    blk = pl.BlockSpec((tr, cols), lambda i: (i, 0))
    outs = pl.pallas_call(
        body, name="adamw", grid=(rows // tr,), in_specs=[blk] * 4, out_specs=[blk] * 3,
        out_shape=[jax.ShapeDtypeStruct((rows, cols), F32)] * 3, compiler_params=_cp(("parallel",)),
    )(*[a.reshape(rows, cols) for a in (w, g, m, v)])
    return tuple(o.reshape(shape) for o in outs)


def _adamw_layer(w, g_l, m, v, l, prev):
    L, A, B = w.shape
    ta = _row_tile(A, 256) if A % 8 == 0 else A
    c1 = 1.0 / (1.0 - ADAM_B1 ** ADAM_STEP)
    c2 = 1.0 / (1.0 - ADAM_B2 ** ADAM_STEP)
    n_prev = 0 if prev is None else 4

    def body(*refs):
        w_ref, g_ref, m_ref, v_ref = refs[:4]
        go_ref, d_ref, nm_ref, nv_ref = refs[4 + n_prev:]
        gv = g_ref[...]
        nm = ADAM_B1 * m_ref[...] + (1.0 - ADAM_B1) * gv
        nv = ADAM_B2 * v_ref[...] + (1.0 - ADAM_B2) * (gv * gv)
        d_ref[...] = -ADAM_LR * ((nm * c1) / (jnp.sqrt(nv * c2) + ADAM_EPS) + ADAM_WD * w_ref[...])
        go_ref[...] = gv
        nm_ref[...] = nm
        nv_ref[...] = nv

    lay = pl.BlockSpec((None, ta, B), lambda i: (l, i, 0))
    return pl.pallas_call(
        body, name="adamw_layer", grid=(A // ta,),
        in_specs=[lay, pl.BlockSpec((ta, B), lambda i: (i, 0)), lay, lay] + [ANY] * n_prev, out_specs=[lay] * 4,
        out_shape=[jax.ShapeDtypeStruct((L, A, B), F32)] * 4, input_output_aliases={4 + j: j for j in range(n_prev)},
        compiler_params=_cp(("parallel",)),
    )(w, g_l, m, v, *(prev or ()))


def _relu2(acc):
    r = jnp.maximum(acc, 0.0)
    return acc, r * r


def _relu2_bwd(acc, up):
    return (acc * (2.0 * jnp.maximum(up.astype(F32), 0.0)),)


def kernel(x, ln_mix_pre, ln_mix_post, ln_mlp_pre, ln_mlp_post, w_in, conv_a_w, proj_a, proj_b, conv_c_w, conv_c_b, norm_c_g, norm_c_b, proj_c, w_o, w_up, w_down, loss_target, m_ln_mix_pre, m_ln_mix_post, m_ln_mlp_pre, m_ln_mlp_post, m_w_in, m_conv_a_w, m_proj_a, m_proj_b, m_conv_c_w, m_conv_c_b, m_norm_c_g, m_norm_c_b, m_proj_c, m_w_o, m_w_up, m_w_down, v_ln_mix_pre, v_ln_mix_post, v_ln_mlp_pre, v_ln_mlp_post, v_w_in, v_conv_a_w, v_proj_a, v_proj_b, v_conv_c_w, v_conv_c_b, v_norm_c_g, v_norm_c_b, v_proj_c, v_w_o, v_w_up, v_w_down):
    L, D, n_in_loc = w_in.shape
    S = x.shape[1]
    U = D // 4
    N_IN = n_in_loc * N_DEV
    D_FF = w_up.shape[2] * N_DEV
    assert N_IN == 23 * U and x.shape[0] == 1
    x_i, y_i, c_i = _place()
    me = 4 * x_i + 2 * y_i + c_i

    def pack(l, which, zero):
        kinds = (lambda: _pack_transposed([proj_b[l], proj_a[l], proj_c[l]], zero), lambda: _pack_plain(w_o[l], zero),
                 lambda: _pack_transposed([w_up[l]], zero), lambda: _pack_plain(w_down[l], zero),
                 lambda: _pack_transposed_mxu(w_in[l], zero))
        return [kinds[k]() for k in which]

    def stage1(wbs, carry, tag):
        return _ag_start(_ag_stage1, wbs, carry, f"ag_s1_{tag}")

    def stage2(started, after, carry, tag):
        send_sems, recv_sems, wbs, _ = started
        wbs = _ag_wait(_ag_stage1, wbs, send_sems, recv_sems, after, f"ag_s1_wait_{tag}")
        return _ag_start(_ag_stage2, wbs, carry, f"ag_s2_{tag}")

    def gather_end(started, after, tag):
        send_sems, recv_sems, wbs, _ = started
        wbs, zero = _ag_finish(_ag_wait(_ag_stage2, wbs, send_sems, recv_sems, after, f"ag_s2_wait_{tag}"))
        return [w.astype(MXU_DTYPE) for w in wbs], zero

    cu = U // N_DEV
    conv_loc = jnp.concatenate([conv_a_w, conv_c_w], axis=1).reshape(L * (SC_WIDTH + CF_WIDTH), cu)
    conv_all, tok = _all_gather_small(conv_loc, False)
    conv_all = conv_all.reshape(N_DEV, L, SC_WIDTH + CF_WIDTH, cu).transpose(1, 2, 0, 3).reshape(L, SC_WIDTH + CF_WIDTH, U)
    wa_full, wc_full = conv_all[:, :SC_WIDTH], conv_all[:, SC_WIDTH:]
    every = (0, 1, 2, 3, 4)
    in_s1 = stage1(pack(0, (4,), tok), jnp.zeros((8, LANE), F32), "0_in")
    rest_packed, next_packed = pack(0, (0, 1, 2, 3), tok), pack(1, every, tok)
    in_s2 = stage2(in_s1, next_packed[4], in_s1[3], "0_in")
    rest_s1 = stage1(rest_packed, in_s2[3], "0_rest")
    wb = []

    def vec(p, l):
        return p[l][None, :]

    xs = x[0]
    saved = []
    h1 = _rms_fwd(xs, vec(ln_mix_pre, 0), "rms_fwd")
    for l in range(L):
        if l == 0:
            (w_in_t,), _ = gather_end(in_s2, rest_s1[3], "0_in")
        else:
            w_p, w_o_l, w_up_t, w_dn, w_in_t = w_next
            if l + 1 < L:
                next_s1 = stage1(pack(l + 1, every, tok), h1, l + 1)
                h1 = next_s1[3]
        proj = _mm(h1, w_in_t, "NT", (F32,), "mm_proj")
        a_out = _branch_a_fwd(proj, wa_full[l], U)
        u1 = _branch_c_conv_fwd(proj, wc_full[l], vec(conv_c_b, l), U)
        u3 = _branch_c_norm_fwd(u1, vec(norm_c_g, l), vec(norm_c_b, l))
        if l == 0:
            rest_s2 = stage2(rest_s1, u3, proj, "0_rest")
            next_s1 = stage1(next_packed, rest_s2[3], 1)
            proj = next_s1[3]
        att, att_f32 = _attn_fwd(proj, U)
        if l == 0:
            (w_p, w_o_l, w_up_t, w_dn), tok = gather_end(rest_s2, att, "0_rest")
        wb.append((w_p, w_o_l, w_up_t, w_dn, w_in_t))
        yb = _mm(att, w_p, "NT", (F32,), "mm_yb", b_view=(0, 0, D, 2 * U))
        ya = _mm(a_out, w_p, "NT", (F32,), "mm_ya", b_view=(0, 2 * U, D, U))
        yc = _mm(u3, w_p, "NT", (F32,), "mm_yc", b_view=(0, 3 * U, D, U))
        merged = _gate_fwd(proj, ya, yb, yc)
        mixed = _mm(merged, w_o_l, "NN", (F32,), "mm_mixed")
        x1, h2 = _resid_post(xs, mixed, vec(ln_mix_post, l), vec(ln_mlp_pre, l), "resid_post_mix")
        if 0 < l < L - 1:
            next_s2 = stage2(next_s1, mixed, h2, l + 1)
            h2 = next_s2[3]
        up, act = _mm(h2, w_up_t, "NT", (MXU_DTYPE, MXU_DTYPE), "mm_up", epilogue=_relu2)
        if l == 0 and L > 1:
            next_s2 = stage2(next_s1, up, act, 1)
            act = next_s2[3]
        f = _mm(act, w_dn, "NN", (F32,), "mm_down")
        saved.append((xs, h1, proj, a_out, u1, u3, att, att_f32, ya, yb, yc, merged, mixed, x1, h2, up, act, f))
        if l + 1 < L:
            xs, h1 = _resid_post(x1, f, vec(ln_mlp_post, l), vec(ln_mix_pre, l + 1), "resid_post_mlp")
            w_next, tok = gather_end(next_s2, h1, l + 1)
        else:
            xs, _ = _resid_post(x1, f, vec(ln_mlp_post, l), None, "resid_post_last")
    dxo, loss_row = _loss_head(xs, loss_target[0])
    loss = lax.psum(loss_row[0, 0], ("x", "y", "c"))

    small = {k: [None] * L for k in ("g1", "g2", "g3", "g4", "cb", "ng", "nb", "wa", "wc")}
    def pair_start(grads, carry, name):
        lands = [lax.empty((4, g.shape[0] // N_DEV, D), WIRE_DTYPE) for g in grads]
        send_sems, recv_sems, grads, lands, carry = _rs_start(_pair_copies, 4, grads, lands, carry, name)
        return (grads, lands, send_sems, recv_sems), carry

    def chips_start(pair_flight, after, carry, name):
        grads, landed = _rs_wait(_pair_copies, *pair_flight, after, name + "_pair_wait")
        pair_sums = [_pair_add(g, ld) for g, ld in zip(grads, landed)]
        lands = [lax.empty((3,) + p.shape[1:], WIRE_DTYPE) for p in pair_sums]
        send_sems, recv_sems, pair_sums, lands, carry = _rs_start(_rs_copies, 3, pair_sums, lands, carry, name)
        return (pair_sums, lands, send_sems, recv_sems), carry

    in_flight = []
    mix_pairs = None
    for l in reversed(range(L)):
        w_p, w_o_l, w_up_t, w_dn, w_in_t = wb[l]
        xs, h1, proj, a_out, u1, u3, att, att_f32, ya, yb, yc, merged, mixed, x1, h2, up, act, f = saved[l]
        df, small["g4"][l] = _rms_bwd(f, vec(ln_mlp_post, l), dxo, None, MXU_DTYPE, "rms_bwd_post_mlp")
        d_up = _mm(df, w_dn, "NT", (MXU_DTYPE,), "mm_d_up", epilogue=_relu2_bwd, extras=(up,))
        g_dn = _mm(act, df, "TN", (WIRE_DTYPE,), "mm_g_down", tm=512, tn=2048)
        if mix_pairs is not None:
            flight, d_up = chips_start(mix_pairs, [g_dn], d_up, f"rs_start_mix_{l + 1}")
            in_flight.append((l + 1, ("p", "o", "in"), flight))
        dh2 = _mm(d_up, w_up_t, "NN", (F32,), "mm_dh2")
        g_up = _mm(d_up, h2, "TN", (WIRE_DTYPE,), "mm_g_up", tm=512, tn=2048)
        mlp_pairs, dh2 = pair_start([g_up, g_dn], dh2, f"rs_pair_mlp_{l}")
        dx1, small["g3"][l] = _rms_bwd(x1, vec(ln_mlp_pre, l), dh2, dxo, F32, "rms_bwd_pre_mlp")
        dmixed, small["g2"][l] = _rms_bwd(mixed, vec(ln_mix_post, l), dx1, None, MXU_DTYPE, "rms_bwd_post_mix")
        dmerged = _mm(dmixed, w_o_l, "NT", (F32,), "mm_dmerged")
        g_o = _mm(merged, dmixed, "TN", (WIRE_DTYPE,), "mm_g_o", tm=512, tn=2048)
        flight, dmerged = chips_start(mlp_pairs, [g_o], dmerged, f"rs_start_mlp_{l}")
        in_flight.append((l, ("up", "dn"), flight))
        dya, dyb, dyc, dgla, dglb, dglc = _gate_bwd(dmerged, proj, ya, yb, yc)
        d_att = _mm(dyb, w_p, "NN", (F32,), "mm_d_att", b_view=(0, 0, D, 2 * U))
        d_a_out = _mm(dya, w_p, "NN", (F32,), "mm_d_a_out", b_view=(0, 2 * U, D, U))
        d_u3 = _mm(dyc, w_p, "NN", (F32,), "mm_d_u3", b_view=(0, 3 * U, D, U))
        g_pb = _mm(dyb, att, "TN", (WIRE_DTYPE,), "mm_g_pb", tm=512, tn=2048)
        g_pa = _mm(dya, a_out, "TN", (WIRE_DTYPE,), "mm_g_pa", tm=512, tn=2048)
        g_pc = _mm(dyc, u3, "TN", (WIRE_DTYPE,), "mm_g_pc", tm=512, tn=2048)
        d_scb, d_scc, d_scu, small["wa"][l] = _branch_a_bwd(d_a_out, proj, wa_full[l], U)
        d_u1, small["ng"][l], small["nb"][l] = _branch_c_norm_bwd(d_u3, u1, vec(norm_c_g, l), vec(norm_c_b, l))
        d_cfa, d_cfg, small["wc"][l], small["cb"][l] = _branch_c_conv_bwd(d_u1, proj, wc_full[l], U)
        dq, dk, dv = _attn_bwd(proj, att_f32, d_att, U)
        dproj = jnp.concatenate([dq, dk, dv, d_scb, d_scc, d_scu, d_cfa, d_cfg, dgla, dglb, dglc], axis=1)
        dh1 = _mm(dproj, w_in_t, "NN", (F32,), "mm_dh1", tk=23 * LANE)
        g_in = _mm(dproj, h1, "TN", (WIRE_DTYPE,), "mm_g_in", tm=512, tn=2048)
        dxo, small["g1"][l] = _rms_bwd(xs, vec(ln_mix_pre, l), dh1, dx1, F32, "rms_bwd_pre_mix")
        carry = dxo if l > 0 else jnp.zeros((8, LANE), F32)
        mix_pairs, carry = pair_start([jnp.concatenate([g_pb, g_pa, g_pc], axis=1), g_o, g_in], carry, f"rs_pair_mix_{l}")
        if l > 0:
            dxo = carry
    grad_x = dxo[None]
    flight, carry = chips_start(mix_pairs, [carry], jnp.zeros((8, LANE), F32), "rs_start_mix_0")
    in_flight.append((0, ("p", "o", "in"), flight))
    big = {"w_in": (w_in, m_w_in, v_w_in), "proj_a": (proj_a, m_proj_a, v_proj_a), "proj_b": (proj_b, m_proj_b, v_proj_b),
           "proj_c": (proj_c, m_proj_c, v_proj_c), "w_o": (w_o, m_w_o, v_w_o), "w_up": (w_up, m_w_up, v_w_up),
           "w_down": (w_down, m_w_down, v_w_down)}
    done = {k: None for k in big}
    for n, (l, keys, flight) in enumerate(in_flight):
        after = [carry]
        if n == len(in_flight) - 1:
            after += [done[k][3] for k in big if done[k] is not None]
        pair_sums, lands = _rs_wait(_rs_copies, *flight, after, f"rs_wait_{keys[0]}_{l}")
        g = {key: _final_sum(p, ld) for key, p, ld in zip(keys, pair_sums, lands)}
        if "up" in g:
            layer_grads = {"w_up": g["up"].T, "w_down": g["dn"]}
        else:
            layer_grads = {"w_in": g["in"].T, "proj_b": g["p"][:, :2 * U].T, "proj_a": g["p"][:, 2 * U:3 * U].T,
                           "proj_c": g["p"][:, 3 * U:].T, "w_o": g["o"]}
        for k, g_l in layer_grads.items():
            done[k] = _adamw_layer(*big[k][:1], g_l, *big[k][1:], l, done[k])

    order = ("g1", "g2", "g3", "g4", "cb", "ng", "nb", "wa", "wc")
    parts = [jnp.stack(small[k]).reshape(-1) for k in order]
    flat = jnp.concatenate(parts)
    n_flat = flat.shape[0]
    pad = (-n_flat) % (8 * LANE)
    flat = jnp.pad(flat, (0, pad)).reshape(-1, LANE)
    tot = _all_gather_small(flat, True).reshape(-1)[:n_flat]
    red, pos = {}, 0
    for k, p in zip(order, parts):
        red[k] = tot[pos:pos + p.shape[0]]
        pos += p.shape[0]
    g_ln_mix_pre, g_ln_mix_post = red["g1"].reshape(L, D), red["g2"].reshape(L, D)
    g_ln_mlp_pre, g_ln_mlp_post = red["g3"].reshape(L, D), red["g4"].reshape(L, D)
    g_conv_c_b, g_norm_c_g, g_norm_c_b = red["cb"].reshape(L, U), red["ng"].reshape(L, U), red["nb"].reshape(L, U)
    g_conv_a_w = lax.dynamic_slice_in_dim(red["wa"].reshape(L, SC_WIDTH, U), me * cu, cu, axis=2)
    g_conv_c_w = lax.dynamic_slice_in_dim(red["wc"].reshape(L, CF_WIDTH, U), me * cu, cu, axis=2)

    small_w = {"ln_mix_pre": (ln_mix_pre, g_ln_mix_pre, m_ln_mix_pre, v_ln_mix_pre),
               "ln_mix_post": (ln_mix_post, g_ln_mix_post, m_ln_mix_post, v_ln_mix_post),
               "ln_mlp_pre": (ln_mlp_pre, g_ln_mlp_pre, m_ln_mlp_pre, v_ln_mlp_pre),
               "ln_mlp_post": (ln_mlp_post, g_ln_mlp_post, m_ln_mlp_post, v_ln_mlp_post),
               "conv_a_w": (conv_a_w, g_conv_a_w, m_conv_a_w, v_conv_a_w), "conv_c_w": (conv_c_w, g_conv_c_w, m_conv_c_w, v_conv_c_w),
               "conv_c_b": (conv_c_b, g_conv_c_b, m_conv_c_b, v_conv_c_b), "norm_c_g": (norm_c_g, g_norm_c_g, m_norm_c_g, v_norm_c_g),
               "norm_c_b": (norm_c_b, g_norm_c_b, m_norm_c_b, v_norm_c_b)}
    for k, (w, g, m, v) in small_w.items():
        done[k] = (g,) + _adamw(w, g, m, v)
    names = ("ln_mix_pre", "ln_mix_post", "ln_mlp_pre", "ln_mlp_post", "w_in", "conv_a_w", "proj_a", "proj_b", "conv_c_w", "conv_c_b",
             "norm_c_g", "norm_c_b", "proj_c", "w_o", "w_up", "w_down")
    return (loss, grad_x, *[done[k][0] for k in names], *[done[k][1] for k in names], *[done[k][2] for k in names],
            *[done[k][3] for k in names])
```

```python
import functools

import jax
import jax.numpy as jnp
from jax import lax
from jax.experimental import pallas as pl
from jax.experimental.pallas import tpu as pltpu

F32 = jnp.float32
MXU_DTYPE = jnp.bfloat16
WIRE_DTYPE = jnp.bfloat16
MESH = pl.DeviceIdType.MESH
ANY = pl.BlockSpec(memory_space=pl.ANY)
HBM = pl.BlockSpec(memory_space=pltpu.HBM)
SEM = pl.BlockSpec(memory_space=pltpu.SEMAPHORE)
EFFECT = pltpu.SideEffectType.DATAFLOW_SIDE_EFFECTING

N_DEV = 8
HEAD_DIM = 128
RMS_EPS = 1e-6
LN_EPS = 1e-5
SC_WIDTH = 3
CF_WIDTH = 31
CONV_PAD = 32
ADAM_LR, ADAM_B1, ADAM_B2, ADAM_EPS, ADAM_WD, ADAM_STEP = 0.001, 0.9, 0.999, 1e-08, 0.01, 10
VMEM_LIMIT = 56 * 1024 * 1024
LANE = 128


def _cp(sem=None, **kw):
    return pltpu.CompilerParams(dimension_semantics=sem, vmem_limit_bytes=VMEM_LIMIT, **kw)


def _sigmoid(x):
    return 1.0 / (1.0 + jnp.exp(-x))


def _row_tile(rows, want):
    t = min(rows, want)
    while rows % t:
        t //= 2
    return t


_DN = {"NN": (((1,), (0,)), ((), ())), "NT": (((1,), (1,)), ((), ())), "TN": (((0,), (0,)), ((), ()))}


def _mm(a, b, mode, out_dtypes, name, *, a_view=None, b_view=None, tm=2048, tn=512, tk=2048, epilogue=None, extras=()):
    a_view = a_view or (0, 0) + tuple(a.shape)
    b_view = b_view or (0, 0) + tuple(b.shape)
    ar, ac, an, am = a_view
    br, bc, bn, bm = b_view
    if mode == "NN":
        M, K, K2, N = an, am, bn, bm
    elif mode == "NT":
        M, K, N, K2 = an, am, bn, bm
    else:
        K, M, K2, N = an, am, bn, bm
    assert K == K2, (name, a_view, b_view)
    tm, tn, tk = _row_tile(M, tm), _row_tile(N, tn), _row_tile(K, tk)
    (a_m_off, a_k_off) = (ac, ar) if mode == "TN" else (ar, ac)
    (b_n_off, b_k_off) = (br, bc) if mode == "NT" else (bc, br)
    while a_m_off % tm:
        tm //= 2
    while b_n_off % tn:
        tn //= 2
    while a_k_off % tk or b_k_off % tk:
        tk //= 2
    nk = K // tk
    a_blk = (tk, tm) if mode == "TN" else (tm, tk)
    b_blk = (tn, tk) if mode == "NT" else (tk, tn)
    assert ar % a_blk[0] == 0 and ac % a_blk[1] == 0, (name, a_view, a_blk)
    assert br % b_blk[0] == 0 and bc % b_blk[1] == 0, (name, b_view, b_blk)
    ao, bo = (ar // a_blk[0], ac // a_blk[1]), (br // b_blk[0], bc // b_blk[1])
    if mode == "TN":
        a_spec = pl.BlockSpec(a_blk, lambda i, j, k: (ao[0] + k, ao[1] + i))
    else:
        a_spec = pl.BlockSpec(a_blk, lambda i, j, k: (ao[0] + i, ao[1] + k))
    if mode == "NT":
        b_spec = pl.BlockSpec(b_blk, lambda i, j, k: (bo[0] + j, bo[1] + k))
    else:
        b_spec = pl.BlockSpec(b_blk, lambda i, j, k: (bo[0] + k, bo[1] + j))
    o_spec = pl.BlockSpec((tm, tn), lambda i, j, k: (i, j))
    n_ex, n_out = len(extras), len(out_dtypes)
    dn = _DN[mode]

    def body(*refs):
        a_ref, b_ref = refs[:2]
        ex_refs = refs[2:2 + n_ex]
        o_refs = refs[2 + n_ex:2 + n_ex + n_out]
        p = lax.dot_general(a_ref[...], b_ref[...], dn, preferred_element_type=F32)

        def finish(acc):
            outs = epilogue(acc, *[r[...] for r in ex_refs]) if epilogue else (acc,)
            for o_ref, o in zip(o_refs, outs):
                o_ref[...] = o.astype(o_ref.dtype)

        if nk == 1:
            finish(p)
        else:
            acc_ref = refs[-1]
            k = pl.program_id(2)

            @pl.when(k == 0)
            def _():
                acc_ref[...] = p

            @pl.when(k > 0)
            def _():
                acc_ref[...] += p

            @pl.when(k == nk - 1)
            def _():
                finish(acc_ref[...])

    outs = pl.pallas_call(
        body, name=name, grid=(M // tm, N // tn, nk),
        in_specs=[a_spec, b_spec] + [o_spec] * n_ex, out_specs=[o_spec] * n_out,
        out_shape=[jax.ShapeDtypeStruct((M, N), d) for d in out_dtypes],
        scratch_shapes=[pltpu.VMEM((tm, tn), F32)] if nk > 1 else [],
        compiler_params=_cp(("parallel", "parallel", "arbitrary")),
    )(a, b, *extras)
    return outs[0] if n_out == 1 else outs


def _rms_fwd(x, g, name):
    S, D = x.shape
    tr = _row_tile(S, 256)

    def body(x_ref, g_ref, h_ref):
        xv = x_ref[...]
        r = lax.rsqrt(jnp.mean(xv * xv, axis=-1, keepdims=True) + RMS_EPS)
        h_ref[...] = ((xv * r) * g_ref[...]).astype(h_ref.dtype)

    return pl.pallas_call(
        body, name=name, grid=(S // tr,),
        in_specs=[pl.BlockSpec((tr, D), lambda i: (i, 0)), pl.BlockSpec((1, D), lambda i: (0, 0))],
        out_specs=pl.BlockSpec((tr, D), lambda i: (i, 0)),
        out_shape=jax.ShapeDtypeStruct((S, D), MXU_DTYPE), compiler_params=_cp(("parallel",)),
    )(x, g)


def _resid_post(xres, y, g_post, g_next, name):
    S, D = y.shape
    tr = _row_tile(S, 256)
    has_next = g_next is not None

    def body(*refs):
        xr_ref, y_ref, gp_ref = refs[:3]
        yv = y_ref[...]
        r = lax.rsqrt(jnp.mean(yv * yv, axis=-1, keepdims=True) + RMS_EPS)
        xn = xr_ref[...] + (yv * r) * gp_ref[...]
        if has_next:
            gn_ref, xo_ref, h_ref = refs[3:]
            r2 = lax.rsqrt(jnp.mean(xn * xn, axis=-1, keepdims=True) + RMS_EPS)
            h_ref[...] = ((xn * r2) * gn_ref[...]).astype(h_ref.dtype)
        else:
            xo_ref = refs[3]
        xo_ref[...] = xn

    row = pl.BlockSpec((tr, D), lambda i: (i, 0))
    vec = pl.BlockSpec((1, D), lambda i: (0, 0))
    outs = pl.pallas_call(
        body, name=name, grid=(S // tr,),
        in_specs=[row, row, vec] + ([vec] if has_next else []),
        out_specs=[row] + ([row] if has_next else []),
        out_shape=[jax.ShapeDtypeStruct((S, D), F32)] + ([jax.ShapeDtypeStruct((S, D), MXU_DTYPE)] if has_next else []),
        compiler_params=_cp(("parallel",)),
    )(xres, y, g_post, *([g_next] if has_next else []))
    return (outs[0], outs[1]) if has_next else (outs[0], None)


def _rms_bwd(xin, g, dy, dres, out_dtype, name):
    S, D = xin.shape
    tr = _row_tile(S, 256)
    has_res = dres is not None

    def body(*refs):
        x_ref, g_ref, dy_ref = refs[:3]
        dx_ref, dg_ref = refs[-2:]
        xv, dyv = x_ref[...], dy_ref[...].astype(F32)
        r = lax.rsqrt(jnp.mean(xv * xv, axis=-1, keepdims=True) + RMS_EPS)
        n = xv * r
        dyg = dyv * g_ref[...]
        dx = r * (dyg - n * jnp.mean(dyg * n, axis=-1, keepdims=True))
        if has_res:
            dx = dx + refs[3][...]
        dx_ref[...] = dx.astype(dx_ref.dtype)

        @pl.when(pl.program_id(0) == 0)
        def _():
            dg_ref[...] = jnp.zeros_like(dg_ref)

        dg_ref[...] += jnp.sum(dyv * n, axis=0, keepdims=True)

    row = pl.BlockSpec((tr, D), lambda i: (i, 0))
    vec = pl.BlockSpec((1, D), lambda i: (0, 0))
    return pl.pallas_call(
        body, name=name, grid=(S // tr,),
        in_specs=[row, vec, row] + ([row] if has_res else []), out_specs=[row, vec],
        out_shape=[jax.ShapeDtypeStruct((S, D), out_dtype), jax.ShapeDtypeStruct((1, D), F32)],
        compiler_params=_cp(("arbitrary",)),
    )(xin, g, dy, *([dres] if has_res else []))


def _loss_head(y, target):
    S, D = y.shape
    tr = _row_tile(S, 256)

    def body(y_ref, t_ref, dy_ref, l_ref):
        e = y_ref[...] - t_ref[...]
        dy_ref[...] = e * (1.0 / D)

        @pl.when(pl.program_id(0) == 0)
        def _():
            l_ref[...] = jnp.zeros_like(l_ref)

        l_ref[...] += 0.5 * jnp.sum(jnp.mean(e * e, axis=-1, keepdims=True), axis=0, keepdims=True)

    row = pl.BlockSpec((tr, D), lambda i: (i, 0))
    return pl.pallas_call(
        body, name="loss_head", grid=(S // tr,), in_specs=[row, row],
        out_specs=[row, pl.BlockSpec((1, LANE), lambda i: (0, 0))],
        out_shape=[jax.ShapeDtypeStruct((S, D), F32), jax.ShapeDtypeStruct((1, LANE), F32)],
        compiler_params=_cp(("arbitrary",)),
    )(y, target)


def _gate_specs(S, U, tr):
    gl = [pl.BlockSpec((tr, U), functools.partial(lambda i, j, o: (i, o + j), o=o)) for o in (11, 15, 19)]
    return gl, pl.BlockSpec((tr, U), lambda i, j: (i, j))


def _gate_fwd(proj, ya, yb, yc):
    S, D = ya.shape
    U = D // 4
    tr = _row_tile(S, 256)

    def body(ga_ref, gb_ref, gc_ref, ya_ref, yb_ref, yc_ref, o_ref):
        m = _sigmoid(ga_ref[...]) * ya_ref[...] + _sigmoid(gb_ref[...]) * yb_ref[...] + _sigmoid(gc_ref[...]) * yc_ref[...]
        o_ref[...] = m.astype(o_ref.dtype)

    gl, blk = _gate_specs(S, U, tr)
    return pl.pallas_call(
        body, name="gate_fwd", grid=(S // tr, 4), in_specs=gl + [blk] * 3, out_specs=blk,
        out_shape=jax.ShapeDtypeStruct((S, D), MXU_DTYPE), compiler_params=_cp(("parallel", "parallel")),
    )(proj, proj, proj, ya, yb, yc)


def _gate_bwd(dm, proj, ya, yb, yc):
    S, D = ya.shape
    U = D // 4
    tr = _row_tile(S, 256)

    def body(dm_ref, ga_ref, gb_ref, gc_ref, ya_ref, yb_ref, yc_ref, da_ref, db_ref, dc_ref, la_ref, lb_ref, lc_ref):
        d = dm_ref[...]
        for g_ref, y_ref, dy_ref, dl_ref in ((ga_ref, ya_ref, da_ref, la_ref), (gb_ref, yb_ref, db_ref, lb_ref),
                                             (gc_ref, yc_ref, dc_ref, lc_ref)):
            g = _sigmoid(g_ref[...])
            dy_ref[...] = (d * g).astype(dy_ref.dtype)
            dl_ref[...] = (d * y_ref[...] * g * (1.0 - g)).astype(dl_ref.dtype)

    gl, blk = _gate_specs(S, U, tr)
    return pl.pallas_call(
        body, name="gate_bwd", grid=(S // tr, 4), in_specs=[blk] + gl + [blk] * 3, out_specs=[blk] * 6,
        out_shape=[jax.ShapeDtypeStruct((S, D), MXU_DTYPE)] * 6, compiler_params=_cp(("parallel", "parallel")),
    )(dm, proj, proj, proj, ya, yb, yc)


def _chunks(S):
    r = _row_tile(S, 256)
    return [(r0, r) for r0 in range(0, S, r)]


def _conv_causal(front_ref, w_ref, K, r0, R):
    acc = None
    for j in range(K):
        term = w_ref[pl.ds(K - 1 - j, 1), :] * front_ref[pl.ds(CONV_PAD + r0 - j, R), :]
        acc = term if acc is None else acc + term
    return acc


def _conv_anticausal(back_ref, w_ref, K, r0, R):
    acc = None
    for j in range(K):
        term = w_ref[pl.ds(K - 1 - j, 1), :] * back_ref[pl.ds(r0 + j, R), :]
        acc = term if acc is None else acc + term
    return acc


def _conv_wgrad(front_ref, back_ref, dw_ref, K, S):
    for j in range(K):
        tot = None
        for r0, R in _chunks(S):
            part = jnp.sum(back_ref[pl.ds(r0, R), :] * front_ref[pl.ds(CONV_PAD + r0 - j, R), :], axis=0, keepdims=True)
            tot = part if tot is None else tot + part
        dw_ref[pl.ds(K - 1 - j, 1), :] = tot


def _col(S, cw, unit_off):
    return pl.BlockSpec((S, cw), functools.partial(lambda cb, o: (0, o + cb), o=unit_off))


def _branch_a_fwd(proj, wa, U):
    S = proj.shape[0]
    cw = min(LANE, U)
    nb = U // cw
    K = SC_WIDTH

    def body(b_ref, c_ref, u_ref, w_ref, o_ref, front):
        front[pl.ds(0, CONV_PAD), :] = jnp.zeros((CONV_PAD, cw), F32)
        front[pl.ds(CONV_PAD, S), :] = c_ref[...] * u_ref[...]
        for r0, R in _chunks(S):
            o_ref[pl.ds(r0, R), :] = (b_ref[pl.ds(r0, R), :] * _conv_causal(front, w_ref, K, r0, R)).astype(o_ref.dtype)

    return pl.pallas_call(
        body, name="branch_a_fwd", grid=(nb,),
        in_specs=[_col(S, cw, 6 * nb), _col(S, cw, 7 * nb), _col(S, cw, 8 * nb), pl.BlockSpec((K, cw), lambda cb: (0, cb))],
        out_specs=pl.BlockSpec((S, cw), lambda cb: (0, cb)), out_shape=jax.ShapeDtypeStruct((S, U), MXU_DTYPE),
        scratch_shapes=[pltpu.VMEM((S + CONV_PAD, cw), F32)], compiler_params=_cp(("parallel",)),
    )(proj, proj, proj, wa)


def _branch_a_bwd(d_out, proj, wa, U):
    S = proj.shape[0]
    cw = min(LANE, U)
    nb = U // cw
    K = SC_WIDTH

    def body(d_ref, b_ref, c_ref, u_ref, w_ref, db_ref, dc_ref, du_ref, dw_ref, front, back):
        front[pl.ds(0, CONV_PAD), :] = jnp.zeros((CONV_PAD, cw), F32)
        front[pl.ds(CONV_PAD, S), :] = c_ref[...] * u_ref[...]
        back[pl.ds(S, CONV_PAD), :] = jnp.zeros((CONV_PAD, cw), F32)
        back[pl.ds(0, S), :] = d_ref[...] * b_ref[...]
        for r0, R in _chunks(S):
            rows = pl.ds(r0, R)
            db_ref[rows, :] = (d_ref[rows, :] * _conv_causal(front, w_ref, K, r0, R)).astype(db_ref.dtype)
            d_ai = _conv_anticausal(back, w_ref, K, r0, R)
            dc_ref[rows, :] = (d_ai * u_ref[rows, :]).astype(dc_ref.dtype)
            du_ref[rows, :] = (d_ai * c_ref[rows, :]).astype(du_ref.dtype)
        _conv_wgrad(front, back, dw_ref, K, S)

    blk = pl.BlockSpec((S, cw), lambda cb: (0, cb))
    wblk = pl.BlockSpec((K, cw), lambda cb: (0, cb))
    return pl.pallas_call(
        body, name="branch_a_bwd", grid=(nb,),
        in_specs=[blk, _col(S, cw, 6 * nb), _col(S, cw, 7 * nb), _col(S, cw, 8 * nb), wblk],
        out_specs=[blk, blk, blk, wblk],
        out_shape=[jax.ShapeDtypeStruct((S, U), MXU_DTYPE)] * 3 + [jax.ShapeDtypeStruct((K, U), F32)],
        scratch_shapes=[pltpu.VMEM((S + CONV_PAD, cw), F32)] * 2, compiler_params=_cp(("parallel",)),
    )(d_out, proj, proj, proj, wa)


def _branch_c_conv_fwd(proj, wc, cb, U):
    S = proj.shape[0]
    cw = min(LANE, U)
    nb = U // cw
    K = CF_WIDTH

    def body(a_ref, g_ref, w_ref, bias_ref, o_ref, front):
        front[pl.ds(0, CONV_PAD), :] = jnp.zeros((CONV_PAD, cw), F32)
        front[pl.ds(CONV_PAD, S), :] = a_ref[...] * _sigmoid(g_ref[...])
        for r0, R in _chunks(S):
            o_ref[pl.ds(r0, R), :] = _conv_causal(front, w_ref, K, r0, R) + bias_ref[...]

    return pl.pallas_call(
        body, name="branch_c_conv_fwd", grid=(nb,),
        in_specs=[_col(S, cw, 9 * nb), _col(S, cw, 10 * nb), pl.BlockSpec((K, cw), lambda c: (0, c)),
                  pl.BlockSpec((1, cw), lambda c: (0, c))],
        out_specs=pl.BlockSpec((S, cw), lambda c: (0, c)), out_shape=jax.ShapeDtypeStruct((S, U), F32),
        scratch_shapes=[pltpu.VMEM((S + CONV_PAD, cw), F32)], compiler_params=_cp(("parallel",)),
    )(proj, proj, wc, cb)


def _branch_c_conv_bwd(d_u1, proj, wc, U):
    S = proj.shape[0]
    cw = min(LANE, U)
    nb = U // cw
    K = CF_WIDTH

    def body(d_ref, a_ref, g_ref, w_ref, da_ref, dg_ref, dw_ref, dbias_ref, front, back):
        sg = _sigmoid(g_ref[...])
        front[pl.ds(0, CONV_PAD), :] = jnp.zeros((CONV_PAD, cw), F32)
        front[pl.ds(CONV_PAD, S), :] = a_ref[...] * sg
        back[pl.ds(S, CONV_PAD), :] = jnp.zeros((CONV_PAD, cw), F32)
        back[pl.ds(0, S), :] = d_ref[...]
        dbias_ref[...] = jnp.sum(d_ref[...], axis=0, keepdims=True)
        for r0, R in _chunks(S):
            rows = pl.ds(r0, R)
            d_u0 = _conv_anticausal(back, w_ref, K, r0, R)
            s = _sigmoid(g_ref[rows, :])
            da_ref[rows, :] = (d_u0 * s).astype(da_ref.dtype)
            dg_ref[rows, :] = (d_u0 * a_ref[rows, :] * s * (1.0 - s)).astype(dg_ref.dtype)
        _conv_wgrad(front, back, dw_ref, K, S)

    blk = pl.BlockSpec((S, cw), lambda c: (0, c))
    wblk = pl.BlockSpec((K, cw), lambda c: (0, c))
    vblk = pl.BlockSpec((1, cw), lambda c: (0, c))
    return pl.pallas_call(
        body, name="branch_c_conv_bwd", grid=(nb,),
        in_specs=[blk, _col(S, cw, 9 * nb), _col(S, cw, 10 * nb), wblk], out_specs=[blk, blk, wblk, vblk],
        out_shape=[jax.ShapeDtypeStruct((S, U), MXU_DTYPE)] * 2 + [jax.ShapeDtypeStruct((K, U), F32), jax.ShapeDtypeStruct((1, U), F32)],
        scratch_shapes=[pltpu.VMEM((S + CONV_PAD, cw), F32)] * 2, compiler_params=_cp(("parallel",)),
    )(d_u1, proj, proj, wc)


def _branch_c_norm_fwd(u1, ng, nbias):
    S, U = u1.shape
    tr = _row_tile(S, 256)

    def body(u_ref, g_ref, b_ref, o_ref):
        u = u_ref[...]
        mu = jnp.mean(u, axis=-1, keepdims=True)
        var = jnp.mean(jnp.square(u - mu), axis=-1, keepdims=True)
        u2 = ((u - mu) * lax.rsqrt(var + LN_EPS)) * g_ref[...] + b_ref[...]
        o_ref[...] = (u2 * _sigmoid(u2)).astype(o_ref.dtype)

    row = pl.BlockSpec((tr, U), lambda i: (i, 0))
    vec = pl.BlockSpec((1, U), lambda i: (0, 0))
    return pl.pallas_call(
        body, name="branch_c_norm_fwd", grid=(S // tr,), in_specs=[row, vec, vec], out_specs=row,
        out_shape=jax.ShapeDtypeStruct((S, U), MXU_DTYPE), compiler_params=_cp(("parallel",)),
    )(u1, ng, nbias)


def _branch_c_norm_bwd(d_u3, u1, ng, nbias):
    S, U = u1.shape
    tr = _row_tile(S, 256)

    def body(d_ref, u_ref, g_ref, b_ref, du_ref, dg_ref, db_ref):
        u = u_ref[...]
        mu = jnp.mean(u, axis=-1, keepdims=True)
        var = jnp.mean(jnp.square(u - mu), axis=-1, keepdims=True)
        rstd = lax.rsqrt(var + LN_EPS)
        xh = (u - mu) * rstd
        u2 = xh * g_ref[...] + b_ref[...]
        s = _sigmoid(u2)
        d_u2 = d_ref[...] * (s * (1.0 + u2 * (1.0 - s)))
        d_xh = d_u2 * g_ref[...]
        du_ref[...] = rstd * (d_xh - jnp.mean(d_xh, axis=-1, keepdims=True) - xh * jnp.mean(d_xh * xh, axis=-1, keepdims=True))

        @pl.when(pl.program_id(0) == 0)
        def _():
            dg_ref[...] = jnp.zeros_like(dg_ref)
            db_ref[...] = jnp.zeros_like(db_ref)

        dg_ref[...] += jnp.sum(d_u2 * xh, axis=0, keepdims=True)
        db_ref[...] += jnp.sum(d_u2, axis=0, keepdims=True)

    row = pl.BlockSpec((tr, U), lambda i: (i, 0))
    vec = pl.BlockSpec((1, U), lambda i: (0, 0))
    return pl.pallas_call(
        body, name="branch_c_norm_bwd", grid=(S // tr,), in_specs=[row, row, vec, vec], out_specs=[row, vec, vec],
        out_shape=[jax.ShapeDtypeStruct((S, U), F32), jax.ShapeDtypeStruct((1, U), F32), jax.ShapeDtypeStruct((1, U), F32)],
        compiler_params=_cp(("arbitrary",)),
    )(d_u3, u1, ng, nbias)


def _tri(T, inclusive):
    j = lax.broadcasted_iota(jnp.int32, (T, T), 0)
    s = lax.broadcasted_iota(jnp.int32, (T, T), 1)
    return ((j >= s) if inclusive else (j > s)).astype(MXU_DTYPE)


def _split_dot(x, tri):
    if MXU_DTYPE == F32:
        return jnp.dot(x, tri, preferred_element_type=F32)
    hi = x.astype(MXU_DTYPE)
    lo = (x - hi.astype(F32)).astype(MXU_DTYPE)
    return jnp.dot(hi, tri, preferred_element_type=F32) + jnp.dot(lo, tri, preferred_element_type=F32)


def _sb_block(qb, kb, T, tri_strict, c_lf, diag):
    z = lax.dot_general(qb, kb, _DN["NT"], preferred_element_type=F32) * (HEAD_DIM ** -0.5)
    e = jnp.exp(-jnp.abs(z))
    lg = jnp.log(1.0 + e)
    log_beta = jnp.minimum(z, 0.0) - lg
    lf = jnp.minimum(-z, 0.0) - lg
    mask = None
    if diag:
        mask = lax.broadcasted_iota(jnp.int32, (T, T), 1) < lax.broadcasted_iota(jnp.int32, (T, T), 0)
        lf = jnp.where(mask, lf, 0.0)
    a = jnp.exp(log_beta + _split_dot(lf, tri_strict) + c_lf)
    if diag:
        a = jnp.where(mask, a, 0.0)
    return z, e, mask, lf, a


def _key_blocks(i, step, init):
    carry = step(i, init, True)
    return lax.fori_loop(1, i + 1, lambda jj, c: step(i - jj, c, False), carry)


def _attn_specs(S, U, h_blocks):
    nh = (2 * U) // HEAD_DIM
    return [pl.BlockSpec((S, HEAD_DIM), functools.partial(lambda h, o: (0, o + h), o=o * nh)) for o in range(h_blocks)]


def _attn_fwd(proj, U):
    S = proj.shape[0]
    nh = (2 * U) // HEAD_DIM
    T = _row_tile(S, 256)
    nq = S // T

    def body(q_ref, k_ref, v_ref, o_ref, of_ref, qs, ks, vs):
        qs[...] = q_ref[...].astype(MXU_DTYPE)
        ks[...] = k_ref[...].astype(MXU_DTYPE)
        vs[...] = v_ref[...].astype(MXU_DTYPE)
        tri = _tri(T, False)

        def q_loop(i, _):
            rows = pl.ds(pl.multiple_of(i * T, T), T)
            qb = qs[rows, :]

            def step(j, carry, diag):
                c_lf, acc = carry
                cols = pl.ds(pl.multiple_of(j * T, T), T)
                _, _, _, lf, a = _sb_block(qb, ks[cols, :], T, tri, c_lf, diag)
                acc = acc + jnp.dot(a.astype(MXU_DTYPE), vs[cols, :], preferred_element_type=F32)
                return c_lf + jnp.sum(lf, axis=1, keepdims=True), acc

            _, acc = _key_blocks(i, step, (jnp.zeros((T, 1), F32), jnp.zeros((T, HEAD_DIM), F32)))
            o_ref[rows, :] = acc.astype(o_ref.dtype)
            of_ref[rows, :] = acc
            return 0

        lax.fori_loop(0, nq, q_loop, 0)

    hblk = pl.BlockSpec((S, HEAD_DIM), lambda h: (0, h))
    return pl.pallas_call(
        body, name="attn_fwd", grid=(nh,), in_specs=_attn_specs(S, U, 3), out_specs=[hblk, hblk],
        out_shape=[jax.ShapeDtypeStruct((S, 2 * U), MXU_DTYPE), jax.ShapeDtypeStruct((S, 2 * U), F32)],
        scratch_shapes=[pltpu.VMEM((S, HEAD_DIM), MXU_DTYPE)] * 3, compiler_params=_cp(("parallel",)),
    )(proj, proj, proj)


def _attn_bwd(proj, att_f32, d_att, U):
    S = proj.shape[0]
    nh = (2 * U) // HEAD_DIM
    T = _row_tile(S, 256)
    nq = S // T
    scale = HEAD_DIM ** -0.5

    def body(q_ref, k_ref, v_ref, o_ref, do_ref, dq_ref, dk_ref, dv_ref, qs, ks, vs, dos, dka, dva):
        qs[...] = q_ref[...].astype(MXU_DTYPE)
        ks[...] = k_ref[...].astype(MXU_DTYPE)
        vs[...] = v_ref[...].astype(MXU_DTYPE)
        dos[...] = do_ref[...].astype(MXU_DTYPE)
        dka[...] = jnp.zeros_like(dka)
        dva[...] = jnp.zeros_like(dva)
        tri = _tri(T, False)
        tri_inc = _tri(T, True)

        def q_loop(i, _):
            rows = pl.ds(pl.multiple_of(i * T, T), T)
            qb = qs[rows, :]
            dob = dos[rows, :]
            delta = jnp.sum(dob.astype(F32) * o_ref[rows, :], axis=1, keepdims=True)

            def step(j, carry, diag):
                c_lf, c_g, dq = carry
                cols = pl.ds(pl.multiple_of(j * T, T), T)
                kb, vb = ks[cols, :], vs[cols, :]
                z, e, mask, lf, a = _sb_block(qb, kb, T, tri, c_lf, diag)
                a_mx = a.astype(MXU_DTYPE)
                d_a = lax.dot_general(dob, vb, _DN["NT"], preferred_element_type=F32)
                g = a_mx.astype(F32) * d_a
                prefix = delta - (_split_dot(g, tri_inc) + c_g)
                inv = 1.0 / (1.0 + e)
                beta = jnp.where(z >= 0.0, 1.0, e) * inv
                one_m_beta = jnp.where(z >= 0.0, e, 1.0) * inv
                dz = (g * one_m_beta - prefix * beta) * scale
                if diag:
                    dz = jnp.where(mask, dz, 0.0)
                dz = dz.astype(MXU_DTYPE)
                dq = dq + jnp.dot(dz, kb, preferred_element_type=F32)
                dka[cols, :] += lax.dot_general(dz, qb, _DN["TN"], preferred_element_type=F32)
                dva[cols, :] += lax.dot_general(a_mx, dob, _DN["TN"], preferred_element_type=F32)
                return c_lf + jnp.sum(lf, axis=1, keepdims=True), c_g + jnp.sum(g, axis=1, keepdims=True), dq

            zero = jnp.zeros((T, 1), F32)
            _, _, dq = _key_blocks(i, step, (zero, zero, jnp.zeros((T, HEAD_DIM), F32)))
            dq_ref[rows, :] = dq.astype(dq_ref.dtype)
            return 0

        lax.fori_loop(0, nq, q_loop, 0)
        dk_ref[...] = dka[...].astype(dk_ref.dtype)
        dv_ref[...] = dva[...].astype(dv_ref.dtype)

    hblk = pl.BlockSpec((S, HEAD_DIM), lambda h: (0, h))
    return pl.pallas_call(
        body, name="attn_bwd", grid=(nh,), in_specs=_attn_specs(S, U, 3) + [hblk, hblk], out_specs=[hblk] * 3,
        out_shape=[jax.ShapeDtypeStruct((S, 2 * U), MXU_DTYPE)] * 3,
        scratch_shapes=[pltpu.VMEM((S, HEAD_DIM), MXU_DTYPE)] * 4 + [pltpu.VMEM((S, HEAD_DIM), F32)] * 2,
        compiler_params=_cp(("parallel",)),
    )(proj, proj, proj, att_f32, d_att)


def _place():
    return lax.axis_index("x"), lax.axis_index("y"), lax.axis_index("c")


def _flip(v, bit):
    return 1 - v if bit else v


def _related(x, y, r):
    return _flip(x, r & 1), _flip(y, r >> 1)


def _own_rows(ref, dev):
    rows = ref.shape[0] // N_DEV
    return ref.at[pl.ds(pl.multiple_of(dev * rows, 16), rows), :]


def _my_block(n_blocks):
    def index(i):
        x, y, c = _place()
        return (4 * x + 2 * y + c) * n_blocks + i, 0
    return index


def _gathered(rows, C):
    return jax.ShapeDtypeStruct((N_DEV * rows, C), WIRE_DTYPE)


def _pack_plain(w, l, zero):
    _, rows, C = w.shape
    t = _shard_tile(rows)

    def body(w_ref, z_ref, o_ref):
        o_ref[...] = (w_ref[...] + z_ref[0:1, 0:1]).astype(o_ref.dtype)

    return pl.pallas_call(
        body, name="pack_plain", grid=(rows // t,),
        in_specs=[pl.BlockSpec((None, t, C), lambda i: (l, i, 0)), pl.BlockSpec((8, LANE), lambda i: (0, 0))],
        out_specs=pl.BlockSpec((t, C), _my_block(rows // t)), out_shape=_gathered(rows, C), compiler_params=_cp(("parallel",)),
    )(w, zero)


def _pack_transposed(ws, l, zero):
    rows = ws[0].shape[2]
    C = sum(w.shape[1] for w in ws)
    t = _row_tile(rows, 256)
    n = len(ws)

    def body(*refs):
        z_ref, o_ref = refs[n], refs[n + 1]
        col = 0
        for w_ref in refs[:n]:
            k = w_ref.shape[0]
            o_ref[:, col:col + k] = (w_ref[...] + z_ref[0:1, 0:1]).T.astype(o_ref.dtype)
            col += k

    return pl.pallas_call(
        body, name="pack_transposed", grid=(rows // t,),
        in_specs=[pl.BlockSpec((None, w.shape[1], t), lambda i: (l, 0, i)) for w in ws] + [pl.BlockSpec((8, LANE), lambda i: (0, 0))],
        out_specs=pl.BlockSpec((t, C), _my_block(rows // t)), out_shape=_gathered(rows, C), compiler_params=_cp(("parallel",)),
    )(*ws, zero)


def _pack_transposed_mxu(w, l, zero):
    _, K, rows = w.shape
    tk = _row_tile(K, 512)

    def body(w_ref, z_ref, o_ref, eye):
        @pl.when(pl.program_id(0) == 0)
        def _():
            eye[...] = (lax.broadcasted_iota(jnp.int32, (rows, rows), 0) == lax.broadcasted_iota(jnp.int32, (rows, rows), 1)).astype(eye.dtype)

        x = (w_ref[...] + z_ref[0:1, 0:1]).astype(MXU_DTYPE)
        o_ref[...] = lax.dot_general(eye[...], x, _DN["NT"], preferred_element_type=F32).astype(o_ref.dtype)

    def out_index(j):
        x, y, c = _place()
        return 4 * x + 2 * y + c, j

    return pl.pallas_call(
        body, name="pack_transposed_mxu", grid=(K // tk,),
        in_specs=[pl.BlockSpec((None, tk, rows), lambda j: (l, j, 0)), pl.BlockSpec((8, LANE), lambda j: (0, 0))],
        out_specs=pl.BlockSpec((rows, tk), out_index), out_shape=_gathered(rows, K),
        scratch_shapes=[pltpu.VMEM((rows, rows), MXU_DTYPE)], compiler_params=_cp(("arbitrary",)),
    )(w, zero)


def _hbm(a):
    return pltpu.with_memory_space_constraint(a, pltpu.HBM)


def _hbm_like(arrays):
    return tuple(pltpu.HBM(a.shape, a.dtype) for a in arrays)


def _dev(px, py, pc):
    return 4 * px + 2 * py + pc


def _block_copies(wb_refs, send_sems, recv_sems, slot, block_out, block_in, peer, outgoing):
    K = len(wb_refs)
    return [pltpu.make_async_remote_copy(
        src_ref=_own_rows(wb_ref, block_out), dst_ref=_own_rows(wb_ref, block_out if outgoing else block_in),
        send_sem=send_sems.at[slot * K + k], recv_sem=recv_sems.at[slot * K + k], device_id=peer, device_id_type=MESH)
        for k, wb_ref in enumerate(wb_refs)]


def _ag_stage1(wb_refs, send_sems, recv_sems, outgoing):
    x, y, c = _place()
    me = _dev(x, y, c)
    out = []
    for slot, peer in enumerate(((x, y, 1 - c), (1 - x, y, c), (x, 1 - y, c))):
        out += _block_copies(wb_refs, send_sems, recv_sems, slot, me, _dev(*peer), peer, outgoing)
    return out


def _ag_stage2(wb_refs, send_sems, recv_sems, outgoing):
    x, y, c = _place()
    via = ((1 - x) + c * (2 * x - 1), y + c * (1 - 2 * y))
    to = (x + c * (1 - 2 * x), (1 - y) + c * (2 * y - 1), c)
    out = _block_copies(wb_refs, send_sems, recv_sems, 0, _dev(*via, c), _dev(1 - x, 1 - y, c), to, outgoing)
    for slot, (px, py) in ((1, (1 - x, y)), (2, (x, 1 - y))):
        out += _block_copies(wb_refs, send_sems, recv_sems, slot, _dev(px, py, c), _dev(px, py, 1 - c), (x, y, 1 - c), outgoing)
    return out


def _ag_start(copies, wbs, carry, name):
    K = len(wbs)

    def body(*refs):
        send_sems, recv_sems = refs[K + 1:K + 3]
        for cp in copies(refs[:K], send_sems, recv_sems, True):
            cp.start()

    outs = pl.pallas_call(
        body, name=name,
        out_shape=(pltpu.SemaphoreType.DMA((3 * K,)), pltpu.SemaphoreType.DMA((3 * K,))) + _hbm_like(list(wbs) + [carry]),
        in_specs=(HBM,) * (K + 1), out_specs=(SEM, SEM) + (HBM,) * (K + 1), input_output_aliases={k: 2 + k for k in range(K + 1)},
        compiler_params=pltpu.CompilerParams(has_side_effects=EFFECT),
    )(*[_hbm(a) for a in wbs], _hbm(carry))
    return outs[0], outs[1], list(outs[2:2 + K]), outs[2 + K]


def _ag_wait(copies, wbs, send_sems, recv_sems, after, name):
    K = len(wbs)

    def body(*refs):
        for cp in copies(refs[:K], refs[K], refs[K + 1], False):
            cp.wait_send()
            cp.wait_recv()

    return list(pl.pallas_call(
        body, name=name, out_shape=_hbm_like(wbs),
        in_specs=(HBM,) * K + (SEM, SEM, ANY), out_specs=(HBM,) * K, input_output_aliases={k: k for k in range(K)},
        compiler_params=pltpu.CompilerParams(has_side_effects=EFFECT),
    )(*wbs, send_sems, recv_sems, after))


def _ag_finish(wbs):
    K = len(wbs)

    def body(*refs):
        ins, outs, tok_ref, send_sems, recv_sems = refs[:K], refs[K:2 * K], refs[2 * K], refs[2 * K + 1], refs[2 * K + 2]
        x, y, c = _place()
        sent = []
        for k in range(K):
            cp = pltpu.make_async_remote_copy(
                src_ref=_own_rows(ins[k], _dev(1 - x, 1 - y, c)), dst_ref=_own_rows(outs[k], _dev(1 - x, 1 - y, c)),
                send_sem=send_sems.at[k], recv_sem=recv_sems.at[k], device_id=(x, y, 1 - c), device_id_type=MESH)
            cp.start()
            sent.append(cp)
        for k in range(K):
            theirs = _own_rows(outs[k], _dev(1 - x, 1 - y, 1 - c))
            pltpu.make_async_remote_copy(src_ref=theirs, dst_ref=theirs, send_sem=send_sems.at[k], recv_sem=recv_sems.at[k],
                                         device_id=(x, y, 1 - c), device_id_type=MESH).wait_recv()
        for cp in sent:
            cp.wait_send()
        tok_ref[...] = jnp.zeros_like(tok_ref)

    outs = pl.pallas_call(
        body, name="ag_finish", in_specs=[ANY] * K, out_specs=[ANY] * K + [pl.BlockSpec(memory_space=pltpu.VMEM)],
        out_shape=[jax.ShapeDtypeStruct(a.shape, a.dtype) for a in wbs] + [jax.ShapeDtypeStruct((8, LANE), F32)],
        input_output_aliases={k: k for k in range(K)},
        scratch_shapes=[pltpu.SemaphoreType.DMA((K,))] * 2, compiler_params=_cp(),
    )(*wbs)
    return list(outs[:K]), outs[K]


def _pair_copies(g_refs, land_refs, send_sems, recv_sems):
    x, y, c = _place()
    K = len(g_refs)
    out = []
    for r in range(4):
        px, py = _related(x, y, r)
        for k in range(K):
            out.append(pltpu.make_async_remote_copy(
                src_ref=_own_rows(g_refs[k], 4 * px + 2 * py + (1 - c)), dst_ref=land_refs[k].at[r],
                send_sem=send_sems.at[r * K + k], recv_sem=recv_sems.at[r * K + k], device_id=(x, y, 1 - c), device_id_type=MESH))
    return out


def _shard_tile(rows):
    for t in (512, 736, 256, 128, 64, 32, 16, 8):
        if rows % t == 0:
            return t
    return rows


def _pair_add(grad, landed):
    _, rows, C = landed.shape
    t = _shard_tile(rows)

    def g_index(r, i):
        x, y, c = _place()
        px = jnp.where(r % 2 == 1, 1 - x, x)
        py = jnp.where(r // 2 == 1, 1 - y, y)
        return (4 * px + 2 * py + c) * (rows // t) + i, 0

    def body(a_ref, b_ref, o_ref):
        o_ref[...] = (a_ref[...].astype(F32) + b_ref[...].astype(F32)).astype(o_ref.dtype)

    slot = pl.BlockSpec((None, t, C), lambda r, i: (r, i, 0))
    return pl.pallas_call(
        body, name="pair_add", grid=(4, rows // t), in_specs=[pl.BlockSpec((t, C), g_index), slot], out_specs=slot,
        out_shape=jax.ShapeDtypeStruct((4, rows, C), grad.dtype), compiler_params=_cp(("parallel", "parallel")),
    )(grad, landed)


def _rs_copies(p_refs, land_refs, send_sems, recv_sems):
    x, y, c = _place()
    K = len(p_refs)
    return [pltpu.make_async_remote_copy(src_ref=p_refs[k].at[r], dst_ref=land_refs[k].at[r - 1], send_sem=send_sems.at[(r - 1) * K + k],
                                         recv_sem=recv_sems.at[(r - 1) * K + k], device_id=(*_related(x, y, r), c), device_id_type=MESH)
            for r in (1, 2, 3) for k in range(K)]


def _rs_start(copies, n_slots, srcs, lands, carry, name):
    K = len(srcs)

    def body(*refs):
        for cp in copies(refs[:K], refs[K:2 * K], refs[2 * K + 1], refs[2 * K + 2]):
            cp.start()

    n_thru = 2 * K + 1
    outs = pl.pallas_call(
        body, name=name,
        out_shape=(pltpu.SemaphoreType.DMA((n_slots * K,)), pltpu.SemaphoreType.DMA((n_slots * K,))) + _hbm_like(list(srcs) + list(lands) + [carry]),
        in_specs=(HBM,) * n_thru, out_specs=(SEM, SEM) + (HBM,) * n_thru, input_output_aliases={k: 2 + k for k in range(n_thru)},
        compiler_params=pltpu.CompilerParams(has_side_effects=EFFECT),
    )(*[_hbm(a) for a in list(srcs) + list(lands) + [carry]])
    return outs[0], outs[1], list(outs[2:2 + K]), list(outs[2 + K:2 + 2 * K]), outs[2 + 2 * K]


def _rs_wait(copies, srcs, lands, send_sems, recv_sems, after, name):
    K = len(srcs)

    def body(*refs):
        for cp in copies(refs[:K], refs[K:2 * K], refs[2 * K], refs[2 * K + 1]):
            cp.wait_send()
            cp.wait_recv()

    outs = pl.pallas_call(
        body, name=name, out_shape=_hbm_like(list(srcs) + list(lands)),
        in_specs=(HBM,) * (2 * K) + (SEM, SEM) + (ANY,) * len(after), out_specs=(HBM,) * (2 * K),
        input_output_aliases={k: k for k in range(2 * K)}, compiler_params=pltpu.CompilerParams(has_side_effects=EFFECT),
    )(*srcs, *lands, send_sems, recv_sems, *after)
    return list(outs[:K]), list(outs[K:])


def _all_gather_small(v, reduce):
    M, N = v.shape

    def body(x_ref, out_ref, sum_ref, send_sems, recv_sems, local_sem):
        x, y, c = _place()
        me, sibling = (x, y, c), (x, y, 1 - c)
        chips = [_related(x, y, r) for r in (1, 2, 3)]

        def rows(px, py, pc):
            return out_ref.at[pl.ds(pl.multiple_of((4 * px + 2 * py + pc) * M, 8), M), :]

        def copy(k, block, to, src=None):
            return pltpu.make_async_remote_copy(src_ref=rows(*block) if src is None else src, dst_ref=rows(*block),
                                                send_sem=send_sems.at[k], recv_sem=recv_sems.at[k], device_id=to, device_id_type=MESH)

        mine = pltpu.make_async_copy(x_ref, rows(*me), local_sem)
        mine.start()
        first = [copy(0, me, sibling, src=x_ref)]
        first += [copy(1 + j, me, (*chip, c), src=x_ref) for j, chip in enumerate(chips)]
        for cp in first:
            cp.start()
        passed = [copy(4 + j, (*chip, c), sibling) for j, chip in enumerate(chips)]
        for j, chip in enumerate(chips):
            copy(1 + j, (*chip, c), me).wait_recv()
            passed[j].start()
        copy(0, sibling, me).wait_recv()
        for j, chip in enumerate(chips):
            copy(4 + j, (*chip, 1 - c), me).wait_recv()
        for cp in first + passed:
            cp.wait_send()
        mine.wait()
        if reduce:
            tot = out_ref[pl.ds(0, M), :]
            for p in range(1, N_DEV):
                tot = tot + out_ref[pl.ds(p * M, M), :]
            sum_ref[...] = tot
        else:
            sum_ref[...] = jnp.zeros_like(sum_ref)

    vm = pl.BlockSpec(memory_space=pltpu.VMEM)
    second = jax.ShapeDtypeStruct((M, N) if reduce else (8, LANE), F32)
    outs = pl.pallas_call(
        body, name="all_reduce_small" if reduce else "all_gather_small", in_specs=[vm], out_specs=[vm, vm],
        out_shape=[jax.ShapeDtypeStruct((N_DEV * M, N), v.dtype), second],
        scratch_shapes=[pltpu.SemaphoreType.DMA((7,)), pltpu.SemaphoreType.DMA((7,)), pltpu.SemaphoreType.DMA],
        compiler_params=_cp(),
    )(v)
    return outs[1] if reduce else outs


def _adamw(w, g, m, v):
    shape = w.shape
    cols = shape[-1]
    rows = w.size // cols
    tr = _row_tile(rows, 256) if rows % 8 == 0 else rows
    c1 = 1.0 / (1.0 - ADAM_B1 ** ADAM_STEP)
    c2 = 1.0 / (1.0 - ADAM_B2 ** ADAM_STEP)

    def body(w_ref, g_ref, m_ref, v_ref, d_ref, nm_ref, nv_ref):
        gv = g_ref[...]
        nm = ADAM_B1 * m_ref[...] + (1.0 - ADAM_B1) * gv
        nv = ADAM_B2 * v_ref[...] + (1.0 - ADAM_B2) * (gv * gv)
        d_ref[...] = -ADAM_LR * ((nm * c1) / (jnp.sqrt(nv * c2) + ADAM_EPS) + ADAM_WD * w_ref[...])
        nm_ref[...] = nm
        nv_ref[...] = nv

    blk = pl.BlockSpec((tr, cols), lambda i: (i, 0))
    outs = pl.pallas_call(
        body, name="adamw", grid=(rows // tr,), in_specs=[blk] * 4, out_specs=[blk] * 3,
        out_shape=[jax.ShapeDtypeStruct((rows, cols), F32)] * 3, compiler_params=_cp(("parallel",)),
    )(*[a.reshape(rows, cols) for a in (w, g, m, v)])
    return tuple(o.reshape(shape) for o in outs)


def _transpose_exact(x):
    t = x.shape[1]
    eye = (lax.broadcasted_iota(jnp.int32, (t, t), 0) == lax.broadcasted_iota(jnp.int32, (t, t), 1)).astype(MXU_DTYPE)
    if MXU_DTYPE == F32:
        return lax.dot_general(eye, x, _DN["NT"], preferred_element_type=F32)
    out = None
    for _ in range(3):
        part = x.astype(MXU_DTYPE)
        x = x - part.astype(F32)
        term = lax.dot_general(eye, part, _DN["NT"], preferred_element_type=F32)
        out = term if out is None else out + term
    return out


def _adamw_layer(w, pair_sums, landed, m, v, l, prev, col_off=None):
    L, A, B = w.shape
    ta = _row_tile(A, 256) if A % 8 == 0 else A
    c1 = 1.0 / (1.0 - ADAM_B1 ** ADAM_STEP)
    c2 = 1.0 / (1.0 - ADAM_B2 ** ADAM_STEP)
    n_prev = 0 if prev is None else 4
    if col_off is None:
        g_specs = [pl.BlockSpec((None, ta, B), functools.partial(lambda i, s: (s, i, 0), s=s)) for s in (0, 0, 1, 2)]
    else:
        assert col_off % ta == 0 and pair_sums.shape[1] == B
        g_specs = [pl.BlockSpec((None, B, ta), functools.partial(lambda i, s: (s, 0, col_off // ta + i), s=s)) for s in (0, 0, 1, 2)]

    def body(*refs):
        w_ref, m_ref, v_ref, p_ref, l1_ref, l2_ref, l3_ref = refs[:7]
        go_ref, d_ref, nm_ref, nv_ref = refs[7 + n_prev:]
        gv = ((p_ref[...].astype(F32) + l1_ref[...].astype(F32)) + l2_ref[...].astype(F32)) + l3_ref[...].astype(F32)
        if col_off is not None:
            gv = _transpose_exact(gv)
        nm = ADAM_B1 * m_ref[...] + (1.0 - ADAM_B1) * gv
        nv = ADAM_B2 * v_ref[...] + (1.0 - ADAM_B2) * (gv * gv)
        d_ref[...] = -ADAM_LR * ((nm * c1) / (jnp.sqrt(nv * c2) + ADAM_EPS) + ADAM_WD * w_ref[...])
        go_ref[...] = gv
        nm_ref[...] = nm
        nv_ref[...] = nv

    lay = pl.BlockSpec((None, ta, B), lambda i: (l, i, 0))
    return pl.pallas_call(
        body, name="adamw_layer", grid=(A // ta,),
        in_specs=[lay, lay, lay] + g_specs + [ANY] * n_prev, out_specs=[lay] * 4,
        out_shape=[jax.ShapeDtypeStruct((L, A, B), F32)] * 4, input_output_aliases={7 + j: j for j in range(n_prev)},
        compiler_params=_cp(("parallel",)),
    )(w, m, v, pair_sums, landed, landed, landed, *(prev or ()))


def _relu2(acc):
    r = jnp.maximum(acc, 0.0)
    return acc, r * r


def _relu2_bwd(acc, up):
    return (acc * (2.0 * jnp.maximum(up.astype(F32), 0.0)),)


def kernel(x, ln_mix_pre, ln_mix_post, ln_mlp_pre, ln_mlp_post, w_in, conv_a_w, proj_a, proj_b, conv_c_w, conv_c_b, norm_c_g, norm_c_b, proj_c, w_o, w_up, w_down, loss_target, m_ln_mix_pre, m_ln_mix_post, m_ln_mlp_pre, m_ln_mlp_post, m_w_in, m_conv_a_w, m_proj_a, m_proj_b, m_conv_c_w, m_conv_c_b, m_norm_c_g, m_norm_c_b, m_proj_c, m_w_o, m_w_up, m_w_down, v_ln_mix_pre, v_ln_mix_post, v_ln_mlp_pre, v_ln_mlp_post, v_w_in, v_conv_a_w, v_proj_a, v_proj_b, v_conv_c_w, v_conv_c_b, v_norm_c_g, v_norm_c_b, v_proj_c, v_w_o, v_w_up, v_w_down):
    L, D, n_in_loc = w_in.shape
    S = x.shape[1]
    U = D // 4
    N_IN = n_in_loc * N_DEV
    D_FF = w_up.shape[2] * N_DEV
    assert N_IN == 23 * U and x.shape[0] == 1
    x_i, y_i, c_i = _place()
    me = 4 * x_i + 2 * y_i + c_i

    def pack(l, which, zero):
        kinds = (lambda: _pack_transposed([proj_b, proj_a, proj_c], l, zero), lambda: _pack_plain(w_o, l, zero),
                 lambda: _pack_transposed([w_up], l, zero), lambda: _pack_plain(w_down, l, zero),
                 lambda: _pack_transposed_mxu(w_in, l, zero))
        return [kinds[k]() for k in which]

    def stage1(wbs, carry, tag):
        return _ag_start(_ag_stage1, wbs, carry, f"ag_s1_{tag}")

    def stage2(started, after, carry, tag):
        send_sems, recv_sems, wbs, _ = started
        wbs = _ag_wait(_ag_stage1, wbs, send_sems, recv_sems, after, f"ag_s1_wait_{tag}")
        return _ag_start(_ag_stage2, wbs, carry, f"ag_s2_{tag}")

    def gather_end(started, after, tag):
        send_sems, recv_sems, wbs, _ = started
        wbs, zero = _ag_finish(_ag_wait(_ag_stage2, wbs, send_sems, recv_sems, after, f"ag_s2_wait_{tag}"))
        return [w.astype(MXU_DTYPE) for w in wbs], zero

    cu = U // N_DEV
    conv_loc = jnp.concatenate([conv_a_w, conv_c_w], axis=1).reshape(L * (SC_WIDTH + CF_WIDTH), cu)
    conv_all, tok = _all_gather_small(conv_loc, False)
    conv_all = conv_all.reshape(N_DEV, L, SC_WIDTH + CF_WIDTH, cu).transpose(1, 2, 0, 3).reshape(L, SC_WIDTH + CF_WIDTH, U)
    wa_full, wc_full = conv_all[:, :SC_WIDTH], conv_all[:, SC_WIDTH:]
    every = (0, 1, 2, 3, 4)
    in_s1 = stage1(pack(0, (4,), tok), jnp.zeros((8, LANE), F32), "0_in")
    rest_packed, next_packed = pack(0, (0, 1, 2, 3), tok), pack(1, every, tok)
    in_s2 = stage2(in_s1, next_packed[4], in_s1[3], "0_in")
    rest_s1 = stage1(rest_packed, in_s2[3], "0_rest")
    wb = []

    def vec(p, l):
        return p[l][None, :]

    xs = x[0]
    saved = []
    h1 = _rms_fwd(xs, vec(ln_mix_pre, 0), "rms_fwd")
    for l in range(L):
        if l == 0:
            (w_in_t,), _ = gather_end(in_s2, rest_s1[3], "0_in")
        else:
            w_p, w_o_l, w_up_t, w_dn, w_in_t = w_next
            if l + 1 < L:
                next_s1 = stage1(pack(l + 1, every, tok), h1, l + 1)
                h1 = next_s1[3]
        proj = _mm(h1, w_in_t, "NT", (F32,), "mm_proj")
        a_out = _branch_a_fwd(proj, wa_full[l], U)
        u1 = _branch_c_conv_fwd(proj, wc_full[l], vec(conv_c_b, l), U)
        u3 = _branch_c_norm_fwd(u1, vec(norm_c_g, l), vec(norm_c_b, l))
        if l == 0:
            rest_s2 = stage2(rest_s1, u3, proj, "0_rest")
            next_s1 = stage1(next_packed, rest_s2[3], 1)
            proj = next_s1[3]
        att, att_f32 = _attn_fwd(proj, U)
        if l == 0:
            (w_p, w_o_l, w_up_t, w_dn), tok = gather_end(rest_s2, att, "0_rest")
        wb.append((w_p, w_o_l, w_up_t, w_dn, w_in_t))
        yb = _mm(att, w_p, "NT", (F32,), "mm_yb", b_view=(0, 0, D, 2 * U))
        ya = _mm(a_out, w_p, "NT", (F32,), "mm_ya", b_view=(0, 2 * U, D, U))
        yc = _mm(u3, w_p, "NT", (F32,), "mm_yc", b_view=(0, 3 * U, D, U))
        merged = _gate_fwd(proj, ya, yb, yc)
        mixed = _mm(merged, w_o_l, "NN", (F32,), "mm_mixed")
        x1, h2 = _resid_post(xs, mixed, vec(ln_mix_post, l), vec(ln_mlp_pre, l), "resid_post_mix")
        if 0 < l < L - 1:
            next_s2 = stage2(next_s1, mixed, h2, l + 1)
            h2 = next_s2[3]
        up, act = _mm(h2, w_up_t, "NT", (MXU_DTYPE, MXU_DTYPE), "mm_up", epilogue=_relu2)
        if l == 0 and L > 1:
            next_s2 = stage2(next_s1, up, act, 1)
            act = next_s2[3]
        f = _mm(act, w_dn, "NN", (F32,), "mm_down")
        saved.append((xs, h1, proj, a_out, u1, u3, att, att_f32, ya, yb, yc, merged, mixed, x1, h2, up, act, f))
        if l + 1 < L:
            xs, h1 = _resid_post(x1, f, vec(ln_mlp_post, l), vec(ln_mix_pre, l + 1), "resid_post_mlp")
            w_next, tok = gather_end(next_s2, h1, l + 1)
        else:
            xs, _ = _resid_post(x1, f, vec(ln_mlp_post, l), None, "resid_post_last")
    dxo, loss_row = _loss_head(xs, loss_target[0])
    loss = lax.psum(loss_row[0, 0], ("x", "y", "c"))

    small = {k: [None] * L for k in ("g1", "g2", "g3", "g4", "cb", "ng", "nb", "wa", "wc")}
    def pair_start(grads, carry, name):
        lands = [lax.empty((4, g.shape[0] // N_DEV, D), WIRE_DTYPE) for g in grads]
        send_sems, recv_sems, grads, lands, carry = _rs_start(_pair_copies, 4, grads, lands, carry, name)
        return (grads, lands, send_sems, recv_sems), carry

    def chips_start(pair_flight, after, carry, name):
        grads, landed = _rs_wait(_pair_copies, *pair_flight, after, name + "_pair_wait")
        pair_sums = [_pair_add(g, ld) for g, ld in zip(grads, landed)]
        lands = [lax.empty((3,) + p.shape[1:], WIRE_DTYPE) for p in pair_sums]
        send_sems, recv_sems, pair_sums, lands, carry = _rs_start(_rs_copies, 3, pair_sums, lands, carry, name)
        return (pair_sums, lands, send_sems, recv_sems), carry

    in_flight = []
    mix_pairs = None
    for l in reversed(range(L)):
        w_p, w_o_l, w_up_t, w_dn, w_in_t = wb[l]
        xs, h1, proj, a_out, u1, u3, att, att_f32, ya, yb, yc, merged, mixed, x1, h2, up, act, f = saved[l]
        df, small["g4"][l] = _rms_bwd(f, vec(ln_mlp_post, l), dxo, None, MXU_DTYPE, "rms_bwd_post_mlp")
        d_up = _mm(df, w_dn, "NT", (MXU_DTYPE,), "mm_d_up", epilogue=_relu2_bwd, extras=(up,))
        g_dn = _mm(act, df, "TN", (WIRE_DTYPE,), "mm_g_down", tm=512, tn=2048)
        if mix_pairs is not None:
            flight, d_up = chips_start(mix_pairs, [g_dn], d_up, f"rs_start_mix_{l + 1}")
            in_flight.append((l + 1, ("p", "o", "in"), flight))
        dh2 = _mm(d_up, w_up_t, "NN", (F32,), "mm_dh2")
        g_up = _mm(d_up, h2, "TN", (WIRE_DTYPE,), "mm_g_up", tm=512, tn=2048)
        mlp_pairs, dh2 = pair_start([g_up, g_dn], dh2, f"rs_pair_mlp_{l}")
        dx1, small["g3"][l] = _rms_bwd(x1, vec(ln_mlp_pre, l), dh2, dxo, F32, "rms_bwd_pre_mlp")
        dmixed, small["g2"][l] = _rms_bwd(mixed, vec(ln_mix_post, l), dx1, None, MXU_DTYPE, "rms_bwd_post_mix")
        dmerged = _mm(dmixed, w_o_l, "NT", (F32,), "mm_dmerged")
        g_o = _mm(merged, dmixed, "TN", (WIRE_DTYPE,), "mm_g_o", tm=512, tn=2048)
        flight, dmerged = chips_start(mlp_pairs, [g_o], dmerged, f"rs_start_mlp_{l}")
        in_flight.append((l, ("up", "dn"), flight))
        dya, dyb, dyc, dgla, dglb, dglc = _gate_bwd(dmerged, proj, ya, yb, yc)
        d_att = _mm(dyb, w_p, "NN", (F32,), "mm_d_att", b_view=(0, 0, D, 2 * U))
        d_a_out = _mm(dya, w_p, "NN", (F32,), "mm_d_a_out", b_view=(0, 2 * U, D, U))
        d_u3 = _mm(dyc, w_p, "NN", (F32,), "mm_d_u3", b_view=(0, 3 * U, D, U))
        g_pb = _mm(dyb, att, "TN", (WIRE_DTYPE,), "mm_g_pb", tm=512, tn=2048)
        g_pa = _mm(dya, a_out, "TN", (WIRE_DTYPE,), "mm_g_pa", tm=512, tn=2048)
        g_pc = _mm(dyc, u3, "TN", (WIRE_DTYPE,), "mm_g_pc", tm=512, tn=2048)
        d_scb, d_scc, d_scu, small["wa"][l] = _branch_a_bwd(d_a_out, proj, wa_full[l], U)
        d_u1, small["ng"][l], small["nb"][l] = _branch_c_norm_bwd(d_u3, u1, vec(norm_c_g, l), vec(norm_c_b, l))
        d_cfa, d_cfg, small["wc"][l], small["cb"][l] = _branch_c_conv_bwd(d_u1, proj, wc_full[l], U)
        dq, dk, dv = _attn_bwd(proj, att_f32, d_att, U)
        dproj = jnp.concatenate([dq, dk, dv, d_scb, d_scc, d_scu, d_cfa, d_cfg, dgla, dglb, dglc], axis=1)
        dh1 = _mm(dproj, w_in_t, "NN", (F32,), "mm_dh1", tk=23 * LANE)
        g_in = _mm(dproj, h1, "TN", (WIRE_DTYPE,), "mm_g_in", tm=512, tn=2048)
        dxo, small["g1"][l] = _rms_bwd(xs, vec(ln_mix_pre, l), dh1, dx1, F32, "rms_bwd_pre_mix")
        carry = dxo if l > 0 else jnp.zeros((8, LANE), F32)
        mix_pairs, carry = pair_start([jnp.concatenate([g_pb, g_pa, g_pc], axis=1), g_o, g_in], carry, f"rs_pair_mix_{l}")
        if l > 0:
            dxo = carry
    grad_x = dxo[None]
    flight, carry = chips_start(mix_pairs, [carry], jnp.zeros((8, LANE), F32), "rs_start_mix_0")
    in_flight.append((0, ("p", "o", "in"), flight))
    big = {"w_in": (w_in, m_w_in, v_w_in), "proj_a": (proj_a, m_proj_a, v_proj_a), "proj_b": (proj_b, m_proj_b, v_proj_b),
           "proj_c": (proj_c, m_proj_c, v_proj_c), "w_o": (w_o, m_w_o, v_w_o), "w_up": (w_up, m_w_up, v_w_up),
           "w_down": (w_down, m_w_down, v_w_down)}
    done = {k: None for k in big}
    for n, (l, keys, flight) in enumerate(in_flight):
        after = [carry]
        if n == len(in_flight) - 1:
            after += [done[k][3] for k in big if done[k] is not None]
        pair_sums, lands = _rs_wait(_rs_copies, *flight, after, f"rs_wait_{keys[0]}_{l}")
        g = dict(zip(keys, zip(pair_sums, lands)))
        if "up" in g:
            layer_grads = {"w_up": (g["up"], 0), "w_down": (g["dn"], None)}
        else:
            layer_grads = {"w_in": (g["in"], 0), "proj_b": (g["p"], 0), "proj_a": (g["p"], 2 * U), "proj_c": (g["p"], 3 * U),
                           "w_o": (g["o"], None)}
        for k, ((p, ld), col_off) in layer_grads.items():
            w, m, v = big[k]
            done[k] = _adamw_layer(w, p, ld, m, v, l, done[k], col_off)

    order = ("g1", "g2", "g3", "g4", "cb", "ng", "nb", "wa", "wc")
    parts = [jnp.stack(small[k]).reshape(-1) for k in order]
    flat = jnp.concatenate(parts)
    n_flat = flat.shape[0]
    pad = (-n_flat) % (8 * LANE)
    flat = jnp.pad(flat, (0, pad)).reshape(-1, LANE)
    tot = _all_gather_small(flat, True).reshape(-1)[:n_flat]
    red, pos = {}, 0
    for k, p in zip(order, parts):
        red[k] = tot[pos:pos + p.shape[0]]
        pos += p.shape[0]
    g_ln_mix_pre, g_ln_mix_post = red["g1"].reshape(L, D), red["g2"].reshape(L, D)
    g_ln_mlp_pre, g_ln_mlp_post = red["g3"].reshape(L, D), red["g4"].reshape(L, D)
    g_conv_c_b, g_norm_c_g, g_norm_c_b = red["cb"].reshape(L, U), red["ng"].reshape(L, U), red["nb"].reshape(L, U)
    g_conv_a_w = lax.dynamic_slice_in_dim(red["wa"].reshape(L, SC_WIDTH, U), me * cu, cu, axis=2)
    g_conv_c_w = lax.dynamic_slice_in_dim(red["wc"].reshape(L, CF_WIDTH, U), me * cu, cu, axis=2)

    small_w = {"ln_mix_pre": (ln_mix_pre, g_ln_mix_pre, m_ln_mix_pre, v_ln_mix_pre),
               "ln_mix_post": (ln_mix_post, g_ln_mix_post, m_ln_mix_post, v_ln_mix_post),
               "ln_mlp_pre": (ln_mlp_pre, g_ln_mlp_pre, m_ln_mlp_pre, v_ln_mlp_pre),
               "ln_mlp_post": (ln_mlp_post, g_ln_mlp_post, m_ln_mlp_post, v_ln_mlp_post),
               "conv_a_w": (conv_a_w, g_conv_a_w, m_conv_a_w, v_conv_a_w), "conv_c_w": (conv_c_w, g_conv_c_w, m_conv_c_w, v_conv_c_w),
               "conv_c_b": (conv_c_b, g_conv_c_b, m_conv_c_b, v_conv_c_b), "norm_c_g": (norm_c_g, g_norm_c_g, m_norm_c_g, v_norm_c_g),
               "norm_c_b": (norm_c_b, g_norm_c_b, m_norm_c_b, v_norm_c_b)}
    for k, (w, g, m, v) in small_w.items():
        done[k] = (g,) + _adamw(w, g, m, v)
    names = ("ln_mix_pre", "ln_mix_post", "ln_mlp_pre", "ln_mlp_post", "w_in", "conv_a_w", "proj_a", "proj_b", "conv_c_w", "conv_c_b",
             "norm_c_g", "norm_c_b", "proj_c", "w_o", "w_up", "w_down")
    return (loss, grad_x, *[done[k][0] for k in names], *[done[k][1] for k in names], *[done[k][2] for k in names],
            *[done[k][3] for k in names])
```

```python
import functools

import jax
import jax.numpy as jnp
from jax import lax
from jax.experimental import pallas as pl
from jax.experimental.pallas import tpu as pltpu

F32 = jnp.float32
MXU_DTYPE = jnp.bfloat16
WIRE_DTYPE = jnp.bfloat16
MESH = pl.DeviceIdType.MESH
ANY = pl.BlockSpec(memory_space=pl.ANY)
HBM = pl.BlockSpec(memory_space=pltpu.HBM)
SEM = pl.BlockSpec(memory_space=pltpu.SEMAPHORE)
EFFECT = pltpu.SideEffectType.DATAFLOW_SIDE_EFFECTING

N_DEV = 8
HEAD_DIM = 128
RMS_EPS = 1e-6
LN_EPS = 1e-5
SC_WIDTH = 3
CF_WIDTH = 31
CONV_PAD = 32
ADAM_LR, ADAM_B1, ADAM_B2, ADAM_EPS, ADAM_WD, ADAM_STEP = 0.001, 0.9, 0.999, 1e-08, 0.01, 10
VMEM_LIMIT = 56 * 1024 * 1024
LANE = 128


def _cp(sem=None, **kw):
    return pltpu.CompilerParams(dimension_semantics=sem, vmem_limit_bytes=VMEM_LIMIT, **kw)


def _sigmoid(x):
    return 1.0 / (1.0 + jnp.exp(-x))


def _row_tile(rows, want):
    t = min(rows, want)
    while rows % t:
        t //= 2
    return t


_DN = {"NN": (((1,), (0,)), ((), ())), "NT": (((1,), (1,)), ((), ())), "TN": (((0,), (0,)), ((), ()))}


def _mm(a, b, mode, out_dtypes, name, *, a_view=None, b_view=None, tm=2048, tn=512, tk=2048, epilogue=None, extras=()):
    a_view = a_view or (0, 0) + tuple(a.shape)
    b_view = b_view or (0, 0) + tuple(b.shape)
    ar, ac, an, am = a_view
    br, bc, bn, bm = b_view
    if mode == "NN":
        M, K, K2, N = an, am, bn, bm
    elif mode == "NT":
        M, K, N, K2 = an, am, bn, bm
    else:
        K, M, K2, N = an, am, bn, bm
    assert K == K2, (name, a_view, b_view)
    tm, tn, tk = _row_tile(M, tm), _row_tile(N, tn), _row_tile(K, tk)
    (a_m_off, a_k_off) = (ac, ar) if mode == "TN" else (ar, ac)
    (b_n_off, b_k_off) = (br, bc) if mode == "NT" else (bc, br)
    while a_m_off % tm:
        tm //= 2
    while b_n_off % tn:
        tn //= 2
    while a_k_off % tk or b_k_off % tk:
        tk //= 2
    nk = K // tk
    a_blk = (tk, tm) if mode == "TN" else (tm, tk)
    b_blk = (tn, tk) if mode == "NT" else (tk, tn)
    assert ar % a_blk[0] == 0 and ac % a_blk[1] == 0, (name, a_view, a_blk)
    assert br % b_blk[0] == 0 and bc % b_blk[1] == 0, (name, b_view, b_blk)
    ao, bo = (ar // a_blk[0], ac // a_blk[1]), (br // b_blk[0], bc // b_blk[1])
    if mode == "TN":
        a_spec = pl.BlockSpec(a_blk, lambda i, j, k: (ao[0] + k, ao[1] + i))
    else:
        a_spec = pl.BlockSpec(a_blk, lambda i, j, k: (ao[0] + i, ao[1] + k))
    if mode == "NT":
        b_spec = pl.BlockSpec(b_blk, lambda i, j, k: (bo[0] + j, bo[1] + k))
    else:
        b_spec = pl.BlockSpec(b_blk, lambda i, j, k: (bo[0] + k, bo[1] + j))
    o_spec = pl.BlockSpec((tm, tn), lambda i, j, k: (i, j))
    n_ex, n_out = len(extras), len(out_dtypes)
    dn = _DN[mode]

    def body(*refs):
        a_ref, b_ref = refs[:2]
        ex_refs = refs[2:2 + n_ex]
        o_refs = refs[2 + n_ex:2 + n_ex + n_out]
        p = lax.dot_general(a_ref[...], b_ref[...], dn, preferred_element_type=F32)

        def finish(acc):
            outs = epilogue(acc, *[r[...] for r in ex_refs]) if epilogue else (acc,)
            for o_ref, o in zip(o_refs, outs):
                o_ref[...] = o.astype(o_ref.dtype)

        if nk == 1:
            finish(p)
        else:
            acc_ref = refs[-1]
            k = pl.program_id(2)

            @pl.when(k == 0)
            def _():
                acc_ref[...] = p

            @pl.when(k > 0)
            def _():
                acc_ref[...] += p

            @pl.when(k == nk - 1)
            def _():
                finish(acc_ref[...])

    outs = pl.pallas_call(
        body, name=name, grid=(M // tm, N // tn, nk),
        in_specs=[a_spec, b_spec] + [o_spec] * n_ex, out_specs=[o_spec] * n_out,
        out_shape=[jax.ShapeDtypeStruct((M, N), d) for d in out_dtypes],
        scratch_shapes=[pltpu.VMEM((tm, tn), F32)] if nk > 1 else [],
        compiler_params=_cp(("parallel", "parallel", "arbitrary")),
    )(a, b, *extras)
    return outs[0] if n_out == 1 else outs


def _rms_fwd(x, g, name):
    S, D = x.shape
    tr = _row_tile(S, 256)

    def body(x_ref, g_ref, h_ref):
        xv = x_ref[...]
        r = lax.rsqrt(jnp.mean(xv * xv, axis=-1, keepdims=True) + RMS_EPS)
        h_ref[...] = ((xv * r) * g_ref[...]).astype(h_ref.dtype)

    return pl.pallas_call(
        body, name=name, grid=(S // tr,),
        in_specs=[pl.BlockSpec((tr, D), lambda i: (i, 0)), pl.BlockSpec((1, D), lambda i: (0, 0))],
        out_specs=pl.BlockSpec((tr, D), lambda i: (i, 0)),
        out_shape=jax.ShapeDtypeStruct((S, D), MXU_DTYPE), compiler_params=_cp(("parallel",)),
    )(x, g)


def _resid_post(xres, y, g_post, g_next, name):
    S, D = y.shape
    tr = _row_tile(S, 256)
    has_next = g_next is not None

    def body(*refs):
        xr_ref, y_ref, gp_ref = refs[:3]
        yv = y_ref[...]
        r = lax.rsqrt(jnp.mean(yv * yv, axis=-1, keepdims=True) + RMS_EPS)
        xn = xr_ref[...] + (yv * r) * gp_ref[...]
        if has_next:
            gn_ref, xo_ref, h_ref = refs[3:]
            r2 = lax.rsqrt(jnp.mean(xn * xn, axis=-1, keepdims=True) + RMS_EPS)
            h_ref[...] = ((xn * r2) * gn_ref[...]).astype(h_ref.dtype)
        else:
            xo_ref = refs[3]
        xo_ref[...] = xn

    row = pl.BlockSpec((tr, D), lambda i: (i, 0))
    vec = pl.BlockSpec((1, D), lambda i: (0, 0))
    outs = pl.pallas_call(
        body, name=name, grid=(S // tr,),
        in_specs=[row, row, vec] + ([vec] if has_next else []),
        out_specs=[row] + ([row] if has_next else []),
        out_shape=[jax.ShapeDtypeStruct((S, D), F32)] + ([jax.ShapeDtypeStruct((S, D), MXU_DTYPE)] if has_next else []),
        compiler_params=_cp(("parallel",)),
    )(xres, y, g_post, *([g_next] if has_next else []))
    return (outs[0], outs[1]) if has_next else (outs[0], None)


def _rms_bwd(xin, g, dy, dres, out_dtype, name):
    S, D = xin.shape
    tr = _row_tile(S, 256)
    has_res = dres is not None

    def body(*refs):
        x_ref, g_ref, dy_ref = refs[:3]
        dx_ref, dg_ref = refs[-2:]
        xv, dyv = x_ref[...], dy_ref[...].astype(F32)
        r = lax.rsqrt(jnp.mean(xv * xv, axis=-1, keepdims=True) + RMS_EPS)
        n = xv * r
        dyg = dyv * g_ref[...]
        dx = r * (dyg - n * jnp.mean(dyg * n, axis=-1, keepdims=True))
        if has_res:
            dx = dx + refs[3][...]
        dx_ref[...] = dx.astype(dx_ref.dtype)

        @pl.when(pl.program_id(0) == 0)
        def _():
            dg_ref[...] = jnp.zeros_like(dg_ref)

        dg_ref[...] += jnp.sum(dyv * n, axis=0, keepdims=True)

    row = pl.BlockSpec((tr, D), lambda i: (i, 0))
    vec = pl.BlockSpec((1, D), lambda i: (0, 0))
    return pl.pallas_call(
        body, name=name, grid=(S // tr,),
        in_specs=[row, vec, row] + ([row] if has_res else []), out_specs=[row, vec],
        out_shape=[jax.ShapeDtypeStruct((S, D), out_dtype), jax.ShapeDtypeStruct((1, D), F32)],
        compiler_params=_cp(("arbitrary",)),
    )(xin, g, dy, *([dres] if has_res else []))


def _loss_head(y, target):
    S, D = y.shape
    tr = _row_tile(S, 256)

    def body(y_ref, t_ref, dy_ref, l_ref):
        e = y_ref[...] - t_ref[...]
        dy_ref[...] = e * (1.0 / D)

        @pl.when(pl.program_id(0) == 0)
        def _():
            l_ref[...] = jnp.zeros_like(l_ref)

        l_ref[...] += 0.5 * jnp.sum(jnp.mean(e * e, axis=-1, keepdims=True), axis=0, keepdims=True)

    row = pl.BlockSpec((tr, D), lambda i: (i, 0))
    return pl.pallas_call(
        body, name="loss_head", grid=(S // tr,), in_specs=[row, row],
        out_specs=[row, pl.BlockSpec((1, LANE), lambda i: (0, 0))],
        out_shape=[jax.ShapeDtypeStruct((S, D), F32), jax.ShapeDtypeStruct((1, LANE), F32)],
        compiler_params=_cp(("arbitrary",)),
    )(y, target)


def _gate_specs(S, U, tr):
    gl = [pl.BlockSpec((tr, U), functools.partial(lambda i, j, o: (i, o + j), o=o)) for o in (11, 15, 19)]
    return gl, pl.BlockSpec((tr, U), lambda i, j: (i, j))


def _merge_fwd(proj, a_out, att, u3, w_p):
    S, U = a_out.shape
    D = 4 * U
    tm = _row_tile(S, 1024)

    def body(ga_ref, gb_ref, gc_ref, a_ref, b_ref, c_ref, w_ref, m_ref, ya_ref, yb_ref, yc_ref):
        yb = lax.dot_general(b_ref[...], w_ref[:, 0:2 * U], _DN["NT"], preferred_element_type=F32)
        ya = lax.dot_general(a_ref[...], w_ref[:, 2 * U:3 * U], _DN["NT"], preferred_element_type=F32)
        yc = lax.dot_general(c_ref[...], w_ref[:, 3 * U:4 * U], _DN["NT"], preferred_element_type=F32)
        m_ref[...] = (_sigmoid(ga_ref[...]) * ya + _sigmoid(gb_ref[...]) * yb + _sigmoid(gc_ref[...]) * yc).astype(m_ref.dtype)
        ya_ref[...] = ya.astype(ya_ref.dtype)
        yb_ref[...] = yb.astype(yb_ref.dtype)
        yc_ref[...] = yc.astype(yc_ref.dtype)

    gl, blk = _gate_specs(S, U, tm)
    rows = lambda k: pl.BlockSpec((tm, k), lambda i, j: (i, 0))
    return pl.pallas_call(
        body, name="merge_fwd", grid=(S // tm, 4),
        in_specs=gl + [rows(U), rows(2 * U), rows(U), pl.BlockSpec((U, D), lambda i, j: (j, 0))], out_specs=[blk] * 4,
        out_shape=[jax.ShapeDtypeStruct((S, D), MXU_DTYPE)] * 4, compiler_params=_cp(("parallel", "parallel")),
    )(proj, proj, proj, a_out, att, u3, w_p)


def _gate_bwd(dm, proj, ya, yb, yc):
    S, D = ya.shape
    U = D // 4
    tr = _row_tile(S, 256)

    def body(dm_ref, ga_ref, gb_ref, gc_ref, ya_ref, yb_ref, yc_ref, da_ref, db_ref, dc_ref, la_ref, lb_ref, lc_ref):
        d = dm_ref[...]
        for g_ref, y_ref, dy_ref, dl_ref in ((ga_ref, ya_ref, da_ref, la_ref), (gb_ref, yb_ref, db_ref, lb_ref),
                                             (gc_ref, yc_ref, dc_ref, lc_ref)):
            g = _sigmoid(g_ref[...])
            dy_ref[...] = (d * g).astype(dy_ref.dtype)
            dl_ref[...] = (d * y_ref[...] * g * (1.0 - g)).astype(dl_ref.dtype)

    gl, blk = _gate_specs(S, U, tr)
    return pl.pallas_call(
        body, name="gate_bwd", grid=(S // tr, 4), in_specs=[blk] + gl + [blk] * 3, out_specs=[blk] * 6,
        out_shape=[jax.ShapeDtypeStruct((S, D), MXU_DTYPE)] * 6, compiler_params=_cp(("parallel", "parallel")),
    )(dm, proj, proj, proj, ya, yb, yc)


def _chunks(S):
    r = _row_tile(S, 256)
    return [(r0, r) for r0 in range(0, S, r)]


def _conv_causal(front_ref, w_ref, K, r0, R):
    acc = None
    for j in range(K):
        term = w_ref[pl.ds(K - 1 - j, 1), :] * front_ref[pl.ds(CONV_PAD + r0 - j, R), :]
        acc = term if acc is None else acc + term
    return acc


def _conv_anticausal(back_ref, w_ref, K, r0, R):
    acc = None
    for j in range(K):
        term = w_ref[pl.ds(K - 1 - j, 1), :] * back_ref[pl.ds(r0 + j, R), :]
        acc = term if acc is None else acc + term
    return acc


def _conv_wgrad(front_ref, back_ref, dw_ref, K, S):
    for j in range(K):
        tot = None
        for r0, R in _chunks(S):
            part = jnp.sum(back_ref[pl.ds(r0, R), :] * front_ref[pl.ds(CONV_PAD + r0 - j, R), :], axis=0, keepdims=True)
            tot = part if tot is None else tot + part
        dw_ref[pl.ds(K - 1 - j, 1), :] = tot


def _col(S, cw, unit_off):
    return pl.BlockSpec((S, cw), functools.partial(lambda cb, o: (0, o + cb), o=unit_off))


def _branch_a_fwd(proj, wa, U):
    S = proj.shape[0]
    cw = min(LANE, U)
    nb = U // cw
    K = SC_WIDTH

    def body(b_ref, c_ref, u_ref, w_ref, o_ref, front):
        front[pl.ds(0, CONV_PAD), :] = jnp.zeros((CONV_PAD, cw), F32)
        front[pl.ds(CONV_PAD, S), :] = c_ref[...] * u_ref[...]
        for r0, R in _chunks(S):
            o_ref[pl.ds(r0, R), :] = (b_ref[pl.ds(r0, R), :] * _conv_causal(front, w_ref, K, r0, R)).astype(o_ref.dtype)

    return pl.pallas_call(
        body, name="branch_a_fwd", grid=(nb,),
        in_specs=[_col(S, cw, 6 * nb), _col(S, cw, 7 * nb), _col(S, cw, 8 * nb), pl.BlockSpec((K, cw), lambda cb: (0, cb))],
        out_specs=pl.BlockSpec((S, cw), lambda cb: (0, cb)), out_shape=jax.ShapeDtypeStruct((S, U), MXU_DTYPE),
        scratch_shapes=[pltpu.VMEM((S + CONV_PAD, cw), F32)], compiler_params=_cp(("parallel",)),
    )(proj, proj, proj, wa)


def _branch_a_bwd(d_out, proj, wa, U):
    S = proj.shape[0]
    cw = min(LANE, U)
    nb = U // cw
    K = SC_WIDTH

    def body(d_ref, b_ref, c_ref, u_ref, w_ref, db_ref, dc_ref, du_ref, dw_ref, front, back):
        front[pl.ds(0, CONV_PAD), :] = jnp.zeros((CONV_PAD, cw), F32)
        front[pl.ds(CONV_PAD, S), :] = c_ref[...] * u_ref[...]
        back[pl.ds(S, CONV_PAD), :] = jnp.zeros((CONV_PAD, cw), F32)
        back[pl.ds(0, S), :] = d_ref[...] * b_ref[...]
        for r0, R in _chunks(S):
            rows = pl.ds(r0, R)
            db_ref[rows, :] = (d_ref[rows, :] * _conv_causal(front, w_ref, K, r0, R)).astype(db_ref.dtype)
            d_ai = _conv_anticausal(back, w_ref, K, r0, R)
            dc_ref[rows, :] = (d_ai * u_ref[rows, :]).astype(dc_ref.dtype)
            du_ref[rows, :] = (d_ai * c_ref[rows, :]).astype(du_ref.dtype)
        _conv_wgrad(front, back, dw_ref, K, S)

    blk = pl.BlockSpec((S, cw), lambda cb: (0, cb))
    wblk = pl.BlockSpec((K, cw), lambda cb: (0, cb))
    return pl.pallas_call(
        body, name="branch_a_bwd", grid=(nb,),
        in_specs=[blk, _col(S, cw, 6 * nb), _col(S, cw, 7 * nb), _col(S, cw, 8 * nb), wblk],
        out_specs=[blk, blk, blk, wblk],
        out_shape=[jax.ShapeDtypeStruct((S, U), MXU_DTYPE)] * 3 + [jax.ShapeDtypeStruct((K, U), F32)],
        scratch_shapes=[pltpu.VMEM((S + CONV_PAD, cw), F32)] * 2, compiler_params=_cp(("parallel",)),
    )(d_out, proj, proj, proj, wa)


def _branch_c_conv_fwd(proj, wc, cb, U):
    S = proj.shape[0]
    cw = min(LANE, U)
    nb = U // cw
    K = CF_WIDTH

    def body(a_ref, g_ref, w_ref, bias_ref, o_ref, front):
        front[pl.ds(0, CONV_PAD), :] = jnp.zeros((CONV_PAD, cw), F32)
        front[pl.ds(CONV_PAD, S), :] = a_ref[...] * _sigmoid(g_ref[...])
        for r0, R in _chunks(S):
            o_ref[pl.ds(r0, R), :] = _conv_causal(front, w_ref, K, r0, R) + bias_ref[...]

    return pl.pallas_call(
        body, name="branch_c_conv_fwd", grid=(nb,),
        in_specs=[_col(S, cw, 9 * nb), _col(S, cw, 10 * nb), pl.BlockSpec((K, cw), lambda c: (0, c)),
                  pl.BlockSpec((1, cw), lambda c: (0, c))],
        out_specs=pl.BlockSpec((S, cw), lambda c: (0, c)), out_shape=jax.ShapeDtypeStruct((S, U), F32),
        scratch_shapes=[pltpu.VMEM((S + CONV_PAD, cw), F32)], compiler_params=_cp(("parallel",)),
    )(proj, proj, wc, cb)


def _branch_c_conv_bwd(d_u1, proj, wc, U):
    S = proj.shape[0]
    cw = min(LANE, U)
    nb = U // cw
    K = CF_WIDTH

    def body(d_ref, a_ref, g_ref, w_ref, da_ref, dg_ref, dw_ref, dbias_ref, front, back):
        sg = _sigmoid(g_ref[...])
        front[pl.ds(0, CONV_PAD), :] = jnp.zeros((CONV_PAD, cw), F32)
        front[pl.ds(CONV_PAD, S), :] = a_ref[...] * sg
        back[pl.ds(S, CONV_PAD), :] = jnp.zeros((CONV_PAD, cw), F32)
        back[pl.ds(0, S), :] = d_ref[...]
        dbias_ref[...] = jnp.sum(d_ref[...], axis=0, keepdims=True)
        for r0, R in _chunks(S):
            rows = pl.ds(r0, R)
            d_u0 = _conv_anticausal(back, w_ref, K, r0, R)
            s = _sigmoid(g_ref[rows, :])
            da_ref[rows, :] = (d_u0 * s).astype(da_ref.dtype)
            dg_ref[rows, :] = (d_u0 * a_ref[rows, :] * s * (1.0 - s)).astype(dg_ref.dtype)
        _conv_wgrad(front, back, dw_ref, K, S)

    blk = pl.BlockSpec((S, cw), lambda c: (0, c))
    wblk = pl.BlockSpec((K, cw), lambda c: (0, c))
    vblk = pl.BlockSpec((1, cw), lambda c: (0, c))
    return pl.pallas_call(
        body, name="branch_c_conv_bwd", grid=(nb,),
        in_specs=[blk, _col(S, cw, 9 * nb), _col(S, cw, 10 * nb), wblk], out_specs=[blk, blk, wblk, vblk],
        out_shape=[jax.ShapeDtypeStruct((S, U), MXU_DTYPE)] * 2 + [jax.ShapeDtypeStruct((K, U), F32), jax.ShapeDtypeStruct((1, U), F32)],
        scratch_shapes=[pltpu.VMEM((S + CONV_PAD, cw), F32)] * 2, compiler_params=_cp(("parallel",)),
    )(d_u1, proj, proj, wc)


def _branch_c_norm_fwd(u1, ng, nbias):
    S, U = u1.shape
    tr = _row_tile(S, 256)

    def body(u_ref, g_ref, b_ref, o_ref):
        u = u_ref[...]
        mu = jnp.mean(u, axis=-1, keepdims=True)
        var = jnp.mean(jnp.square(u - mu), axis=-1, keepdims=True)
        u2 = ((u - mu) * lax.rsqrt(var + LN_EPS)) * g_ref[...] + b_ref[...]
        o_ref[...] = (u2 * _sigmoid(u2)).astype(o_ref.dtype)

    row = pl.BlockSpec((tr, U), lambda i: (i, 0))
    vec = pl.BlockSpec((1, U), lambda i: (0, 0))
    return pl.pallas_call(
        body, name="branch_c_norm_fwd", grid=(S // tr,), in_specs=[row, vec, vec], out_specs=row,
        out_shape=jax.ShapeDtypeStruct((S, U), MXU_DTYPE), compiler_params=_cp(("parallel",)),
    )(u1, ng, nbias)


def _branch_c_norm_bwd(d_u3, u1, ng, nbias):
    S, U = u1.shape
    tr = _row_tile(S, 256)

    def body(d_ref, u_ref, g_ref, b_ref, du_ref, dg_ref, db_ref):
        u = u_ref[...]
        mu = jnp.mean(u, axis=-1, keepdims=True)
        var = jnp.mean(jnp.square(u - mu), axis=-1, keepdims=True)
        rstd = lax.rsqrt(var + LN_EPS)
        xh = (u - mu) * rstd
        u2 = xh * g_ref[...] + b_ref[...]
        s = _sigmoid(u2)
        d_u2 = d_ref[...] * (s * (1.0 + u2 * (1.0 - s)))
        d_xh = d_u2 * g_ref[...]
        du_ref[...] = rstd * (d_xh - jnp.mean(d_xh, axis=-1, keepdims=True) - xh * jnp.mean(d_xh * xh, axis=-1, keepdims=True))

        @pl.when(pl.program_id(0) == 0)
        def _():
            dg_ref[...] = jnp.zeros_like(dg_ref)
            db_ref[...] = jnp.zeros_like(db_ref)

        dg_ref[...] += jnp.sum(d_u2 * xh, axis=0, keepdims=True)
        db_ref[...] += jnp.sum(d_u2, axis=0, keepdims=True)

    row = pl.BlockSpec((tr, U), lambda i: (i, 0))
    vec = pl.BlockSpec((1, U), lambda i: (0, 0))
    return pl.pallas_call(
        body, name="branch_c_norm_bwd", grid=(S // tr,), in_specs=[row, row, vec, vec], out_specs=[row, vec, vec],
        out_shape=[jax.ShapeDtypeStruct((S, U), F32), jax.ShapeDtypeStruct((1, U), F32), jax.ShapeDtypeStruct((1, U), F32)],
        compiler_params=_cp(("arbitrary",)),
    )(d_u3, u1, ng, nbias)


def _tri(T, inclusive):
    j = lax.broadcasted_iota(jnp.int32, (T, T), 0)
    s = lax.broadcasted_iota(jnp.int32, (T, T), 1)
    return ((j >= s) if inclusive else (j > s)).astype(MXU_DTYPE)


def _split_dot(x, tri):
    if MXU_DTYPE == F32:
        return jnp.dot(x, tri, preferred_element_type=F32)
    hi = x.astype(MXU_DTYPE)
    lo = (x - hi.astype(F32)).astype(MXU_DTYPE)
    return jnp.dot(hi, tri, preferred_element_type=F32) + jnp.dot(lo, tri, preferred_element_type=F32)


def _sb_block(qb, kb, T, tri_strict, c_lf, diag):
    z = lax.dot_general(qb, kb, _DN["NT"], preferred_element_type=F32) * (HEAD_DIM ** -0.5)
    e = jnp.exp(-jnp.abs(z))
    lg = jnp.log(1.0 + e)
    log_beta = jnp.minimum(z, 0.0) - lg
    lf = jnp.minimum(-z, 0.0) - lg
    mask = None
    if diag:
        mask = lax.broadcasted_iota(jnp.int32, (T, T), 1) < lax.broadcasted_iota(jnp.int32, (T, T), 0)
        lf = jnp.where(mask, lf, 0.0)
    a = jnp.exp(log_beta + _split_dot(lf, tri_strict) + c_lf)
    if diag:
        a = jnp.where(mask, a, 0.0)
    return z, e, mask, lf, a


def _key_blocks(i, step, init):
    carry = step(i, init, True)
    return lax.fori_loop(1, i + 1, lambda jj, c: step(i - jj, c, False), carry)


def _attn_specs(S, U, h_blocks):
    nh = (2 * U) // HEAD_DIM
    return [pl.BlockSpec((S, HEAD_DIM), functools.partial(lambda h, o: (0, o + h), o=o * nh)) for o in range(h_blocks)]


def _attn_fwd(proj, U):
    S = proj.shape[0]
    nh = (2 * U) // HEAD_DIM
    T = _row_tile(S, 256)
    nq = S // T

    def body(q_ref, k_ref, v_ref, o_ref, of_ref, qs, ks, vs):
        qs[...] = q_ref[...].astype(MXU_DTYPE)
        ks[...] = k_ref[...].astype(MXU_DTYPE)
        vs[...] = v_ref[...].astype(MXU_DTYPE)
        tri = _tri(T, False)

        def q_loop(i, _):
            rows = pl.ds(pl.multiple_of(i * T, T), T)
            qb = qs[rows, :]

            def step(j, carry, diag):
                c_lf, acc = carry
                cols = pl.ds(pl.multiple_of(j * T, T), T)
                _, _, _, lf, a = _sb_block(qb, ks[cols, :], T, tri, c_lf, diag)
                acc = acc + jnp.dot(a.astype(MXU_DTYPE), vs[cols, :], preferred_element_type=F32)
                return c_lf + jnp.sum(lf, axis=1, keepdims=True), acc

            _, acc = _key_blocks(i, step, (jnp.zeros((T, 1), F32), jnp.zeros((T, HEAD_DIM), F32)))
            o_ref[rows, :] = acc.astype(o_ref.dtype)
            of_ref[rows, :] = acc
            return 0

        lax.fori_loop(0, nq, q_loop, 0)

    hblk = pl.BlockSpec((S, HEAD_DIM), lambda h: (0, h))
    return pl.pallas_call(
        body, name="attn_fwd", grid=(nh,), in_specs=_attn_specs(S, U, 3), out_specs=[hblk, hblk],
        out_shape=[jax.ShapeDtypeStruct((S, 2 * U), MXU_DTYPE), jax.ShapeDtypeStruct((S, 2 * U), F32)],
        scratch_shapes=[pltpu.VMEM((S, HEAD_DIM), MXU_DTYPE)] * 3, compiler_params=_cp(("parallel",)),
    )(proj, proj, proj)


def _attn_bwd(proj, att_f32, d_att, U):
    S = proj.shape[0]
    nh = (2 * U) // HEAD_DIM
    T = _row_tile(S, 256)
    nq = S // T
    scale = HEAD_DIM ** -0.5

    def body(q_ref, k_ref, v_ref, o_ref, do_ref, dq_ref, dk_ref, dv_ref, qs, ks, vs, dos, dka, dva):
        qs[...] = q_ref[...].astype(MXU_DTYPE)
        ks[...] = k_ref[...].astype(MXU_DTYPE)
        vs[...] = v_ref[...].astype(MXU_DTYPE)
        dos[...] = do_ref[...].astype(MXU_DTYPE)
        dka[...] = jnp.zeros_like(dka)
        dva[...] = jnp.zeros_like(dva)
        tri = _tri(T, False)
        tri_inc = _tri(T, True)

        def q_loop(i, _):
            rows = pl.ds(pl.multiple_of(i * T, T), T)
            qb = qs[rows, :]
            dob = dos[rows, :]
            delta = jnp.sum(dob.astype(F32) * o_ref[rows, :], axis=1, keepdims=True)

            def step(j, carry, diag):
                c_lf, c_g, dq = carry
                cols = pl.ds(pl.multiple_of(j * T, T), T)
                kb, vb = ks[cols, :], vs[cols, :]
                z, e, mask, lf, a = _sb_block(qb, kb, T, tri, c_lf, diag)
                a_mx = a.astype(MXU_DTYPE)
                d_a = lax.dot_general(dob, vb, _DN["NT"], preferred_element_type=F32)
                g = a_mx.astype(F32) * d_a
                prefix = delta - (_split_dot(g, tri_inc) + c_g)
                inv = 1.0 / (1.0 + e)
                beta = jnp.where(z >= 0.0, 1.0, e) * inv
                one_m_beta = jnp.where(z >= 0.0, e, 1.0) * inv
                dz = (g * one_m_beta - prefix * beta) * scale
                if diag:
                    dz = jnp.where(mask, dz, 0.0)
                dz = dz.astype(MXU_DTYPE)
                dq = dq + jnp.dot(dz, kb, preferred_element_type=F32)
                dka[cols, :] += lax.dot_general(dz, qb, _DN["TN"], preferred_element_type=F32)
                dva[cols, :] += lax.dot_general(a_mx, dob, _DN["TN"], preferred_element_type=F32)
                return c_lf + jnp.sum(lf, axis=1, keepdims=True), c_g + jnp.sum(g, axis=1, keepdims=True), dq

            zero = jnp.zeros((T, 1), F32)
            _, _, dq = _key_blocks(i, step, (zero, zero, jnp.zeros((T, HEAD_DIM), F32)))
            dq_ref[rows, :] = dq.astype(dq_ref.dtype)
            return 0

        lax.fori_loop(0, nq, q_loop, 0)
        dk_ref[...] = dka[...].astype(dk_ref.dtype)
        dv_ref[...] = dva[...].astype(dv_ref.dtype)

    hblk = pl.BlockSpec((S, HEAD_DIM), lambda h: (0, h))
    return pl.pallas_call(
        body, name="attn_bwd", grid=(nh,), in_specs=_attn_specs(S, U, 3) + [hblk, hblk], out_specs=[hblk] * 3,
        out_shape=[jax.ShapeDtypeStruct((S, 2 * U), MXU_DTYPE)] * 3,
        scratch_shapes=[pltpu.VMEM((S, HEAD_DIM), MXU_DTYPE)] * 4 + [pltpu.VMEM((S, HEAD_DIM), F32)] * 2,
        compiler_params=_cp(("parallel",)),
    )(proj, proj, proj, att_f32, d_att)


def _place():
    return lax.axis_index("x"), lax.axis_index("y"), lax.axis_index("c")


def _flip(v, bit):
    return 1 - v if bit else v


def _related(x, y, r):
    return _flip(x, r & 1), _flip(y, r >> 1)


def _own_rows(ref, dev):
    rows = ref.shape[0] // N_DEV
    return ref.at[pl.ds(pl.multiple_of(dev * rows, 16), rows), :]


def _my_block(n_blocks):
    def index(i):
        x, y, c = _place()
        return (4 * x + 2 * y + c) * n_blocks + i, 0
    return index


def _gathered(rows, C):
    return jax.ShapeDtypeStruct((N_DEV * rows, C), WIRE_DTYPE)


def _pack_plain(w, l, zero):
    _, rows, C = w.shape
    t = _shard_tile(rows)

    def body(w_ref, z_ref, o_ref):
        o_ref[...] = (w_ref[...] + z_ref[0:1, 0:1]).astype(o_ref.dtype)

    return pl.pallas_call(
        body, name="pack_plain", grid=(rows // t,),
        in_specs=[pl.BlockSpec((None, t, C), lambda i: (l, i, 0)), pl.BlockSpec((8, LANE), lambda i: (0, 0))],
        out_specs=pl.BlockSpec((t, C), _my_block(rows // t)), out_shape=_gathered(rows, C), compiler_params=_cp(("parallel",)),
    )(w, zero)


def _pack_transposed(ws, l, zero):
    rows = ws[0].shape[2]
    C = sum(w.shape[1] for w in ws)
    t = _row_tile(rows, 256)
    n = len(ws)

    def body(*refs):
        z_ref, o_ref = refs[n], refs[n + 1]
        col = 0
        for w_ref in refs[:n]:
            k = w_ref.shape[0]
            o_ref[:, col:col + k] = (w_ref[...] + z_ref[0:1, 0:1]).T.astype(o_ref.dtype)
            col += k

    return pl.pallas_call(
        body, name="pack_transposed", grid=(rows // t,),
        in_specs=[pl.BlockSpec((None, w.shape[1], t), lambda i: (l, 0, i)) for w in ws] + [pl.BlockSpec((8, LANE), lambda i: (0, 0))],
        out_specs=pl.BlockSpec((t, C), _my_block(rows // t)), out_shape=_gathered(rows, C), compiler_params=_cp(("parallel",)),
    )(*ws, zero)


def _pack_transposed_mxu(w, l, zero):
    _, K, rows = w.shape
    tk = _row_tile(K, 512)

    def body(w_ref, z_ref, o_ref, eye):
        @pl.when(pl.program_id(0) == 0)
        def _():
            eye[...] = (lax.broadcasted_iota(jnp.int32, (rows, rows), 0) == lax.broadcasted_iota(jnp.int32, (rows, rows), 1)).astype(eye.dtype)

        x = (w_ref[...] + z_ref[0:1, 0:1]).astype(MXU_DTYPE)
        o_ref[...] = lax.dot_general(eye[...], x, _DN["NT"], preferred_element_type=F32).astype(o_ref.dtype)

    def out_index(j):
        x, y, c = _place()
        return 4 * x + 2 * y + c, j

    return pl.pallas_call(
        body, name="pack_transposed_mxu", grid=(K // tk,),
        in_specs=[pl.BlockSpec((None, tk, rows), lambda j: (l, j, 0)), pl.BlockSpec((8, LANE), lambda j: (0, 0))],
        out_specs=pl.BlockSpec((rows, tk), out_index), out_shape=_gathered(rows, K),
        scratch_shapes=[pltpu.VMEM((rows, rows), MXU_DTYPE)], compiler_params=_cp(("arbitrary",)),
    )(w, zero)


def _hbm(a):
    return pltpu.with_memory_space_constraint(a, pltpu.HBM)


def _hbm_like(arrays):
    return tuple(pltpu.HBM(a.shape, a.dtype) for a in arrays)


def _dev(px, py, pc):
    return 4 * px + 2 * py + pc


def _block_copies(wb_refs, send_sems, recv_sems, slot, block_out, block_in, peer, outgoing):
    K = len(wb_refs)
    return [pltpu.make_async_remote_copy(
        src_ref=_own_rows(wb_ref, block_out), dst_ref=_own_rows(wb_ref, block_out if outgoing else block_in),
        send_sem=send_sems.at[slot * K + k], recv_sem=recv_sems.at[slot * K + k], device_id=peer, device_id_type=MESH)
        for k, wb_ref in enumerate(wb_refs)]


def _ag_stage1(wb_refs, send_sems, recv_sems, outgoing):
    x, y, c = _place()
    me = _dev(x, y, c)
    out = []
    for slot, peer in enumerate(((x, y, 1 - c), (1 - x, y, c), (x, 1 - y, c))):
        out += _block_copies(wb_refs, send_sems, recv_sems, slot, me, _dev(*peer), peer, outgoing)
    return out


def _ag_stage2(wb_refs, send_sems, recv_sems, outgoing):
    x, y, c = _place()
    via = ((1 - x) + c * (2 * x - 1), y + c * (1 - 2 * y))
    to = (x + c * (1 - 2 * x), (1 - y) + c * (2 * y - 1), c)
    out = _block_copies(wb_refs, send_sems, recv_sems, 0, _dev(*via, c), _dev(1 - x, 1 - y, c), to, outgoing)
    for slot, (px, py) in ((1, (1 - x, y)), (2, (x, 1 - y))):
        out += _block_copies(wb_refs, send_sems, recv_sems, slot, _dev(px, py, c), _dev(px, py, 1 - c), (x, y, 1 - c), outgoing)
    return out


def _ag_start(copies, wbs, carry, name):
    K = len(wbs)

    def body(*refs):
        send_sems, recv_sems = refs[K + 1:K + 3]
        for cp in copies(refs[:K], send_sems, recv_sems, True):
            cp.start()

    outs = pl.pallas_call(
        body, name=name,
        out_shape=(pltpu.SemaphoreType.DMA((3 * K,)), pltpu.SemaphoreType.DMA((3 * K,))) + _hbm_like(list(wbs) + [carry]),
        in_specs=(HBM,) * (K + 1), out_specs=(SEM, SEM) + (HBM,) * (K + 1), input_output_aliases={k: 2 + k for k in range(K + 1)},
        compiler_params=pltpu.CompilerParams(has_side_effects=EFFECT),
    )(*[_hbm(a) for a in wbs], _hbm(carry))
    return outs[0], outs[1], list(outs[2:2 + K]), outs[2 + K]


def _ag_wait(copies, wbs, send_sems, recv_sems, after, name):
    K = len(wbs)

    def body(*refs):
        for cp in copies(refs[:K], refs[K], refs[K + 1], False):
            cp.wait_send()
            cp.wait_recv()

    return list(pl.pallas_call(
        body, name=name, out_shape=_hbm_like(wbs),
        in_specs=(HBM,) * K + (SEM, SEM, ANY), out_specs=(HBM,) * K, input_output_aliases={k: k for k in range(K)},
        compiler_params=pltpu.CompilerParams(has_side_effects=EFFECT),
    )(*wbs, send_sems, recv_sems, after))


def _ag_finish(wbs):
    K = len(wbs)

    def body(*refs):
        ins, outs, tok_ref, send_sems, recv_sems = refs[:K], refs[K:2 * K], refs[2 * K], refs[2 * K + 1], refs[2 * K + 2]
        x, y, c = _place()
        sent = []
        for k in range(K):
            cp = pltpu.make_async_remote_copy(
                src_ref=_own_rows(ins[k], _dev(1 - x, 1 - y, c)), dst_ref=_own_rows(outs[k], _dev(1 - x, 1 - y, c)),
                send_sem=send_sems.at[k], recv_sem=recv_sems.at[k], device_id=(x, y, 1 - c), device_id_type=MESH)
            cp.start()
            sent.append(cp)
        for k in range(K):
            theirs = _own_rows(outs[k], _dev(1 - x, 1 - y, 1 - c))
            pltpu.make_async_remote_copy(src_ref=theirs, dst_ref=theirs, send_sem=send_sems.at[k], recv_sem=recv_sems.at[k],
                                         device_id=(x, y, 1 - c), device_id_type=MESH).wait_recv()
        for cp in sent:
            cp.wait_send()
        tok_ref[...] = jnp.zeros_like(tok_ref)

    outs = pl.pallas_call(
        body, name="ag_finish", in_specs=[ANY] * K, out_specs=[ANY] * K + [pl.BlockSpec(memory_space=pltpu.VMEM)],
        out_shape=[jax.ShapeDtypeStruct(a.shape, a.dtype) for a in wbs] + [jax.ShapeDtypeStruct((8, LANE), F32)],
        input_output_aliases={k: k for k in range(K)},
        scratch_shapes=[pltpu.SemaphoreType.DMA((K,))] * 2, compiler_params=_cp(),
    )(*wbs)
    return list(outs[:K]), outs[K]


def _pair_copies(g_refs, land_refs, send_sems, recv_sems):
    x, y, c = _place()
    K = len(g_refs)
    out = []
    for r in range(4):
        px, py = _related(x, y, r)
        for k in range(K):
            out.append(pltpu.make_async_remote_copy(
                src_ref=_own_rows(g_refs[k], 4 * px + 2 * py + (1 - c)), dst_ref=land_refs[k].at[r],
                send_sem=send_sems.at[r * K + k], recv_sem=recv_sems.at[r * K + k], device_id=(x, y, 1 - c), device_id_type=MESH))
    return out


def _shard_tile(rows):
    for t in (512, 736, 256, 128, 64, 32, 16, 8):
        if rows % t == 0:
            return t
    return rows


def _pair_add(grad, landed):
    _, rows, C = landed.shape
    t = _shard_tile(rows)

    def g_index(r, i):
        x, y, c = _place()
        px = jnp.where(r % 2 == 1, 1 - x, x)
        py = jnp.where(r // 2 == 1, 1 - y, y)
        return (4 * px + 2 * py + c) * (rows // t) + i, 0

    def body(a_ref, b_ref, o_ref):
        o_ref[...] = (a_ref[...].astype(F32) + b_ref[...].astype(F32)).astype(o_ref.dtype)

    slot = pl.BlockSpec((None, t, C), lambda r, i: (r, i, 0))
    return pl.pallas_call(
        body, name="pair_add", grid=(4, rows // t), in_specs=[pl.BlockSpec((t, C), g_index), slot], out_specs=slot,
        out_shape=jax.ShapeDtypeStruct((4, rows, C), grad.dtype), compiler_params=_cp(("parallel", "parallel")),
    )(grad, landed)


def _rs_copies(p_refs, land_refs, send_sems, recv_sems):
    x, y, c = _place()
    K = len(p_refs)
    return [pltpu.make_async_remote_copy(src_ref=p_refs[k].at[r], dst_ref=land_refs[k].at[r - 1], send_sem=send_sems.at[(r - 1) * K + k],
                                         recv_sem=recv_sems.at[(r - 1) * K + k], device_id=(*_related(x, y, r), c), device_id_type=MESH)
            for r in (1, 2, 3) for k in range(K)]


def _rs_start(copies, n_slots, srcs, lands, carry, name):
    K = len(srcs)

    def body(*refs):
        for cp in copies(refs[:K], refs[K:2 * K], refs[2 * K + 1], refs[2 * K + 2]):
            cp.start()

    n_thru = 2 * K + 1
    outs = pl.pallas_call(
        body, name=name,
        out_shape=(pltpu.SemaphoreType.DMA((n_slots * K,)), pltpu.SemaphoreType.DMA((n_slots * K,))) + _hbm_like(list(srcs) + list(lands) + [carry]),
        in_specs=(HBM,) * n_thru, out_specs=(SEM, SEM) + (HBM,) * n_thru, input_output_aliases={k: 2 + k for k in range(n_thru)},
        compiler_params=pltpu.CompilerParams(has_side_effects=EFFECT),
    )(*[_hbm(a) for a in list(srcs) + list(lands) + [carry]])
    return outs[0], outs[1], list(outs[2:2 + K]), list(outs[2 + K:2 + 2 * K]), outs[2 + 2 * K]


def _rs_wait(copies, srcs, lands, send_sems, recv_sems, after, name):
    K = len(srcs)

    def body(*refs):
        for cp in copies(refs[:K], refs[K:2 * K], refs[2 * K], refs[2 * K + 1]):
            cp.wait_send()
            cp.wait_recv()

    outs = pl.pallas_call(
        body, name=name, out_shape=_hbm_like(list(srcs) + list(lands)),
        in_specs=(HBM,) * (2 * K) + (SEM, SEM) + (ANY,) * len(after), out_specs=(HBM,) * (2 * K),
        input_output_aliases={k: k for k in range(2 * K)}, compiler_params=pltpu.CompilerParams(has_side_effects=EFFECT),
    )(*srcs, *lands, send_sems, recv_sems, *after)
    return list(outs[:K]), list(outs[K:])


def _all_gather_small(v, reduce):
    M, N = v.shape

    def body(x_ref, out_ref, sum_ref, send_sems, recv_sems, local_sem):
        x, y, c = _place()
        me, sibling = (x, y, c), (x, y, 1 - c)
        chips = [_related(x, y, r) for r in (1, 2, 3)]

        def rows(px, py, pc):
            return out_ref.at[pl.ds(pl.multiple_of((4 * px + 2 * py + pc) * M, 8), M), :]

        def copy(k, block, to, src=None):
            return pltpu.make_async_remote_copy(src_ref=rows(*block) if src is None else src, dst_ref=rows(*block),
                                                send_sem=send_sems.at[k], recv_sem=recv_sems.at[k], device_id=to, device_id_type=MESH)

        mine = pltpu.make_async_copy(x_ref, rows(*me), local_sem)
        mine.start()
        first = [copy(0, me, sibling, src=x_ref)]
        first += [copy(1 + j, me, (*chip, c), src=x_ref) for j, chip in enumerate(chips)]
        for cp in first:
            cp.start()
        passed = [copy(4 + j, (*chip, c), sibling) for j, chip in enumerate(chips)]
        for j, chip in enumerate(chips):
            copy(1 + j, (*chip, c), me).wait_recv()
            passed[j].start()
        copy(0, sibling, me).wait_recv()
        for j, chip in enumerate(chips):
            copy(4 + j, (*chip, 1 - c), me).wait_recv()
        for cp in first + passed:
            cp.wait_send()
        mine.wait()
        if reduce:
            tot = out_ref[pl.ds(0, M), :]
            for p in range(1, N_DEV):
                tot = tot + out_ref[pl.ds(p * M, M), :]
            sum_ref[...] = tot
        else:
            sum_ref[...] = jnp.zeros_like(sum_ref)

    vm = pl.BlockSpec(memory_space=pltpu.VMEM)
    second = jax.ShapeDtypeStruct((M, N) if reduce else (8, LANE), F32)
    outs = pl.pallas_call(
        body, name="all_reduce_small" if reduce else "all_gather_small", in_specs=[vm], out_specs=[vm, vm],
        out_shape=[jax.ShapeDtypeStruct((N_DEV * M, N), v.dtype), second],
        scratch_shapes=[pltpu.SemaphoreType.DMA((7,)), pltpu.SemaphoreType.DMA((7,)), pltpu.SemaphoreType.DMA],
        compiler_params=_cp(),
    )(v)
    return outs[1] if reduce else outs


def _adamw(w, g, m, v):
    shape = w.shape
    cols = shape[-1]
    rows = w.size // cols
    tr = _row_tile(rows, 256) if rows % 8 == 0 else rows
    c1 = 1.0 / (1.0 - ADAM_B1 ** ADAM_STEP)
    c2 = 1.0 / (1.0 - ADAM_B2 ** ADAM_STEP)

    def body(w_ref, g_ref, m_ref, v_ref, d_ref, nm_ref, nv_ref):
        gv = g_ref[...]
        nm = ADAM_B1 * m_ref[...] + (1.0 - ADAM_B1) * gv
        nv = ADAM_B2 * v_ref[...] + (1.0 - ADAM_B2) * (gv * gv)
        d_ref[...] = -ADAM_LR * ((nm * c1) / (jnp.sqrt(nv * c2) + ADAM_EPS) + ADAM_WD * w_ref[...])
        nm_ref[...] = nm
        nv_ref[...] = nv

    blk = pl.BlockSpec((tr, cols), lambda i: (i, 0))
    outs = pl.pallas_call(
        body, name="adamw", grid=(rows // tr,), in_specs=[blk] * 4, out_specs=[blk] * 3,
        out_shape=[jax.ShapeDtypeStruct((rows, cols), F32)] * 3, compiler_params=_cp(("parallel",)),
    )(*[a.reshape(rows, cols) for a in (w, g, m, v)])
    return tuple(o.reshape(shape) for o in outs)


def _transpose_exact(x):
    t = x.shape[1]
    eye = (lax.broadcasted_iota(jnp.int32, (t, t), 0) == lax.broadcasted_iota(jnp.int32, (t, t), 1)).astype(MXU_DTYPE)
    if MXU_DTYPE == F32:
        return lax.dot_general(eye, x, _DN["NT"], preferred_element_type=F32)
    out = None
    for _ in range(3):
        part = x.astype(MXU_DTYPE)
        x = x - part.astype(F32)
        term = lax.dot_general(eye, part, _DN["NT"], preferred_element_type=F32)
        out = term if out is None else out + term
    return out


def _adamw_layer(w, pair_sums, landed, m, v, l, prev, col_off=None):
    L, A, B = w.shape
    ta = _row_tile(A, 256) if A % 8 == 0 else A
    c1 = 1.0 / (1.0 - ADAM_B1 ** ADAM_STEP)
    c2 = 1.0 / (1.0 - ADAM_B2 ** ADAM_STEP)
    n_prev = 0 if prev is None else 4
    if col_off is None:
        g_specs = [pl.BlockSpec((None, ta, B), functools.partial(lambda i, s: (s, i, 0), s=s)) for s in (0, 0, 1, 2)]
    else:
        assert col_off % ta == 0 and pair_sums.shape[1] == B
        g_specs = [pl.BlockSpec((None, B, ta), functools.partial(lambda i, s: (s, 0, col_off // ta + i), s=s)) for s in (0, 0, 1, 2)]

    def body(*refs):
        w_ref, m_ref, v_ref, p_ref, l1_ref, l2_ref, l3_ref = refs[:7]
        go_ref, d_ref, nm_ref, nv_ref = refs[7 + n_prev:]
        gv = ((p_ref[...].astype(F32) + l1_ref[...].astype(F32)) + l2_ref[...].astype(F32)) + l3_ref[...].astype(F32)
        if col_off is not None:
            gv = _transpose_exact(gv)
        nm = ADAM_B1 * m_ref[...] + (1.0 - ADAM_B1) * gv
        nv = ADAM_B2 * v_ref[...] + (1.0 - ADAM_B2) * (gv * gv)
        d_ref[...] = -ADAM_LR * ((nm * c1) / (jnp.sqrt(nv * c2) + ADAM_EPS) + ADAM_WD * w_ref[...])
        go_ref[...] = gv
        nm_ref[...] = nm
        nv_ref[...] = nv

    lay = pl.BlockSpec((None, ta, B), lambda i: (l, i, 0))
    return pl.pallas_call(
        body, name="adamw_layer", grid=(A // ta,),
        in_specs=[lay, lay, lay] + g_specs + [ANY] * n_prev, out_specs=[lay] * 4,
        out_shape=[jax.ShapeDtypeStruct((L, A, B), F32)] * 4, input_output_aliases={7 + j: j for j in range(n_prev)},
        compiler_params=_cp(("parallel",)),
    )(w, m, v, pair_sums, landed, landed, landed, *(prev or ()))


def _relu2(acc):
    r = jnp.maximum(acc, 0.0)
    return acc, r * r


def _relu2_bwd(acc, up):
    return (acc * (2.0 * jnp.maximum(up.astype(F32), 0.0)),)


def kernel(x, ln_mix_pre, ln_mix_post, ln_mlp_pre, ln_mlp_post, w_in, conv_a_w, proj_a, proj_b, conv_c_w, conv_c_b, norm_c_g, norm_c_b, proj_c, w_o, w_up, w_down, loss_target, m_ln_mix_pre, m_ln_mix_post, m_ln_mlp_pre, m_ln_mlp_post, m_w_in, m_conv_a_w, m_proj_a, m_proj_b, m_conv_c_w, m_conv_c_b, m_norm_c_g, m_norm_c_b, m_proj_c, m_w_o, m_w_up, m_w_down, v_ln_mix_pre, v_ln_mix_post, v_ln_mlp_pre, v_ln_mlp_post, v_w_in, v_conv_a_w, v_proj_a, v_proj_b, v_conv_c_w, v_conv_c_b, v_norm_c_g, v_norm_c_b, v_proj_c, v_w_o, v_w_up, v_w_down):
    L, D, n_in_loc = w_in.shape
    S = x.shape[1]
    U = D // 4
    N_IN = n_in_loc * N_DEV
    D_FF = w_up.shape[2] * N_DEV
    assert N_IN == 23 * U and x.shape[0] == 1
    x_i, y_i, c_i = _place()
    me = 4 * x_i + 2 * y_i + c_i

    def pack(l, which, zero):
        kinds = (lambda: _pack_transposed([proj_b, proj_a, proj_c], l, zero), lambda: _pack_plain(w_o, l, zero),
                 lambda: _pack_transposed([w_up], l, zero), lambda: _pack_plain(w_down, l, zero),
                 lambda: _pack_transposed_mxu(w_in, l, zero))
        return [kinds[k]() for k in which]

    def stage1(wbs, carry, tag):
        return _ag_start(_ag_stage1, wbs, carry, f"ag_s1_{tag}")

    def stage2(started, after, carry, tag):
        send_sems, recv_sems, wbs, _ = started
        wbs = _ag_wait(_ag_stage1, wbs, send_sems, recv_sems, after, f"ag_s1_wait_{tag}")
        return _ag_start(_ag_stage2, wbs, carry, f"ag_s2_{tag}")

    def gather_end(started, after, tag):
        send_sems, recv_sems, wbs, _ = started
        wbs, zero = _ag_finish(_ag_wait(_ag_stage2, wbs, send_sems, recv_sems, after, f"ag_s2_wait_{tag}"))
        return [w.astype(MXU_DTYPE) for w in wbs], zero

    cu = U // N_DEV
    conv_loc = jnp.concatenate([conv_a_w, conv_c_w], axis=1).reshape(L * (SC_WIDTH + CF_WIDTH), cu)
    conv_all, tok = _all_gather_small(conv_loc, False)
    conv_all = conv_all.reshape(N_DEV, L, SC_WIDTH + CF_WIDTH, cu).transpose(1, 2, 0, 3).reshape(L, SC_WIDTH + CF_WIDTH, U)
    wa_full, wc_full = conv_all[:, :SC_WIDTH], conv_all[:, SC_WIDTH:]
    every = (0, 1, 2, 3, 4)
    in_s1 = stage1(pack(0, (4,), tok), jnp.zeros((8, LANE), F32), "0_in")
    rest_packed, next_packed = pack(0, (0, 1, 2, 3), tok), pack(1, every, tok)
    in_s2 = stage2(in_s1, next_packed[4], in_s1[3], "0_in")
    rest_s1 = stage1(rest_packed, in_s2[3], "0_rest")
    wb = []

    def vec(p, l):
        return p[l][None, :]

    xs = x[0]
    saved = []
    h1 = _rms_fwd(xs, vec(ln_mix_pre, 0), "rms_fwd")
    for l in range(L):
        if l == 0:
            (w_in_t,), _ = gather_end(in_s2, rest_s1[3], "0_in")
        else:
            w_p, w_o_l, w_up_t, w_dn, w_in_t = w_next
            if l + 1 < L:
                next_s1 = stage1(pack(l + 1, every, tok), h1, l + 1)
                h1 = next_s1[3]
        proj = _mm(h1, w_in_t, "NT", (F32,), "mm_proj")
        a_out = _branch_a_fwd(proj, wa_full[l], U)
        u1 = _branch_c_conv_fwd(proj, wc_full[l], vec(conv_c_b, l), U)
        u3 = _branch_c_norm_fwd(u1, vec(norm_c_g, l), vec(norm_c_b, l))
        if l == 0:
            rest_s2 = stage2(rest_s1, u3, proj, "0_rest")
            next_s1 = stage1(next_packed, rest_s2[3], 1)
            proj = next_s1[3]
        att, att_f32 = _attn_fwd(proj, U)
        if l == 0:
            (w_p, w_o_l, w_up_t, w_dn), tok = gather_end(rest_s2, att, "0_rest")
        wb.append((w_p, w_o_l, w_up_t, w_dn, w_in_t))
        merged, ya, yb, yc = _merge_fwd(proj, a_out, att, u3, w_p)
        mixed = _mm(merged, w_o_l, "NN", (F32,), "mm_mixed")
        x1, h2 = _resid_post(xs, mixed, vec(ln_mix_post, l), vec(ln_mlp_pre, l), "resid_post_mix")
        if 0 < l < L - 1:
            next_s2 = stage2(next_s1, mixed, h2, l + 1)
            h2 = next_s2[3]
        up, act = _mm(h2, w_up_t, "NT", (MXU_DTYPE, MXU_DTYPE), "mm_up", epilogue=_relu2)
        if l == 0 and L > 1:
            next_s2 = stage2(next_s1, up, act, 1)
            act = next_s2[3]
        f = _mm(act, w_dn, "NN", (F32,), "mm_down")
        saved.append((xs, h1, proj, a_out, u1, u3, att, att_f32, ya, yb, yc, merged, mixed, x1, h2, up, act, f))
        if l + 1 < L:
            xs, h1 = _resid_post(x1, f, vec(ln_mlp_post, l), vec(ln_mix_pre, l + 1), "resid_post_mlp")
            w_next, tok = gather_end(next_s2, h1, l + 1)
        else:
            xs, _ = _resid_post(x1, f, vec(ln_mlp_post, l), None, "resid_post_last")
    dxo, loss_row = _loss_head(xs, loss_target[0])
    loss = lax.psum(loss_row[0, 0], ("x", "y", "c"))

    small = {k: [None] * L for k in ("g1", "g2", "g3", "g4", "cb", "ng", "nb", "wa", "wc")}
    def pair_start(grads, carry, name):
        lands = [lax.empty((4, g.shape[0] // N_DEV, D), WIRE_DTYPE) for g in grads]
        send_sems, recv_sems, grads, lands, carry = _rs_start(_pair_copies, 4, grads, lands, carry, name)
        return (grads, lands, send_sems, recv_sems), carry

    def chips_start(pair_flight, after, carry, name):
        grads, landed = _rs_wait(_pair_copies, *pair_flight, after, name + "_pair_wait")
        pair_sums = [_pair_add(g, ld) for g, ld in zip(grads, landed)]
        lands = [lax.empty((3,) + p.shape[1:], WIRE_DTYPE) for p in pair_sums]
        send_sems, recv_sems, pair_sums, lands, carry = _rs_start(_rs_copies, 3, pair_sums, lands, carry, name)
        return (pair_sums, lands, send_sems, recv_sems), carry

    in_flight = []
    mix_pairs = None
    for l in reversed(range(L)):
        w_p, w_o_l, w_up_t, w_dn, w_in_t = wb[l]
        xs, h1, proj, a_out, u1, u3, att, att_f32, ya, yb, yc, merged, mixed, x1, h2, up, act, f = saved[l]
        df, small["g4"][l] = _rms_bwd(f, vec(ln_mlp_post, l), dxo, None, MXU_DTYPE, "rms_bwd_post_mlp")
        d_up = _mm(df, w_dn, "NT", (MXU_DTYPE,), "mm_d_up", epilogue=_relu2_bwd, extras=(up,))
        g_dn = _mm(act, df, "TN", (WIRE_DTYPE,), "mm_g_down", tm=512, tn=2048)
        if mix_pairs is not None:
            flight, d_up = chips_start(mix_pairs, [g_dn], d_up, f"rs_start_mix_{l + 1}")
            in_flight.append((l + 1, ("p", "o", "in"), flight))
        dh2 = _mm(d_up, w_up_t, "NN", (F32,), "mm_dh2")
        g_up = _mm(d_up, h2, "TN", (WIRE_DTYPE,), "mm_g_up", tm=512, tn=2048)
        mlp_pairs, dh2 = pair_start([g_up, g_dn], dh2, f"rs_pair_mlp_{l}")
        dx1, small["g3"][l] = _rms_bwd(x1, vec(ln_mlp_pre, l), dh2, dxo, F32, "rms_bwd_pre_mlp")
        dmixed, small["g2"][l] = _rms_bwd(mixed, vec(ln_mix_post, l), dx1, None, MXU_DTYPE, "rms_bwd_post_mix")
        dmerged = _mm(dmixed, w_o_l, "NT", (F32,), "mm_dmerged")
        g_o = _mm(merged, dmixed, "TN", (WIRE_DTYPE,), "mm_g_o", tm=512, tn=2048)
        flight, dmerged = chips_start(mlp_pairs, [g_o], dmerged, f"rs_start_mlp_{l}")
        in_flight.append((l, ("up", "dn"), flight))
        dya, dyb, dyc, dgla, dglb, dglc = _gate_bwd(dmerged, proj, ya, yb, yc)
        d_att = _mm(dyb, w_p, "NN", (F32,), "mm_d_att", b_view=(0, 0, D, 2 * U))
        d_a_out = _mm(dya, w_p, "NN", (F32,), "mm_d_a_out", b_view=(0, 2 * U, D, U))
        d_u3 = _mm(dyc, w_p, "NN", (F32,), "mm_d_u3", b_view=(0, 3 * U, D, U))
        g_pb = _mm(dyb, att, "TN", (WIRE_DTYPE,), "mm_g_pb", tm=512, tn=2048)
        g_pa = _mm(dya, a_out, "TN", (WIRE_DTYPE,), "mm_g_pa", tm=512, tn=2048)
        g_pc = _mm(dyc, u3, "TN", (WIRE_DTYPE,), "mm_g_pc", tm=512, tn=2048)
        d_scb, d_scc, d_scu, small["wa"][l] = _branch_a_bwd(d_a_out, proj, wa_full[l], U)
        d_u1, small["ng"][l], small["nb"][l] = _branch_c_norm_bwd(d_u3, u1, vec(norm_c_g, l), vec(norm_c_b, l))
        d_cfa, d_cfg, small["wc"][l], small["cb"][l] = _branch_c_conv_bwd(d_u1, proj, wc_full[l], U)
        dq, dk, dv = _attn_bwd(proj, att_f32, d_att, U)
        dproj = jnp.concatenate([dq, dk, dv, d_scb, d_scc, d_scu, d_cfa, d_cfg, dgla, dglb, dglc], axis=1)
        dh1 = _mm(dproj, w_in_t, "NN", (F32,), "mm_dh1", tk=23 * LANE)
        g_in = _mm(dproj, h1, "TN", (WIRE_DTYPE,), "mm_g_in", tm=512, tn=2048)
        dxo, small["g1"][l] = _rms_bwd(xs, vec(ln_mix_pre, l), dh1, dx1, F32, "rms_bwd_pre_mix")
        carry = dxo if l > 0 else jnp.zeros((8, LANE), F32)
        mix_pairs, carry = pair_start([jnp.concatenate([g_pb, g_pa, g_pc], axis=1), g_o, g_in], carry, f"rs_pair_mix_{l}")
        if l > 0:
            dxo = carry
    grad_x = dxo[None]
    flight, carry = chips_start(mix_pairs, [carry], jnp.zeros((8, LANE), F32), "rs_start_mix_0")
    in_flight.append((0, ("p", "o", "in"), flight))
    big = {"w_in": (w_in, m_w_in, v_w_in), "proj_a": (proj_a, m_proj_a, v_proj_a), "proj_b": (proj_b, m_proj_b, v_proj_b),
           "proj_c": (proj_c, m_proj_c, v_proj_c), "w_o": (w_o, m_w_o, v_w_o), "w_up": (w_up, m_w_up, v_w_up),
           "w_down": (w_down, m_w_down, v_w_down)}
    done = {k: None for k in big}
    for n, (l, keys, flight) in enumerate(in_flight):
        after = [carry]
        if n == len(in_flight) - 1:
            after += [done[k][3] for k in big if done[k] is not None]
        pair_sums, lands = _rs_wait(_rs_copies, *flight, after, f"rs_wait_{keys[0]}_{l}")
        g = dict(zip(keys, zip(pair_sums, lands)))
        if "up" in g:
            layer_grads = {"w_up": (g["up"], 0), "w_down": (g["dn"], None)}
        else:
            layer_grads = {"w_in": (g["in"], 0), "proj_b": (g["p"], 0), "proj_a": (g["p"], 2 * U), "proj_c": (g["p"], 3 * U),
                           "w_o": (g["o"], None)}
        for k, ((p, ld), col_off) in layer_grads.items():
            w, m, v = big[k]
            done[k] = _adamw_layer(w, p, ld, m, v, l, done[k], col_off)

    order = ("g1", "g2", "g3", "g4", "cb", "ng", "nb", "wa", "wc")
    parts = [jnp.stack(small[k]).reshape(-1) for k in order]
    flat = jnp.concatenate(parts)
    n_flat = flat.shape[0]
    pad = (-n_flat) % (8 * LANE)
    flat = jnp.pad(flat, (0, pad)).reshape(-1, LANE)
    tot = _all_gather_small(flat, True).reshape(-1)[:n_flat]
    red, pos = {}, 0
    for k, p in zip(order, parts):
        red[k] = tot[pos:pos + p.shape[0]]
        pos += p.shape[0]
    g_ln_mix_pre, g_ln_mix_post = red["g1"].reshape(L, D), red["g2"].reshape(L, D)
    g_ln_mlp_pre, g_ln_mlp_post = red["g3"].reshape(L, D), red["g4"].reshape(L, D)
    g_conv_c_b, g_norm_c_g, g_norm_c_b = red["cb"].reshape(L, U), red["ng"].reshape(L, U), red["nb"].reshape(L, U)
    g_conv_a_w = lax.dynamic_slice_in_dim(red["wa"].reshape(L, SC_WIDTH, U), me * cu, cu, axis=2)
    g_conv_c_w = lax.dynamic_slice_in_dim(red["wc"].reshape(L, CF_WIDTH, U), me * cu, cu, axis=2)

    small_w = {"ln_mix_pre": (ln_mix_pre, g_ln_mix_pre, m_ln_mix_pre, v_ln_mix_pre),
               "ln_mix_post": (ln_mix_post, g_ln_mix_post, m_ln_mix_post, v_ln_mix_post),
               "ln_mlp_pre": (ln_mlp_pre, g_ln_mlp_pre, m_ln_mlp_pre, v_ln_mlp_pre),
               "ln_mlp_post": (ln_mlp_post, g_ln_mlp_post, m_ln_mlp_post, v_ln_mlp_post),
               "conv_a_w": (conv_a_w, g_conv_a_w, m_conv_a_w, v_conv_a_w), "conv_c_w": (conv_c_w, g_conv_c_w, m_conv_c_w, v_conv_c_w),
               "conv_c_b": (conv_c_b, g_conv_c_b, m_conv_c_b, v_conv_c_b), "norm_c_g": (norm_c_g, g_norm_c_g, m_norm_c_g, v_norm_c_g),
               "norm_c_b": (norm_c_b, g_norm_c_b, m_norm_c_b, v_norm_c_b)}
    for k, (w, g, m, v) in small_w.items():
        done[k] = (g,) + _adamw(w, g, m, v)
    names = ("ln_mix_pre", "ln_mix_post", "ln_mlp_pre", "ln_mlp_post", "w_in", "conv_a_w", "proj_a", "proj_b", "conv_c_w", "conv_c_b",
             "norm_c_g", "norm_c_b", "proj_c", "w_o", "w_up", "w_down")
    return (loss, grad_x, *[done[k][0] for k in names], *[done[k][1] for k in names], *[done[k][2] for k in names],
            *[done[k][3] for k in names])
```

```python
import functools

import jax
import jax.numpy as jnp
from jax import lax
from jax.experimental import pallas as pl
from jax.experimental.pallas import tpu as pltpu

F32 = jnp.float32
MXU_DTYPE = jnp.bfloat16
WIRE_DTYPE = jnp.bfloat16
MESH = pl.DeviceIdType.MESH
ANY = pl.BlockSpec(memory_space=pl.ANY)
HBM = pl.BlockSpec(memory_space=pltpu.HBM)
SEM = pl.BlockSpec(memory_space=pltpu.SEMAPHORE)
EFFECT = pltpu.SideEffectType.DATAFLOW_SIDE_EFFECTING

N_DEV = 8
HEAD_DIM = 128
RMS_EPS = 1e-6
LN_EPS = 1e-5
SC_WIDTH = 3
CF_WIDTH = 31
CONV_PAD = 32
ADAM_LR, ADAM_B1, ADAM_B2, ADAM_EPS, ADAM_WD, ADAM_STEP = 0.001, 0.9, 0.999, 1e-08, 0.01, 10
VMEM_LIMIT = 56 * 1024 * 1024
LANE = 128


def _cp(sem=None, **kw):
    return pltpu.CompilerParams(dimension_semantics=sem, vmem_limit_bytes=VMEM_LIMIT, **kw)


def _sigmoid(x):
    return 1.0 / (1.0 + jnp.exp(-x))


def _row_tile(rows, want):
    t = min(rows, want)
    while rows % t:
        t //= 2
    return t


_DN = {"NN": (((1,), (0,)), ((), ())), "NT": (((1,), (1,)), ((), ())), "TN": (((0,), (0,)), ((), ()))}


def _mm(a, b, mode, out_dtypes, name, *, a_view=None, b_view=None, tm=2048, tn=512, tk=2048, epilogue=None, extras=()):
    a_view = a_view or (0, 0) + tuple(a.shape)
    b_view = b_view or (0, 0) + tuple(b.shape)
    ar, ac, an, am = a_view
    br, bc, bn, bm = b_view
    if mode == "NN":
        M, K, K2, N = an, am, bn, bm
    elif mode == "NT":
        M, K, N, K2 = an, am, bn, bm
    else:
        K, M, K2, N = an, am, bn, bm
    assert K == K2, (name, a_view, b_view)
    tm, tn, tk = _row_tile(M, tm), _row_tile(N, tn), _row_tile(K, tk)
    (a_m_off, a_k_off) = (ac, ar) if mode == "TN" else (ar, ac)
    (b_n_off, b_k_off) = (br, bc) if mode == "NT" else (bc, br)
    while a_m_off % tm:
        tm //= 2
    while b_n_off % tn:
        tn //= 2
    while a_k_off % tk or b_k_off % tk:
        tk //= 2
    nk = K // tk
    a_blk = (tk, tm) if mode == "TN" else (tm, tk)
    b_blk = (tn, tk) if mode == "NT" else (tk, tn)
    assert ar % a_blk[0] == 0 and ac % a_blk[1] == 0, (name, a_view, a_blk)
    assert br % b_blk[0] == 0 and bc % b_blk[1] == 0, (name, b_view, b_blk)
    ao, bo = (ar // a_blk[0], ac // a_blk[1]), (br // b_blk[0], bc // b_blk[1])
    if mode == "TN":
        a_spec = pl.BlockSpec(a_blk, lambda i, j, k: (ao[0] + k, ao[1] + i))
    else:
        a_spec = pl.BlockSpec(a_blk, lambda i, j, k: (ao[0] + i, ao[1] + k))
    if mode == "NT":
        b_spec = pl.BlockSpec(b_blk, lambda i, j, k: (bo[0] + j, bo[1] + k))
    else:
        b_spec = pl.BlockSpec(b_blk, lambda i, j, k: (bo[0] + k, bo[1] + j))
    o_spec = pl.BlockSpec((tm, tn), lambda i, j, k: (i, j))
    n_ex, n_out = len(extras), len(out_dtypes)
    dn = _DN[mode]

    def body(*refs):
        a_ref, b_ref = refs[:2]
        ex_refs = refs[2:2 + n_ex]
        o_refs = refs[2 + n_ex:2 + n_ex + n_out]
        p = lax.dot_general(a_ref[...], b_ref[...], dn, preferred_element_type=F32)

        def finish(acc):
            outs = epilogue(acc, *[r[...] for r in ex_refs]) if epilogue else (acc,)
            for o_ref, o in zip(o_refs, outs):
                o_ref[...] = o.astype(o_ref.dtype)

        if nk == 1:
            finish(p)
        else:
            acc_ref = refs[-1]
            k = pl.program_id(2)

            @pl.when(k == 0)
            def _():
                acc_ref[...] = p

            @pl.when(k > 0)
            def _():
                acc_ref[...] += p

            @pl.when(k == nk - 1)
            def _():
                finish(acc_ref[...])

    outs = pl.pallas_call(
        body, name=name, grid=(M // tm, N // tn, nk),
        in_specs=[a_spec, b_spec] + [o_spec] * n_ex, out_specs=[o_spec] * n_out,
        out_shape=[jax.ShapeDtypeStruct((M, N), d) for d in out_dtypes],
        scratch_shapes=[pltpu.VMEM((tm, tn), F32)] if nk > 1 else [],
        compiler_params=_cp(("parallel", "parallel", "arbitrary")),
    )(a, b, *extras)
    return outs[0] if n_out == 1 else outs


def _rms_fwd(x, g, name):
    S, D = x.shape
    tr = _row_tile(S, 256)

    def body(x_ref, g_ref, h_ref):
        xv = x_ref[...]
        r = lax.rsqrt(jnp.mean(xv * xv, axis=-1, keepdims=True) + RMS_EPS)
        h_ref[...] = ((xv * r) * g_ref[...]).astype(h_ref.dtype)

    return pl.pallas_call(
        body, name=name, grid=(S // tr,),
        in_specs=[pl.BlockSpec((tr, D), lambda i: (i, 0)), pl.BlockSpec((1, D), lambda i: (0, 0))],
        out_specs=pl.BlockSpec((tr, D), lambda i: (i, 0)),
        out_shape=jax.ShapeDtypeStruct((S, D), MXU_DTYPE), compiler_params=_cp(("parallel",)),
    )(x, g)


def _resid_post(xres, y, g_post, g_next, name):
    S, D = y.shape
    tr = _row_tile(S, 256)
    has_next = g_next is not None

    def body(*refs):
        xr_ref, y_ref, gp_ref = refs[:3]
        yv = y_ref[...]
        r = lax.rsqrt(jnp.mean(yv * yv, axis=-1, keepdims=True) + RMS_EPS)
        xn = xr_ref[...] + (yv * r) * gp_ref[...]
        if has_next:
            gn_ref, xo_ref, h_ref = refs[3:]
            r2 = lax.rsqrt(jnp.mean(xn * xn, axis=-1, keepdims=True) + RMS_EPS)
            h_ref[...] = ((xn * r2) * gn_ref[...]).astype(h_ref.dtype)
        else:
            xo_ref = refs[3]
        xo_ref[...] = xn

    row = pl.BlockSpec((tr, D), lambda i: (i, 0))
    vec = pl.BlockSpec((1, D), lambda i: (0, 0))
    outs = pl.pallas_call(
        body, name=name, grid=(S // tr,),
        in_specs=[row, row, vec] + ([vec] if has_next else []),
        out_specs=[row] + ([row] if has_next else []),
        out_shape=[jax.ShapeDtypeStruct((S, D), F32)] + ([jax.ShapeDtypeStruct((S, D), MXU_DTYPE)] if has_next else []),
        compiler_params=_cp(("parallel",)),
    )(xres, y, g_post, *([g_next] if has_next else []))
    return (outs[0], outs[1]) if has_next else (outs[0], None)


def _rms_bwd(xin, g, dy, dres, out_dtype, name):
    S, D = xin.shape
    tr = _row_tile(S, 256)
    has_res = dres is not None

    def body(*refs):
        x_ref, g_ref, dy_ref = refs[:3]
        dx_ref, dg_ref = refs[-2:]
        xv, dyv = x_ref[...], dy_ref[...].astype(F32)
        r = lax.rsqrt(jnp.mean(xv * xv, axis=-1, keepdims=True) + RMS_EPS)
        n = xv * r
        dyg = dyv * g_ref[...]
        dx = r * (dyg - n * jnp.mean(dyg * n, axis=-1, keepdims=True))
        if has_res:
            dx = dx + refs[3][...]
        dx_ref[...] = dx.astype(dx_ref.dtype)

        @pl.when(pl.program_id(0) == 0)
        def _():
            dg_ref[...] = jnp.zeros_like(dg_ref)

        dg_ref[...] += jnp.sum(dyv * n, axis=0, keepdims=True)

    row = pl.BlockSpec((tr, D), lambda i: (i, 0))
    vec = pl.BlockSpec((1, D), lambda i: (0, 0))
    return pl.pallas_call(
        body, name=name, grid=(S // tr,),
        in_specs=[row, vec, row] + ([row] if has_res else []), out_specs=[row, vec],
        out_shape=[jax.ShapeDtypeStruct((S, D), out_dtype), jax.ShapeDtypeStruct((1, D), F32)],
        compiler_params=_cp(("arbitrary",)),
    )(xin, g, dy, *([dres] if has_res else []))


def _loss_head(y, target):
    S, D = y.shape
    tr = _row_tile(S, 256)

    def body(y_ref, t_ref, dy_ref, l_ref):
        e = y_ref[...] - t_ref[...]
        dy_ref[...] = e * (1.0 / D)

        @pl.when(pl.program_id(0) == 0)
        def _():
            l_ref[...] = jnp.zeros_like(l_ref)

        l_ref[...] += 0.5 * jnp.sum(jnp.mean(e * e, axis=-1, keepdims=True), axis=0, keepdims=True)

    row = pl.BlockSpec((tr, D), lambda i: (i, 0))
    return pl.pallas_call(
        body, name="loss_head", grid=(S // tr,), in_specs=[row, row],
        out_specs=[row, pl.BlockSpec((1, LANE), lambda i: (0, 0))],
        out_shape=[jax.ShapeDtypeStruct((S, D), F32), jax.ShapeDtypeStruct((1, LANE), F32)],
        compiler_params=_cp(("arbitrary",)),
    )(y, target)


def _gate_specs(S, U, tr):
    gl = [pl.BlockSpec((tr, U), functools.partial(lambda i, j, o: (i, o + j), o=o)) for o in (11, 15, 19)]
    return gl, pl.BlockSpec((tr, U), lambda i, j: (i, j))


def _merge_fwd(proj, a_out, att, u3, w_p):
    S, U = a_out.shape
    D = 4 * U
    tm = _row_tile(S, 1024)

    def body(ga_ref, gb_ref, gc_ref, a_ref, b_ref, c_ref, w_ref, m_ref, ya_ref, yb_ref, yc_ref):
        yb = lax.dot_general(b_ref[...], w_ref[:, 0:2 * U], _DN["NT"], preferred_element_type=F32)
        ya = lax.dot_general(a_ref[...], w_ref[:, 2 * U:3 * U], _DN["NT"], preferred_element_type=F32)
        yc = lax.dot_general(c_ref[...], w_ref[:, 3 * U:4 * U], _DN["NT"], preferred_element_type=F32)
        m_ref[...] = (_sigmoid(ga_ref[...]) * ya + _sigmoid(gb_ref[...]) * yb + _sigmoid(gc_ref[...]) * yc).astype(m_ref.dtype)
        ya_ref[...] = ya.astype(ya_ref.dtype)
        yb_ref[...] = yb.astype(yb_ref.dtype)
        yc_ref[...] = yc.astype(yc_ref.dtype)

    gl, blk = _gate_specs(S, U, tm)
    rows = lambda k: pl.BlockSpec((tm, k), lambda i, j: (i, 0))
    return pl.pallas_call(
        body, name="merge_fwd", grid=(S // tm, 4),
        in_specs=gl + [rows(U), rows(2 * U), rows(U), pl.BlockSpec((U, D), lambda i, j: (j, 0))], out_specs=[blk] * 4,
        out_shape=[jax.ShapeDtypeStruct((S, D), MXU_DTYPE)] * 4, compiler_params=_cp(("parallel", "parallel")),
    )(proj, proj, proj, a_out, att, u3, w_p)


def _gate_bwd(dm, proj, ya, yb, yc):
    S, D = ya.shape
    U = D // 4
    tr = _row_tile(S, 256)

    def body(dm_ref, ga_ref, gb_ref, gc_ref, ya_ref, yb_ref, yc_ref, da_ref, db_ref, dc_ref, la_ref, lb_ref, lc_ref):
        d = dm_ref[...]
        for g_ref, y_ref, dy_ref, dl_ref in ((ga_ref, ya_ref, da_ref, la_ref), (gb_ref, yb_ref, db_ref, lb_ref),
                                             (gc_ref, yc_ref, dc_ref, lc_ref)):
            g = _sigmoid(g_ref[...])
            dy_ref[...] = (d * g).astype(dy_ref.dtype)
            dl_ref[...] = (d * y_ref[...] * g * (1.0 - g)).astype(dl_ref.dtype)

    gl, blk = _gate_specs(S, U, tr)
    return pl.pallas_call(
        body, name="gate_bwd", grid=(S // tr, 4), in_specs=[blk] + gl + [blk] * 3, out_specs=[blk] * 6,
        out_shape=[jax.ShapeDtypeStruct((S, D), MXU_DTYPE)] * 6, compiler_params=_cp(("parallel", "parallel")),
    )(dm, proj, proj, proj, ya, yb, yc)


def _chunks(S):
    r = _row_tile(S, 256)
    return [(r0, r) for r0 in range(0, S, r)]


def _conv_causal(front_ref, w_ref, K, r0, R):
    acc = None
    for j in range(K):
        term = w_ref[pl.ds(K - 1 - j, 1), :] * front_ref[pl.ds(CONV_PAD + r0 - j, R), :]
        acc = term if acc is None else acc + term
    return acc


def _conv_anticausal(back_ref, w_ref, K, r0, R):
    acc = None
    for j in range(K):
        term = w_ref[pl.ds(K - 1 - j, 1), :] * back_ref[pl.ds(r0 + j, R), :]
        acc = term if acc is None else acc + term
    return acc


def _conv_wgrad(front_ref, back_ref, dw_ref, K, S):
    for j in range(K):
        tot = None
        for r0, R in _chunks(S):
            part = jnp.sum(back_ref[pl.ds(r0, R), :] * front_ref[pl.ds(CONV_PAD + r0 - j, R), :], axis=0, keepdims=True)
            tot = part if tot is None else tot + part
        dw_ref[pl.ds(K - 1 - j, 1), :] = tot


def _col(S, cw, unit_off):
    return pl.BlockSpec((S, cw), functools.partial(lambda cb, o: (0, o + cb), o=unit_off))


def _branch_a_fwd(proj, wa, U):
    S = proj.shape[0]
    cw = min(LANE, U)
    nb = U // cw
    K = SC_WIDTH

    def body(b_ref, c_ref, u_ref, w_ref, o_ref, front):
        front[pl.ds(0, CONV_PAD), :] = jnp.zeros((CONV_PAD, cw), F32)
        front[pl.ds(CONV_PAD, S), :] = c_ref[...] * u_ref[...]
        for r0, R in _chunks(S):
            o_ref[pl.ds(r0, R), :] = (b_ref[pl.ds(r0, R), :] * _conv_causal(front, w_ref, K, r0, R)).astype(o_ref.dtype)

    return pl.pallas_call(
        body, name="branch_a_fwd", grid=(nb,),
        in_specs=[_col(S, cw, 6 * nb), _col(S, cw, 7 * nb), _col(S, cw, 8 * nb), pl.BlockSpec((K, cw), lambda cb: (0, cb))],
        out_specs=pl.BlockSpec((S, cw), lambda cb: (0, cb)), out_shape=jax.ShapeDtypeStruct((S, U), MXU_DTYPE),
        scratch_shapes=[pltpu.VMEM((S + CONV_PAD, cw), F32)], compiler_params=_cp(("parallel",)),
    )(proj, proj, proj, wa)


def _branch_a_bwd(d_out, proj, wa, U):
    S = proj.shape[0]
    cw = min(LANE, U)
    nb = U // cw
    K = SC_WIDTH

    def body(d_ref, b_ref, c_ref, u_ref, w_ref, db_ref, dc_ref, du_ref, dw_ref, front, back):
        front[pl.ds(0, CONV_PAD), :] = jnp.zeros((CONV_PAD, cw), F32)
        front[pl.ds(CONV_PAD, S), :] = c_ref[...] * u_ref[...]
        back[pl.ds(S, CONV_PAD), :] = jnp.zeros((CONV_PAD, cw), F32)
        back[pl.ds(0, S), :] = d_ref[...] * b_ref[...]
        for r0, R in _chunks(S):
            rows = pl.ds(r0, R)
            db_ref[rows, :] = (d_ref[rows, :] * _conv_causal(front, w_ref, K, r0, R)).astype(db_ref.dtype)
            d_ai = _conv_anticausal(back, w_ref, K, r0, R)
            dc_ref[rows, :] = (d_ai * u_ref[rows, :]).astype(dc_ref.dtype)
            du_ref[rows, :] = (d_ai * c_ref[rows, :]).astype(du_ref.dtype)
        _conv_wgrad(front, back, dw_ref, K, S)

    blk = pl.BlockSpec((S, cw), lambda cb: (0, cb))
    wblk = pl.BlockSpec((K, cw), lambda cb: (0, cb))
    return pl.pallas_call(
        body, name="branch_a_bwd", grid=(nb,),
        in_specs=[blk, _col(S, cw, 6 * nb), _col(S, cw, 7 * nb), _col(S, cw, 8 * nb), wblk],
        out_specs=[blk, blk, blk, wblk],
        out_shape=[jax.ShapeDtypeStruct((S, U), MXU_DTYPE)] * 3 + [jax.ShapeDtypeStruct((K, U), F32)],
        scratch_shapes=[pltpu.VMEM((S + CONV_PAD, cw), F32)] * 2, compiler_params=_cp(("parallel",)),
    )(d_out, proj, proj, proj, wa)


def _branch_c_conv_fwd(proj, wc, cb, U):
    S = proj.shape[0]
    cw = min(LANE, U)
    nb = U // cw
    K = CF_WIDTH

    def body(a_ref, g_ref, w_ref, bias_ref, o_ref, front):
        front[pl.ds(0, CONV_PAD), :] = jnp.zeros((CONV_PAD, cw), F32)
        front[pl.ds(CONV_PAD, S), :] = a_ref[...] * _sigmoid(g_ref[...])
        for r0, R in _chunks(S):
            o_ref[pl.ds(r0, R), :] = _conv_causal(front, w_ref, K, r0, R) + bias_ref[...]

    return pl.pallas_call(
        body, name="branch_c_conv_fwd", grid=(nb,),
        in_specs=[_col(S, cw, 9 * nb), _col(S, cw, 10 * nb), pl.BlockSpec((K, cw), lambda c: (0, c)),
                  pl.BlockSpec((1, cw), lambda c: (0, c))],
        out_specs=pl.BlockSpec((S, cw), lambda c: (0, c)), out_shape=jax.ShapeDtypeStruct((S, U), F32),
        scratch_shapes=[pltpu.VMEM((S + CONV_PAD, cw), F32)], compiler_params=_cp(("parallel",)),
    )(proj, proj, wc, cb)


def _branch_c_conv_bwd(d_u1, proj, wc, U):
    S = proj.shape[0]
    cw = min(LANE, U)
    nb = U // cw
    K = CF_WIDTH

    def body(d_ref, a_ref, g_ref, w_ref, da_ref, dg_ref, dw_ref, dbias_ref, front, back):
        sg = _sigmoid(g_ref[...])
        front[pl.ds(0, CONV_PAD), :] = jnp.zeros((CONV_PAD, cw), F32)
        front[pl.ds(CONV_PAD, S), :] = a_ref[...] * sg
        back[pl.ds(S, CONV_PAD), :] = jnp.zeros((CONV_PAD, cw), F32)
        back[pl.ds(0, S), :] = d_ref[...]
        dbias_ref[...] = jnp.sum(d_ref[...], axis=0, keepdims=True)
        for r0, R in _chunks(S):
            rows = pl.ds(r0, R)
            d_u0 = _conv_anticausal(back, w_ref, K, r0, R)
            s = _sigmoid(g_ref[rows, :])
            da_ref[rows, :] = (d_u0 * s).astype(da_ref.dtype)
            dg_ref[rows, :] = (d_u0 * a_ref[rows, :] * s * (1.0 - s)).astype(dg_ref.dtype)
        _conv_wgrad(front, back, dw_ref, K, S)

    blk = pl.BlockSpec((S, cw), lambda c: (0, c))
    wblk = pl.BlockSpec((K, cw), lambda c: (0, c))
    vblk = pl.BlockSpec((1, cw), lambda c: (0, c))
    return pl.pallas_call(
        body, name="branch_c_conv_bwd", grid=(nb,),
        in_specs=[blk, _col(S, cw, 9 * nb), _col(S, cw, 10 * nb), wblk], out_specs=[blk, blk, wblk, vblk],
        out_shape=[jax.ShapeDtypeStruct((S, U), MXU_DTYPE)] * 2 + [jax.ShapeDtypeStruct((K, U), F32), jax.ShapeDtypeStruct((1, U), F32)],
        scratch_shapes=[pltpu.VMEM((S + CONV_PAD, cw), F32)] * 2, compiler_params=_cp(("parallel",)),
    )(d_u1, proj, proj, wc)


def _branch_c_norm_fwd(u1, ng, nbias):
    S, U = u1.shape
    tr = _row_tile(S, 256)

    def body(u_ref, g_ref, b_ref, o_ref):
        u = u_ref[...]
        mu = jnp.mean(u, axis=-1, keepdims=True)
        var = jnp.mean(jnp.square(u - mu), axis=-1, keepdims=True)
        u2 = ((u - mu) * lax.rsqrt(var + LN_EPS)) * g_ref[...] + b_ref[...]
        o_ref[...] = (u2 * _sigmoid(u2)).astype(o_ref.dtype)

    row = pl.BlockSpec((tr, U), lambda i: (i, 0))
    vec = pl.BlockSpec((1, U), lambda i: (0, 0))
    return pl.pallas_call(
        body, name="branch_c_norm_fwd", grid=(S // tr,), in_specs=[row, vec, vec], out_specs=row,
        out_shape=jax.ShapeDtypeStruct((S, U), MXU_DTYPE), compiler_params=_cp(("parallel",)),
    )(u1, ng, nbias)


def _branch_c_norm_bwd(d_u3, u1, ng, nbias):
    S, U = u1.shape
    tr = _row_tile(S, 256)

    def body(d_ref, u_ref, g_ref, b_ref, du_ref, dg_ref, db_ref):
        u = u_ref[...]
        mu = jnp.mean(u, axis=-1, keepdims=True)
        var = jnp.mean(jnp.square(u - mu), axis=-1, keepdims=True)
        rstd = lax.rsqrt(var + LN_EPS)
        xh = (u - mu) * rstd
        u2 = xh * g_ref[...] + b_ref[...]
        s = _sigmoid(u2)
        d_u2 = d_ref[...] * (s * (1.0 + u2 * (1.0 - s)))
        d_xh = d_u2 * g_ref[...]
        du_ref[...] = rstd * (d_xh - jnp.mean(d_xh, axis=-1, keepdims=True) - xh * jnp.mean(d_xh * xh, axis=-1, keepdims=True))

        @pl.when(pl.program_id(0) == 0)
        def _():
            dg_ref[...] = jnp.zeros_like(dg_ref)
            db_ref[...] = jnp.zeros_like(db_ref)

        dg_ref[...] += jnp.sum(d_u2 * xh, axis=0, keepdims=True)
        db_ref[...] += jnp.sum(d_u2, axis=0, keepdims=True)

    row = pl.BlockSpec((tr, U), lambda i: (i, 0))
    vec = pl.BlockSpec((1, U), lambda i: (0, 0))
    return pl.pallas_call(
        body, name="branch_c_norm_bwd", grid=(S // tr,), in_specs=[row, row, vec, vec], out_specs=[row, vec, vec],
        out_shape=[jax.ShapeDtypeStruct((S, U), F32), jax.ShapeDtypeStruct((1, U), F32), jax.ShapeDtypeStruct((1, U), F32)],
        compiler_params=_cp(("arbitrary",)),
    )(d_u3, u1, ng, nbias)


def _tri(T, inclusive):
    j = lax.broadcasted_iota(jnp.int32, (T, T), 0)
    s = lax.broadcasted_iota(jnp.int32, (T, T), 1)
    return ((j >= s) if inclusive else (j > s)).astype(MXU_DTYPE)


def _split_dot(x, tri):
    if MXU_DTYPE == F32:
        return jnp.dot(x, tri, preferred_element_type=F32)
    hi = x.astype(MXU_DTYPE)
    lo = (x - hi.astype(F32)).astype(MXU_DTYPE)
    return jnp.dot(hi, tri, preferred_element_type=F32) + jnp.dot(lo, tri, preferred_element_type=F32)


def _sb_block(qb, kb, T, tri_strict, c_lf, diag):
    z = lax.dot_general(qb, kb, _DN["NT"], preferred_element_type=F32) * (HEAD_DIM ** -0.5)
    e = jnp.exp(-jnp.abs(z))
    lg = jnp.log(1.0 + e)
    log_beta = jnp.minimum(z, 0.0) - lg
    lf = jnp.minimum(-z, 0.0) - lg
    mask = None
    if diag:
        mask = lax.broadcasted_iota(jnp.int32, (T, T), 1) < lax.broadcasted_iota(jnp.int32, (T, T), 0)
        lf = jnp.where(mask, lf, 0.0)
    a = jnp.exp(log_beta + _split_dot(lf, tri_strict) + c_lf)
    if diag:
        a = jnp.where(mask, a, 0.0)
    return z, e, mask, lf, a


def _key_blocks(i, step, init):
    carry = step(i, init, True)
    return lax.fori_loop(1, i + 1, lambda jj, c: step(i - jj, c, False), carry)


def _attn_specs(S, U, h_blocks):
    nh = (2 * U) // HEAD_DIM
    return [pl.BlockSpec((S, HEAD_DIM), functools.partial(lambda h, o: (0, o + h), o=o * nh)) for o in range(h_blocks)]


def _attn_fwd(proj, U):
    S = proj.shape[0]
    nh = (2 * U) // HEAD_DIM
    T = _row_tile(S, 256)
    nq = S // T

    def body(q_ref, k_ref, v_ref, o_ref, of_ref, qs, ks, vs):
        qs[...] = q_ref[...].astype(MXU_DTYPE)
        ks[...] = k_ref[...].astype(MXU_DTYPE)
        vs[...] = v_ref[...].astype(MXU_DTYPE)
        tri = _tri(T, False)

        def q_loop(i, _):
            rows = pl.ds(pl.multiple_of(i * T, T), T)
            qb = qs[rows, :]

            def step(j, carry, diag):
                c_lf, acc = carry
                cols = pl.ds(pl.multiple_of(j * T, T), T)
                _, _, _, lf, a = _sb_block(qb, ks[cols, :], T, tri, c_lf, diag)
                acc = acc + jnp.dot(a.astype(MXU_DTYPE), vs[cols, :], preferred_element_type=F32)
                return c_lf + jnp.sum(lf, axis=1, keepdims=True), acc

            _, acc = _key_blocks(i, step, (jnp.zeros((T, 1), F32), jnp.zeros((T, HEAD_DIM), F32)))
            o_ref[rows, :] = acc.astype(o_ref.dtype)
            of_ref[rows, :] = acc
            return 0

        lax.fori_loop(0, nq, q_loop, 0)

    hblk = pl.BlockSpec((S, HEAD_DIM), lambda h: (0, h))
    return pl.pallas_call(
        body, name="attn_fwd", grid=(nh,), in_specs=_attn_specs(S, U, 3), out_specs=[hblk, hblk],
        out_shape=[jax.ShapeDtypeStruct((S, 2 * U), MXU_DTYPE), jax.ShapeDtypeStruct((S, 2 * U), F32)],
        scratch_shapes=[pltpu.VMEM((S, HEAD_DIM), MXU_DTYPE)] * 3, compiler_params=_cp(("parallel",)),
    )(proj, proj, proj)


def _attn_bwd(proj, att_f32, d_att, U):
    S = proj.shape[0]
    nh = (2 * U) // HEAD_DIM
    T = _row_tile(S, 256)
    nq = S // T
    scale = HEAD_DIM ** -0.5

    def body(q_ref, k_ref, v_ref, o_ref, do_ref, dq_ref, dk_ref, dv_ref, qs, ks, vs, dos, dka, dva):
        qs[...] = q_ref[...].astype(MXU_DTYPE)
        ks[...] = k_ref[...].astype(MXU_DTYPE)
        vs[...] = v_ref[...].astype(MXU_DTYPE)
        dos[...] = do_ref[...].astype(MXU_DTYPE)
        dka[...] = jnp.zeros_like(dka)
        dva[...] = jnp.zeros_like(dva)
        tri = _tri(T, False)
        tri_inc = _tri(T, True)

        def q_loop(i, _):
            rows = pl.ds(pl.multiple_of(i * T, T), T)
            qb = qs[rows, :]
            dob = dos[rows, :]
            delta = jnp.sum(dob.astype(F32) * o_ref[rows, :], axis=1, keepdims=True)

            def step(j, carry, diag):
                c_lf, c_g, dq = carry
                cols = pl.ds(pl.multiple_of(j * T, T), T)
                kb, vb = ks[cols, :], vs[cols, :]
                z, e, mask, lf, a = _sb_block(qb, kb, T, tri, c_lf, diag)
                a_mx = a.astype(MXU_DTYPE)
                d_a = lax.dot_general(dob, vb, _DN["NT"], preferred_element_type=F32)
                g = a_mx.astype(F32) * d_a
                prefix = delta - (_split_dot(g, tri_inc) + c_g)
                inv = 1.0 / (1.0 + e)
                beta = jnp.where(z >= 0.0, 1.0, e) * inv
                one_m_beta = jnp.where(z >= 0.0, e, 1.0) * inv
                dz = (g * one_m_beta - prefix * beta) * scale
                if diag:
                    dz = jnp.where(mask, dz, 0.0)
                dz = dz.astype(MXU_DTYPE)
                dq = dq + jnp.dot(dz, kb, preferred_element_type=F32)
                dka[cols, :] += lax.dot_general(dz, qb, _DN["TN"], preferred_element_type=F32)
                dva[cols, :] += lax.dot_general(a_mx, dob, _DN["TN"], preferred_element_type=F32)
                return c_lf + jnp.sum(lf, axis=1, keepdims=True), c_g + jnp.sum(g, axis=1, keepdims=True), dq

            zero = jnp.zeros((T, 1), F32)
            _, _, dq = _key_blocks(i, step, (zero, zero, jnp.zeros((T, HEAD_DIM), F32)))
            dq_ref[rows, :] = dq.astype(dq_ref.dtype)
            return 0

        lax.fori_loop(0, nq, q_loop, 0)
        dk_ref[...] = dka[...].astype(dk_ref.dtype)
        dv_ref[...] = dva[...].astype(dv_ref.dtype)

    hblk = pl.BlockSpec((S, HEAD_DIM), lambda h: (0, h))
    return pl.pallas_call(
        body, name="attn_bwd", grid=(nh,), in_specs=_attn_specs(S, U, 3) + [hblk, hblk], out_specs=[hblk] * 3,
        out_shape=[jax.ShapeDtypeStruct((S, 2 * U), MXU_DTYPE)] * 3,
        scratch_shapes=[pltpu.VMEM((S, HEAD_DIM), MXU_DTYPE)] * 4 + [pltpu.VMEM((S, HEAD_DIM), F32)] * 2,
        compiler_params=_cp(("parallel",)),
    )(proj, proj, proj, att_f32, d_att)


def _place():
    return lax.axis_index("x"), lax.axis_index("y"), lax.axis_index("c")


def _flip(v, bit):
    return 1 - v if bit else v


def _related(x, y, r):
    return _flip(x, r & 1), _flip(y, r >> 1)


def _own_rows(ref, dev):
    rows = ref.shape[0] // N_DEV
    return ref.at[pl.ds(pl.multiple_of(dev * rows, 16), rows), :]


def _my_block(n_blocks):
    def index(i):
        x, y, c = _place()
        return (4 * x + 2 * y + c) * n_blocks + i, 0
    return index


def _gathered(rows, C):
    return jax.ShapeDtypeStruct((N_DEV * rows, C), WIRE_DTYPE)


def _pack_plain(w, l, zero):
    _, rows, C = w.shape
    t = _shard_tile(rows)

    def body(w_ref, z_ref, o_ref):
        o_ref[...] = (w_ref[...] + z_ref[0:1, 0:1]).astype(o_ref.dtype)

    return pl.pallas_call(
        body, name="pack_plain", grid=(rows // t,),
        in_specs=[pl.BlockSpec((None, t, C), lambda i: (l, i, 0)), pl.BlockSpec((8, LANE), lambda i: (0, 0))],
        out_specs=pl.BlockSpec((t, C), _my_block(rows // t)), out_shape=_gathered(rows, C), compiler_params=_cp(("parallel",)),
    )(w, zero)


def _pack_transposed(ws, l, zero):
    rows = ws[0].shape[2]
    C = sum(w.shape[1] for w in ws)
    t = _row_tile(rows, 256)
    n = len(ws)

    def body(*refs):
        z_ref, o_ref = refs[n], refs[n + 1]
        col = 0
        for w_ref in refs[:n]:
            k = w_ref.shape[0]
            o_ref[:, col:col + k] = (w_ref[...] + z_ref[0:1, 0:1]).T.astype(o_ref.dtype)
            col += k

    return pl.pallas_call(
        body, name="pack_transposed", grid=(rows // t,),
        in_specs=[pl.BlockSpec((None, w.shape[1], t), lambda i: (l, 0, i)) for w in ws] + [pl.BlockSpec((8, LANE), lambda i: (0, 0))],
        out_specs=pl.BlockSpec((t, C), _my_block(rows // t)), out_shape=_gathered(rows, C), compiler_params=_cp(("parallel",)),
    )(*ws, zero)


def _pack_transposed_mxu(w, l, zero):
    _, K, rows = w.shape
    tk = _row_tile(K, 512)

    def body(w_ref, z_ref, o_ref, eye):
        @pl.when(pl.program_id(0) == 0)
        def _():
            eye[...] = (lax.broadcasted_iota(jnp.int32, (rows, rows), 0) == lax.broadcasted_iota(jnp.int32, (rows, rows), 1)).astype(eye.dtype)

        x = (w_ref[...] + z_ref[0:1, 0:1]).astype(MXU_DTYPE)
        o_ref[...] = lax.dot_general(eye[...], x, _DN["NT"], preferred_element_type=F32).astype(o_ref.dtype)

    def out_index(j):
        x, y, c = _place()
        return 4 * x + 2 * y + c, j

    return pl.pallas_call(
        body, name="pack_transposed_mxu", grid=(K // tk,),
        in_specs=[pl.BlockSpec((None, tk, rows), lambda j: (l, j, 0)), pl.BlockSpec((8, LANE), lambda j: (0, 0))],
        out_specs=pl.BlockSpec((rows, tk), out_index), out_shape=_gathered(rows, K),
        scratch_shapes=[pltpu.VMEM((rows, rows), MXU_DTYPE)], compiler_params=_cp(("arbitrary",)),
    )(w, zero)


def _hbm(a):
    return pltpu.with_memory_space_constraint(a, pltpu.HBM)


def _hbm_like(arrays):
    return tuple(pltpu.HBM(a.shape, a.dtype) for a in arrays)


def _dev(px, py, pc):
    return 4 * px + 2 * py + pc


def _block_copies(wb_refs, send_sems, recv_sems, slot, block_out, block_in, peer, outgoing):
    K = len(wb_refs)
    return [pltpu.make_async_remote_copy(
        src_ref=_own_rows(wb_ref, block_out), dst_ref=_own_rows(wb_ref, block_out if outgoing else block_in),
        send_sem=send_sems.at[slot * K + k], recv_sem=recv_sems.at[slot * K + k], device_id=peer, device_id_type=MESH)
        for k, wb_ref in enumerate(wb_refs)]


def _ag_stage1(wb_refs, send_sems, recv_sems, outgoing):
    x, y, c = _place()
    me = _dev(x, y, c)
    out = []
    for slot, peer in enumerate(((x, y, 1 - c), (1 - x, y, c), (x, 1 - y, c))):
        out += _block_copies(wb_refs, send_sems, recv_sems, slot, me, _dev(*peer), peer, outgoing)
    return out


def _ag_stage2(wb_refs, send_sems, recv_sems, outgoing):
    x, y, c = _place()
    via = ((1 - x) + c * (2 * x - 1), y + c * (1 - 2 * y))
    to = (x + c * (1 - 2 * x), (1 - y) + c * (2 * y - 1), c)
    out = _block_copies(wb_refs, send_sems, recv_sems, 0, _dev(*via, c), _dev(1 - x, 1 - y, c), to, outgoing)
    for slot, (px, py) in ((1, (1 - x, y)), (2, (x, 1 - y))):
        out += _block_copies(wb_refs, send_sems, recv_sems, slot, _dev(px, py, c), _dev(px, py, 1 - c), (x, y, 1 - c), outgoing)
    return out


def _ag_start(copies, wbs, carry, name):
    K = len(wbs)

    def body(*refs):
        send_sems, recv_sems = refs[K + 1:K + 3]
        for cp in copies(refs[:K], send_sems, recv_sems, True):
            cp.start()

    outs = pl.pallas_call(
        body, name=name,
        out_shape=(pltpu.SemaphoreType.DMA((3 * K,)), pltpu.SemaphoreType.DMA((3 * K,))) + _hbm_like(list(wbs) + [carry]),
        in_specs=(HBM,) * (K + 1), out_specs=(SEM, SEM) + (HBM,) * (K + 1), input_output_aliases={k: 2 + k for k in range(K + 1)},
        compiler_params=pltpu.CompilerParams(has_side_effects=EFFECT),
    )(*[_hbm(a) for a in wbs], _hbm(carry))
    return outs[0], outs[1], list(outs[2:2 + K]), outs[2 + K]


def _ag_wait(copies, wbs, send_sems, recv_sems, after, name):
    K = len(wbs)

    def body(*refs):
        for cp in copies(refs[:K], refs[K], refs[K + 1], False):
            cp.wait_send()
            cp.wait_recv()

    return list(pl.pallas_call(
        body, name=name, out_shape=_hbm_like(wbs),
        in_specs=(HBM,) * K + (SEM, SEM, ANY), out_specs=(HBM,) * K, input_output_aliases={k: k for k in range(K)},
        compiler_params=pltpu.CompilerParams(has_side_effects=EFFECT),
    )(*wbs, send_sems, recv_sems, after))


def _ag_finish(wbs):
    K = len(wbs)

    def body(*refs):
        ins, outs, tok_ref, send_sems, recv_sems = refs[:K], refs[K:2 * K], refs[2 * K], refs[2 * K + 1], refs[2 * K + 2]
        x, y, c = _place()
        sent = []
        for k in range(K):
            cp = pltpu.make_async_remote_copy(
                src_ref=_own_rows(ins[k], _dev(1 - x, 1 - y, c)), dst_ref=_own_rows(outs[k], _dev(1 - x, 1 - y, c)),
                send_sem=send_sems.at[k], recv_sem=recv_sems.at[k], device_id=(x, y, 1 - c), device_id_type=MESH)
            cp.start()
            sent.append(cp)
        for k in range(K):
            theirs = _own_rows(outs[k], _dev(1 - x, 1 - y, 1 - c))
            pltpu.make_async_remote_copy(src_ref=theirs, dst_ref=theirs, send_sem=send_sems.at[k], recv_sem=recv_sems.at[k],
                                         device_id=(x, y, 1 - c), device_id_type=MESH).wait_recv()
        for cp in sent:
            cp.wait_send()
        tok_ref[...] = jnp.zeros_like(tok_ref)

    outs = pl.pallas_call(
        body, name="ag_finish", in_specs=[ANY] * K, out_specs=[ANY] * K + [pl.BlockSpec(memory_space=pltpu.VMEM)],
        out_shape=[jax.ShapeDtypeStruct(a.shape, a.dtype) for a in wbs] + [jax.ShapeDtypeStruct((8, LANE), F32)],
        input_output_aliases={k: k for k in range(K)},
        scratch_shapes=[pltpu.SemaphoreType.DMA((K,))] * 2, compiler_params=_cp(),
    )(*wbs)
    return list(outs[:K]), outs[K]


def _pair_copies(g_refs, land_refs, send_sems, recv_sems):
    x, y, c = _place()
    K = len(g_refs)
    out = []
    for r in range(4):
        px, py = _related(x, y, r)
        for k in range(K):
            out.append(pltpu.make_async_remote_copy(
                src_ref=_own_rows(g_refs[k], 4 * px + 2 * py + (1 - c)), dst_ref=land_refs[k].at[r],
                send_sem=send_sems.at[r * K + k], recv_sem=recv_sems.at[r * K + k], device_id=(x, y, 1 - c), device_id_type=MESH))
    return out


def _shard_tile(rows):
    for t in (512, 736, 256, 128, 64, 32, 16, 8):
        if rows % t == 0:
            return t
    return rows


def _pair_add(grad, landed):
    _, rows, C = landed.shape
    t = _shard_tile(rows)

    def g_index(r, i):
        x, y, c = _place()
        px = jnp.where(r % 2 == 1, 1 - x, x)
        py = jnp.where(r // 2 == 1, 1 - y, y)
        return (4 * px + 2 * py + c) * (rows // t) + i, 0

    def body(a_ref, b_ref, o_ref):
        o_ref[...] = (a_ref[...].astype(F32) + b_ref[...].astype(F32)).astype(o_ref.dtype)

    slot = pl.BlockSpec((None, t, C), lambda r, i: (r, i, 0))
    return pl.pallas_call(
        body, name="pair_add", grid=(4, rows // t), in_specs=[pl.BlockSpec((t, C), g_index), slot], out_specs=slot,
        out_shape=jax.ShapeDtypeStruct((4, rows, C), grad.dtype), compiler_params=_cp(("parallel", "parallel")),
    )(grad, landed)


def _rs_copies(p_refs, land_refs, send_sems, recv_sems):
    x, y, c = _place()
    K = len(p_refs)
    return [pltpu.make_async_remote_copy(src_ref=p_refs[k].at[r], dst_ref=land_refs[k].at[r - 1], send_sem=send_sems.at[(r - 1) * K + k],
                                         recv_sem=recv_sems.at[(r - 1) * K + k], device_id=(*_related(x, y, r), c), device_id_type=MESH)
            for r in (1, 2, 3) for k in range(K)]


def _rs_start(copies, n_slots, srcs, lands, carry, name):
    K = len(srcs)

    def body(*refs):
        for cp in copies(refs[:K], refs[K:2 * K], refs[2 * K + 1], refs[2 * K + 2]):
            cp.start()

    n_thru = 2 * K + 1
    outs = pl.pallas_call(
        body, name=name,
        out_shape=(pltpu.SemaphoreType.DMA((n_slots * K,)), pltpu.SemaphoreType.DMA((n_slots * K,))) + _hbm_like(list(srcs) + list(lands) + [carry]),
        in_specs=(HBM,) * n_thru, out_specs=(SEM, SEM) + (HBM,) * n_thru, input_output_aliases={k: 2 + k for k in range(n_thru)},
        compiler_params=pltpu.CompilerParams(has_side_effects=EFFECT),
    )(*[_hbm(a) for a in list(srcs) + list(lands) + [carry]])
    return outs[0], outs[1], list(outs[2:2 + K]), list(outs[2 + K:2 + 2 * K]), outs[2 + 2 * K]


def _rs_wait(copies, srcs, lands, send_sems, recv_sems, after, name):
    K = len(srcs)

    def body(*refs):
        for cp in copies(refs[:K], refs[K:2 * K], refs[2 * K], refs[2 * K + 1]):
            cp.wait_send()
            cp.wait_recv()

    outs = pl.pallas_call(
        body, name=name, out_shape=_hbm_like(list(srcs) + list(lands)),
        in_specs=(HBM,) * (2 * K) + (SEM, SEM) + (ANY,) * len(after), out_specs=(HBM,) * (2 * K),
        input_output_aliases={k: k for k in range(2 * K)}, compiler_params=pltpu.CompilerParams(has_side_effects=EFFECT),
    )(*srcs, *lands, send_sems, recv_sems, *after)
    return list(outs[:K]), list(outs[K:])


def _all_gather_small(v, reduce):
    M, N = v.shape

    def body(x_ref, out_ref, sum_ref, send_sems, recv_sems, local_sem):
        x, y, c = _place()
        me, sibling = (x, y, c), (x, y, 1 - c)
        chips = [_related(x, y, r) for r in (1, 2, 3)]

        def rows(px, py, pc):
            return out_ref.at[pl.ds(pl.multiple_of((4 * px + 2 * py + pc) * M, 8), M), :]

        def copy(k, block, to, src=None):
            return pltpu.make_async_remote_copy(src_ref=rows(*block) if src is None else src, dst_ref=rows(*block),
                                                send_sem=send_sems.at[k], recv_sem=recv_sems.at[k], device_id=to, device_id_type=MESH)

        mine = pltpu.make_async_copy(x_ref, rows(*me), local_sem)
        mine.start()
        first = [copy(0, me, sibling, src=x_ref)]
        first += [copy(1 + j, me, (*chip, c), src=x_ref) for j, chip in enumerate(chips)]
        for cp in first:
            cp.start()
        passed = [copy(4 + j, (*chip, c), sibling) for j, chip in enumerate(chips)]
        for j, chip in enumerate(chips):
            copy(1 + j, (*chip, c), me).wait_recv()
            passed[j].start()
        copy(0, sibling, me).wait_recv()
        for j, chip in enumerate(chips):
            copy(4 + j, (*chip, 1 - c), me).wait_recv()
        for cp in first + passed:
            cp.wait_send()
        mine.wait()
        if reduce:
            tot = out_ref[pl.ds(0, M), :]
            for p in range(1, N_DEV):
                tot = tot + out_ref[pl.ds(p * M, M), :]
            sum_ref[...] = tot
        else:
            sum_ref[...] = jnp.zeros_like(sum_ref)

    vm = pl.BlockSpec(memory_space=pltpu.VMEM)
    second = jax.ShapeDtypeStruct((M, N) if reduce else (8, LANE), F32)
    outs = pl.pallas_call(
        body, name="all_reduce_small" if reduce else "all_gather_small", in_specs=[vm], out_specs=[vm, vm],
        out_shape=[jax.ShapeDtypeStruct((N_DEV * M, N), v.dtype), second],
        scratch_shapes=[pltpu.SemaphoreType.DMA((7,)), pltpu.SemaphoreType.DMA((7,)), pltpu.SemaphoreType.DMA],
        compiler_params=_cp(),
    )(v)
    return outs[1] if reduce else outs


def _adamw(w, g, m, v):
    shape = w.shape
    cols = shape[-1]
    rows = w.size // cols
    tr = _row_tile(rows, 256) if rows % 8 == 0 else rows
    c1 = 1.0 / (1.0 - ADAM_B1 ** ADAM_STEP)
    c2 = 1.0 / (1.0 - ADAM_B2 ** ADAM_STEP)

    def body(w_ref, g_ref, m_ref, v_ref, d_ref, nm_ref, nv_ref):
        gv = g_ref[...]
        nm = ADAM_B1 * m_ref[...] + (1.0 - ADAM_B1) * gv
        nv = ADAM_B2 * v_ref[...] + (1.0 - ADAM_B2) * (gv * gv)
        d_ref[...] = -ADAM_LR * ((nm * c1) / (jnp.sqrt(nv * c2) + ADAM_EPS) + ADAM_WD * w_ref[...])
        nm_ref[...] = nm
        nv_ref[...] = nv

    blk = pl.BlockSpec((tr, cols), lambda i: (i, 0))
    outs = pl.pallas_call(
        body, name="adamw", grid=(rows // tr,), in_specs=[blk] * 4, out_specs=[blk] * 3,
        out_shape=[jax.ShapeDtypeStruct((rows, cols), F32)] * 3, compiler_params=_cp(("parallel",)),
    )(*[a.reshape(rows, cols) for a in (w, g, m, v)])
    return tuple(o.reshape(shape) for o in outs)


def _transpose_exact(x):
    t = x.shape[1]
    eye = (lax.broadcasted_iota(jnp.int32, (t, t), 0) == lax.broadcasted_iota(jnp.int32, (t, t), 1)).astype(MXU_DTYPE)
    if MXU_DTYPE == F32:
        return lax.dot_general(eye, x, _DN["NT"], preferred_element_type=F32)
    out = None
    for _ in range(3):
        part = x.astype(MXU_DTYPE)
        x = x - part.astype(F32)
        term = lax.dot_general(eye, part, _DN["NT"], preferred_element_type=F32)
        out = term if out is None else out + term
    return out


def _adamw_layer(w, pair_sums, landed, m, v, l, prev, col_off=None):
    L, A, B = w.shape
    ta = _row_tile(A, 256) if A % 8 == 0 else A
    c1 = 1.0 / (1.0 - ADAM_B1 ** ADAM_STEP)
    c2 = 1.0 / (1.0 - ADAM_B2 ** ADAM_STEP)
    n_prev = 0 if prev is None else 4
    if col_off is None:
        g_specs = [pl.BlockSpec((None, ta, B), functools.partial(lambda i, s: (s, i, 0), s=s)) for s in (0, 0, 1, 2)]
    else:
        assert col_off % ta == 0 and pair_sums.shape[1] == B
        g_specs = [pl.BlockSpec((None, B, ta), functools.partial(lambda i, s: (s, 0, col_off // ta + i), s=s)) for s in (0, 0, 1, 2)]

    def body(*refs):
        w_ref, m_ref, v_ref, p_ref, l1_ref, l2_ref, l3_ref = refs[:7]
        go_ref, d_ref, nm_ref, nv_ref = refs[7 + n_prev:]
        gv = ((p_ref[...].astype(F32) + l1_ref[...].astype(F32)) + l2_ref[...].astype(F32)) + l3_ref[...].astype(F32)
        if col_off is not None:
            gv = _transpose_exact(gv)
        nm = ADAM_B1 * m_ref[...] + (1.0 - ADAM_B1) * gv
        nv = ADAM_B2 * v_ref[...] + (1.0 - ADAM_B2) * (gv * gv)
        d_ref[...] = -ADAM_LR * ((nm * c1) / (jnp.sqrt(nv * c2) + ADAM_EPS) + ADAM_WD * w_ref[...])
        go_ref[...] = gv
        nm_ref[...] = nm
        nv_ref[...] = nv

    lay = pl.BlockSpec((None, ta, B), lambda i: (l, i, 0))
    return pl.pallas_call(
        body, name="adamw_layer", grid=(A // ta,),
        in_specs=[lay, lay, lay] + g_specs + [ANY] * n_prev, out_specs=[lay] * 4,
        out_shape=[jax.ShapeDtypeStruct((L, A, B), F32)] * 4, input_output_aliases={7 + j: j for j in range(n_prev)},
        compiler_params=_cp(("parallel",)),
    )(w, m, v, pair_sums, landed, landed, landed, *(prev or ()))


def _relu2(acc):
    r = jnp.maximum(acc, 0.0)
    return acc, r * r


def _relu2_bwd(acc, up):
    return (acc * (2.0 * jnp.maximum(up.astype(F32), 0.0)),)


def kernel(x, ln_mix_pre, ln_mix_post, ln_mlp_pre, ln_mlp_post, w_in, conv_a_w, proj_a, proj_b, conv_c_w, conv_c_b, norm_c_g, norm_c_b, proj_c, w_o, w_up, w_down, loss_target, m_ln_mix_pre, m_ln_mix_post, m_ln_mlp_pre, m_ln_mlp_post, m_w_in, m_conv_a_w, m_proj_a, m_proj_b, m_conv_c_w, m_conv_c_b, m_norm_c_g, m_norm_c_b, m_proj_c, m_w_o, m_w_up, m_w_down, v_ln_mix_pre, v_ln_mix_post, v_ln_mlp_pre, v_ln_mlp_post, v_w_in, v_conv_a_w, v_proj_a, v_proj_b, v_conv_c_w, v_conv_c_b, v_norm_c_g, v_norm_c_b, v_proj_c, v_w_o, v_w_up, v_w_down):
    L, D, n_in_loc = w_in.shape
    S = x.shape[1]
    U = D // 4
    N_IN = n_in_loc * N_DEV
    D_FF = w_up.shape[2] * N_DEV
    assert N_IN == 23 * U and x.shape[0] == 1
    x_i, y_i, c_i = _place()
    me = 4 * x_i + 2 * y_i + c_i

    def pack(l, which, zero):
        kinds = (lambda: _pack_transposed([proj_b, proj_a, proj_c], l, zero), lambda: _pack_plain(w_o, l, zero),
                 lambda: _pack_transposed([w_up], l, zero), lambda: _pack_plain(w_down, l, zero),
                 lambda: _pack_transposed_mxu(w_in, l, zero))
        return [kinds[k]() for k in which]

    def stage1(wbs, carry, tag):
        return _ag_start(_ag_stage1, wbs, carry, f"ag_s1_{tag}")

    def stage2(started, after, carry, tag):
        send_sems, recv_sems, wbs, _ = started
        wbs = _ag_wait(_ag_stage1, wbs, send_sems, recv_sems, after, f"ag_s1_wait_{tag}")
        return _ag_start(_ag_stage2, wbs, carry, f"ag_s2_{tag}")

    def gather_end(started, after, tag):
        send_sems, recv_sems, wbs, _ = started
        wbs, zero = _ag_finish(_ag_wait(_ag_stage2, wbs, send_sems, recv_sems, after, f"ag_s2_wait_{tag}"))
        return [w.astype(MXU_DTYPE) for w in wbs], zero

    cu = U // N_DEV
    conv_loc = jnp.concatenate([conv_a_w, conv_c_w], axis=1).reshape(L * (SC_WIDTH + CF_WIDTH), cu)
    conv_all, tok = _all_gather_small(conv_loc, False)
    conv_all = conv_all.reshape(N_DEV, L, SC_WIDTH + CF_WIDTH, cu).transpose(1, 2, 0, 3).reshape(L, SC_WIDTH + CF_WIDTH, U)
    wa_full, wc_full = conv_all[:, :SC_WIDTH], conv_all[:, SC_WIDTH:]
    every = (0, 1, 2, 3, 4)
    in_s1 = stage1(pack(0, (4,), tok), jnp.zeros((8, LANE), F32), "0_in")
    rest_packed, next_packed = pack(0, (0, 1, 2, 3), tok), pack(1, every, tok)
    in_s2 = stage2(in_s1, next_packed[4], in_s1[3], "0_in")
    rest_s1 = stage1(rest_packed, in_s2[3], "0_rest")
    wb = []

    def vec(p, l):
        return p[l][None, :]

    xs = x[0]
    saved = []
    h1 = _rms_fwd(xs, vec(ln_mix_pre, 0), "rms_fwd")
    for l in range(L):
        if l == 0:
            (w_in_t,), _ = gather_end(in_s2, rest_s1[3], "0_in")
        else:
            w_p, w_o_l, w_up_t, w_dn, w_in_t = w_next
            if l + 1 < L:
                next_s1 = stage1(pack(l + 1, every, tok), h1, l + 1)
                h1 = next_s1[3]
        proj = _mm(h1, w_in_t, "NT", (F32,), "mm_proj")
        a_out = _branch_a_fwd(proj, wa_full[l], U)
        u1 = _branch_c_conv_fwd(proj, wc_full[l], vec(conv_c_b, l), U)
        u3 = _branch_c_norm_fwd(u1, vec(norm_c_g, l), vec(norm_c_b, l))
        if l == 0:
            rest_s2 = stage2(rest_s1, u3, proj, "0_rest")
            next_s1 = stage1(next_packed, rest_s2[3], 1)
            proj = next_s1[3]
        att, att_f32 = _attn_fwd(proj, U)
        if l == 0:
            (w_p, w_o_l, w_up_t, w_dn), tok = gather_end(rest_s2, att, "0_rest")
        wb.append((w_p, w_o_l, w_up_t, w_dn, w_in_t))
        merged, ya, yb, yc = _merge_fwd(proj, a_out, att, u3, w_p)
        mixed = _mm(merged, w_o_l, "NN", (F32,), "mm_mixed")
        x1, h2 = _resid_post(xs, mixed, vec(ln_mix_post, l), vec(ln_mlp_pre, l), "resid_post_mix")
        if 0 < l < L - 1:
            next_s2 = stage2(next_s1, mixed, h2, l + 1)
            h2 = next_s2[3]
        up, act = _mm(h2, w_up_t, "NT", (MXU_DTYPE, MXU_DTYPE), "mm_up", epilogue=_relu2)
        if l == 0 and L > 1:
            next_s2 = stage2(next_s1, up, act, 1)
            act = next_s2[3]
        f = _mm(act, w_dn, "NN", (F32,), "mm_down", tm=1024, tn=1024, tk=4096)
        saved.append((xs, h1, proj, a_out, u1, u3, att, att_f32, ya, yb, yc, merged, mixed, x1, h2, up, act, f))
        if l + 1 < L:
            xs, h1 = _resid_post(x1, f, vec(ln_mlp_post, l), vec(ln_mix_pre, l + 1), "resid_post_mlp")
            w_next, tok = gather_end(next_s2, h1, l + 1)
        else:
            xs, _ = _resid_post(x1, f, vec(ln_mlp_post, l), None, "resid_post_last")
    dxo, loss_row = _loss_head(xs, loss_target[0])
    loss = lax.psum(loss_row[0, 0], ("x", "y", "c"))

    small = {k: [None] * L for k in ("g1", "g2", "g3", "g4", "cb", "ng", "nb", "wa", "wc")}
    def pair_start(grads, carry, name):
        lands = [lax.empty((4, g.shape[0] // N_DEV, D), WIRE_DTYPE) for g in grads]
        send_sems, recv_sems, grads, lands, carry = _rs_start(_pair_copies, 4, grads, lands, carry, name)
        return (grads, lands, send_sems, recv_sems), carry

    def chips_start(pair_flight, after, carry, name):
        grads, landed = _rs_wait(_pair_copies, *pair_flight, after, name + "_pair_wait")
        pair_sums = [_pair_add(g, ld) for g, ld in zip(grads, landed)]
        lands = [lax.empty((3,) + p.shape[1:], WIRE_DTYPE) for p in pair_sums]
        send_sems, recv_sems, pair_sums, lands, carry = _rs_start(_rs_copies, 3, pair_sums, lands, carry, name)
        return (pair_sums, lands, send_sems, recv_sems), carry

    in_flight = []
    mix_pairs = None
    for l in reversed(range(L)):
        w_p, w_o_l, w_up_t, w_dn, w_in_t = wb[l]
        xs, h1, proj, a_out, u1, u3, att, att_f32, ya, yb, yc, merged, mixed, x1, h2, up, act, f = saved[l]
        df, small["g4"][l] = _rms_bwd(f, vec(ln_mlp_post, l), dxo, None, MXU_DTYPE, "rms_bwd_post_mlp")
        d_up = _mm(df, w_dn, "NT", (MXU_DTYPE,), "mm_d_up", epilogue=_relu2_bwd, extras=(up,))
        g_dn = _mm(act, df, "TN", (WIRE_DTYPE,), "mm_g_down", tm=512, tn=2048)
        if mix_pairs is not None:
            flight, d_up = chips_start(mix_pairs, [g_dn], d_up, f"rs_start_mix_{l + 1}")
            in_flight.append((l + 1, ("p", "o", "in"), flight))
        dh2 = _mm(d_up, w_up_t, "NN", (F32,), "mm_dh2", tm=1024, tn=1024, tk=4096)
        g_up = _mm(d_up, h2, "TN", (WIRE_DTYPE,), "mm_g_up", tm=512, tn=2048)
        mlp_pairs, dh2 = pair_start([g_up, g_dn], dh2, f"rs_pair_mlp_{l}")
        dx1, small["g3"][l] = _rms_bwd(x1, vec(ln_mlp_pre, l), dh2, dxo, F32, "rms_bwd_pre_mlp")
        dmixed, small["g2"][l] = _rms_bwd(mixed, vec(ln_mix_post, l), dx1, None, MXU_DTYPE, "rms_bwd_post_mix")
        dmerged = _mm(dmixed, w_o_l, "NT", (F32,), "mm_dmerged")
        g_o = _mm(merged, dmixed, "TN", (WIRE_DTYPE,), "mm_g_o", tm=512, tn=2048)
        flight, dmerged = chips_start(mlp_pairs, [g_o], dmerged, f"rs_start_mlp_{l}")
        in_flight.append((l, ("up", "dn"), flight))
        dya, dyb, dyc, dgla, dglb, dglc = _gate_bwd(dmerged, proj, ya, yb, yc)
        d_att = _mm(dyb, w_p, "NN", (F32,), "mm_d_att", b_view=(0, 0, D, 2 * U))
        d_a_out = _mm(dya, w_p, "NN", (F32,), "mm_d_a_out", b_view=(0, 2 * U, D, U))
        d_u3 = _mm(dyc, w_p, "NN", (F32,), "mm_d_u3", b_view=(0, 3 * U, D, U))
        g_pb = _mm(dyb, att, "TN", (WIRE_DTYPE,), "mm_g_pb", tm=512, tn=2048)
        g_pa = _mm(dya, a_out, "TN", (WIRE_DTYPE,), "mm_g_pa", tm=512, tn=2048)
        g_pc = _mm(dyc, u3, "TN", (WIRE_DTYPE,), "mm_g_pc", tm=512, tn=2048)
        d_scb, d_scc, d_scu, small["wa"][l] = _branch_a_bwd(d_a_out, proj, wa_full[l], U)
        d_u1, small["ng"][l], small["nb"][l] = _branch_c_norm_bwd(d_u3, u1, vec(norm_c_g, l), vec(norm_c_b, l))
        d_cfa, d_cfg, small["wc"][l], small["cb"][l] = _branch_c_conv_bwd(d_u1, proj, wc_full[l], U)
        dq, dk, dv = _attn_bwd(proj, att_f32, d_att, U)
        dproj = jnp.concatenate([dq, dk, dv, d_scb, d_scc, d_scu, d_cfa, d_cfg, dgla, dglb, dglc], axis=1)
        dh1 = _mm(dproj, w_in_t, "NN", (F32,), "mm_dh1", tm=1024, tk=46 * LANE)
        g_in = _mm(dproj, h1, "TN", (WIRE_DTYPE,), "mm_g_in", tm=512, tn=2048)
        dxo, small["g1"][l] = _rms_bwd(xs, vec(ln_mix_pre, l), dh1, dx1, F32, "rms_bwd_pre_mix")
        carry = dxo if l > 0 else jnp.zeros((8, LANE), F32)
        mix_pairs, carry = pair_start([jnp.concatenate([g_pb, g_pa, g_pc], axis=1), g_o, g_in], carry, f"rs_pair_mix_{l}")
        if l > 0:
            dxo = carry
    grad_x = dxo[None]
    flight, carry = chips_start(mix_pairs, [carry], jnp.zeros((8, LANE), F32), "rs_start_mix_0")
    in_flight.append((0, ("p", "o", "in"), flight))
    big = {"w_in": (w_in, m_w_in, v_w_in), "proj_a": (proj_a, m_proj_a, v_proj_a), "proj_b": (proj_b, m_proj_b, v_proj_b),
           "proj_c": (proj_c, m_proj_c, v_proj_c), "w_o": (w_o, m_w_o, v_w_o), "w_up": (w_up, m_w_up, v_w_up),
           "w_down": (w_down, m_w_down, v_w_down)}
    done = {k: None for k in big}
    for n, (l, keys, flight) in enumerate(in_flight):
        after = [carry]
        if n == len(in_flight) - 1:
            after += [done[k][3] for k in big if done[k] is not None]
        pair_sums, lands = _rs_wait(_rs_copies, *flight, after, f"rs_wait_{keys[0]}_{l}")
        g = dict(zip(keys, zip(pair_sums, lands)))
        if "up" in g:
            layer_grads = {"w_up": (g["up"], 0), "w_down": (g["dn"], None)}
        else:
            layer_grads = {"w_in": (g["in"], 0), "proj_b": (g["p"], 0), "proj_a": (g["p"], 2 * U), "proj_c": (g["p"], 3 * U),
                           "w_o": (g["o"], None)}
        for k, ((p, ld), col_off) in layer_grads.items():
            w, m, v = big[k]
            done[k] = _adamw_layer(w, p, ld, m, v, l, done[k], col_off)

    order = ("g1", "g2", "g3", "g4", "cb", "ng", "nb", "wa", "wc")
    parts = [jnp.stack(small[k]).reshape(-1) for k in order]
    flat = jnp.concatenate(parts)
    n_flat = flat.shape[0]
    pad = (-n_flat) % (8 * LANE)
    flat = jnp.pad(flat, (0, pad)).reshape(-1, LANE)
    tot = _all_gather_small(flat, True).reshape(-1)[:n_flat]
    red, pos = {}, 0
    for k, p in zip(order, parts):
        red[k] = tot[pos:pos + p.shape[0]]
        pos += p.shape[0]
    g_ln_mix_pre, g_ln_mix_post = red["g1"].reshape(L, D), red["g2"].reshape(L, D)
    g_ln_mlp_pre, g_ln_mlp_post = red["g3"].reshape(L, D), red["g4"].reshape(L, D)
    g_conv_c_b, g_norm_c_g, g_norm_c_b = red["cb"].reshape(L, U), red["ng"].reshape(L, U), red["nb"].reshape(L, U)
    g_conv_a_w = lax.dynamic_slice_in_dim(red["wa"].reshape(L, SC_WIDTH, U), me * cu, cu, axis=2)
    g_conv_c_w = lax.dynamic_slice_in_dim(red["wc"].reshape(L, CF_WIDTH, U), me * cu, cu, axis=2)

    small_w = {"ln_mix_pre": (ln_mix_pre, g_ln_mix_pre, m_ln_mix_pre, v_ln_mix_pre),
               "ln_mix_post": (ln_mix_post, g_ln_mix_post, m_ln_mix_post, v_ln_mix_post),
               "ln_mlp_pre": (ln_mlp_pre, g_ln_mlp_pre, m_ln_mlp_pre, v_ln_mlp_pre),
               "ln_mlp_post": (ln_mlp_post, g_ln_mlp_post, m_ln_mlp_post, v_ln_mlp_post),
               "conv_a_w": (conv_a_w, g_conv_a_w, m_conv_a_w, v_conv_a_w), "conv_c_w": (conv_c_w, g_conv_c_w, m_conv_c_w, v_conv_c_w),
               "conv_c_b": (conv_c_b, g_conv_c_b, m_conv_c_b, v_conv_c_b), "norm_c_g": (norm_c_g, g_norm_c_g, m_norm_c_g, v_norm_c_g),
               "norm_c_b": (norm_c_b, g_norm_c_b, m_norm_c_b, v_norm_c_b)}
    for k, (w, g, m, v) in small_w.items():
        done[k] = (g,) + _adamw(w, g, m, v)
    names = ("ln_mix_pre", "ln_mix_post", "ln_mlp_pre", "ln_mlp_post", "w_in", "conv_a_w", "proj_a", "proj_b", "conv_c_w", "conv_c_b",
             "norm_c_g", "norm_c_b", "proj_c", "w_o", "w_up", "w_down")
    return (loss, grad_x, *[done[k][0] for k in names], *[done[k][1] for k in names], *[done[k][2] for k in names],
            *[done[k][3] for k in names])
```

```python
import functools

import jax
import jax.numpy as jnp
from jax import lax
from jax.experimental import pallas as pl
from jax.experimental.pallas import tpu as pltpu

F32 = jnp.float32
MXU_DTYPE = jnp.bfloat16
WIRE_DTYPE = jnp.bfloat16
MESH = pl.DeviceIdType.MESH
ANY = pl.BlockSpec(memory_space=pl.ANY)
HBM = pl.BlockSpec(memory_space=pltpu.HBM)
SEM = pl.BlockSpec(memory_space=pltpu.SEMAPHORE)
EFFECT = pltpu.SideEffectType.DATAFLOW_SIDE_EFFECTING

N_DEV = 8
HEAD_DIM = 128
RMS_EPS = 1e-6
LN_EPS = 1e-5
SC_WIDTH = 3
CF_WIDTH = 31
CONV_PAD = 32
ADAM_LR, ADAM_B1, ADAM_B2, ADAM_EPS, ADAM_WD, ADAM_STEP = 0.001, 0.9, 0.999, 1e-08, 0.01, 10
VMEM_LIMIT = 56 * 1024 * 1024
LANE = 128


def _cp(sem=None, **kw):
    return pltpu.CompilerParams(dimension_semantics=sem, vmem_limit_bytes=VMEM_LIMIT, **kw)


def _sigmoid(x):
    return 1.0 / (1.0 + jnp.exp(-x))


def _row_tile(rows, want):
    t = min(rows, want)
    while rows % t:
        t //= 2
    return t


_DN = {"NN": (((1,), (0,)), ((), ())), "NT": (((1,), (1,)), ((), ())), "TN": (((0,), (0,)), ((), ()))}


def _mm(a, b, mode, out_dtypes, name, *, a_view=None, b_view=None, tm=2048, tn=512, tk=2048, epilogue=None, extras=()):
    a_view = a_view or (0, 0) + tuple(a.shape)
    b_view = b_view or (0, 0) + tuple(b.shape)
    ar, ac, an, am = a_view
    br, bc, bn, bm = b_view
    if mode == "NN":
        M, K, K2, N = an, am, bn, bm
    elif mode == "NT":
        M, K, N, K2 = an, am, bn, bm
    else:
        K, M, K2, N = an, am, bn, bm
    assert K == K2, (name, a_view, b_view)
    tm, tn, tk = _row_tile(M, tm), _row_tile(N, tn), _row_tile(K, tk)
    (a_m_off, a_k_off) = (ac, ar) if mode == "TN" else (ar, ac)
    (b_n_off, b_k_off) = (br, bc) if mode == "NT" else (bc, br)
    while a_m_off % tm:
        tm //= 2
    while b_n_off % tn:
        tn //= 2
    while a_k_off % tk or b_k_off % tk:
        tk //= 2
    nk = K // tk
    a_blk = (tk, tm) if mode == "TN" else (tm, tk)
    b_blk = (tn, tk) if mode == "NT" else (tk, tn)
    assert ar % a_blk[0] == 0 and ac % a_blk[1] == 0, (name, a_view, a_blk)
    assert br % b_blk[0] == 0 and bc % b_blk[1] == 0, (name, b_view, b_blk)
    ao, bo = (ar // a_blk[0], ac // a_blk[1]), (br // b_blk[0], bc // b_blk[1])
    if mode == "TN":
        a_spec = pl.BlockSpec(a_blk, lambda i, j, k: (ao[0] + k, ao[1] + i))
    else:
        a_spec = pl.BlockSpec(a_blk, lambda i, j, k: (ao[0] + i, ao[1] + k))
    if mode == "NT":
        b_spec = pl.BlockSpec(b_blk, lambda i, j, k: (bo[0] + j, bo[1] + k))
    else:
        b_spec = pl.BlockSpec(b_blk, lambda i, j, k: (bo[0] + k, bo[1] + j))
    o_spec = pl.BlockSpec((tm, tn), lambda i, j, k: (i, j))
    n_ex, n_out = len(extras), len(out_dtypes)
    dn = _DN[mode]

    def body(*refs):
        a_ref, b_ref = refs[:2]
        ex_refs = refs[2:2 + n_ex]
        o_refs = refs[2 + n_ex:2 + n_ex + n_out]
        p = lax.dot_general(a_ref[...], b_ref[...], dn, preferred_element_type=F32)

        def finish(acc):
            outs = epilogue(acc, *[r[...] for r in ex_refs]) if epilogue else (acc,)
            for o_ref, o in zip(o_refs, outs):
                o_ref[...] = o.astype(o_ref.dtype)

        if nk == 1:
            finish(p)
        else:
            acc_ref = refs[-1]
            k = pl.program_id(2)

            @pl.when(k == 0)
            def _():
                acc_ref[...] = p

            @pl.when(k > 0)
            def _():
                acc_ref[...] += p

            @pl.when(k == nk - 1)
            def _():
                finish(acc_ref[...])

    outs = pl.pallas_call(
        body, name=name, grid=(M // tm, N // tn, nk),
        in_specs=[a_spec, b_spec] + [o_spec] * n_ex, out_specs=[o_spec] * n_out,
        out_shape=[jax.ShapeDtypeStruct((M, N), d) for d in out_dtypes],
        scratch_shapes=[pltpu.VMEM((tm, tn), F32)] if nk > 1 else [],
        compiler_params=_cp(("parallel", "parallel", "arbitrary")),
    )(a, b, *extras)
    return outs[0] if n_out == 1 else outs


def _rms_fwd(x, g, name):
    S, D = x.shape
    tr = _row_tile(S, 256)

    def body(x_ref, g_ref, h_ref):
        xv = x_ref[...]
        r = lax.rsqrt(jnp.mean(xv * xv, axis=-1, keepdims=True) + RMS_EPS)
        h_ref[...] = ((xv * r) * g_ref[...]).astype(h_ref.dtype)

    return pl.pallas_call(
        body, name=name, grid=(S // tr,),
        in_specs=[pl.BlockSpec((tr, D), lambda i: (i, 0)), pl.BlockSpec((1, D), lambda i: (0, 0))],
        out_specs=pl.BlockSpec((tr, D), lambda i: (i, 0)),
        out_shape=jax.ShapeDtypeStruct((S, D), MXU_DTYPE), compiler_params=_cp(("parallel",)),
    )(x, g)


def _resid_post(xres, y, g_post, g_next, name):
    S, D = y.shape
    tr = _row_tile(S, 256)
    has_next = g_next is not None

    def body(*refs):
        xr_ref, y_ref, gp_ref = refs[:3]
        yv = y_ref[...]
        r = lax.rsqrt(jnp.mean(yv * yv, axis=-1, keepdims=True) + RMS_EPS)
        xn = xr_ref[...] + (yv * r) * gp_ref[...]
        if has_next:
            gn_ref, xo_ref, h_ref = refs[3:]
            r2 = lax.rsqrt(jnp.mean(xn * xn, axis=-1, keepdims=True) + RMS_EPS)
            h_ref[...] = ((xn * r2) * gn_ref[...]).astype(h_ref.dtype)
        else:
            xo_ref = refs[3]
        xo_ref[...] = xn

    row = pl.BlockSpec((tr, D), lambda i: (i, 0))
    vec = pl.BlockSpec((1, D), lambda i: (0, 0))
    outs = pl.pallas_call(
        body, name=name, grid=(S // tr,),
        in_specs=[row, row, vec] + ([vec] if has_next else []),
        out_specs=[row] + ([row] if has_next else []),
        out_shape=[jax.ShapeDtypeStruct((S, D), F32)] + ([jax.ShapeDtypeStruct((S, D), MXU_DTYPE)] if has_next else []),
        compiler_params=_cp(("parallel",)),
    )(xres, y, g_post, *([g_next] if has_next else []))
    return (outs[0], outs[1]) if has_next else (outs[0], None)


def _rms_bwd(xin, g, dy, dres, out_dtype, name):
    S, D = xin.shape
    tr = _row_tile(S, 256)
    has_res = dres is not None

    def body(*refs):
        x_ref, g_ref, dy_ref = refs[:3]
        dx_ref, dg_ref = refs[-2:]
        xv, dyv = x_ref[...], dy_ref[...].astype(F32)
        r = lax.rsqrt(jnp.mean(xv * xv, axis=-1, keepdims=True) + RMS_EPS)
        n = xv * r
        dyg = dyv * g_ref[...]
        dx = r * (dyg - n * jnp.mean(dyg * n, axis=-1, keepdims=True))
        if has_res:
            dx = dx + refs[3][...]
        dx_ref[...] = dx.astype(dx_ref.dtype)

        @pl.when(pl.program_id(0) == 0)
        def _():
            dg_ref[...] = jnp.zeros_like(dg_ref)

        dg_ref[...] += jnp.sum(dyv * n, axis=0, keepdims=True)

    row = pl.BlockSpec((tr, D), lambda i: (i, 0))
    vec = pl.BlockSpec((1, D), lambda i: (0, 0))
    return pl.pallas_call(
        body, name=name, grid=(S // tr,),
        in_specs=[row, vec, row] + ([row] if has_res else []), out_specs=[row, vec],
        out_shape=[jax.ShapeDtypeStruct((S, D), out_dtype), jax.ShapeDtypeStruct((1, D), F32)],
        compiler_params=_cp(("arbitrary",)),
    )(xin, g, dy, *([dres] if has_res else []))


def _loss_head(y, target):
    S, D = y.shape
    tr = _row_tile(S, 256)

    def body(y_ref, t_ref, dy_ref, l_ref):
        e = y_ref[...] - t_ref[...]
        dy_ref[...] = e * (1.0 / D)

        @pl.when(pl.program_id(0) == 0)
        def _():
            l_ref[...] = jnp.zeros_like(l_ref)

        l_ref[...] += 0.5 * jnp.sum(jnp.mean(e * e, axis=-1, keepdims=True), axis=0, keepdims=True)

    row = pl.BlockSpec((tr, D), lambda i: (i, 0))
    return pl.pallas_call(
        body, name="loss_head", grid=(S // tr,), in_specs=[row, row],
        out_specs=[row, pl.BlockSpec((1, LANE), lambda i: (0, 0))],
        out_shape=[jax.ShapeDtypeStruct((S, D), F32), jax.ShapeDtypeStruct((1, LANE), F32)],
        compiler_params=_cp(("arbitrary",)),
    )(y, target)


def _gate_specs(S, U, tr):
    gl = [pl.BlockSpec((tr, U), functools.partial(lambda i, j, o: (i, o + j), o=o)) for o in (11, 15, 19)]
    return gl, pl.BlockSpec((tr, U), lambda i, j: (i, j))


def _merge_fwd(proj, a_out, att, u3, w_p):
    S, U = a_out.shape
    D = 4 * U
    tm = _row_tile(S, 1024)

    def body(ga_ref, gb_ref, gc_ref, a_ref, b_ref, c_ref, w_ref, m_ref, ya_ref, yb_ref, yc_ref):
        yb = lax.dot_general(b_ref[...], w_ref[:, 0:2 * U], _DN["NT"], preferred_element_type=F32)
        ya = lax.dot_general(a_ref[...], w_ref[:, 2 * U:3 * U], _DN["NT"], preferred_element_type=F32)
        yc = lax.dot_general(c_ref[...], w_ref[:, 3 * U:4 * U], _DN["NT"], preferred_element_type=F32)
        m_ref[...] = (_sigmoid(ga_ref[...]) * ya + _sigmoid(gb_ref[...]) * yb + _sigmoid(gc_ref[...]) * yc).astype(m_ref.dtype)
        ya_ref[...] = ya.astype(ya_ref.dtype)
        yb_ref[...] = yb.astype(yb_ref.dtype)
        yc_ref[...] = yc.astype(yc_ref.dtype)

    gl, blk = _gate_specs(S, U, tm)
    rows = lambda k: pl.BlockSpec((tm, k), lambda i, j: (i, 0))
    return pl.pallas_call(
        body, name="merge_fwd", grid=(S // tm, 4),
        in_specs=gl + [rows(U), rows(2 * U), rows(U), pl.BlockSpec((U, D), lambda i, j: (j, 0))], out_specs=[blk] * 4,
        out_shape=[jax.ShapeDtypeStruct((S, D), MXU_DTYPE)] * 4, compiler_params=_cp(("parallel", "parallel")),
    )(proj, proj, proj, a_out, att, u3, w_p)


def _gate_bwd(dm, proj, ya, yb, yc):
    S, D = ya.shape
    U = D // 4
    tr = _row_tile(S, 256)

    def body(dm_ref, ga_ref, gb_ref, gc_ref, ya_ref, yb_ref, yc_ref, da_ref, db_ref, dc_ref, la_ref, lb_ref, lc_ref):
        d = dm_ref[...]
        for g_ref, y_ref, dy_ref, dl_ref in ((ga_ref, ya_ref, da_ref, la_ref), (gb_ref, yb_ref, db_ref, lb_ref),
                                             (gc_ref, yc_ref, dc_ref, lc_ref)):
            g = _sigmoid(g_ref[...])
            dy_ref[...] = (d * g).astype(dy_ref.dtype)
            dl_ref[...] = (d * y_ref[...] * g * (1.0 - g)).astype(dl_ref.dtype)

    gl, blk = _gate_specs(S, U, tr)
    return pl.pallas_call(
        body, name="gate_bwd", grid=(S // tr, 4), in_specs=[blk] + gl + [blk] * 3, out_specs=[blk] * 6,
        out_shape=[jax.ShapeDtypeStruct((S, D), MXU_DTYPE)] * 6, compiler_params=_cp(("parallel", "parallel")),
    )(dm, proj, proj, proj, ya, yb, yc)


def _branch_proj_bwd(dya, dyb, dyc, a_out, att, u3, w_p):
    S, U = a_out.shape
    D = 4 * U
    tm = _row_tile(S, 512)
    tn = _row_tile(D, 512)

    def d_body(da_ref, db_ref, dc_ref, w_ref, oa_ref, ob_ref, oc_ref):
        ob_ref[...] = jnp.dot(db_ref[...], w_ref[:, 0:2 * U], preferred_element_type=F32)
        oa_ref[...] = jnp.dot(da_ref[...], w_ref[:, 2 * U:3 * U], preferred_element_type=F32)
        oc_ref[...] = jnp.dot(dc_ref[...], w_ref[:, 3 * U:4 * U], preferred_element_type=F32)

    row = lambda k: pl.BlockSpec((tm, k), lambda i: (i, 0))
    d_a_out, d_att, d_u3 = pl.pallas_call(
        d_body, name="branch_proj_bwd_act", grid=(S // tm,),
        in_specs=[row(D), row(D), row(D), pl.BlockSpec((D, D), lambda i: (0, 0))], out_specs=[row(U), row(2 * U), row(U)],
        out_shape=[jax.ShapeDtypeStruct((S, U), F32), jax.ShapeDtypeStruct((S, 2 * U), F32), jax.ShapeDtypeStruct((S, U), F32)],
        compiler_params=_cp(("parallel",)),
    )(dya, dyb, dyc, w_p)

    def g_body(da_ref, db_ref, dc_ref, a_ref, b_ref, c_ref, o_ref):
        o_ref[:, 0:2 * U] = lax.dot_general(db_ref[...], b_ref[...], _DN["TN"], preferred_element_type=F32).astype(o_ref.dtype)
        o_ref[:, 2 * U:3 * U] = lax.dot_general(da_ref[...], a_ref[...], _DN["TN"], preferred_element_type=F32).astype(o_ref.dtype)
        o_ref[:, 3 * U:4 * U] = lax.dot_general(dc_ref[...], c_ref[...], _DN["TN"], preferred_element_type=F32).astype(o_ref.dtype)

    col = pl.BlockSpec((S, tn), lambda i: (0, i))
    whole = lambda k: pl.BlockSpec((S, k), lambda i: (0, 0))
    g_p = pl.pallas_call(
        g_body, name="branch_proj_bwd_w", grid=(D // tn,),
        in_specs=[col, col, col, whole(U), whole(2 * U), whole(U)], out_specs=pl.BlockSpec((tn, D), lambda i: (i, 0)),
        out_shape=jax.ShapeDtypeStruct((D, D), WIRE_DTYPE), compiler_params=_cp(("parallel",)),
    )(dya, dyb, dyc, a_out, att, u3)
    return d_a_out, d_att, d_u3, g_p


def _chunks(S):
    r = _row_tile(S, 256)
    return [(r0, r) for r0 in range(0, S, r)]


def _conv_causal(front_ref, w_ref, K, r0, R):
    acc = None
    for j in range(K):
        term = w_ref[pl.ds(K - 1 - j, 1), :] * front_ref[pl.ds(CONV_PAD + r0 - j, R), :]
        acc = term if acc is None else acc + term
    return acc


def _conv_anticausal(back_ref, w_ref, K, r0, R):
    acc = None
    for j in range(K):
        term = w_ref[pl.ds(K - 1 - j, 1), :] * back_ref[pl.ds(r0 + j, R), :]
        acc = term if acc is None else acc + term
    return acc


def _conv_wgrad(front_ref, back_ref, dw_ref, K, S):
    for j in range(K):
        tot = None
        for r0, R in _chunks(S):
            part = jnp.sum(back_ref[pl.ds(r0, R), :] * front_ref[pl.ds(CONV_PAD + r0 - j, R), :], axis=0, keepdims=True)
            tot = part if tot is None else tot + part
        dw_ref[pl.ds(K - 1 - j, 1), :] = tot


def _col(S, cw, unit_off):
    return pl.BlockSpec((S, cw), functools.partial(lambda cb, o: (0, o + cb), o=unit_off))


def _branch_a_fwd(proj, wa, U):
    S = proj.shape[0]
    cw = min(LANE, U)
    nb = U // cw
    K = SC_WIDTH

    def body(b_ref, c_ref, u_ref, w_ref, o_ref, front):
        front[pl.ds(0, CONV_PAD), :] = jnp.zeros((CONV_PAD, cw), F32)
        front[pl.ds(CONV_PAD, S), :] = c_ref[...] * u_ref[...]
        for r0, R in _chunks(S):
            o_ref[pl.ds(r0, R), :] = (b_ref[pl.ds(r0, R), :] * _conv_causal(front, w_ref, K, r0, R)).astype(o_ref.dtype)

    return pl.pallas_call(
        body, name="branch_a_fwd", grid=(nb,),
        in_specs=[_col(S, cw, 6 * nb), _col(S, cw, 7 * nb), _col(S, cw, 8 * nb), pl.BlockSpec((K, cw), lambda cb: (0, cb))],
        out_specs=pl.BlockSpec((S, cw), lambda cb: (0, cb)), out_shape=jax.ShapeDtypeStruct((S, U), MXU_DTYPE),
        scratch_shapes=[pltpu.VMEM((S + CONV_PAD, cw), F32)], compiler_params=_cp(("parallel",)),
    )(proj, proj, proj, wa)


def _branch_a_bwd(d_out, proj, wa, U):
    S = proj.shape[0]
    cw = min(LANE, U)
    nb = U // cw
    K = SC_WIDTH

    def body(d_ref, b_ref, c_ref, u_ref, w_ref, db_ref, dc_ref, du_ref, dw_ref, front, back):
        front[pl.ds(0, CONV_PAD), :] = jnp.zeros((CONV_PAD, cw), F32)
        front[pl.ds(CONV_PAD, S), :] = c_ref[...] * u_ref[...]
        back[pl.ds(S, CONV_PAD), :] = jnp.zeros((CONV_PAD, cw), F32)
        back[pl.ds(0, S), :] = d_ref[...] * b_ref[...]
        for r0, R in _chunks(S):
            rows = pl.ds(r0, R)
            db_ref[rows, :] = (d_ref[rows, :] * _conv_causal(front, w_ref, K, r0, R)).astype(db_ref.dtype)
            d_ai = _conv_anticausal(back, w_ref, K, r0, R)
            dc_ref[rows, :] = (d_ai * u_ref[rows, :]).astype(dc_ref.dtype)
            du_ref[rows, :] = (d_ai * c_ref[rows, :]).astype(du_ref.dtype)
        _conv_wgrad(front, back, dw_ref, K, S)

    blk = pl.BlockSpec((S, cw), lambda cb: (0, cb))
    wblk = pl.BlockSpec((K, cw), lambda cb: (0, cb))
    return pl.pallas_call(
        body, name="branch_a_bwd", grid=(nb,),
        in_specs=[blk, _col(S, cw, 6 * nb), _col(S, cw, 7 * nb), _col(S, cw, 8 * nb), wblk],
        out_specs=[blk, blk, blk, wblk],
        out_shape=[jax.ShapeDtypeStruct((S, U), MXU_DTYPE)] * 3 + [jax.ShapeDtypeStruct((K, U), F32)],
        scratch_shapes=[pltpu.VMEM((S + CONV_PAD, cw), F32)] * 2, compiler_params=_cp(("parallel",)),
    )(d_out, proj, proj, proj, wa)


def _branch_c_conv_fwd(proj, wc, cb, U):
    S = proj.shape[0]
    cw = min(LANE, U)
    nb = U // cw
    K = CF_WIDTH

    def body(a_ref, g_ref, w_ref, bias_ref, o_ref, front):
        front[pl.ds(0, CONV_PAD), :] = jnp.zeros((CONV_PAD, cw), F32)
        front[pl.ds(CONV_PAD, S), :] = a_ref[...] * _sigmoid(g_ref[...])
        for r0, R in _chunks(S):
            o_ref[pl.ds(r0, R), :] = _conv_causal(front, w_ref, K, r0, R) + bias_ref[...]

    return pl.pallas_call(
        body, name="branch_c_conv_fwd", grid=(nb,),
        in_specs=[_col(S, cw, 9 * nb), _col(S, cw, 10 * nb), pl.BlockSpec((K, cw), lambda c: (0, c)),
                  pl.BlockSpec((1, cw), lambda c: (0, c))],
        out_specs=pl.BlockSpec((S, cw), lambda c: (0, c)), out_shape=jax.ShapeDtypeStruct((S, U), F32),
        scratch_shapes=[pltpu.VMEM((S + CONV_PAD, cw), F32)], compiler_params=_cp(("parallel",)),
    )(proj, proj, wc, cb)


def _branch_c_conv_bwd(d_u1, proj, wc, U):
    S = proj.shape[0]
    cw = min(LANE, U)
    nb = U // cw
    K = CF_WIDTH

    def body(d_ref, a_ref, g_ref, w_ref, da_ref, dg_ref, dw_ref, dbias_ref, front, back):
        sg = _sigmoid(g_ref[...])
        front[pl.ds(0, CONV_PAD), :] = jnp.zeros((CONV_PAD, cw), F32)
        front[pl.ds(CONV_PAD, S), :] = a_ref[...] * sg
        back[pl.ds(S, CONV_PAD), :] = jnp.zeros((CONV_PAD, cw), F32)
        back[pl.ds(0, S), :] = d_ref[...]
        dbias_ref[...] = jnp.sum(d_ref[...], axis=0, keepdims=True)
        for r0, R in _chunks(S):
            rows = pl.ds(r0, R)
            d_u0 = _conv_anticausal(back, w_ref, K, r0, R)
            s = _sigmoid(g_ref[rows, :])
            da_ref[rows, :] = (d_u0 * s).astype(da_ref.dtype)
            dg_ref[rows, :] = (d_u0 * a_ref[rows, :] * s * (1.0 - s)).astype(dg_ref.dtype)
        _conv_wgrad(front, back, dw_ref, K, S)

    blk = pl.BlockSpec((S, cw), lambda c: (0, c))
    wblk = pl.BlockSpec((K, cw), lambda c: (0, c))
    vblk = pl.BlockSpec((1, cw), lambda c: (0, c))
    return pl.pallas_call(
        body, name="branch_c_conv_bwd", grid=(nb,),
        in_specs=[blk, _col(S, cw, 9 * nb), _col(S, cw, 10 * nb), wblk], out_specs=[blk, blk, wblk, vblk],
        out_shape=[jax.ShapeDtypeStruct((S, U), MXU_DTYPE)] * 2 + [jax.ShapeDtypeStruct((K, U), F32), jax.ShapeDtypeStruct((1, U), F32)],
        scratch_shapes=[pltpu.VMEM((S + CONV_PAD, cw), F32)] * 2, compiler_params=_cp(("parallel",)),
    )(d_u1, proj, proj, wc)


def _branch_c_norm_fwd(u1, ng, nbias):
    S, U = u1.shape
    tr = _row_tile(S, 256)

    def body(u_ref, g_ref, b_ref, o_ref):
        u = u_ref[...]
        mu = jnp.mean(u, axis=-1, keepdims=True)
        var = jnp.mean(jnp.square(u - mu), axis=-1, keepdims=True)
        u2 = ((u - mu) * lax.rsqrt(var + LN_EPS)) * g_ref[...] + b_ref[...]
        o_ref[...] = (u2 * _sigmoid(u2)).astype(o_ref.dtype)

    row = pl.BlockSpec((tr, U), lambda i: (i, 0))
    vec = pl.BlockSpec((1, U), lambda i: (0, 0))
    return pl.pallas_call(
        body, name="branch_c_norm_fwd", grid=(S // tr,), in_specs=[row, vec, vec], out_specs=row,
        out_shape=jax.ShapeDtypeStruct((S, U), MXU_DTYPE), compiler_params=_cp(("parallel",)),
    )(u1, ng, nbias)


def _branch_c_norm_bwd(d_u3, u1, ng, nbias):
    S, U = u1.shape
    tr = _row_tile(S, 256)

    def body(d_ref, u_ref, g_ref, b_ref, du_ref, dg_ref, db_ref):
        u = u_ref[...]
        mu = jnp.mean(u, axis=-1, keepdims=True)
        var = jnp.mean(jnp.square(u - mu), axis=-1, keepdims=True)
        rstd = lax.rsqrt(var + LN_EPS)
        xh = (u - mu) * rstd
        u2 = xh * g_ref[...] + b_ref[...]
        s = _sigmoid(u2)
        d_u2 = d_ref[...] * (s * (1.0 + u2 * (1.0 - s)))
        d_xh = d_u2 * g_ref[...]
        du_ref[...] = rstd * (d_xh - jnp.mean(d_xh, axis=-1, keepdims=True) - xh * jnp.mean(d_xh * xh, axis=-1, keepdims=True))

        @pl.when(pl.program_id(0) == 0)
        def _():
            dg_ref[...] = jnp.zeros_like(dg_ref)
            db_ref[...] = jnp.zeros_like(db_ref)

        dg_ref[...] += jnp.sum(d_u2 * xh, axis=0, keepdims=True)
        db_ref[...] += jnp.sum(d_u2, axis=0, keepdims=True)

    row = pl.BlockSpec((tr, U), lambda i: (i, 0))
    vec = pl.BlockSpec((1, U), lambda i: (0, 0))
    return pl.pallas_call(
        body, name="branch_c_norm_bwd", grid=(S // tr,), in_specs=[row, row, vec, vec], out_specs=[row, vec, vec],
        out_shape=[jax.ShapeDtypeStruct((S, U), F32), jax.ShapeDtypeStruct((1, U), F32), jax.ShapeDtypeStruct((1, U), F32)],
        compiler_params=_cp(("arbitrary",)),
    )(d_u3, u1, ng, nbias)


def _tri(T, inclusive):
    j = lax.broadcasted_iota(jnp.int32, (T, T), 0)
    s = lax.broadcasted_iota(jnp.int32, (T, T), 1)
    return ((j >= s) if inclusive else (j > s)).astype(MXU_DTYPE)


def _split_dot(x, tri):
    if MXU_DTYPE == F32:
        return jnp.dot(x, tri, preferred_element_type=F32)
    hi = x.astype(MXU_DTYPE)
    lo = (x - hi.astype(F32)).astype(MXU_DTYPE)
    return jnp.dot(hi, tri, preferred_element_type=F32) + jnp.dot(lo, tri, preferred_element_type=F32)


def _sb_block(qb, kb, T, tri_strict, c_lf, diag):
    z = lax.dot_general(qb, kb, _DN["NT"], preferred_element_type=F32) * (HEAD_DIM ** -0.5)
    e = jnp.exp(-jnp.abs(z))
    lg = jnp.log(1.0 + e)
    log_beta = jnp.minimum(z, 0.0) - lg
    lf = jnp.minimum(-z, 0.0) - lg
    mask = None
    if diag:
        mask = lax.broadcasted_iota(jnp.int32, (T, T), 1) < lax.broadcasted_iota(jnp.int32, (T, T), 0)
        lf = jnp.where(mask, lf, 0.0)
    a = jnp.exp(log_beta + _split_dot(lf, tri_strict) + c_lf)
    if diag:
        a = jnp.where(mask, a, 0.0)
    return z, e, mask, lf, a


def _key_blocks(i, step, init):
    carry = step(i, init, True)
    return lax.fori_loop(1, i + 1, lambda jj, c: step(i - jj, c, False), carry)


def _attn_specs(S, U, h_blocks):
    nh = (2 * U) // HEAD_DIM
    return [pl.BlockSpec((S, HEAD_DIM), functools.partial(lambda h, o: (0, o + h), o=o * nh)) for o in range(h_blocks)]


def _attn_fwd(proj, U):
    S = proj.shape[0]
    nh = (2 * U) // HEAD_DIM
    T = _row_tile(S, 256)
    nq = S // T

    def body(q_ref, k_ref, v_ref, o_ref, of_ref, qs, ks, vs):
        qs[...] = q_ref[...].astype(MXU_DTYPE)
        ks[...] = k_ref[...].astype(MXU_DTYPE)
        vs[...] = v_ref[...].astype(MXU_DTYPE)
        tri = _tri(T, False)

        def q_loop(i, _):
            rows = pl.ds(pl.multiple_of(i * T, T), T)
            qb = qs[rows, :]

            def step(j, carry, diag):
                c_lf, acc = carry
                cols = pl.ds(pl.multiple_of(j * T, T), T)
                _, _, _, lf, a = _sb_block(qb, ks[cols, :], T, tri, c_lf, diag)
                acc = acc + jnp.dot(a.astype(MXU_DTYPE), vs[cols, :], preferred_element_type=F32)
                return c_lf + jnp.sum(lf, axis=1, keepdims=True), acc

            _, acc = _key_blocks(i, step, (jnp.zeros((T, 1), F32), jnp.zeros((T, HEAD_DIM), F32)))
            o_ref[rows, :] = acc.astype(o_ref.dtype)
            of_ref[rows, :] = acc
            return 0

        lax.fori_loop(0, nq, q_loop, 0)

    hblk = pl.BlockSpec((S, HEAD_DIM), lambda h: (0, h))
    return pl.pallas_call(
        body, name="attn_fwd", grid=(nh,), in_specs=_attn_specs(S, U, 3), out_specs=[hblk, hblk],
        out_shape=[jax.ShapeDtypeStruct((S, 2 * U), MXU_DTYPE), jax.ShapeDtypeStruct((S, 2 * U), F32)],
        scratch_shapes=[pltpu.VMEM((S, HEAD_DIM), MXU_DTYPE)] * 3, compiler_params=_cp(("parallel",)),
    )(proj, proj, proj)


def _attn_bwd(proj, att_f32, d_att, U):
    S = proj.shape[0]
    nh = (2 * U) // HEAD_DIM
    T = _row_tile(S, 256)
    nq = S // T
    scale = HEAD_DIM ** -0.5

    def body(q_ref, k_ref, v_ref, o_ref, do_ref, dq_ref, dk_ref, dv_ref, qs, ks, vs, dos, dka, dva):
        qs[...] = q_ref[...].astype(MXU_DTYPE)
        ks[...] = k_ref[...].astype(MXU_DTYPE)
        vs[...] = v_ref[...].astype(MXU_DTYPE)
        dos[...] = do_ref[...].astype(MXU_DTYPE)
        dka[...] = jnp.zeros_like(dka)
        dva[...] = jnp.zeros_like(dva)
        tri = _tri(T, False)
        tri_inc = _tri(T, True)

        def q_loop(i, _):
            rows = pl.ds(pl.multiple_of(i * T, T), T)
            qb = qs[rows, :]
            dob = dos[rows, :]
            delta = jnp.sum(dob.astype(F32) * o_ref[rows, :], axis=1, keepdims=True)

            def step(j, carry, diag):
                c_lf, c_g, dq = carry
                cols = pl.ds(pl.multiple_of(j * T, T), T)
                kb, vb = ks[cols, :], vs[cols, :]
                z, e, mask, lf, a = _sb_block(qb, kb, T, tri, c_lf, diag)
                a_mx = a.astype(MXU_DTYPE)
                d_a = lax.dot_general(dob, vb, _DN["NT"], preferred_element_type=F32)
                g = a_mx.astype(F32) * d_a
                prefix = delta - (_split_dot(g, tri_inc) + c_g)
                inv = 1.0 / (1.0 + e)
                beta = jnp.where(z >= 0.0, 1.0, e) * inv
                one_m_beta = jnp.where(z >= 0.0, e, 1.0) * inv
                dz = (g * one_m_beta - prefix * beta) * scale
                if diag:
                    dz = jnp.where(mask, dz, 0.0)
                dz = dz.astype(MXU_DTYPE)
                dq = dq + jnp.dot(dz, kb, preferred_element_type=F32)
                dka[cols, :] += lax.dot_general(dz, qb, _DN["TN"], preferred_element_type=F32)
                dva[cols, :] += lax.dot_general(a_mx, dob, _DN["TN"], preferred_element_type=F32)
                return c_lf + jnp.sum(lf, axis=1, keepdims=True), c_g + jnp.sum(g, axis=1, keepdims=True), dq

            zero = jnp.zeros((T, 1), F32)
            _, _, dq = _key_blocks(i, step, (zero, zero, jnp.zeros((T, HEAD_DIM), F32)))
            dq_ref[rows, :] = dq.astype(dq_ref.dtype)
            return 0

        lax.fori_loop(0, nq, q_loop, 0)
        dk_ref[...] = dka[...].astype(dk_ref.dtype)
        dv_ref[...] = dva[...].astype(dv_ref.dtype)

    hblk = pl.BlockSpec((S, HEAD_DIM), lambda h: (0, h))
    return pl.pallas_call(
        body, name="attn_bwd", grid=(nh,), in_specs=_attn_specs(S, U, 3) + [hblk, hblk], out_specs=[hblk] * 3,
        out_shape=[jax.ShapeDtypeStruct((S, 2 * U), MXU_DTYPE)] * 3,
        scratch_shapes=[pltpu.VMEM((S, HEAD_DIM), MXU_DTYPE)] * 4 + [pltpu.VMEM((S, HEAD_DIM), F32)] * 2,
        compiler_params=_cp(("parallel",)),
    )(proj, proj, proj, att_f32, d_att)


def _place():
    return lax.axis_index("x"), lax.axis_index("y"), lax.axis_index("c")


def _flip(v, bit):
    return 1 - v if bit else v


def _related(x, y, r):
    return _flip(x, r & 1), _flip(y, r >> 1)


def _own_rows(ref, dev):
    rows = ref.shape[0] // N_DEV
    return ref.at[pl.ds(pl.multiple_of(dev * rows, 16), rows), :]


def _my_block(n_blocks):
    def index(i):
        x, y, c = _place()
        return (4 * x + 2 * y + c) * n_blocks + i, 0
    return index


def _gathered(rows, C):
    return jax.ShapeDtypeStruct((N_DEV * rows, C), WIRE_DTYPE)


def _pack_plain(w, l, zero):
    _, rows, C = w.shape
    t = _shard_tile(rows)

    def body(w_ref, z_ref, o_ref):
        o_ref[...] = (w_ref[...] + z_ref[0:1, 0:1]).astype(o_ref.dtype)

    return pl.pallas_call(
        body, name="pack_plain", grid=(rows // t,),
        in_specs=[pl.BlockSpec((None, t, C), lambda i: (l, i, 0)), pl.BlockSpec((8, LANE), lambda i: (0, 0))],
        out_specs=pl.BlockSpec((t, C), _my_block(rows // t)), out_shape=_gathered(rows, C), compiler_params=_cp(("parallel",)),
    )(w, zero)


def _pack_transposed(ws, l, zero):
    rows = ws[0].shape[2]
    C = sum(w.shape[1] for w in ws)
    t = _row_tile(rows, 256)
    n = len(ws)

    def body(*refs):
        z_ref, o_ref = refs[n], refs[n + 1]
        col = 0
        for w_ref in refs[:n]:
            k = w_ref.shape[0]
            o_ref[:, col:col + k] = (w_ref[...] + z_ref[0:1, 0:1]).T.astype(o_ref.dtype)
            col += k

    return pl.pallas_call(
        body, name="pack_transposed", grid=(rows // t,),
        in_specs=[pl.BlockSpec((None, w.shape[1], t), lambda i: (l, 0, i)) for w in ws] + [pl.BlockSpec((8, LANE), lambda i: (0, 0))],
        out_specs=pl.BlockSpec((t, C), _my_block(rows // t)), out_shape=_gathered(rows, C), compiler_params=_cp(("parallel",)),
    )(*ws, zero)


def _pack_transposed_mxu(w, l, zero):
    _, K, rows = w.shape
    tk = _row_tile(K, 512)

    def body(w_ref, z_ref, o_ref, eye):
        @pl.when(pl.program_id(0) == 0)
        def _():
            eye[...] = (lax.broadcasted_iota(jnp.int32, (rows, rows), 0) == lax.broadcasted_iota(jnp.int32, (rows, rows), 1)).astype(eye.dtype)

        x = (w_ref[...] + z_ref[0:1, 0:1]).astype(MXU_DTYPE)
        o_ref[...] = lax.dot_general(eye[...], x, _DN["NT"], preferred_element_type=F32).astype(o_ref.dtype)

    def out_index(j):
        x, y, c = _place()
        return 4 * x + 2 * y + c, j

    return pl.pallas_call(
        body, name="pack_transposed_mxu", grid=(K // tk,),
        in_specs=[pl.BlockSpec((None, tk, rows), lambda j: (l, j, 0)), pl.BlockSpec((8, LANE), lambda j: (0, 0))],
        out_specs=pl.BlockSpec((rows, tk), out_index), out_shape=_gathered(rows, K),
        scratch_shapes=[pltpu.VMEM((rows, rows), MXU_DTYPE)], compiler_params=_cp(("arbitrary",)),
    )(w, zero)


def _hbm(a):
    return pltpu.with_memory_space_constraint(a, pltpu.HBM)


def _hbm_like(arrays):
    return tuple(pltpu.HBM(a.shape, a.dtype) for a in arrays)


def _dev(px, py, pc):
    return 4 * px + 2 * py + pc


def _block_copies(wb_refs, send_sems, recv_sems, slot, block_out, block_in, peer, outgoing):
    K = len(wb_refs)
    return [pltpu.make_async_remote_copy(
        src_ref=_own_rows(wb_ref, block_out), dst_ref=_own_rows(wb_ref, block_out if outgoing else block_in),
        send_sem=send_sems.at[slot * K + k], recv_sem=recv_sems.at[slot * K + k], device_id=peer, device_id_type=MESH)
        for k, wb_ref in enumerate(wb_refs)]


def _ag_stage1(wb_refs, send_sems, recv_sems, outgoing):
    x, y, c = _place()
    me = _dev(x, y, c)
    out = []
    for slot, peer in enumerate(((x, y, 1 - c), (1 - x, y, c), (x, 1 - y, c))):
        out += _block_copies(wb_refs, send_sems, recv_sems, slot, me, _dev(*peer), peer, outgoing)
    return out


def _ag_stage2(wb_refs, send_sems, recv_sems, outgoing):
    x, y, c = _place()
    via = ((1 - x) + c * (2 * x - 1), y + c * (1 - 2 * y))
    to = (x + c * (1 - 2 * x), (1 - y) + c * (2 * y - 1), c)
    out = _block_copies(wb_refs, send_sems, recv_sems, 0, _dev(*via, c), _dev(1 - x, 1 - y, c), to, outgoing)
    for slot, (px, py) in ((1, (1 - x, y)), (2, (x, 1 - y))):
        out += _block_copies(wb_refs, send_sems, recv_sems, slot, _dev(px, py, c), _dev(px, py, 1 - c), (x, y, 1 - c), outgoing)
    return out


def _ag_start(copies, wbs, carry, name):
    K = len(wbs)

    def body(*refs):
        send_sems, recv_sems = refs[K + 1:K + 3]
        for cp in copies(refs[:K], send_sems, recv_sems, True):
            cp.start()

    outs = pl.pallas_call(
        body, name=name,
        out_shape=(pltpu.SemaphoreType.DMA((3 * K,)), pltpu.SemaphoreType.DMA((3 * K,))) + _hbm_like(list(wbs) + [carry]),
        in_specs=(HBM,) * (K + 1), out_specs=(SEM, SEM) + (HBM,) * (K + 1), input_output_aliases={k: 2 + k for k in range(K + 1)},
        compiler_params=pltpu.CompilerParams(has_side_effects=EFFECT),
    )(*[_hbm(a) for a in wbs], _hbm(carry))
    return outs[0], outs[1], list(outs[2:2 + K]), outs[2 + K]


def _ag_wait(copies, wbs, send_sems, recv_sems, after, name):
    K = len(wbs)

    def body(*refs):
        for cp in copies(refs[:K], refs[K], refs[K + 1], False):
            cp.wait_send()
            cp.wait_recv()

    return list(pl.pallas_call(
        body, name=name, out_shape=_hbm_like(wbs),
        in_specs=(HBM,) * K + (SEM, SEM, ANY), out_specs=(HBM,) * K, input_output_aliases={k: k for k in range(K)},
        compiler_params=pltpu.CompilerParams(has_side_effects=EFFECT),
    )(*wbs, send_sems, recv_sems, after))


def _ag_finish(wbs):
    K = len(wbs)

    def body(*refs):
        ins, outs, tok_ref, send_sems, recv_sems = refs[:K], refs[K:2 * K], refs[2 * K], refs[2 * K + 1], refs[2 * K + 2]
        x, y, c = _place()
        sent = []
        for k in range(K):
            cp = pltpu.make_async_remote_copy(
                src_ref=_own_rows(ins[k], _dev(1 - x, 1 - y, c)), dst_ref=_own_rows(outs[k], _dev(1 - x, 1 - y, c)),
                send_sem=send_sems.at[k], recv_sem=recv_sems.at[k], device_id=(x, y, 1 - c), device_id_type=MESH)
            cp.start()
            sent.append(cp)
        for k in range(K):
            theirs = _own_rows(outs[k], _dev(1 - x, 1 - y, 1 - c))
            pltpu.make_async_remote_copy(src_ref=theirs, dst_ref=theirs, send_sem=send_sems.at[k], recv_sem=recv_sems.at[k],
                                         device_id=(x, y, 1 - c), device_id_type=MESH).wait_recv()
        for cp in sent:
            cp.wait_send()
        tok_ref[...] = jnp.zeros_like(tok_ref)

    outs = pl.pallas_call(
        body, name="ag_finish", in_specs=[ANY] * K, out_specs=[ANY] * K + [pl.BlockSpec(memory_space=pltpu.VMEM)],
        out_shape=[jax.ShapeDtypeStruct(a.shape, a.dtype) for a in wbs] + [jax.ShapeDtypeStruct((8, LANE), F32)],
        input_output_aliases={k: k for k in range(K)},
        scratch_shapes=[pltpu.SemaphoreType.DMA((K,))] * 2, compiler_params=_cp(),
    )(*wbs)
    return list(outs[:K]), outs[K]


def _pair_copies(g_refs, land_refs, send_sems, recv_sems):
    x, y, c = _place()
    K = len(g_refs)
    out = []
    for r in range(4):
        px, py = _related(x, y, r)
        for k in range(K):
            out.append(pltpu.make_async_remote_copy(
                src_ref=_own_rows(g_refs[k], 4 * px + 2 * py + (1 - c)), dst_ref=land_refs[k].at[r],
                send_sem=send_sems.at[r * K + k], recv_sem=recv_sems.at[r * K + k], device_id=(x, y, 1 - c), device_id_type=MESH))
    return out


def _shard_tile(rows):
    for t in (512, 736, 256, 128, 64, 32, 16, 8):
        if rows % t == 0:
            return t
    return rows


def _pair_add(grad, landed):
    _, rows, C = landed.shape
    t = _shard_tile(rows)

    def g_index(r, i):
        x, y, c = _place()
        px = jnp.where(r % 2 == 1, 1 - x, x)
        py = jnp.where(r // 2 == 1, 1 - y, y)
        return (4 * px + 2 * py + c) * (rows // t) + i, 0

    def body(a_ref, b_ref, o_ref):
        o_ref[...] = (a_ref[...].astype(F32) + b_ref[...].astype(F32)).astype(o_ref.dtype)

    slot = pl.BlockSpec((None, t, C), lambda r, i: (r, i, 0))
    return pl.pallas_call(
        body, name="pair_add", grid=(4, rows // t), in_specs=[pl.BlockSpec((t, C), g_index), slot], out_specs=slot,
        out_shape=jax.ShapeDtypeStruct((4, rows, C), grad.dtype), compiler_params=_cp(("parallel", "parallel")),
    )(grad, landed)


def _rs_copies(p_refs, land_refs, send_sems, recv_sems):
    x, y, c = _place()
    K = len(p_refs)
    return [pltpu.make_async_remote_copy(src_ref=p_refs[k].at[r], dst_ref=land_refs[k].at[r - 1], send_sem=send_sems.at[(r - 1) * K + k],
                                         recv_sem=recv_sems.at[(r - 1) * K + k], device_id=(*_related(x, y, r), c), device_id_type=MESH)
            for r in (1, 2, 3) for k in range(K)]


def _rs_start(copies, n_slots, srcs, lands, carry, name):
    K = len(srcs)

    def body(*refs):
        for cp in copies(refs[:K], refs[K:2 * K], refs[2 * K + 1], refs[2 * K + 2]):
            cp.start()

    n_thru = 2 * K + 1
    outs = pl.pallas_call(
        body, name=name,
        out_shape=(pltpu.SemaphoreType.DMA((n_slots * K,)), pltpu.SemaphoreType.DMA((n_slots * K,))) + _hbm_like(list(srcs) + list(lands) + [carry]),
        in_specs=(HBM,) * n_thru, out_specs=(SEM, SEM) + (HBM,) * n_thru, input_output_aliases={k: 2 + k for k in range(n_thru)},
        compiler_params=pltpu.CompilerParams(has_side_effects=EFFECT),
    )(*[_hbm(a) for a in list(srcs) + list(lands) + [carry]])
    return outs[0], outs[1], list(outs[2:2 + K]), list(outs[2 + K:2 + 2 * K]), outs[2 + 2 * K]


def _rs_wait(copies, srcs, lands, send_sems, recv_sems, after, name):
    K = len(srcs)

    def body(*refs):
        for cp in copies(refs[:K], refs[K:2 * K], refs[2 * K], refs[2 * K + 1]):
            cp.wait_send()
            cp.wait_recv()

    outs = pl.pallas_call(
        body, name=name, out_shape=_hbm_like(list(srcs) + list(lands)),
        in_specs=(HBM,) * (2 * K) + (SEM, SEM) + (ANY,) * len(after), out_specs=(HBM,) * (2 * K),
        input_output_aliases={k: k for k in range(2 * K)}, compiler_params=pltpu.CompilerParams(has_side_effects=EFFECT),
    )(*srcs, *lands, send_sems, recv_sems, *after)
    return list(outs[:K]), list(outs[K:])


def _all_gather_small(v, reduce):
    M, N = v.shape

    def body(x_ref, out_ref, sum_ref, send_sems, recv_sems, local_sem):
        x, y, c = _place()
        me, sibling = (x, y, c), (x, y, 1 - c)
        chips = [_related(x, y, r) for r in (1, 2, 3)]

        def rows(px, py, pc):
            return out_ref.at[pl.ds(pl.multiple_of((4 * px + 2 * py + pc) * M, 8), M), :]

        def copy(k, block, to, src=None):
            return pltpu.make_async_remote_copy(src_ref=rows(*block) if src is None else src, dst_ref=rows(*block),
                                                send_sem=send_sems.at[k], recv_sem=recv_sems.at[k], device_id=to, device_id_type=MESH)

        mine = pltpu.make_async_copy(x_ref, rows(*me), local_sem)
        mine.start()
        first = [copy(0, me, sibling, src=x_ref)]
        first += [copy(1 + j, me, (*chip, c), src=x_ref) for j, chip in enumerate(chips)]
        for cp in first:
            cp.start()
        passed = [copy(4 + j, (*chip, c), sibling) for j, chip in enumerate(chips)]
        for j, chip in enumerate(chips):
            copy(1 + j, (*chip, c), me).wait_recv()
            passed[j].start()
        copy(0, sibling, me).wait_recv()
        for j, chip in enumerate(chips):
            copy(4 + j, (*chip, 1 - c), me).wait_recv()
        for cp in first + passed:
            cp.wait_send()
        mine.wait()
        if reduce:
            tot = out_ref[pl.ds(0, M), :]
            for p in range(1, N_DEV):
                tot = tot + out_ref[pl.ds(p * M, M), :]
            sum_ref[...] = tot
        else:
            sum_ref[...] = jnp.zeros_like(sum_ref)

    vm = pl.BlockSpec(memory_space=pltpu.VMEM)
    second = jax.ShapeDtypeStruct((M, N) if reduce else (8, LANE), F32)
    outs = pl.pallas_call(
        body, name="all_reduce_small" if reduce else "all_gather_small", in_specs=[vm], out_specs=[vm, vm],
        out_shape=[jax.ShapeDtypeStruct((N_DEV * M, N), v.dtype), second],
        scratch_shapes=[pltpu.SemaphoreType.DMA((7,)), pltpu.SemaphoreType.DMA((7,)), pltpu.SemaphoreType.DMA],
        compiler_params=_cp(),
    )(v)
    return outs[1] if reduce else outs


def _adamw(w, g, m, v):
    shape = w.shape
    cols = shape[-1]
    rows = w.size // cols
    tr = _row_tile(rows, 256) if rows % 8 == 0 else rows
    c1 = 1.0 / (1.0 - ADAM_B1 ** ADAM_STEP)
    c2 = 1.0 / (1.0 - ADAM_B2 ** ADAM_STEP)

    def body(w_ref, g_ref, m_ref, v_ref, d_ref, nm_ref, nv_ref):
        gv = g_ref[...]
        nm = ADAM_B1 * m_ref[...] + (1.0 - ADAM_B1) * gv
        nv = ADAM_B2 * v_ref[...] + (1.0 - ADAM_B2) * (gv * gv)
        d_ref[...] = -ADAM_LR * ((nm * c1) / (jnp.sqrt(nv * c2) + ADAM_EPS) + ADAM_WD * w_ref[...])
        nm_ref[...] = nm
        nv_ref[...] = nv

    blk = pl.BlockSpec((tr, cols), lambda i: (i, 0))
    outs = pl.pallas_call(
        body, name="adamw", grid=(rows // tr,), in_specs=[blk] * 4, out_specs=[blk] * 3,
        out_shape=[jax.ShapeDtypeStruct((rows, cols), F32)] * 3, compiler_params=_cp(("parallel",)),
    )(*[a.reshape(rows, cols) for a in (w, g, m, v)])
    return tuple(o.reshape(shape) for o in outs)


def _transpose_exact(x):
    t = x.shape[1]
    eye = (lax.broadcasted_iota(jnp.int32, (t, t), 0) == lax.broadcasted_iota(jnp.int32, (t, t), 1)).astype(MXU_DTYPE)
    if MXU_DTYPE == F32:
        return lax.dot_general(eye, x, _DN["NT"], preferred_element_type=F32)
    out = None
    for _ in range(3):
        part = x.astype(MXU_DTYPE)
        x = x - part.astype(F32)
        term = lax.dot_general(eye, part, _DN["NT"], preferred_element_type=F32)
        out = term if out is None else out + term
    return out


def _adamw_layer(w, pair_sums, landed, m, v, l, prev, col_off=None):
    L, A, B = w.shape
    ta = _row_tile(A, 256) if A % 8 == 0 else A
    c1 = 1.0 / (1.0 - ADAM_B1 ** ADAM_STEP)
    c2 = 1.0 / (1.0 - ADAM_B2 ** ADAM_STEP)
    n_prev = 0 if prev is None else 4
    if col_off is None:
        g_specs = [pl.BlockSpec((None, ta, B), functools.partial(lambda i, s: (s, i, 0), s=s)) for s in (0, 0, 1, 2)]
    else:
        assert col_off % ta == 0 and pair_sums.shape[1] == B
        g_specs = [pl.BlockSpec((None, B, ta), functools.partial(lambda i, s: (s, 0, col_off // ta + i), s=s)) for s in (0, 0, 1, 2)]

    def body(*refs):
        w_ref, m_ref, v_ref, p_ref, l1_ref, l2_ref, l3_ref = refs[:7]
        go_ref, d_ref, nm_ref, nv_ref = refs[7 + n_prev:]
        gv = ((p_ref[...].astype(F32) + l1_ref[...].astype(F32)) + l2_ref[...].astype(F32)) + l3_ref[...].astype(F32)
        if col_off is not None:
            gv = _transpose_exact(gv)
        nm = ADAM_B1 * m_ref[...] + (1.0 - ADAM_B1) * gv
        nv = ADAM_B2 * v_ref[...] + (1.0 - ADAM_B2) * (gv * gv)
        d_ref[...] = -ADAM_LR * ((nm * c1) / (jnp.sqrt(nv * c2) + ADAM_EPS) + ADAM_WD * w_ref[...])
        go_ref[...] = gv
        nm_ref[...] = nm
        nv_ref[...] = nv

    lay = pl.BlockSpec((None, ta, B), lambda i: (l, i, 0))
    return pl.pallas_call(
        body, name="adamw_layer", grid=(A // ta,),
        in_specs=[lay, lay, lay] + g_specs + [ANY] * n_prev, out_specs=[lay] * 4,
        out_shape=[jax.ShapeDtypeStruct((L, A, B), F32)] * 4, input_output_aliases={7 + j: j for j in range(n_prev)},
        compiler_params=_cp(("parallel",)),
    )(w, m, v, pair_sums, landed, landed, landed, *(prev or ()))


def _relu2(acc):
    r = jnp.maximum(acc, 0.0)
    return acc, r * r


def _relu2_bwd(acc, up):
    return (acc * (2.0 * jnp.maximum(up.astype(F32), 0.0)),)


def kernel(x, ln_mix_pre, ln_mix_post, ln_mlp_pre, ln_mlp_post, w_in, conv_a_w, proj_a, proj_b, conv_c_w, conv_c_b, norm_c_g, norm_c_b, proj_c, w_o, w_up, w_down, loss_target, m_ln_mix_pre, m_ln_mix_post, m_ln_mlp_pre, m_ln_mlp_post, m_w_in, m_conv_a_w, m_proj_a, m_proj_b, m_conv_c_w, m_conv_c_b, m_norm_c_g, m_norm_c_b, m_proj_c, m_w_o, m_w_up, m_w_down, v_ln_mix_pre, v_ln_mix_post, v_ln_mlp_pre, v_ln_mlp_post, v_w_in, v_conv_a_w, v_proj_a, v_proj_b, v_conv_c_w, v_conv_c_b, v_norm_c_g, v_norm_c_b, v_proj_c, v_w_o, v_w_up, v_w_down):
    L, D, n_in_loc = w_in.shape
    S = x.shape[1]
    U = D // 4
    N_IN = n_in_loc * N_DEV
    D_FF = w_up.shape[2] * N_DEV
    assert N_IN == 23 * U and x.shape[0] == 1
    x_i, y_i, c_i = _place()
    me = 4 * x_i + 2 * y_i + c_i

    def pack(l, which, zero):
        kinds = (lambda: _pack_transposed([proj_b, proj_a, proj_c], l, zero), lambda: _pack_plain(w_o, l, zero),
                 lambda: _pack_transposed([w_up], l, zero), lambda: _pack_plain(w_down, l, zero),
                 lambda: _pack_transposed_mxu(w_in, l, zero))
        return [kinds[k]() for k in which]

    def stage1(wbs, carry, tag):
        return _ag_start(_ag_stage1, wbs, carry, f"ag_s1_{tag}")

    def stage2(started, after, carry, tag):
        send_sems, recv_sems, wbs, _ = started
        wbs = _ag_wait(_ag_stage1, wbs, send_sems, recv_sems, after, f"ag_s1_wait_{tag}")
        return _ag_start(_ag_stage2, wbs, carry, f"ag_s2_{tag}")

    def gather_end(started, after, tag):
        send_sems, recv_sems, wbs, _ = started
        wbs, zero = _ag_finish(_ag_wait(_ag_stage2, wbs, send_sems, recv_sems, after, f"ag_s2_wait_{tag}"))
        return [w.astype(MXU_DTYPE) for w in wbs], zero

    cu = U // N_DEV
    conv_loc = jnp.concatenate([conv_a_w, conv_c_w], axis=1).reshape(L * (SC_WIDTH + CF_WIDTH), cu)
    conv_all, tok = _all_gather_small(conv_loc, False)
    conv_all = conv_all.reshape(N_DEV, L, SC_WIDTH + CF_WIDTH, cu).transpose(1, 2, 0, 3).reshape(L, SC_WIDTH + CF_WIDTH, U)
    wa_full, wc_full = conv_all[:, :SC_WIDTH], conv_all[:, SC_WIDTH:]
    every = (0, 1, 2, 3, 4)
    in_s1 = stage1(pack(0, (4,), tok), jnp.zeros((8, LANE), F32), "0_in")
    rest_packed, next_packed = pack(0, (0, 1, 2, 3), tok), pack(1, every, tok)
    in_s2 = stage2(in_s1, next_packed[4], in_s1[3], "0_in")
    rest_s1 = stage1(rest_packed, in_s2[3], "0_rest")
    wb = []

    def vec(p, l):
        return p[l][None, :]

    xs = x[0]
    saved = []
    h1 = _rms_fwd(xs, vec(ln_mix_pre, 0), "rms_fwd")
    for l in range(L):
        if l == 0:
            (w_in_t,), _ = gather_end(in_s2, rest_s1[3], "0_in")
        else:
            w_p, w_o_l, w_up_t, w_dn, w_in_t = w_next
            if l + 1 < L:
                next_s1 = stage1(pack(l + 1, every, tok), h1, l + 1)
                h1 = next_s1[3]
        proj = _mm(h1, w_in_t, "NT", (F32,), "mm_proj")
        a_out = _branch_a_fwd(proj, wa_full[l], U)
        u1 = _branch_c_conv_fwd(proj, wc_full[l], vec(conv_c_b, l), U)
        u3 = _branch_c_norm_fwd(u1, vec(norm_c_g, l), vec(norm_c_b, l))
        if l == 0:
            rest_s2 = stage2(rest_s1, u3, proj, "0_rest")
            next_s1 = stage1(next_packed, rest_s2[3], 1)
            proj = next_s1[3]
        att, att_f32 = _attn_fwd(proj, U)
        if l == 0:
            (w_p, w_o_l, w_up_t, w_dn), tok = gather_end(rest_s2, att, "0_rest")
        wb.append((w_p, w_o_l, w_up_t, w_dn, w_in_t))
        merged, ya, yb, yc = _merge_fwd(proj, a_out, att, u3, w_p)
        mixed = _mm(merged, w_o_l, "NN", (F32,), "mm_mixed")
        x1, h2 = _resid_post(xs, mixed, vec(ln_mix_post, l), vec(ln_mlp_pre, l), "resid_post_mix")
        if 0 < l < L - 1:
            next_s2 = stage2(next_s1, mixed, h2, l + 1)
            h2 = next_s2[3]
        up, act = _mm(h2, w_up_t, "NT", (MXU_DTYPE, MXU_DTYPE), "mm_up", epilogue=_relu2)
        if l == 0 and L > 1:
            next_s2 = stage2(next_s1, up, act, 1)
            act = next_s2[3]
        f = _mm(act, w_dn, "NN", (F32,), "mm_down", tm=1024, tn=1024, tk=4096)
        saved.append((xs, h1, proj, a_out, u1, u3, att, att_f32, ya, yb, yc, merged, mixed, x1, h2, up, act, f))
        if l + 1 < L:
            xs, h1 = _resid_post(x1, f, vec(ln_mlp_post, l), vec(ln_mix_pre, l + 1), "resid_post_mlp")
            w_next, tok = gather_end(next_s2, h1, l + 1)
        else:
            xs, _ = _resid_post(x1, f, vec(ln_mlp_post, l), None, "resid_post_last")
    dxo, loss_row = _loss_head(xs, loss_target[0])
    loss = lax.psum(loss_row[0, 0], ("x", "y", "c"))

    small = {k: [None] * L for k in ("g1", "g2", "g3", "g4", "cb", "ng", "nb", "wa", "wc")}
    def pair_start(grads, carry, name):
        lands = [lax.empty((4, g.shape[0] // N_DEV, D), WIRE_DTYPE) for g in grads]
        send_sems, recv_sems, grads, lands, carry = _rs_start(_pair_copies, 4, grads, lands, carry, name)
        return (grads, lands, send_sems, recv_sems), carry

    def chips_start(pair_flight, after, carry, name):
        grads, landed = _rs_wait(_pair_copies, *pair_flight, after, name + "_pair_wait")
        pair_sums = [_pair_add(g, ld) for g, ld in zip(grads, landed)]
        lands = [lax.empty((3,) + p.shape[1:], WIRE_DTYPE) for p in pair_sums]
        send_sems, recv_sems, pair_sums, lands, carry = _rs_start(_rs_copies, 3, pair_sums, lands, carry, name)
        return (pair_sums, lands, send_sems, recv_sems), carry

    in_flight = []
    mix_pairs = None
    for l in reversed(range(L)):
        w_p, w_o_l, w_up_t, w_dn, w_in_t = wb[l]
        xs, h1, proj, a_out, u1, u3, att, att_f32, ya, yb, yc, merged, mixed, x1, h2, up, act, f = saved[l]
        df, small["g4"][l] = _rms_bwd(f, vec(ln_mlp_post, l), dxo, None, MXU_DTYPE, "rms_bwd_post_mlp")
        d_up = _mm(df, w_dn, "NT", (MXU_DTYPE,), "mm_d_up", epilogue=_relu2_bwd, extras=(up,))
        g_dn = _mm(act, df, "TN", (WIRE_DTYPE,), "mm_g_down", tm=512, tn=2048)
        if mix_pairs is not None:
            flight, d_up = chips_start(mix_pairs, [g_dn], d_up, f"rs_start_mix_{l + 1}")
            in_flight.append((l + 1, ("p", "o", "in"), flight))
        dh2 = _mm(d_up, w_up_t, "NN", (F32,), "mm_dh2", tm=1024, tn=1024, tk=4096)
        g_up = _mm(d_up, h2, "TN", (WIRE_DTYPE,), "mm_g_up", tm=512, tn=2048)
        mlp_pairs, dh2 = pair_start([g_up, g_dn], dh2, f"rs_pair_mlp_{l}")
        dx1, small["g3"][l] = _rms_bwd(x1, vec(ln_mlp_pre, l), dh2, dxo, F32, "rms_bwd_pre_mlp")
        dmixed, small["g2"][l] = _rms_bwd(mixed, vec(ln_mix_post, l), dx1, None, MXU_DTYPE, "rms_bwd_post_mix")
        dmerged = _mm(dmixed, w_o_l, "NT", (F32,), "mm_dmerged")
        g_o = _mm(merged, dmixed, "TN", (WIRE_DTYPE,), "mm_g_o", tm=512, tn=2048)
        flight, dmerged = chips_start(mlp_pairs, [g_o], dmerged, f"rs_start_mlp_{l}")
        in_flight.append((l, ("up", "dn"), flight))
        dya, dyb, dyc, dgla, dglb, dglc = _gate_bwd(dmerged, proj, ya, yb, yc)
        d_a_out, d_att, d_u3, g_p = _branch_proj_bwd(dya, dyb, dyc, a_out, att, u3, w_p)
        d_scb, d_scc, d_scu, small["wa"][l] = _branch_a_bwd(d_a_out, proj, wa_full[l], U)
        d_u1, small["ng"][l], small["nb"][l] = _branch_c_norm_bwd(d_u3, u1, vec(norm_c_g, l), vec(norm_c_b, l))
        d_cfa, d_cfg, small["wc"][l], small["cb"][l] = _branch_c_conv_bwd(d_u1, proj, wc_full[l], U)
        dq, dk, dv = _attn_bwd(proj, att_f32, d_att, U)
        dproj = jnp.concatenate([dq, dk, dv, d_scb, d_scc, d_scu, d_cfa, d_cfg, dgla, dglb, dglc], axis=1)
        dh1 = _mm(dproj, w_in_t, "NN", (F32,), "mm_dh1", tm=1024, tk=46 * LANE)
        g_in = _mm(dproj, h1, "TN", (WIRE_DTYPE,), "mm_g_in", tm=512, tn=2048)
        dxo, small["g1"][l] = _rms_bwd(xs, vec(ln_mix_pre, l), dh1, dx1, F32, "rms_bwd_pre_mix")
        carry = dxo if l > 0 else jnp.zeros((8, LANE), F32)
        mix_pairs, carry = pair_start([g_p, g_o, g_in], carry, f"rs_pair_mix_{l}")
        if l > 0:
            dxo = carry
    grad_x = dxo[None]
    flight, carry = chips_start(mix_pairs, [carry], jnp.zeros((8, LANE), F32), "rs_start_mix_0")
    in_flight.append((0, ("p", "o", "in"), flight))
    big = {"w_in": (w_in, m_w_in, v_w_in), "proj_a": (proj_a, m_proj_a, v_proj_a), "proj_b": (proj_b, m_proj_b, v_proj_b),
           "proj_c": (proj_c, m_proj_c, v_proj_c), "w_o": (w_o, m_w_o, v_w_o), "w_up": (w_up, m_w_up, v_w_up),
           "w_down": (w_down, m_w_down, v_w_down)}
    done = {k: None for k in big}
    for n, (l, keys, flight) in enumerate(in_flight):
        after = [carry]
        if n == len(in_flight) - 1:
            after += [done[k][3] for k in big if done[k] is not None]
        pair_sums, lands = _rs_wait(_rs_copies, *flight, after, f"rs_wait_{keys[0]}_{l}")
        g = dict(zip(keys, zip(pair_sums, lands)))
        if "up" in g:
            layer_grads = {"w_up": (g["up"], 0), "w_down": (g["dn"], None)}
        else:
            layer_grads = {"w_in": (g["in"], 0), "proj_b": (g["p"], 0), "proj_a": (g["p"], 2 * U), "proj_c": (g["p"], 3 * U),
                           "w_o": (g["o"], None)}
        for k, ((p, ld), col_off) in layer_grads.items():
            w, m, v = big[k]
            done[k] = _adamw_layer(w, p, ld, m, v, l, done[k], col_off)

    order = ("g1", "g2", "g3", "g4", "cb", "ng", "nb", "wa", "wc")
    parts = [jnp.stack(small[k]).reshape(-1) for k in order]
    flat = jnp.concatenate(parts)
    n_flat = flat.shape[0]
    pad = (-n_flat) % (8 * LANE)
    flat = jnp.pad(flat, (0, pad)).reshape(-1, LANE)
    tot = _all_gather_small(flat, True).reshape(-1)[:n_flat]
    red, pos = {}, 0
    for k, p in zip(order, parts):
        red[k] = tot[pos:pos + p.shape[0]]
        pos += p.shape[0]
    g_ln_mix_pre, g_ln_mix_post = red["g1"].reshape(L, D), red["g2"].reshape(L, D)
    g_ln_mlp_pre, g_ln_mlp_post = red["g3"].reshape(L, D), red["g4"].reshape(L, D)
    g_conv_c_b, g_norm_c_g, g_norm_c_b = red["cb"].reshape(L, U), red["ng"].reshape(L, U), red["nb"].reshape(L, U)
    g_conv_a_w = lax.dynamic_slice_in_dim(red["wa"].reshape(L, SC_WIDTH, U), me * cu, cu, axis=2)
    g_conv_c_w = lax.dynamic_slice_in_dim(red["wc"].reshape(L, CF_WIDTH, U), me * cu, cu, axis=2)

    small_w = {"ln_mix_pre": (ln_mix_pre, g_ln_mix_pre, m_ln_mix_pre, v_ln_mix_pre),
               "ln_mix_post": (ln_mix_post, g_ln_mix_post, m_ln_mix_post, v_ln_mix_post),
               "ln_mlp_pre": (ln_mlp_pre, g_ln_mlp_pre, m_ln_mlp_pre, v_ln_mlp_pre),
               "ln_mlp_post": (ln_mlp_post, g_ln_mlp_post, m_ln_mlp_post, v_ln_mlp_post),
               "conv_a_w": (conv_a_w, g_conv_a_w, m_conv_a_w, v_conv_a_w), "conv_c_w": (conv_c_w, g_conv_c_w, m_conv_c_w, v_conv_c_w),
               "conv_c_b": (conv_c_b, g_conv_c_b, m_conv_c_b, v_conv_c_b), "norm_c_g": (norm_c_g, g_norm_c_g, m_norm_c_g, v_norm_c_g),
               "norm_c_b": (norm_c_b, g_norm_c_b, m_norm_c_b, v_norm_c_b)}
    for k, (w, g, m, v) in small_w.items():
        done[k] = (g,) + _adamw(w, g, m, v)
    names = ("ln_mix_pre", "ln_mix_post", "ln_mlp_pre", "ln_mlp_post", "w_in", "conv_a_w", "proj_a", "proj_b", "conv_c_w", "conv_c_b",
             "norm_c_g", "norm_c_b", "proj_c", "w_o", "w_up", "w_down")
    return (loss, grad_x, *[done[k][0] for k in names], *[done[k][1] for k in names], *[done[k][2] for k in names],
            *[done[k][3] for k in names])
```

```python
import functools

import jax
import jax.numpy as jnp
from jax import lax
from jax.experimental import pallas as pl
from jax.experimental.pallas import tpu as pltpu

F32 = jnp.float32
MXU_DTYPE = jnp.bfloat16
WIRE_DTYPE = jnp.bfloat16
MESH = pl.DeviceIdType.MESH
ANY = pl.BlockSpec(memory_space=pl.ANY)
HBM = pl.BlockSpec(memory_space=pltpu.HBM)
SEM = pl.BlockSpec(memory_space=pltpu.SEMAPHORE)
EFFECT = pltpu.SideEffectType.DATAFLOW_SIDE_EFFECTING

N_DEV = 8
HEAD_DIM = 128
RMS_EPS = 1e-6
LN_EPS = 1e-5
SC_WIDTH = 3
CF_WIDTH = 31
CONV_PAD = 32
ADAM_LR, ADAM_B1, ADAM_B2, ADAM_EPS, ADAM_WD, ADAM_STEP = 0.001, 0.9, 0.999, 1e-08, 0.01, 10
VMEM_LIMIT = 56 * 1024 * 1024
LANE = 128


def _cp(sem=None, **kw):
    return pltpu.CompilerParams(dimension_semantics=sem, vmem_limit_bytes=VMEM_LIMIT, **kw)


def _sigmoid(x):
    return 1.0 / (1.0 + jnp.exp(-x))


def _row_tile(rows, want):
    t = min(rows, want)
    while rows % t:
        t //= 2
    return t


_DN = {"NN": (((1,), (0,)), ((), ())), "NT": (((1,), (1,)), ((), ())), "TN": (((0,), (0,)), ((), ()))}


def _mm(a, b, mode, out_dtypes, name, *, a_view=None, b_view=None, tm=2048, tn=512, tk=2048, epilogue=None, extras=()):
    a_view = a_view or (0, 0) + tuple(a.shape)
    b_view = b_view or (0, 0) + tuple(b.shape)
    ar, ac, an, am = a_view
    br, bc, bn, bm = b_view
    if mode == "NN":
        M, K, K2, N = an, am, bn, bm
    elif mode == "NT":
        M, K, N, K2 = an, am, bn, bm
    else:
        K, M, K2, N = an, am, bn, bm
    assert K == K2, (name, a_view, b_view)
    tm, tn, tk = _row_tile(M, tm), _row_tile(N, tn), _row_tile(K, tk)
    (a_m_off, a_k_off) = (ac, ar) if mode == "TN" else (ar, ac)
    (b_n_off, b_k_off) = (br, bc) if mode == "NT" else (bc, br)
    while a_m_off % tm:
        tm //= 2
    while b_n_off % tn:
        tn //= 2
    while a_k_off % tk or b_k_off % tk:
        tk //= 2
    nk = K // tk
    a_blk = (tk, tm) if mode == "TN" else (tm, tk)
    b_blk = (tn, tk) if mode == "NT" else (tk, tn)
    assert ar % a_blk[0] == 0 and ac % a_blk[1] == 0, (name, a_view, a_blk)
    assert br % b_blk[0] == 0 and bc % b_blk[1] == 0, (name, b_view, b_blk)
    ao, bo = (ar // a_blk[0], ac // a_blk[1]), (br // b_blk[0], bc // b_blk[1])
    if mode == "TN":
        a_spec = pl.BlockSpec(a_blk, lambda i, j, k: (ao[0] + k, ao[1] + i))
    else:
        a_spec = pl.BlockSpec(a_blk, lambda i, j, k: (ao[0] + i, ao[1] + k))
    if mode == "NT":
        b_spec = pl.BlockSpec(b_blk, lambda i, j, k: (bo[0] + j, bo[1] + k))
    else:
        b_spec = pl.BlockSpec(b_blk, lambda i, j, k: (bo[0] + k, bo[1] + j))
    o_spec = pl.BlockSpec((tm, tn), lambda i, j, k: (i, j))
    n_ex, n_out = len(extras), len(out_dtypes)
    dn = _DN[mode]

    def body(*refs):
        a_ref, b_ref = refs[:2]
        ex_refs = refs[2:2 + n_ex]
        o_refs = refs[2 + n_ex:2 + n_ex + n_out]
        p = lax.dot_general(a_ref[...], b_ref[...], dn, preferred_element_type=F32)

        def finish(acc):
            outs = epilogue(acc, *[r[...] for r in ex_refs]) if epilogue else (acc,)
            for o_ref, o in zip(o_refs, outs):
                o_ref[...] = o.astype(o_ref.dtype)

        if nk == 1:
            finish(p)
        else:
            acc_ref = refs[-1]
            k = pl.program_id(2)

            @pl.when(k == 0)
            def _():
                acc_ref[...] = p

            @pl.when(k > 0)
            def _():
                acc_ref[...] += p

            @pl.when(k == nk - 1)
            def _():
                finish(acc_ref[...])

    outs = pl.pallas_call(
        body, name=name, grid=(M // tm, N // tn, nk),
        in_specs=[a_spec, b_spec] + [o_spec] * n_ex, out_specs=[o_spec] * n_out,
        out_shape=[jax.ShapeDtypeStruct((M, N), d) for d in out_dtypes],
        scratch_shapes=[pltpu.VMEM((tm, tn), F32)] if nk > 1 else [],
        compiler_params=_cp(("parallel", "parallel", "arbitrary")),
    )(a, b, *extras)
    return outs[0] if n_out == 1 else outs


def _rms_fwd(x, g, name):
    S, D = x.shape
    tr = _row_tile(S, 256)

    def body(x_ref, g_ref, h_ref):
        xv = x_ref[...]
        r = lax.rsqrt(jnp.mean(xv * xv, axis=-1, keepdims=True) + RMS_EPS)
        h_ref[...] = ((xv * r) * g_ref[...]).astype(h_ref.dtype)

    return pl.pallas_call(
        body, name=name, grid=(S // tr,),
        in_specs=[pl.BlockSpec((tr, D), lambda i: (i, 0)), pl.BlockSpec((1, D), lambda i: (0, 0))],
        out_specs=pl.BlockSpec((tr, D), lambda i: (i, 0)),
        out_shape=jax.ShapeDtypeStruct((S, D), MXU_DTYPE), compiler_params=_cp(("parallel",)),
    )(x, g)


def _resid_post(xres, y, g_post, g_next, name):
    S, D = y.shape
    tr = _row_tile(S, 256)
    has_next = g_next is not None

    def body(*refs):
        xr_ref, y_ref, gp_ref = refs[:3]
        yv = y_ref[...]
        r = lax.rsqrt(jnp.mean(yv * yv, axis=-1, keepdims=True) + RMS_EPS)
        xn = xr_ref[...] + (yv * r) * gp_ref[...]
        if has_next:
            gn_ref, xo_ref, h_ref = refs[3:]
            r2 = lax.rsqrt(jnp.mean(xn * xn, axis=-1, keepdims=True) + RMS_EPS)
            h_ref[...] = ((xn * r2) * gn_ref[...]).astype(h_ref.dtype)
        else:
            xo_ref = refs[3]
        xo_ref[...] = xn

    row = pl.BlockSpec((tr, D), lambda i: (i, 0))
    vec = pl.BlockSpec((1, D), lambda i: (0, 0))
    outs = pl.pallas_call(
        body, name=name, grid=(S // tr,),
        in_specs=[row, row, vec] + ([vec] if has_next else []),
        out_specs=[row] + ([row] if has_next else []),
        out_shape=[jax.ShapeDtypeStruct((S, D), F32)] + ([jax.ShapeDtypeStruct((S, D), MXU_DTYPE)] if has_next else []),
        compiler_params=_cp(("parallel",)),
    )(xres, y, g_post, *([g_next] if has_next else []))
    return (outs[0], outs[1]) if has_next else (outs[0], None)


def _rms_bwd(xin, g, dy, dres, out_dtype, name):
    S, D = xin.shape
    tr = _row_tile(S, 256)
    has_res = dres is not None

    def body(*refs):
        x_ref, g_ref, dy_ref = refs[:3]
        dx_ref, dg_ref = refs[-2:]
        xv, dyv = x_ref[...], dy_ref[...].astype(F32)
        r = lax.rsqrt(jnp.mean(xv * xv, axis=-1, keepdims=True) + RMS_EPS)
        n = xv * r
        dyg = dyv * g_ref[...]
        dx = r * (dyg - n * jnp.mean(dyg * n, axis=-1, keepdims=True))
        if has_res:
            dx = dx + refs[3][...]
        dx_ref[...] = dx.astype(dx_ref.dtype)

        @pl.when(pl.program_id(0) == 0)
        def _():
            dg_ref[...] = jnp.zeros_like(dg_ref)

        dg_ref[...] += jnp.sum(dyv * n, axis=0, keepdims=True)

    row = pl.BlockSpec((tr, D), lambda i: (i, 0))
    vec = pl.BlockSpec((1, D), lambda i: (0, 0))
    return pl.pallas_call(
        body, name=name, grid=(S // tr,),
        in_specs=[row, vec, row] + ([row] if has_res else []), out_specs=[row, vec],
        out_shape=[jax.ShapeDtypeStruct((S, D), out_dtype), jax.ShapeDtypeStruct((1, D), F32)],
        compiler_params=_cp(("arbitrary",)),
    )(xin, g, dy, *([dres] if has_res else []))


def _loss_head(y, target):
    S, D = y.shape
    tr = _row_tile(S, 256)

    def body(y_ref, t_ref, dy_ref, l_ref):
        e = y_ref[...] - t_ref[...]
        dy_ref[...] = e * (1.0 / D)

        @pl.when(pl.program_id(0) == 0)
        def _():
            l_ref[...] = jnp.zeros_like(l_ref)

        l_ref[...] += 0.5 * jnp.sum(jnp.mean(e * e, axis=-1, keepdims=True), axis=0, keepdims=True)

    row = pl.BlockSpec((tr, D), lambda i: (i, 0))
    return pl.pallas_call(
        body, name="loss_head", grid=(S // tr,), in_specs=[row, row],
        out_specs=[row, pl.BlockSpec((1, LANE), lambda i: (0, 0))],
        out_shape=[jax.ShapeDtypeStruct((S, D), F32), jax.ShapeDtypeStruct((1, LANE), F32)],
        compiler_params=_cp(("arbitrary",)),
    )(y, target)


def _gate_specs(S, U, tr):
    gl = [pl.BlockSpec((tr, U), functools.partial(lambda i, j, o: (i, o + j), o=o)) for o in (11, 15, 19)]
    return gl, pl.BlockSpec((tr, U), lambda i, j: (i, j))


def _merge_fwd(proj, a_out, att, u3, w_p):
    S, U = a_out.shape
    D = 4 * U
    tm = _row_tile(S, 1024)

    def body(ga_ref, gb_ref, gc_ref, a_ref, b_ref, c_ref, w_ref, m_ref, ya_ref, yb_ref, yc_ref):
        yb = lax.dot_general(b_ref[...], w_ref[:, 0:2 * U], _DN["NT"], preferred_element_type=F32)
        ya = lax.dot_general(a_ref[...], w_ref[:, 2 * U:3 * U], _DN["NT"], preferred_element_type=F32)
        yc = lax.dot_general(c_ref[...], w_ref[:, 3 * U:4 * U], _DN["NT"], preferred_element_type=F32)
        m_ref[...] = (_sigmoid(ga_ref[...]) * ya + _sigmoid(gb_ref[...]) * yb + _sigmoid(gc_ref[...]) * yc).astype(m_ref.dtype)
        ya_ref[...] = ya.astype(ya_ref.dtype)
        yb_ref[...] = yb.astype(yb_ref.dtype)
        yc_ref[...] = yc.astype(yc_ref.dtype)

    gl, blk = _gate_specs(S, U, tm)
    rows = lambda k: pl.BlockSpec((tm, k), lambda i, j: (i, 0))
    return pl.pallas_call(
        body, name="merge_fwd", grid=(S // tm, 4),
        in_specs=gl + [rows(U), rows(2 * U), rows(U), pl.BlockSpec((U, D), lambda i, j: (j, 0))], out_specs=[blk] * 4,
        out_shape=[jax.ShapeDtypeStruct((S, D), MXU_DTYPE)] * 4, compiler_params=_cp(("parallel", "parallel")),
    )(proj, proj, proj, a_out, att, u3, w_p)


def _gate_bwd(dm, proj, ya, yb, yc):
    S, D = ya.shape
    U = D // 4
    tr = _row_tile(S, 256)

    def body(dm_ref, ga_ref, gb_ref, gc_ref, ya_ref, yb_ref, yc_ref, da_ref, db_ref, dc_ref, la_ref, lb_ref, lc_ref):
        d = dm_ref[...]
        for g_ref, y_ref, dy_ref, dl_ref in ((ga_ref, ya_ref, da_ref, la_ref), (gb_ref, yb_ref, db_ref, lb_ref),
                                             (gc_ref, yc_ref, dc_ref, lc_ref)):
            g = _sigmoid(g_ref[...])
            dy_ref[...] = (d * g).astype(dy_ref.dtype)
            dl_ref[...] = (d * y_ref[...] * g * (1.0 - g)).astype(dl_ref.dtype)

    gl, blk = _gate_specs(S, U, tr)
    return pl.pallas_call(
        body, name="gate_bwd", grid=(S // tr, 4), in_specs=[blk] + gl + [blk] * 3, out_specs=[blk] * 6,
        out_shape=[jax.ShapeDtypeStruct((S, D), MXU_DTYPE)] * 6, compiler_params=_cp(("parallel", "parallel")),
    )(dm, proj, proj, proj, ya, yb, yc)


def _branch_proj_bwd(dya, dyb, dyc, a_out, att, u3, w_p):
    S, U = a_out.shape
    D = 4 * U
    tm = _row_tile(S, 512)
    tn = _row_tile(D, 512)

    def d_body(da_ref, db_ref, dc_ref, w_ref, oa_ref, ob_ref, oc_ref):
        ob_ref[...] = jnp.dot(db_ref[...], w_ref[:, 0:2 * U], preferred_element_type=F32)
        oa_ref[...] = jnp.dot(da_ref[...], w_ref[:, 2 * U:3 * U], preferred_element_type=F32)
        oc_ref[...] = jnp.dot(dc_ref[...], w_ref[:, 3 * U:4 * U], preferred_element_type=F32)

    row = lambda k: pl.BlockSpec((tm, k), lambda i: (i, 0))
    d_a_out, d_att, d_u3 = pl.pallas_call(
        d_body, name="branch_proj_bwd_act", grid=(S // tm,),
        in_specs=[row(D), row(D), row(D), pl.BlockSpec((D, D), lambda i: (0, 0))], out_specs=[row(U), row(2 * U), row(U)],
        out_shape=[jax.ShapeDtypeStruct((S, U), F32), jax.ShapeDtypeStruct((S, 2 * U), F32), jax.ShapeDtypeStruct((S, U), F32)],
        compiler_params=_cp(("parallel",)),
    )(dya, dyb, dyc, w_p)

    def g_body(da_ref, db_ref, dc_ref, a_ref, b_ref, c_ref, o_ref):
        o_ref[:, 0:2 * U] = lax.dot_general(db_ref[...], b_ref[...], _DN["TN"], preferred_element_type=F32).astype(o_ref.dtype)
        o_ref[:, 2 * U:3 * U] = lax.dot_general(da_ref[...], a_ref[...], _DN["TN"], preferred_element_type=F32).astype(o_ref.dtype)
        o_ref[:, 3 * U:4 * U] = lax.dot_general(dc_ref[...], c_ref[...], _DN["TN"], preferred_element_type=F32).astype(o_ref.dtype)

    col = pl.BlockSpec((S, tn), lambda i: (0, i))
    whole = lambda k: pl.BlockSpec((S, k), lambda i: (0, 0))
    g_p = pl.pallas_call(
        g_body, name="branch_proj_bwd_w", grid=(D // tn,),
        in_specs=[col, col, col, whole(U), whole(2 * U), whole(U)], out_specs=pl.BlockSpec((tn, D), lambda i: (i, 0)),
        out_shape=jax.ShapeDtypeStruct((D, D), WIRE_DTYPE), compiler_params=_cp(("parallel",)),
    )(dya, dyb, dyc, a_out, att, u3)
    return d_a_out, d_att, d_u3, g_p


def _chunks(S):
    r = _row_tile(S, 256)
    return [(r0, r) for r0 in range(0, S, r)]


def _conv_causal(front_ref, w_ref, K, r0, R):
    acc = None
    for j in range(K):
        term = w_ref[pl.ds(K - 1 - j, 1), :] * front_ref[pl.ds(CONV_PAD + r0 - j, R), :]
        acc = term if acc is None else acc + term
    return acc


def _conv_anticausal(back_ref, w_ref, K, r0, R):
    acc = None
    for j in range(K):
        term = w_ref[pl.ds(K - 1 - j, 1), :] * back_ref[pl.ds(r0 + j, R), :]
        acc = term if acc is None else acc + term
    return acc


def _conv_wgrad(front_ref, back_ref, dw_ref, K, S):
    for j in range(K):
        tot = None
        for r0, R in _chunks(S):
            part = jnp.sum(back_ref[pl.ds(r0, R), :] * front_ref[pl.ds(CONV_PAD + r0 - j, R), :], axis=0, keepdims=True)
            tot = part if tot is None else tot + part
        dw_ref[pl.ds(K - 1 - j, 1), :] = tot


def _col(S, cw, unit_off):
    return pl.BlockSpec((S, cw), functools.partial(lambda cb, o: (0, o + cb), o=unit_off))


def _branch_a_fwd(proj, wa, U):
    S = proj.shape[0]
    cw = min(LANE, U)
    nb = U // cw
    K = SC_WIDTH

    def body(b_ref, c_ref, u_ref, w_ref, o_ref, front):
        front[pl.ds(0, CONV_PAD), :] = jnp.zeros((CONV_PAD, cw), F32)
        front[pl.ds(CONV_PAD, S), :] = c_ref[...] * u_ref[...]
        for r0, R in _chunks(S):
            o_ref[pl.ds(r0, R), :] = (b_ref[pl.ds(r0, R), :] * _conv_causal(front, w_ref, K, r0, R)).astype(o_ref.dtype)

    return pl.pallas_call(
        body, name="branch_a_fwd", grid=(nb,),
        in_specs=[_col(S, cw, 6 * nb), _col(S, cw, 7 * nb), _col(S, cw, 8 * nb), pl.BlockSpec((K, cw), lambda cb: (0, cb))],
        out_specs=pl.BlockSpec((S, cw), lambda cb: (0, cb)), out_shape=jax.ShapeDtypeStruct((S, U), MXU_DTYPE),
        scratch_shapes=[pltpu.VMEM((S + CONV_PAD, cw), F32)], compiler_params=_cp(("parallel",)),
    )(proj, proj, proj, wa)


def _branch_a_bwd(d_out, proj, wa, U):
    S = proj.shape[0]
    cw = min(LANE, U)
    nb = U // cw
    K = SC_WIDTH

    def body(d_ref, b_ref, c_ref, u_ref, w_ref, db_ref, dc_ref, du_ref, dw_ref, front, back):
        front[pl.ds(0, CONV_PAD), :] = jnp.zeros((CONV_PAD, cw), F32)
        front[pl.ds(CONV_PAD, S), :] = c_ref[...] * u_ref[...]
        back[pl.ds(S, CONV_PAD), :] = jnp.zeros((CONV_PAD, cw), F32)
        back[pl.ds(0, S), :] = d_ref[...] * b_ref[...]
        for r0, R in _chunks(S):
            rows = pl.ds(r0, R)
            db_ref[rows, :] = (d_ref[rows, :] * _conv_causal(front, w_ref, K, r0, R)).astype(db_ref.dtype)
            d_ai = _conv_anticausal(back, w_ref, K, r0, R)
            dc_ref[rows, :] = (d_ai * u_ref[rows, :]).astype(dc_ref.dtype)
            du_ref[rows, :] = (d_ai * c_ref[rows, :]).astype(du_ref.dtype)
        _conv_wgrad(front, back, dw_ref, K, S)

    blk = pl.BlockSpec((S, cw), lambda cb: (0, cb))
    wblk = pl.BlockSpec((K, cw), lambda cb: (0, cb))
    return pl.pallas_call(
        body, name="branch_a_bwd", grid=(nb,),
        in_specs=[blk, _col(S, cw, 6 * nb), _col(S, cw, 7 * nb), _col(S, cw, 8 * nb), wblk],
        out_specs=[blk, blk, blk, wblk],
        out_shape=[jax.ShapeDtypeStruct((S, U), MXU_DTYPE)] * 3 + [jax.ShapeDtypeStruct((K, U), F32)],
        scratch_shapes=[pltpu.VMEM((S + CONV_PAD, cw), F32)] * 2, compiler_params=_cp(("parallel",)),
    )(d_out, proj, proj, proj, wa)


def _branch_c_conv_fwd(proj, wc, cb, U):
    S = proj.shape[0]
    cw = min(LANE, U)
    nb = U // cw
    K = CF_WIDTH

    def body(a_ref, g_ref, w_ref, bias_ref, o_ref, front):
        front[pl.ds(0, CONV_PAD), :] = jnp.zeros((CONV_PAD, cw), F32)
        front[pl.ds(CONV_PAD, S), :] = a_ref[...] * _sigmoid(g_ref[...])
        for r0, R in _chunks(S):
            o_ref[pl.ds(r0, R), :] = _conv_causal(front, w_ref, K, r0, R) + bias_ref[...]

    return pl.pallas_call(
        body, name="branch_c_conv_fwd", grid=(nb,),
        in_specs=[_col(S, cw, 9 * nb), _col(S, cw, 10 * nb), pl.BlockSpec((K, cw), lambda c: (0, c)),
                  pl.BlockSpec((1, cw), lambda c: (0, c))],
        out_specs=pl.BlockSpec((S, cw), lambda c: (0, c)), out_shape=jax.ShapeDtypeStruct((S, U), F32),
        scratch_shapes=[pltpu.VMEM((S + CONV_PAD, cw), F32)], compiler_params=_cp(("parallel",)),
    )(proj, proj, wc, cb)


def _branch_c_conv_bwd(d_u1, proj, wc, U):
    S = proj.shape[0]
    cw = min(LANE, U)
    nb = U // cw
    K = CF_WIDTH

    def body(d_ref, a_ref, g_ref, w_ref, da_ref, dg_ref, dw_ref, dbias_ref, front, back):
        sg = _sigmoid(g_ref[...])
        front[pl.ds(0, CONV_PAD), :] = jnp.zeros((CONV_PAD, cw), F32)
        front[pl.ds(CONV_PAD, S), :] = a_ref[...] * sg
        back[pl.ds(S, CONV_PAD), :] = jnp.zeros((CONV_PAD, cw), F32)
        back[pl.ds(0, S), :] = d_ref[...]
        dbias_ref[...] = jnp.sum(d_ref[...], axis=0, keepdims=True)
        for r0, R in _chunks(S):
            rows = pl.ds(r0, R)
            d_u0 = _conv_anticausal(back, w_ref, K, r0, R)
            s = _sigmoid(g_ref[rows, :])
            da_ref[rows, :] = (d_u0 * s).astype(da_ref.dtype)
            dg_ref[rows, :] = (d_u0 * a_ref[rows, :] * s * (1.0 - s)).astype(dg_ref.dtype)
        _conv_wgrad(front, back, dw_ref, K, S)

    blk = pl.BlockSpec((S, cw), lambda c: (0, c))
    wblk = pl.BlockSpec((K, cw), lambda c: (0, c))
    vblk = pl.BlockSpec((1, cw), lambda c: (0, c))
    return pl.pallas_call(
        body, name="branch_c_conv_bwd", grid=(nb,),
        in_specs=[blk, _col(S, cw, 9 * nb), _col(S, cw, 10 * nb), wblk], out_specs=[blk, blk, wblk, vblk],
        out_shape=[jax.ShapeDtypeStruct((S, U), MXU_DTYPE)] * 2 + [jax.ShapeDtypeStruct((K, U), F32), jax.ShapeDtypeStruct((1, U), F32)],
        scratch_shapes=[pltpu.VMEM((S + CONV_PAD, cw), F32)] * 2, compiler_params=_cp(("parallel",)),
    )(d_u1, proj, proj, wc)


def _branch_c_norm_fwd(u1, ng, nbias):
    S, U = u1.shape
    tr = _row_tile(S, 256)

    def body(u_ref, g_ref, b_ref, o_ref):
        u = u_ref[...]
        mu = jnp.mean(u, axis=-1, keepdims=True)
        var = jnp.mean(jnp.square(u - mu), axis=-1, keepdims=True)
        u2 = ((u - mu) * lax.rsqrt(var + LN_EPS)) * g_ref[...] + b_ref[...]
        o_ref[...] = (u2 * _sigmoid(u2)).astype(o_ref.dtype)

    row = pl.BlockSpec((tr, U), lambda i: (i, 0))
    vec = pl.BlockSpec((1, U), lambda i: (0, 0))
    return pl.pallas_call(
        body, name="branch_c_norm_fwd", grid=(S // tr,), in_specs=[row, vec, vec], out_specs=row,
        out_shape=jax.ShapeDtypeStruct((S, U), MXU_DTYPE), compiler_params=_cp(("parallel",)),
    )(u1, ng, nbias)


def _branch_c_norm_bwd(d_u3, u1, ng, nbias):
    S, U = u1.shape
    tr = _row_tile(S, 256)

    def body(d_ref, u_ref, g_ref, b_ref, du_ref, dg_ref, db_ref):
        u = u_ref[...]
        mu = jnp.mean(u, axis=-1, keepdims=True)
        var = jnp.mean(jnp.square(u - mu), axis=-1, keepdims=True)
        rstd = lax.rsqrt(var + LN_EPS)
        xh = (u - mu) * rstd
        u2 = xh * g_ref[...] + b_ref[...]
        s = _sigmoid(u2)
        d_u2 = d_ref[...] * (s * (1.0 + u2 * (1.0 - s)))
        d_xh = d_u2 * g_ref[...]
        du_ref[...] = rstd * (d_xh - jnp.mean(d_xh, axis=-1, keepdims=True) - xh * jnp.mean(d_xh * xh, axis=-1, keepdims=True))

        @pl.when(pl.program_id(0) == 0)
        def _():
            dg_ref[...] = jnp.zeros_like(dg_ref)
            db_ref[...] = jnp.zeros_like(db_ref)

        dg_ref[...] += jnp.sum(d_u2 * xh, axis=0, keepdims=True)
        db_ref[...] += jnp.sum(d_u2, axis=0, keepdims=True)

    row = pl.BlockSpec((tr, U), lambda i: (i, 0))
    vec = pl.BlockSpec((1, U), lambda i: (0, 0))
    return pl.pallas_call(
        body, name="branch_c_norm_bwd", grid=(S // tr,), in_specs=[row, row, vec, vec], out_specs=[row, vec, vec],
        out_shape=[jax.ShapeDtypeStruct((S, U), F32), jax.ShapeDtypeStruct((1, U), F32), jax.ShapeDtypeStruct((1, U), F32)],
        compiler_params=_cp(("arbitrary",)),
    )(d_u3, u1, ng, nbias)


def _tri(T, inclusive):
    j = lax.broadcasted_iota(jnp.int32, (T, T), 0)
    s = lax.broadcasted_iota(jnp.int32, (T, T), 1)
    return ((j >= s) if inclusive else (j > s)).astype(MXU_DTYPE)


def _split_dot(x, tri):
    if MXU_DTYPE == F32:
        return jnp.dot(x, tri, preferred_element_type=F32)
    hi = x.astype(MXU_DTYPE)
    lo = (x - hi.astype(F32)).astype(MXU_DTYPE)
    return jnp.dot(hi, tri, preferred_element_type=F32) + jnp.dot(lo, tri, preferred_element_type=F32)


def _sb_block(qb, kb, T, tri_strict, c_lf, diag):
    z = lax.dot_general(qb, kb, _DN["NT"], preferred_element_type=F32) * (HEAD_DIM ** -0.5)
    e = jnp.exp(-jnp.abs(z))
    lg = jnp.log(1.0 + e)
    log_beta = jnp.minimum(z, 0.0) - lg
    lf = jnp.minimum(-z, 0.0) - lg
    mask = None
    if diag:
        mask = lax.broadcasted_iota(jnp.int32, (T, T), 1) < lax.broadcasted_iota(jnp.int32, (T, T), 0)
        lf = jnp.where(mask, lf, 0.0)
    a = jnp.exp(log_beta + _split_dot(lf, tri_strict) + c_lf)
    if diag:
        a = jnp.where(mask, a, 0.0)
    return z, e, mask, lf, a


def _key_blocks(i, step, init):
    carry = step(i, init, True)
    return lax.fori_loop(1, i + 1, lambda jj, c: step(i - jj, c, False), carry)


def _attn_specs(S, U, h_blocks):
    nh = (2 * U) // HEAD_DIM
    return [pl.BlockSpec((S, HEAD_DIM), functools.partial(lambda h, o: (0, o + h), o=o * nh)) for o in range(h_blocks)]


def _attn_fwd(proj, U):
    S = proj.shape[0]
    nh = (2 * U) // HEAD_DIM
    T = _row_tile(S, 512)
    nq = S // T

    def body(q_ref, k_ref, v_ref, o_ref, of_ref, qs, ks, vs):
        qs[...] = q_ref[...].astype(MXU_DTYPE)
        ks[...] = k_ref[...].astype(MXU_DTYPE)
        vs[...] = v_ref[...].astype(MXU_DTYPE)
        tri = _tri(T, False)

        def q_loop(i, _):
            rows = pl.ds(pl.multiple_of(i * T, T), T)
            qb = qs[rows, :]

            def step(j, carry, diag):
                c_lf, acc = carry
                cols = pl.ds(pl.multiple_of(j * T, T), T)
                _, _, _, lf, a = _sb_block(qb, ks[cols, :], T, tri, c_lf, diag)
                acc = acc + jnp.dot(a.astype(MXU_DTYPE), vs[cols, :], preferred_element_type=F32)
                return c_lf + jnp.sum(lf, axis=1, keepdims=True), acc

            _, acc = _key_blocks(i, step, (jnp.zeros((T, 1), F32), jnp.zeros((T, HEAD_DIM), F32)))
            o_ref[rows, :] = acc.astype(o_ref.dtype)
            of_ref[rows, :] = acc
            return 0

        lax.fori_loop(0, nq, q_loop, 0)

    hblk = pl.BlockSpec((S, HEAD_DIM), lambda h: (0, h))
    return pl.pallas_call(
        body, name="attn_fwd", grid=(nh,), in_specs=_attn_specs(S, U, 3), out_specs=[hblk, hblk],
        out_shape=[jax.ShapeDtypeStruct((S, 2 * U), MXU_DTYPE), jax.ShapeDtypeStruct((S, 2 * U), F32)],
        scratch_shapes=[pltpu.VMEM((S, HEAD_DIM), MXU_DTYPE)] * 3, compiler_params=_cp(("parallel",)),
    )(proj, proj, proj)


def _attn_bwd(proj, att_f32, d_att, U):
    S = proj.shape[0]
    nh = (2 * U) // HEAD_DIM
    T = _row_tile(S, 512)
    nq = S // T
    scale = HEAD_DIM ** -0.5

    def body(q_ref, k_ref, v_ref, o_ref, do_ref, dq_ref, dk_ref, dv_ref, qs, ks, vs, dos, dka, dva):
        qs[...] = q_ref[...].astype(MXU_DTYPE)
        ks[...] = k_ref[...].astype(MXU_DTYPE)
        vs[...] = v_ref[...].astype(MXU_DTYPE)
        dos[...] = do_ref[...].astype(MXU_DTYPE)
        dka[...] = jnp.zeros_like(dka)
        dva[...] = jnp.zeros_like(dva)
        tri = _tri(T, False)
        tri_inc = _tri(T, True)

        def q_loop(i, _):
            rows = pl.ds(pl.multiple_of(i * T, T), T)
            qb = qs[rows, :]
            dob = dos[rows, :]
            delta = jnp.sum(dob.astype(F32) * o_ref[rows, :], axis=1, keepdims=True)

            def step(j, carry, diag):
                c_lf, c_g, dq = carry
                cols = pl.ds(pl.multiple_of(j * T, T), T)
                kb, vb = ks[cols, :], vs[cols, :]
                z, e, mask, lf, a = _sb_block(qb, kb, T, tri, c_lf, diag)
                a_mx = a.astype(MXU_DTYPE)
                d_a = lax.dot_general(dob, vb, _DN["NT"], preferred_element_type=F32)
                g = a_mx.astype(F32) * d_a
                prefix = delta - (_split_dot(g, tri_inc) + c_g)
                inv = 1.0 / (1.0 + e)
                beta = jnp.where(z >= 0.0, 1.0, e) * inv
                one_m_beta = jnp.where(z >= 0.0, e, 1.0) * inv
                dz = (g * one_m_beta - prefix * beta) * scale
                if diag:
                    dz = jnp.where(mask, dz, 0.0)
                dz = dz.astype(MXU_DTYPE)
                dq = dq + jnp.dot(dz, kb, preferred_element_type=F32)
                dka[cols, :] += lax.dot_general(dz, qb, _DN["TN"], preferred_element_type=F32)
                dva[cols, :] += lax.dot_general(a_mx, dob, _DN["TN"], preferred_element_type=F32)
                return c_lf + jnp.sum(lf, axis=1, keepdims=True), c_g + jnp.sum(g, axis=1, keepdims=True), dq

            zero = jnp.zeros((T, 1), F32)
            _, _, dq = _key_blocks(i, step, (zero, zero, jnp.zeros((T, HEAD_DIM), F32)))
            dq_ref[rows, :] = dq.astype(dq_ref.dtype)
            return 0

        lax.fori_loop(0, nq, q_loop, 0)
        dk_ref[...] = dka[...].astype(dk_ref.dtype)
        dv_ref[...] = dva[...].astype(dv_ref.dtype)

    hblk = pl.BlockSpec((S, HEAD_DIM), lambda h: (0, h))
    return pl.pallas_call(
        body, name="attn_bwd", grid=(nh,), in_specs=_attn_specs(S, U, 3) + [hblk, hblk], out_specs=[hblk] * 3,
        out_shape=[jax.ShapeDtypeStruct((S, 2 * U), MXU_DTYPE)] * 3,
        scratch_shapes=[pltpu.VMEM((S, HEAD_DIM), MXU_DTYPE)] * 4 + [pltpu.VMEM((S, HEAD_DIM), F32)] * 2,
        compiler_params=_cp(("parallel",)),
    )(proj, proj, proj, att_f32, d_att)


def _place():
    return lax.axis_index("x"), lax.axis_index("y"), lax.axis_index("c")


def _flip(v, bit):
    return 1 - v if bit else v


def _related(x, y, r):
    return _flip(x, r & 1), _flip(y, r >> 1)


def _own_rows(ref, dev):
    rows = ref.shape[0] // N_DEV
    return ref.at[pl.ds(pl.multiple_of(dev * rows, 16), rows), :]


def _my_block(n_blocks):
    def index(i):
        x, y, c = _place()
        return (4 * x + 2 * y + c) * n_blocks + i, 0
    return index


def _gathered(rows, C):
    return jax.ShapeDtypeStruct((N_DEV * rows, C), WIRE_DTYPE)


def _pack_plain(w, l, zero):
    _, rows, C = w.shape
    t = _shard_tile(rows)

    def body(w_ref, z_ref, o_ref):
        o_ref[...] = (w_ref[...] + z_ref[0:1, 0:1]).astype(o_ref.dtype)

    return pl.pallas_call(
        body, name="pack_plain", grid=(rows // t,),
        in_specs=[pl.BlockSpec((None, t, C), lambda i: (l, i, 0)), pl.BlockSpec((8, LANE), lambda i: (0, 0))],
        out_specs=pl.BlockSpec((t, C), _my_block(rows // t)), out_shape=_gathered(rows, C), compiler_params=_cp(("parallel",)),
    )(w, zero)


def _pack_transposed(ws, l, zero):
    rows = ws[0].shape[2]
    C = sum(w.shape[1] for w in ws)
    t = _row_tile(rows, 256)
    n = len(ws)

    def body(*refs):
        z_ref, o_ref = refs[n], refs[n + 1]
        col = 0
        for w_ref in refs[:n]:
            k = w_ref.shape[0]
            o_ref[:, col:col + k] = (w_ref[...] + z_ref[0:1, 0:1]).T.astype(o_ref.dtype)
            col += k

    return pl.pallas_call(
        body, name="pack_transposed", grid=(rows // t,),
        in_specs=[pl.BlockSpec((None, w.shape[1], t), lambda i: (l, 0, i)) for w in ws] + [pl.BlockSpec((8, LANE), lambda i: (0, 0))],
        out_specs=pl.BlockSpec((t, C), _my_block(rows // t)), out_shape=_gathered(rows, C), compiler_params=_cp(("parallel",)),
    )(*ws, zero)


def _pack_transposed_mxu(w, l, zero):
    _, K, rows = w.shape
    tk = _row_tile(K, 512)

    def body(w_ref, z_ref, o_ref, eye):
        @pl.when(pl.program_id(0) == 0)
        def _():
            eye[...] = (lax.broadcasted_iota(jnp.int32, (rows, rows), 0) == lax.broadcasted_iota(jnp.int32, (rows, rows), 1)).astype(eye.dtype)

        x = (w_ref[...] + z_ref[0:1, 0:1]).astype(MXU_DTYPE)
        o_ref[...] = lax.dot_general(eye[...], x, _DN["NT"], preferred_element_type=F32).astype(o_ref.dtype)

    def out_index(j):
        x, y, c = _place()
        return 4 * x + 2 * y + c, j

    return pl.pallas_call(
        body, name="pack_transposed_mxu", grid=(K // tk,),
        in_specs=[pl.BlockSpec((None, tk, rows), lambda j: (l, j, 0)), pl.BlockSpec((8, LANE), lambda j: (0, 0))],
        out_specs=pl.BlockSpec((rows, tk), out_index), out_shape=_gathered(rows, K),
        scratch_shapes=[pltpu.VMEM((rows, rows), MXU_DTYPE)], compiler_params=_cp(("arbitrary",)),
    )(w, zero)


def _hbm(a):
    return pltpu.with_memory_space_constraint(a, pltpu.HBM)


def _hbm_like(arrays):
    return tuple(pltpu.HBM(a.shape, a.dtype) for a in arrays)


def _dev(px, py, pc):
    return 4 * px + 2 * py + pc


def _block_copies(wb_refs, send_sems, recv_sems, slot, block_out, block_in, peer, outgoing):
    K = len(wb_refs)
    return [pltpu.make_async_remote_copy(
        src_ref=_own_rows(wb_ref, block_out), dst_ref=_own_rows(wb_ref, block_out if outgoing else block_in),
        send_sem=send_sems.at[slot * K + k], recv_sem=recv_sems.at[slot * K + k], device_id=peer, device_id_type=MESH)
        for k, wb_ref in enumerate(wb_refs)]


def _ag_stage1(wb_refs, send_sems, recv_sems, outgoing):
    x, y, c = _place()
    me = _dev(x, y, c)
    out = []
    for slot, peer in enumerate(((x, y, 1 - c), (1 - x, y, c), (x, 1 - y, c))):
        out += _block_copies(wb_refs, send_sems, recv_sems, slot, me, _dev(*peer), peer, outgoing)
    return out


def _ag_stage2(wb_refs, send_sems, recv_sems, outgoing):
    x, y, c = _place()
    via = ((1 - x) + c * (2 * x - 1), y + c * (1 - 2 * y))
    to = (x + c * (1 - 2 * x), (1 - y) + c * (2 * y - 1), c)
    out = _block_copies(wb_refs, send_sems, recv_sems, 0, _dev(*via, c), _dev(1 - x, 1 - y, c), to, outgoing)
    for slot, (px, py) in ((1, (1 - x, y)), (2, (x, 1 - y))):
        out += _block_copies(wb_refs, send_sems, recv_sems, slot, _dev(px, py, c), _dev(px, py, 1 - c), (x, y, 1 - c), outgoing)
    return out


def _ag_start(copies, wbs, carry, name):
    K = len(wbs)

    def body(*refs):
        send_sems, recv_sems = refs[K + 1:K + 3]
        for cp in copies(refs[:K], send_sems, recv_sems, True):
            cp.start()

    outs = pl.pallas_call(
        body, name=name,
        out_shape=(pltpu.SemaphoreType.DMA((3 * K,)), pltpu.SemaphoreType.DMA((3 * K,))) + _hbm_like(list(wbs) + [carry]),
        in_specs=(HBM,) * (K + 1), out_specs=(SEM, SEM) + (HBM,) * (K + 1), input_output_aliases={k: 2 + k for k in range(K + 1)},
        compiler_params=pltpu.CompilerParams(has_side_effects=EFFECT),
    )(*[_hbm(a) for a in wbs], _hbm(carry))
    return outs[0], outs[1], list(outs[2:2 + K]), outs[2 + K]


def _ag_wait(copies, wbs, send_sems, recv_sems, after, name):
    K = len(wbs)

    def body(*refs):
        for cp in copies(refs[:K], refs[K], refs[K + 1], False):
            cp.wait_send()
            cp.wait_recv()

    return list(pl.pallas_call(
        body, name=name, out_shape=_hbm_like(wbs),
        in_specs=(HBM,) * K + (SEM, SEM, ANY), out_specs=(HBM,) * K, input_output_aliases={k: k for k in range(K)},
        compiler_params=pltpu.CompilerParams(has_side_effects=EFFECT),
    )(*wbs, send_sems, recv_sems, after))


def _ag_finish(wbs):
    K = len(wbs)

    def body(*refs):
        ins, outs, tok_ref, send_sems, recv_sems = refs[:K], refs[K:2 * K], refs[2 * K], refs[2 * K + 1], refs[2 * K + 2]
        x, y, c = _place()
        sent = []
        for k in range(K):
            cp = pltpu.make_async_remote_copy(
                src_ref=_own_rows(ins[k], _dev(1 - x, 1 - y, c)), dst_ref=_own_rows(outs[k], _dev(1 - x, 1 - y, c)),
                send_sem=send_sems.at[k], recv_sem=recv_sems.at[k], device_id=(x, y, 1 - c), device_id_type=MESH)
            cp.start()
            sent.append(cp)
        for k in range(K):
            theirs = _own_rows(outs[k], _dev(1 - x, 1 - y, 1 - c))
            pltpu.make_async_remote_copy(src_ref=theirs, dst_ref=theirs, send_sem=send_sems.at[k], recv_sem=recv_sems.at[k],
                                         device_id=(x, y, 1 - c), device_id_type=MESH).wait_recv()
        for cp in sent:
            cp.wait_send()
        tok_ref[...] = jnp.zeros_like(tok_ref)

    outs = pl.pallas_call(
        body, name="ag_finish", in_specs=[ANY] * K, out_specs=[ANY] * K + [pl.BlockSpec(memory_space=pltpu.VMEM)],
        out_shape=[jax.ShapeDtypeStruct(a.shape, a.dtype) for a in wbs] + [jax.ShapeDtypeStruct((8, LANE), F32)],
        input_output_aliases={k: k for k in range(K)},
        scratch_shapes=[pltpu.SemaphoreType.DMA((K,))] * 2, compiler_params=_cp(),
    )(*wbs)
    return list(outs[:K]), outs[K]


def _pair_copies(g_refs, land_refs, send_sems, recv_sems):
    x, y, c = _place()
    K = len(g_refs)
    out = []
    for r in range(4):
        px, py = _related(x, y, r)
        for k in range(K):
            out.append(pltpu.make_async_remote_copy(
                src_ref=_own_rows(g_refs[k], 4 * px + 2 * py + (1 - c)), dst_ref=land_refs[k].at[r],
                send_sem=send_sems.at[r * K + k], recv_sem=recv_sems.at[r * K + k], device_id=(x, y, 1 - c), device_id_type=MESH))
    return out


def _shard_tile(rows):
    for t in (512, 736, 256, 128, 64, 32, 16, 8):
        if rows % t == 0:
            return t
    return rows


def _pair_add(grad, landed):
    _, rows, C = landed.shape
    t = _shard_tile(rows)

    def g_index(r, i):
        x, y, c = _place()
        px = jnp.where(r % 2 == 1, 1 - x, x)
        py = jnp.where(r // 2 == 1, 1 - y, y)
        return (4 * px + 2 * py + c) * (rows // t) + i, 0

    def body(a_ref, b_ref, o_ref):
        o_ref[...] = (a_ref[...].astype(F32) + b_ref[...].astype(F32)).astype(o_ref.dtype)

    slot = pl.BlockSpec((None, t, C), lambda r, i: (r, i, 0))
    return pl.pallas_call(
        body, name="pair_add", grid=(4, rows // t), in_specs=[pl.BlockSpec((t, C), g_index), slot], out_specs=slot,
        out_shape=jax.ShapeDtypeStruct((4, rows, C), grad.dtype), compiler_params=_cp(("parallel", "parallel")),
    )(grad, landed)


def _rs_copies(p_refs, land_refs, send_sems, recv_sems):
    x, y, c = _place()
    K = len(p_refs)
    return [pltpu.make_async_remote_copy(src_ref=p_refs[k].at[r], dst_ref=land_refs[k].at[r - 1], send_sem=send_sems.at[(r - 1) * K + k],
                                         recv_sem=recv_sems.at[(r - 1) * K + k], device_id=(*_related(x, y, r), c), device_id_type=MESH)
            for r in (1, 2, 3) for k in range(K)]


def _rs_start(copies, n_slots, srcs, lands, carry, name):
    K = len(srcs)

    def body(*refs):
        for cp in copies(refs[:K], refs[K:2 * K], refs[2 * K + 1], refs[2 * K + 2]):
            cp.start()

    n_thru = 2 * K + 1
    outs = pl.pallas_call(
        body, name=name,
        out_shape=(pltpu.SemaphoreType.DMA((n_slots * K,)), pltpu.SemaphoreType.DMA((n_slots * K,))) + _hbm_like(list(srcs) + list(lands) + [carry]),
        in_specs=(HBM,) * n_thru, out_specs=(SEM, SEM) + (HBM,) * n_thru, input_output_aliases={k: 2 + k for k in range(n_thru)},
        compiler_params=pltpu.CompilerParams(has_side_effects=EFFECT),
    )(*[_hbm(a) for a in list(srcs) + list(lands) + [carry]])
    return outs[0], outs[1], list(outs[2:2 + K]), list(outs[2 + K:2 + 2 * K]), outs[2 + 2 * K]


def _rs_wait(copies, srcs, lands, send_sems, recv_sems, after, name):
    K = len(srcs)

    def body(*refs):
        for cp in copies(refs[:K], refs[K:2 * K], refs[2 * K], refs[2 * K + 1]):
            cp.wait_send()
            cp.wait_recv()

    outs = pl.pallas_call(
        body, name=name, out_shape=_hbm_like(list(srcs) + list(lands)),
        in_specs=(HBM,) * (2 * K) + (SEM, SEM) + (ANY,) * len(after), out_specs=(HBM,) * (2 * K),
        input_output_aliases={k: k for k in range(2 * K)}, compiler_params=pltpu.CompilerParams(has_side_effects=EFFECT),
    )(*srcs, *lands, send_sems, recv_sems, *after)
    return list(outs[:K]), list(outs[K:])


def _all_gather_small(v, reduce):
    M, N = v.shape

    def body(x_ref, out_ref, sum_ref, send_sems, recv_sems, local_sem):
        x, y, c = _place()
        me, sibling = (x, y, c), (x, y, 1 - c)
        chips = [_related(x, y, r) for r in (1, 2, 3)]

        def rows(px, py, pc):
            return out_ref.at[pl.ds(pl.multiple_of((4 * px + 2 * py + pc) * M, 8), M), :]

        def copy(k, block, to, src=None):
            return pltpu.make_async_remote_copy(src_ref=rows(*block) if src is None else src, dst_ref=rows(*block),
                                                send_sem=send_sems.at[k], recv_sem=recv_sems.at[k], device_id=to, device_id_type=MESH)

        mine = pltpu.make_async_copy(x_ref, rows(*me), local_sem)
        mine.start()
        first = [copy(0, me, sibling, src=x_ref)]
        first += [copy(1 + j, me, (*chip, c), src=x_ref) for j, chip in enumerate(chips)]
        for cp in first:
            cp.start()
        passed = [copy(4 + j, (*chip, c), sibling) for j, chip in enumerate(chips)]
        for j, chip in enumerate(chips):
            copy(1 + j, (*chip, c), me).wait_recv()
            passed[j].start()
        copy(0, sibling, me).wait_recv()
        for j, chip in enumerate(chips):
            copy(4 + j, (*chip, 1 - c), me).wait_recv()
        for cp in first + passed:
            cp.wait_send()
        mine.wait()
        if reduce:
            tot = out_ref[pl.ds(0, M), :]
            for p in range(1, N_DEV):
                tot = tot + out_ref[pl.ds(p * M, M), :]
            sum_ref[...] = tot
        else:
            sum_ref[...] = jnp.zeros_like(sum_ref)

    vm = pl.BlockSpec(memory_space=pltpu.VMEM)
    second = jax.ShapeDtypeStruct((M, N) if reduce else (8, LANE), F32)
    outs = pl.pallas_call(
        body, name="all_reduce_small" if reduce else "all_gather_small", in_specs=[vm], out_specs=[vm, vm],
        out_shape=[jax.ShapeDtypeStruct((N_DEV * M, N), v.dtype), second],
        scratch_shapes=[pltpu.SemaphoreType.DMA((7,)), pltpu.SemaphoreType.DMA((7,)), pltpu.SemaphoreType.DMA],
        compiler_params=_cp(),
    )(v)
    return outs[1] if reduce else outs


def _adamw(w, g, m, v):
    shape = w.shape
    cols = shape[-1]
    rows = w.size // cols
    tr = _row_tile(rows, 256) if rows % 8 == 0 else rows
    c1 = 1.0 / (1.0 - ADAM_B1 ** ADAM_STEP)
    c2 = 1.0 / (1.0 - ADAM_B2 ** ADAM_STEP)

    def body(w_ref, g_ref, m_ref, v_ref, d_ref, nm_ref, nv_ref):
        gv = g_ref[...]
        nm = ADAM_B1 * m_ref[...] + (1.0 - ADAM_B1) * gv
        nv = ADAM_B2 * v_ref[...] + (1.0 - ADAM_B2) * (gv * gv)
        d_ref[...] = -ADAM_LR * ((nm * c1) / (jnp.sqrt(nv * c2) + ADAM_EPS) + ADAM_WD * w_ref[...])
        nm_ref[...] = nm
        nv_ref[...] = nv

    blk = pl.BlockSpec((tr, cols), lambda i: (i, 0))
    outs = pl.pallas_call(
        body, name="adamw", grid=(rows // tr,), in_specs=[blk] * 4, out_specs=[blk] * 3,
        out_shape=[jax.ShapeDtypeStruct((rows, cols), F32)] * 3, compiler_params=_cp(("parallel",)),
    )(*[a.reshape(rows, cols) for a in (w, g, m, v)])
    return tuple(o.reshape(shape) for o in outs)


def _transpose_exact(x):
    t = x.shape[1]
    eye = (lax.broadcasted_iota(jnp.int32, (t, t), 0) == lax.broadcasted_iota(jnp.int32, (t, t), 1)).astype(MXU_DTYPE)
    if MXU_DTYPE == F32:
        return lax.dot_general(eye, x, _DN["NT"], preferred_element_type=F32)
    out = None
    for _ in range(3):
        part = x.astype(MXU_DTYPE)
        x = x - part.astype(F32)
        term = lax.dot_general(eye, part, _DN["NT"], preferred_element_type=F32)
        out = term if out is None else out + term
    return out


def _adamw_layer(w, pair_sums, landed, m, v, l, prev, col_off=None):
    L, A, B = w.shape
    ta = _row_tile(A, 256) if A % 8 == 0 else A
    c1 = 1.0 / (1.0 - ADAM_B1 ** ADAM_STEP)
    c2 = 1.0 / (1.0 - ADAM_B2 ** ADAM_STEP)
    n_prev = 0 if prev is None else 4
    if col_off is None:
        g_specs = [pl.BlockSpec((None, ta, B), functools.partial(lambda i, s: (s, i, 0), s=s)) for s in (0, 0, 1, 2)]
    else:
        assert col_off % ta == 0 and pair_sums.shape[1] == B
        g_specs = [pl.BlockSpec((None, B, ta), functools.partial(lambda i, s: (s, 0, col_off // ta + i), s=s)) for s in (0, 0, 1, 2)]

    def body(*refs):
        w_ref, m_ref, v_ref, p_ref, l1_ref, l2_ref, l3_ref = refs[:7]
        go_ref, d_ref, nm_ref, nv_ref = refs[7 + n_prev:]
        gv = ((p_ref[...].astype(F32) + l1_ref[...].astype(F32)) + l2_ref[...].astype(F32)) + l3_ref[...].astype(F32)
        if col_off is not None:
            gv = _transpose_exact(gv)
        nm = ADAM_B1 * m_ref[...] + (1.0 - ADAM_B1) * gv
        nv = ADAM_B2 * v_ref[...] + (1.0 - ADAM_B2) * (gv * gv)
        d_ref[...] = -ADAM_LR * ((nm * c1) / (jnp.sqrt(nv * c2) + ADAM_EPS) + ADAM_WD * w_ref[...])
        go_ref[...] = gv
        nm_ref[...] = nm
        nv_ref[...] = nv

    lay = pl.BlockSpec((None, ta, B), lambda i: (l, i, 0))
    return pl.pallas_call(
        body, name="adamw_layer", grid=(A // ta,),
        in_specs=[lay, lay, lay] + g_specs + [ANY] * n_prev, out_specs=[lay] * 4,
        out_shape=[jax.ShapeDtypeStruct((L, A, B), F32)] * 4, input_output_aliases={7 + j: j for j in range(n_prev)},
        compiler_params=_cp(("parallel",)),
    )(w, m, v, pair_sums, landed, landed, landed, *(prev or ()))


def _relu2(acc):
    r = jnp.maximum(acc, 0.0)
    return acc, r * r


def _relu2_bwd(acc, up):
    return (acc * (2.0 * jnp.maximum(up.astype(F32), 0.0)),)


def kernel(x, ln_mix_pre, ln_mix_post, ln_mlp_pre, ln_mlp_post, w_in, conv_a_w, proj_a, proj_b, conv_c_w, conv_c_b, norm_c_g, norm_c_b, proj_c, w_o, w_up, w_down, loss_target, m_ln_mix_pre, m_ln_mix_post, m_ln_mlp_pre, m_ln_mlp_post, m_w_in, m_conv_a_w, m_proj_a, m_proj_b, m_conv_c_w, m_conv_c_b, m_norm_c_g, m_norm_c_b, m_proj_c, m_w_o, m_w_up, m_w_down, v_ln_mix_pre, v_ln_mix_post, v_ln_mlp_pre, v_ln_mlp_post, v_w_in, v_conv_a_w, v_proj_a, v_proj_b, v_conv_c_w, v_conv_c_b, v_norm_c_g, v_norm_c_b, v_proj_c, v_w_o, v_w_up, v_w_down):
    L, D, n_in_loc = w_in.shape
    S = x.shape[1]
    U = D // 4
    N_IN = n_in_loc * N_DEV
    D_FF = w_up.shape[2] * N_DEV
    assert N_IN == 23 * U and x.shape[0] == 1
    x_i, y_i, c_i = _place()
    me = 4 * x_i + 2 * y_i + c_i

    def pack(l, which, zero):
        kinds = (lambda: _pack_transposed([proj_b, proj_a, proj_c], l, zero), lambda: _pack_plain(w_o, l, zero),
                 lambda: _pack_transposed([w_up], l, zero), lambda: _pack_plain(w_down, l, zero),
                 lambda: _pack_transposed_mxu(w_in, l, zero))
        return [kinds[k]() for k in which]

    def stage1(wbs, carry, tag):
        return _ag_start(_ag_stage1, wbs, carry, f"ag_s1_{tag}")

    def stage2(started, after, carry, tag):
        send_sems, recv_sems, wbs, _ = started
        wbs = _ag_wait(_ag_stage1, wbs, send_sems, recv_sems, after, f"ag_s1_wait_{tag}")
        return _ag_start(_ag_stage2, wbs, carry, f"ag_s2_{tag}")

    def gather_end(started, after, tag):
        send_sems, recv_sems, wbs, _ = started
        wbs, zero = _ag_finish(_ag_wait(_ag_stage2, wbs, send_sems, recv_sems, after, f"ag_s2_wait_{tag}"))
        return [w.astype(MXU_DTYPE) for w in wbs], zero

    cu = U // N_DEV
    conv_loc = jnp.concatenate([conv_a_w, conv_c_w], axis=1).reshape(L * (SC_WIDTH + CF_WIDTH), cu)
    conv_all, tok = _all_gather_small(conv_loc, False)
    conv_all = conv_all.reshape(N_DEV, L, SC_WIDTH + CF_WIDTH, cu).transpose(1, 2, 0, 3).reshape(L, SC_WIDTH + CF_WIDTH, U)
    wa_full, wc_full = conv_all[:, :SC_WIDTH], conv_all[:, SC_WIDTH:]
    every = (0, 1, 2, 3, 4)
    in_s1 = stage1(pack(0, (4,), tok), jnp.zeros((8, LANE), F32), "0_in")
    rest_packed, next_packed = pack(0, (0, 1, 2, 3), tok), pack(1, every, tok)
    in_s2 = stage2(in_s1, next_packed[4], in_s1[3], "0_in")
    rest_s1 = stage1(rest_packed, in_s2[3], "0_rest")
    wb = []

    def vec(p, l):
        return p[l][None, :]

    xs = x[0]
    saved = []
    h1 = _rms_fwd(xs, vec(ln_mix_pre, 0), "rms_fwd")
    for l in range(L):
        if l == 0:
            (w_in_t,), _ = gather_end(in_s2, rest_s1[3], "0_in")
        else:
            w_p, w_o_l, w_up_t, w_dn, w_in_t = w_next
            if l + 1 < L:
                next_s1 = stage1(pack(l + 1, every, tok), h1, l + 1)
                h1 = next_s1[3]
        proj = _mm(h1, w_in_t, "NT", (F32,), "mm_proj")
        a_out = _branch_a_fwd(proj, wa_full[l], U)
        u1 = _branch_c_conv_fwd(proj, wc_full[l], vec(conv_c_b, l), U)
        u3 = _branch_c_norm_fwd(u1, vec(norm_c_g, l), vec(norm_c_b, l))
        if l == 0:
            rest_s2 = stage2(rest_s1, u3, proj, "0_rest")
            next_s1 = stage1(next_packed, rest_s2[3], 1)
            proj = next_s1[3]
        att, att_f32 = _attn_fwd(proj, U)
        if l == 0:
            (w_p, w_o_l, w_up_t, w_dn), tok = gather_end(rest_s2, att, "0_rest")
        wb.append((w_p, w_o_l, w_up_t, w_dn, w_in_t))
        merged, ya, yb, yc = _merge_fwd(proj, a_out, att, u3, w_p)
        mixed = _mm(merged, w_o_l, "NN", (F32,), "mm_mixed")
        x1, h2 = _resid_post(xs, mixed, vec(ln_mix_post, l), vec(ln_mlp_pre, l), "resid_post_mix")
        if 0 < l < L - 1:
            next_s2 = stage2(next_s1, mixed, h2, l + 1)
            h2 = next_s2[3]
        up, act = _mm(h2, w_up_t, "NT", (MXU_DTYPE, MXU_DTYPE), "mm_up", epilogue=_relu2)
        if l == 0 and L > 1:
            next_s2 = stage2(next_s1, up, act, 1)
            act = next_s2[3]
        f = _mm(act, w_dn, "NN", (F32,), "mm_down", tm=1024, tn=1024, tk=4096)
        saved.append((xs, h1, proj, a_out, u1, u3, att, att_f32, ya, yb, yc, merged, mixed, x1, h2, up, act, f))
        if l + 1 < L:
            xs, h1 = _resid_post(x1, f, vec(ln_mlp_post, l), vec(ln_mix_pre, l + 1), "resid_post_mlp")
            w_next, tok = gather_end(next_s2, h1, l + 1)
        else:
            xs, _ = _resid_post(x1, f, vec(ln_mlp_post, l), None, "resid_post_last")
    dxo, loss_row = _loss_head(xs, loss_target[0])
    loss = lax.psum(loss_row[0, 0], ("x", "y", "c"))

    small = {k: [None] * L for k in ("g1", "g2", "g3", "g4", "cb", "ng", "nb", "wa", "wc")}
    def pair_start(grads, carry, name):
        lands = [lax.empty((4, g.shape[0] // N_DEV, D), WIRE_DTYPE) for g in grads]
        send_sems, recv_sems, grads, lands, carry = _rs_start(_pair_copies, 4, grads, lands, carry, name)
        return (grads, lands, send_sems, recv_sems), carry

    def chips_start(pair_flight, after, carry, name):
        grads, landed = _rs_wait(_pair_copies, *pair_flight, after, name + "_pair_wait")
        pair_sums = [_pair_add(g, ld) for g, ld in zip(grads, landed)]
        lands = [lax.empty((3,) + p.shape[1:], WIRE_DTYPE) for p in pair_sums]
        send_sems, recv_sems, pair_sums, lands, carry = _rs_start(_rs_copies, 3, pair_sums, lands, carry, name)
        return (pair_sums, lands, send_sems, recv_sems), carry

    in_flight = []
    mix_pairs = None
    for l in reversed(range(L)):
        w_p, w_o_l, w_up_t, w_dn, w_in_t = wb[l]
        xs, h1, proj, a_out, u1, u3, att, att_f32, ya, yb, yc, merged, mixed, x1, h2, up, act, f = saved[l]
        df, small["g4"][l] = _rms_bwd(f, vec(ln_mlp_post, l), dxo, None, MXU_DTYPE, "rms_bwd_post_mlp")
        d_up = _mm(df, w_dn, "NT", (MXU_DTYPE,), "mm_d_up", epilogue=_relu2_bwd, extras=(up,))
        g_dn = _mm(act, df, "TN", (WIRE_DTYPE,), "mm_g_down", tm=512, tn=2048)
        if mix_pairs is not None:
            flight, d_up = chips_start(mix_pairs, [g_dn], d_up, f"rs_start_mix_{l + 1}")
            in_flight.append((l + 1, ("p", "o", "in"), flight))
        dh2 = _mm(d_up, w_up_t, "NN", (F32,), "mm_dh2", tm=1024, tn=1024, tk=4096)
        g_up = _mm(d_up, h2, "TN", (WIRE_DTYPE,), "mm_g_up", tm=512, tn=2048)
        mlp_pairs, dh2 = pair_start([g_up, g_dn], dh2, f"rs_pair_mlp_{l}")
        dx1, small["g3"][l] = _rms_bwd(x1, vec(ln_mlp_pre, l), dh2, dxo, F32, "rms_bwd_pre_mlp")
        dmixed, small["g2"][l] = _rms_bwd(mixed, vec(ln_mix_post, l), dx1, None, MXU_DTYPE, "rms_bwd_post_mix")
        dmerged = _mm(dmixed, w_o_l, "NT", (F32,), "mm_dmerged")
        g_o = _mm(merged, dmixed, "TN", (WIRE_DTYPE,), "mm_g_o", tm=512, tn=2048)
        flight, dmerged = chips_start(mlp_pairs, [g_o], dmerged, f"rs_start_mlp_{l}")
        in_flight.append((l, ("up", "dn"), flight))
        dya, dyb, dyc, dgla, dglb, dglc = _gate_bwd(dmerged, proj, ya, yb, yc)
        d_a_out, d_att, d_u3, g_p = _branch_proj_bwd(dya, dyb, dyc, a_out, att, u3, w_p)
        d_scb, d_scc, d_scu, small["wa"][l] = _branch_a_bwd(d_a_out, proj, wa_full[l], U)
        d_u1, small["ng"][l], small["nb"][l] = _branch_c_norm_bwd(d_u3, u1, vec(norm_c_g, l), vec(norm_c_b, l))
        d_cfa, d_cfg, small["wc"][l], small["cb"][l] = _branch_c_conv_bwd(d_u1, proj, wc_full[l], U)
        dq, dk, dv = _attn_bwd(proj, att_f32, d_att, U)
        dproj = jnp.concatenate([dq, dk, dv, d_scb, d_scc, d_scu, d_cfa, d_cfg, dgla, dglb, dglc], axis=1)
        dh1 = _mm(dproj, w_in_t, "NN", (F32,), "mm_dh1", tm=1024, tk=46 * LANE)
        g_in = _mm(dproj, h1, "TN", (WIRE_DTYPE,), "mm_g_in", tm=512, tn=2048)
        dxo, small["g1"][l] = _rms_bwd(xs, vec(ln_mix_pre, l), dh1, dx1, F32, "rms_bwd_pre_mix")
        carry = dxo if l > 0 else jnp.zeros((8, LANE), F32)
        mix_pairs, carry = pair_start([g_p, g_o, g_in], carry, f"rs_pair_mix_{l}")
        if l > 0:
            dxo = carry
    grad_x = dxo[None]
    flight, carry = chips_start(mix_pairs, [carry], jnp.zeros((8, LANE), F32), "rs_start_mix_0")
    in_flight.append((0, ("p", "o", "in"), flight))
    big = {"w_in": (w_in, m_w_in, v_w_in), "proj_a": (proj_a, m_proj_a, v_proj_a), "proj_b": (proj_b, m_proj_b, v_proj_b),
           "proj_c": (proj_c, m_proj_c, v_proj_c), "w_o": (w_o, m_w_o, v_w_o), "w_up": (w_up, m_w_up, v_w_up),
           "w_down": (w_down, m_w_down, v_w_down)}
    done = {k: None for k in big}
    for n, (l, keys, flight) in enumerate(in_flight):
        after = [carry]
        if n == len(in_flight) - 1:
            after += [done[k][3] for k in big if done[k] is not None]
        pair_sums, lands = _rs_wait(_rs_copies, *flight, after, f"rs_wait_{keys[0]}_{l}")
        g = dict(zip(keys, zip(pair_sums, lands)))
        if "up" in g:
            layer_grads = {"w_up": (g["up"], 0), "w_down": (g["dn"], None)}
        else:
            layer_grads = {"w_in": (g["in"], 0), "proj_b": (g["p"], 0), "proj_a": (g["p"], 2 * U), "proj_c": (g["p"], 3 * U),
                           "w_o": (g["o"], None)}
        for k, ((p, ld), col_off) in layer_grads.items():
            w, m, v = big[k]
            done[k] = _adamw_layer(w, p, ld, m, v, l, done[k], col_off)

    order = ("g1", "g2", "g3", "g4", "cb", "ng", "nb", "wa", "wc")
    parts = [jnp.stack(small[k]).reshape(-1) for k in order]
    flat = jnp.concatenate(parts)
    n_flat = flat.shape[0]
    pad = (-n_flat) % (8 * LANE)
    flat = jnp.pad(flat, (0, pad)).reshape(-1, LANE)
    tot = _all_gather_small(flat, True).reshape(-1)[:n_flat]
    red, pos = {}, 0
    for k, p in zip(order, parts):
        red[k] = tot[pos:pos + p.shape[0]]
        pos += p.shape[0]
    g_ln_mix_pre, g_ln_mix_post = red["g1"].reshape(L, D), red["g2"].reshape(L, D)
    g_ln_mlp_pre, g_ln_mlp_post = red["g3"].reshape(L, D), red["g4"].reshape(L, D)
    g_conv_c_b, g_norm_c_g, g_norm_c_b = red["cb"].reshape(L, U), red["ng"].reshape(L, U), red["nb"].reshape(L, U)
    g_conv_a_w = lax.dynamic_slice_in_dim(red["wa"].reshape(L, SC_WIDTH, U), me * cu, cu, axis=2)
    g_conv_c_w = lax.dynamic_slice_in_dim(red["wc"].reshape(L, CF_WIDTH, U), me * cu, cu, axis=2)

    small_w = {"ln_mix_pre": (ln_mix_pre, g_ln_mix_pre, m_ln_mix_pre, v_ln_mix_pre),
               "ln_mix_post": (ln_mix_post, g_ln_mix_post, m_ln_mix_post, v_ln_mix_post),
               "ln_mlp_pre": (ln_mlp_pre, g_ln_mlp_pre, m_ln_mlp_pre, v_ln_mlp_pre),
               "ln_mlp_post": (ln_mlp_post, g_ln_mlp_post, m_ln_mlp_post, v_ln_mlp_post),
               "conv_a_w": (conv_a_w, g_conv_a_w, m_conv_a_w, v_conv_a_w), "conv_c_w": (conv_c_w, g_conv_c_w, m_conv_c_w, v_conv_c_w),
               "conv_c_b": (conv_c_b, g_conv_c_b, m_conv_c_b, v_conv_c_b), "norm_c_g": (norm_c_g, g_norm_c_g, m_norm_c_g, v_norm_c_g),
               "norm_c_b": (norm_c_b, g_norm_c_b, m_norm_c_b, v_norm_c_b)}
    for k, (w, g, m, v) in small_w.items():
        done[k] = (g,) + _adamw(w, g, m, v)
    names = ("ln_mix_pre", "ln_mix_post", "ln_mlp_pre", "ln_mlp_post", "w_in", "conv_a_w", "proj_a", "proj_b", "conv_c_w", "conv_c_b",
             "norm_c_g", "norm_c_b", "proj_c", "w_o", "w_up", "w_down")
    return (loss, grad_x, *[done[k][0] for k in names], *[done[k][1] for k in names], *[done[k][2] for k in names],
            *[done[k][3] for k in names])
```
